```python
import math
import jax
import jax.numpy as jnp
from jax import lax
import numpy as np

D_MODEL = 1024
BATCH = 8
SEQ = 2048
DEPTH = 2

CTX_LEN = 256
GRID_W = 64
EPS = 1e-6
D_RNN = 1280
RNN_BLOCKS = 20
RNN_BLOCK_W = D_RNN // RNN_BLOCKS
CONV_W = 4
LRU_C = 8.0
N_HEADS = 8
HEAD_DIM = 64
V_DIM = 2 * HEAD_DIM
QK_W = N_HEADS * 2 * HEAD_DIM
ATTN_W = N_HEADS * V_DIM
Q_BLOCK = 128
ROPE_THETA = 10000.0
ROPE_FREQS = HEAD_DIM // 4
D_FF = 2816
N_EXPERTS = 8
TOP_K = 2
D_FF_EXPERT = 2816
N_DENSE = (DEPTH + 1) // 2
N_MOE = DEPTH // 2
IN_SPLITS = (D_RNN, D_RNN + QK_W, D_RNN + QK_W + ATTN_W, 2 * D_RNN + QK_W + ATTN_W, 2 * D_RNN + 2 * QK_W + ATTN_W)
CTX_STATE_COLS = IN_SPLITS[2]
IN_W = 2 * D_RNN + 2 * QK_W + ATTN_W + 2 * D_MODEL

kernel_name = 'hybrid_rglru_diffattn_moe_dit_block'


def rmsnorm(x, g):
    xf = x.astype(jnp.float32)
    xf = xf * lax.rsqrt(jnp.mean(xf * xf, axis=-1, keepdims=True) + EPS)
    return xf.astype(x.dtype) * g


def modulate(h, shift, scale):
    return h * (1 + scale) + shift


def axial_rope_tables(n_tokens):
    rows = n_tokens // GRID_W
    row = jnp.repeat(jnp.arange(rows), GRID_W)
    col = jnp.tile(jnp.arange(GRID_W), rows)
    inv_freq = jnp.power(ROPE_THETA, -jnp.arange(ROPE_FREQS, dtype=jnp.float32) / ROPE_FREQS)
    ang = jnp.stack([row, col], axis=-1).astype(jnp.float32)[..., None] * inv_freq
    return jnp.cos(ang), jnp.sin(ang)


def apply_axial_rope(t, cos, sin):
    shp = t.shape
    tr = t.reshape(*shp[:-1], 2, 2, ROPE_FREQS)
    rot = jnp.stack([-tr[..., 1, :], tr[..., 0, :]], axis=-2)
    c = cos[None, :, None, None, :, None, :].astype(t.dtype)
    s = sin[None, :, None, None, :, None, :].astype(t.dtype)
    return (tr * c + rot * s).reshape(shp)


def centred_dwconv(x, w, b):
    left = CONV_W // 2
    right = CONV_W - 1 - left
    t_len = x.shape[1]
    xp = jnp.pad(x, ((0, 0), (left, right), (0, 0)))
    out = b
    for tap in range(CONV_W):
        out = out + xp[:, tap:tap + t_len] * w[tap]
    return out


def rglru_coeffs(xc, gate_w, gate_b, lam):
    bsz, t_len = xc.shape[:2]
    xb = xc.reshape(bsz, t_len, RNN_BLOCKS, RNN_BLOCK_W)
    gates = jnp.einsum('btnc,gncd->gbtnd', xb, gate_w).reshape(2, bsz, t_len, D_RNN) + gate_b[:, None, None, :]
    gates = jax.nn.sigmoid(gates.astype(jnp.float32))
    r, i = gates[0], gates[1]
    log_a = -LRU_C * r * jax.nn.softplus(-lam.astype(jnp.float32))
    a = jnp.exp(log_a)
    mult = jnp.sqrt(-jnp.expm1(2.0 * log_a))
    return a, mult * (i * xc.astype(jnp.float32))


def _lru_combine(left, right):
    a_l, b_l = left
    a_r, b_r = right
    return a_l * a_r, a_r * b_l + b_r


def lru_scan(a, b, h0, reverse):
    if h0 is not None:
        if reverse:
            b = b.at[:, -1].add(a[:, -1] * h0)
        else:
            b = b.at[:, 0].add(a[:, 0] * h0)
    _, h = lax.associative_scan(_lru_combine, (a, b), axis=1, reverse=reverse)
    return h


def diff_attend(q, k, v, lam):
    s = jnp.einsum('bqhjd,bkhjd->bhjqk', q, k).astype(jnp.float32) * (HEAD_DIM ** -0.5)
    p = jax.nn.softmax(s, axis=-1)
    w = p[:, :, 0] - lam * p[:, :, 1]
    return jnp.einsum('bhqk,bkhe->bqhe', w.astype(v.dtype), v)


def head_out(o, subln_g, lam_init):
    return (rmsnorm(o, subln_g) * (1.0 - lam_init)).reshape(*o.shape[:2], ATTN_W)


def merge_branches(gate_logits, y_rnn, y_attn, w_out):
    g = jax.nn.sigmoid(gate_logits.astype(jnp.float32)).astype(y_rnn.dtype)
    g_rnn, g_attn = jnp.split(g, 2, axis=-1)
    return (g_rnn * y_rnn + g_attn * y_attn) @ w_out


def token_mixer(hx, hc, w_in, conv_w, conv_b, gate_w, gate_b, lru_lam, lam_vecs, lam_init, subln_g,
                w_proj_rnn, w_proj_attn, w_out, cos, sin, need_ctx_out):
    bsz, s_len = hx.shape[:2]
    c_len = hc.shape[1]
    xr_x, k_x, v_x, gr_x, q_x, g_x = jnp.split(hx @ w_in, IN_SPLITS, axis=-1)
    if need_ctx_out:
        xr_c, k_c, v_c, gr_c, q_c, g_c = jnp.split(hc @ w_in, IN_SPLITS, axis=-1)
    else:
        xr_c, k_c, v_c = jnp.split(hc @ w_in[:, :CTX_STATE_COLS], IN_SPLITS[:2], axis=-1)

    xc_x = centred_dwconv(xr_x, conv_w, conv_b)
    xc_c = centred_dwconv(xr_c, conv_w, conv_b)
    hs_x, hs_c = [], []
    for d, rev in enumerate((False, True)):
        a_c, b_c = rglru_coeffs(xc_c, gate_w[d], gate_b[d], lru_lam[d])
        h_c = lru_scan(a_c, b_c, None, rev)
        h_last = h_c[:, 0] if rev else h_c[:, -1]
        a_x, b_x = rglru_coeffs(xc_x, gate_w[d], gate_b[d], lru_lam[d])
        hs_x.append(lru_scan(a_x, b_x, h_last, rev))
        if need_ctx_out:
            hs_c.append(h_c)
    y_rnn_x = ((hs_x[0] + hs_x[1]).astype(hx.dtype) * jax.nn.gelu(gr_x)) @ w_proj_rnn

    lv = lam_vecs.astype(jnp.float32)
    lam = jnp.exp(jnp.sum(lv[0] * lv[1])) - jnp.exp(jnp.sum(lv[2] * lv[3])) + lam_init
    q_x = apply_axial_rope(q_x.reshape(bsz, s_len, N_HEADS, 2, HEAD_DIM), cos, sin)
    k_x = apply_axial_rope(k_x.reshape(bsz, s_len, N_HEADS, 2, HEAD_DIM), cos, sin)
    k_c = k_c.reshape(bsz, c_len, N_HEADS, 2, HEAD_DIM)
    v_c = v_c.reshape(bsz, c_len, N_HEADS, V_DIM)
    k_all = jnp.concatenate([k_c, k_x], axis=1)
    v_all = jnp.concatenate([v_c, v_x.reshape(bsz, s_len, N_HEADS, V_DIM)], axis=1)
    n_blk = s_len // Q_BLOCK
    q_blocks = jnp.moveaxis(q_x.reshape(bsz, n_blk, Q_BLOCK, N_HEADS, 2, HEAD_DIM), 1, 0)
    o_blocks = lax.map(lambda qb: diff_attend(qb, k_all, v_all, lam), q_blocks)
    o_x = jnp.moveaxis(o_blocks, 0, 1).reshape(bsz, s_len, N_HEADS, V_DIM)
    y_attn_x = head_out(o_x, subln_g, lam_init) @ w_proj_attn

    out_x = merge_branches(g_x, y_rnn_x, y_attn_x, w_out)
    if not need_ctx_out:
        return out_x, None

    y_rnn_c = ((hs_c[0] + hs_c[1]).astype(hc.dtype) * jax.nn.gelu(gr_c)) @ w_proj_rnn
    q_c = q_c.reshape(bsz, c_len, N_HEADS, 2, HEAD_DIM)
    y_attn_c = head_out(diff_attend(q_c, k_c, v_c, lam), subln_g, lam_init) @ w_proj_attn
    out_c = merge_branches(g_c, y_rnn_c, y_attn_c, w_out)
    return out_x, out_c


def swiglu(h, w_gu, w_down):
    g, u = jnp.split(h @ w_gu, 2, axis=-1)
    return (jax.nn.silu(g) * u) @ w_down


def moe_swiglu(h, router_w, router_b, w_gu, w_down):
    logits = (h @ router_w).astype(jnp.float32) + router_b.astype(jnp.float32)
    top_v, top_i = lax.top_k(logits, TOP_K)
    top_w = jax.nn.softmax(top_v, axis=-1)
    combine = jnp.einsum('...k,...ke->...e', top_w, jax.nn.one_hot(top_i, N_EXPERTS, dtype=jnp.float32))
    combine = combine.astype(h.dtype)
    out = jnp.zeros_like(h)
    for e in range(N_EXPERTS):
        out = out + combine[..., e:e + 1] * swiglu(h, w_gu[e], w_down[e])
    return out


def setup_inputs(seed: int = 0) -> dict:
    key = jax.random.key(seed)
    ks = jax.random.split(key, 24)

    def nrm(k, shape, scale):
        return jax.random.normal(k, shape, jnp.float32) * scale

    u = jax.random.uniform(ks[12], (DEPTH, 2, D_RNN), jnp.float32, 0.9, 0.999)
    s = u ** (1.0 / LRU_C)
    return {
        'x': nrm(ks[0], (BATCH, SEQ, D_MODEL), 1.0),
        'c': nrm(ks[1], (BATCH, D_MODEL), 1.0),
        'ctx': nrm(ks[2], (BATCH, CTX_LEN, D_MODEL), 1.0),
        'c_ctx': nrm(ks[3], (D_MODEL,), 1.0),
        'ada_w': nrm(ks[4], (DEPTH, D_MODEL, 6 * D_MODEL), D_MODEL ** -0.5),
        'ada_b': nrm(ks[5], (DEPTH, 6 * D_MODEL), 0.02),
        'norm_g': 1.0 + nrm(ks[6], (DEPTH, 4, D_MODEL), 0.02),
        'w_in': nrm(ks[7], (DEPTH, D_MODEL, IN_W), D_MODEL ** -0.5),
        'conv_w': nrm(ks[8], (DEPTH, CONV_W, D_RNN), CONV_W ** -0.5),
        'conv_b': nrm(ks[9], (DEPTH, D_RNN), 0.02),
        'lru_gate_w': nrm(ks[10], (DEPTH, 2, 2, RNN_BLOCKS, RNN_BLOCK_W, RNN_BLOCK_W), RNN_BLOCK_W ** -0.5),
        'lru_gate_b': nrm(ks[11], (DEPTH, 2, 2, D_RNN), 0.02),
        'lru_lambda': jnp.log(s) - jnp.log1p(-s),
        'diff_lambda': nrm(ks[13], (DEPTH, 4, HEAD_DIM), 0.1),
        'subln_g': 1.0 + nrm(ks[14], (DEPTH, V_DIM), 0.02),
        'w_proj_rnn': nrm(ks[15], (DEPTH, D_RNN, D_MODEL), D_RNN ** -0.5),
        'w_proj_attn': nrm(ks[16], (DEPTH, ATTN_W, D_MODEL), ATTN_W ** -0.5),
        'w_out': nrm(ks[17], (DEPTH, D_MODEL, D_MODEL), D_MODEL ** -0.5),
        'ffn_w_gu': nrm(ks[18], (N_DENSE, D_MODEL, 2 * D_FF), D_MODEL ** -0.5),
        'ffn_w_down': nrm(ks[19], (N_DENSE, D_FF, D_MODEL), D_FF ** -0.5),
        'router_w': nrm(ks[20], (N_MOE, D_MODEL, N_EXPERTS), D_MODEL ** -0.5),
        'router_b': nrm(ks[21], (N_MOE, N_EXPERTS), 0.01),
        'moe_w_gu': nrm(ks[22], (N_MOE, N_EXPERTS, D_MODEL, 2 * D_FF_EXPERT), D_MODEL ** -0.5),
        'moe_w_down': nrm(ks[23], (N_MOE, N_EXPERTS, D_FF_EXPERT, D_MODEL), D_FF_EXPERT ** -0.5),
    }


def reference(x, c, ctx, c_ctx, ada_w, ada_b, norm_g, w_in, conv_w, conv_b, lru_gate_w, lru_gate_b,
              lru_lambda, diff_lambda, subln_g, w_proj_rnn, w_proj_attn, w_out, ffn_w_gu, ffn_w_down,
              router_w, router_b, moe_w_gu, moe_w_down):
    cos, sin = axial_rope_tables(x.shape[1])
    silu_c = jax.nn.silu(c)
    silu_cc = jax.nn.silu(c_ctx)
    for l in range(DEPTH):
        last = l == DEPTH - 1
        lam_init = 0.8 - 0.6 * math.exp(-0.3 * l)
        mod_x = (silu_c @ ada_w[l] + ada_b[l])[:, None, :]
        mod_c = silu_cc @ ada_w[l] + ada_b[l]
        sh1x, sc1x, g1x, sh2x, sc2x, g2x = jnp.split(mod_x, 6, axis=-1)
        sh1c, sc1c, g1c, sh2c, sc2c, g2c = jnp.split(mod_c, 6, axis=-1)

        hx = modulate(rmsnorm(x, norm_g[l, 0]), sh1x, sc1x)
        hc = modulate(rmsnorm(ctx, norm_g[l, 0]), sh1c, sc1c)
        mx, mc = token_mixer(hx, hc, w_in[l], conv_w[l], conv_b[l], lru_gate_w[l], lru_gate_b[l],
                             lru_lambda[l], diff_lambda[l], lam_init, subln_g[l], w_proj_rnn[l],
                             w_proj_attn[l], w_out[l], cos, sin, not last)
        x = x + g1x * rmsnorm(mx, norm_g[l, 1])
        if not last:
            ctx = ctx + g1c * rmsnorm(mc, norm_g[l, 1])

        if l % 2 == 0:
            channel = lambda h: swiglu(h, ffn_w_gu[l // 2], ffn_w_down[l // 2])
        else:
            channel = lambda h: moe_swiglu(h, router_w[l // 2], router_b[l // 2], moe_w_gu[l // 2], moe_w_down[l // 2])
        fx = channel(modulate(rmsnorm(x, norm_g[l, 2]), sh2x, sc2x))
        x = x + g2x * rmsnorm(fx, norm_g[l, 3])
        if not last:
            fc = channel(modulate(rmsnorm(ctx, norm_g[l, 2]), sh2c, sc2c))
            ctx = ctx + g2c * rmsnorm(fc, norm_g[l, 3])
    return x
```

```python
import functools
import math

import jax
import jax.numpy as jnp
from jax import lax
from jax.experimental import pallas as pl
from jax.experimental.pallas import tpu as pltpu

F32 = jnp.float32
BF16 = jnp.bfloat16

D_MODEL = 1024
BATCH = 8
SEQ = 2048
DEPTH = 2
CTX_LEN = 256
GRID_W = 64
EPS = 1e-6
D_RNN = 1280
RNN_BLOCKS = 20
RNN_BLOCK_W = D_RNN // RNN_BLOCKS
CONV_W = 4
LRU_C = 8.0
N_HEADS = 8
HEAD_DIM = 64
V_DIM = 2 * HEAD_DIM
QK_W = N_HEADS * 2 * HEAD_DIM
ATTN_W = N_HEADS * V_DIM
ROPE_THETA = 10000.0
ROPE_FREQS = HEAD_DIM // 4
D_FF = 2816
N_EXPERTS = 8
TOP_K = 2

LANES = 128
SUBLANES = 8
VMEM_LIMIT_BYTES = 52 * 1024 * 1024

COL_Q = 0
COL_K = COL_Q + QK_W
COL_V = COL_K + QK_W
COL_G_RNN = COL_V + ATTN_W
COL_G_ATTN = COL_G_RNN + D_MODEL
COL_XR = COL_G_ATTN + D_MODEL
COL_GR = COL_XR + D_RNN
IN_W = COL_GR + D_RNN
CCOL_K = 0
CCOL_V = CCOL_K + QK_W
CCOL_XR = CCOL_V + ATTN_W
CTX_STATE_W = CCOL_XR + D_RNN

RNN_CB = 256
SCAN_CHUNK = 256


def _params(*sem):
    return pltpu.CompilerParams(dimension_semantics=sem, vmem_limit_bytes=VMEM_LIMIT_BYTES)


def _rms(x):
    return x * lax.rsqrt(jnp.mean(x * x, axis=-1, keepdims=True) + EPS)


def _ada_kernel(c_ref, w_ref, b_ref, o_ref):
    c = c_ref[...]
    s = c * jax.nn.sigmoid(c)
    o_ref[0] = jnp.dot(s, w_ref[0], preferred_element_type=F32, precision=lax.Precision.HIGHEST) + b_ref[0]


def ada_modulation(cvec, ada_w, ada_b):
    rows = cvec.shape[0]
    tn = 1536
    n = 6 * D_MODEL
    return pl.pallas_call(
        _ada_kernel,
        grid=(DEPTH, n // tn),
        in_specs=[
            pl.BlockSpec((rows, D_MODEL), lambda l, j: (0, 0)),
            pl.BlockSpec((1, D_MODEL, tn), lambda l, j: (l, 0, j)),
            pl.BlockSpec((1, 1, tn), lambda l, j: (l, 0, j)),
        ],
        out_specs=pl.BlockSpec((1, rows, tn), lambda l, j: (l, 0, j)),
        out_shape=jax.ShapeDtypeStruct((DEPTH, rows, n), F32),
        compiler_params=_params("parallel", "parallel"),
        name="ada_modulation",
    )(cvec, ada_w, ada_b.reshape(DEPTH, 1, n))


def _nmm_kernel(x_ref, g_ref, sh_ref, sc_ref, w_ref, o_ref, h_ref):
    @pl.when(pl.program_id(1) == 0)
    def _():
        h = _rms(x_ref[...]) * g_ref[...]
        h = h * (1.0 + sc_ref[0]) + sh_ref[0]
        h_ref[...] = h.astype(BF16)

    o_ref[...] = jnp.dot(h_ref[...], w_ref[...], preferred_element_type=F32).astype(o_ref.dtype)


def norm_mod_matmul(x, gain, shift, scale, w, *, tm, tn):
    t, d = x.shape
    n = w.shape[1]
    nb = shift.shape[0]
    bpb = (t // tm) // nb
    return pl.pallas_call(
        _nmm_kernel,
        grid=(t // tm, n // tn),
        in_specs=[
            pl.BlockSpec((tm, d), lambda i, j: (i, 0)),
            pl.BlockSpec((1, d), lambda i, j: (0, 0)),
            pl.BlockSpec((1, 1, d), lambda i, j: (i // bpb, 0, 0)),
            pl.BlockSpec((1, 1, d), lambda i, j: (i // bpb, 0, 0)),
            pl.BlockSpec((d, tn), lambda i, j: (0, j)),
        ],
        out_specs=pl.BlockSpec((tm, tn), lambda i, j: (i, j)),
        out_shape=jax.ShapeDtypeStruct((t, n), BF16),
        scratch_shapes=[pltpu.VMEM((tm, d), BF16)],
        compiler_params=_params("parallel", "arbitrary"),
        name="norm_mod_matmul",
    )(x, gain.reshape(1, d), shift, scale, w)


def _group_scan(a, b, reverse):
    rows, c = a.shape
    a = a.reshape(rows // SUBLANES, SUBLANES, c)
    b = b.reshape(rows // SUBLANES, SUBLANES, c)
    sub = lax.broadcasted_iota(jnp.int32, a.shape, 1)
    for sh in (1, 2, 4):
        if reverse:
            keep = sub < SUBLANES - sh
            amount = SUBLANES - sh
        else:
            keep = sub >= sh
            amount = sh
        a_sh = jnp.where(keep, pltpu.roll(a, amount, 1), 1.0)
        b_sh = jnp.where(keep, pltpu.roll(b, amount, 1), 0.0)
        b = a * b_sh + b
        a = a * a_sh
    return a.reshape(rows, c), b.reshape(rows, c)


def _dwconv(x, cw, cb):
    n = x.shape[0]
    rows = lax.broadcasted_iota(jnp.int32, x.shape, 0)
    xm2 = jnp.where(rows >= 2, pltpu.roll(x, 2, 0), 0.0)
    xm1 = jnp.where(rows >= 1, pltpu.roll(x, 1, 0), 0.0)
    xp1 = jnp.where(rows < n - 1, pltpu.roll(x, n - 1, 0), 0.0)
    return cb + xm2 * cw[0:1] + xm1 * cw[1:2] + x * cw[2:3] + xp1 * cw[3:4]


def _gelu_tanh(x):
    return 0.5 * x * (1.0 + jnp.tanh(math.sqrt(2.0 / math.pi) * (x + 0.044715 * (x * x * x))))


def _rglru_kernel(*refs, n_c, n_x, ctx_out):
    if ctx_out:
        (xrx_ref, grx_ref, xrc_ref, grc_ref, cw_ref, cb_ref, wg_ref, gb_ref, lam_ref,
         yx_ref, yc_ref, xc_s, a0_s, b0_s, a1_s, b1_s) = refs
    else:
        (xrx_ref, grx_ref, xrc_ref, cw_ref, cb_ref, wg_ref, gb_ref, lam_ref,
         yx_ref, xc_s, a0_s, b0_s, a1_s, b1_s) = refs
    c = RNN_CB
    n_all = n_c + n_x
    cw = cw_ref[...]
    cb = cb_ref[...]

    xc_s[0:n_c, :] = _dwconv(xrc_ref[...].astype(F32), cw, cb)
    xc_s[n_c:n_all, :] = _dwconv(xrx_ref[...].astype(F32), cw, cb)

    lam = lam_ref[0]
    neg_sp = -LRU_C * jax.nn.softplus(-lam)

    def coeff_chunk(ci, carry):
        r0 = pl.multiple_of(ci * SCAN_CHUNK, SCAN_CHUNK)
        xc = xc_s[pl.ds(r0, SCAN_CHUNK), :]
        gates = jnp.dot(xc.astype(BF16), wg_ref[0], preferred_element_type=F32) + gb_ref[0]
        gates = jax.nn.sigmoid(gates)
        for d, (a_s, b_s) in enumerate(((a0_s, b0_s), (a1_s, b1_s))):
            r = gates[:, (2 * d) * c:(2 * d + 1) * c]
            i = gates[:, (2 * d + 1) * c:(2 * d + 2) * c]
            log_a = r * neg_sp[:, d * c:(d + 1) * c]
            a = jnp.exp(log_a)
            mult = jnp.sqrt(1.0 - jnp.exp(2.0 * log_a))
            bb = mult * (i * xc)
            a_cum, b_cum = _group_scan(a, bb, reverse=(d == 1))
            a_s[pl.ds(r0, SCAN_CHUNK), :] = a_cum
            b_s[pl.ds(r0, SCAN_CHUNK), :] = b_cum
        return carry

    lax.fori_loop(0, n_all // SCAN_CHUNK, coeff_chunk, 0)

    g_c = n_c // SUBLANES
    g_all = n_all // SUBLANES

    def carry_step(k, carry):
        hf, hr = carry
        rf = pl.multiple_of(k * SUBLANES, SUBLANES)
        h = a0_s[pl.ds(rf, SUBLANES), :] * hf + b0_s[pl.ds(rf, SUBLANES), :]
        a0_s[pl.ds(rf, SUBLANES), :] = h
        hf = h[SUBLANES - 1:SUBLANES, :]
        kr = jnp.where(k < g_c, g_c - 1 - k, g_all + g_c - 1 - k)
        rr = pl.multiple_of(kr * SUBLANES, SUBLANES)
        h = a1_s[pl.ds(rr, SUBLANES), :] * hr + b1_s[pl.ds(rr, SUBLANES), :]
        a1_s[pl.ds(rr, SUBLANES), :] = h
        hr = h[0:1, :]
        return hf, hr

    zero = jnp.zeros((1, c), F32)
    lax.fori_loop(0, g_all, carry_step, (zero, zero), unroll=4)

    yx = (a0_s[n_c:n_all, :] + a1_s[n_c:n_all, :]) * _gelu_tanh(grx_ref[...].astype(F32))
    yx_ref[...] = yx.astype(yx_ref.dtype)
    if ctx_out:
        yc = (a0_s[0:n_c, :] + a1_s[0:n_c, :]) * _gelu_tanh(grc_ref[...].astype(F32))
        yc_ref[...] = yc.astype(yc_ref.dtype)


def rglru(proj_x, proj_c, conv_w, conv_b, wg, gb, lam, *, xr_col, gr_col, c_xr_col, c_gr_col, ctx_out):
    c = RNN_CB
    n_cb = D_RNN // c
    xr_b, gr_b, cxr_b = xr_col // c, gr_col // c, c_xr_col // c
    in_specs = [
        pl.BlockSpec((SEQ, c), lambda b, j: (b, xr_b + j)),
        pl.BlockSpec((SEQ, c), lambda b, j: (b, gr_b + j)),
        pl.BlockSpec((CTX_LEN, c), lambda b, j: (b, cxr_b + j)),
    ]
    args = [proj_x, proj_x, proj_c]
    if ctx_out:
        cgr_b = c_gr_col // c
        in_specs.append(pl.BlockSpec((CTX_LEN, c), lambda b, j: (b, cgr_b + j)))
        args.append(proj_c)
    in_specs += [
        pl.BlockSpec((CONV_W, c), lambda b, j: (0, j)),
        pl.BlockSpec((1, c), lambda b, j: (0, j)),
        pl.BlockSpec((1, c, 4 * c), lambda b, j: (j, 0, 0)),
        pl.BlockSpec((1, 1, 4 * c), lambda b, j: (j, 0, 0)),
        pl.BlockSpec((1, 1, 2 * c), lambda b, j: (j, 0, 0)),
    ]
    args += [conv_w, conv_b.reshape(1, D_RNN), wg, gb, lam]
    out_specs = [pl.BlockSpec((SEQ, c), lambda b, j: (b, j))]
    out_shape = [jax.ShapeDtypeStruct((BATCH * SEQ, D_RNN), BF16)]
    if ctx_out:
        out_specs.append(pl.BlockSpec((CTX_LEN, c), lambda b, j: (b, j)))
        out_shape.append(jax.ShapeDtypeStruct((BATCH * CTX_LEN, D_RNN), BF16))
    n_all = SEQ + CTX_LEN
    out = pl.pallas_call(
        functools.partial(_rglru_kernel, n_c=CTX_LEN, n_x=SEQ, ctx_out=ctx_out),
        grid=(BATCH, n_cb),
        in_specs=in_specs,
        out_specs=out_specs,
        out_shape=out_shape,
        scratch_shapes=[pltpu.VMEM((n_all, c), F32) for _ in range(5)],
        compiler_params=_params("parallel", "parallel"),
        name="rglru",
    )(*args)
    return out if ctx_out else (out[0], None)


def _rope(t, cos, sin_signed):
    lane = lax.broadcasted_iota(jnp.int32, t.shape, 1)
    first_half = (lane % (2 * ROPE_FREQS)) < ROPE_FREQS
    partner = jnp.where(first_half, pltpu.roll(t, LANES - ROPE_FREQS, 1), pltpu.roll(t, ROPE_FREQS, 1))
    return t * cos + partner * sin_signed


def _attn_kernel(*refs, tq, n_c, n_x, lam_init):
    if n_x:
        (q_ref, kx_ref, vx_ref, kc_ref, vc_ref, cos_ref, sin_ref, dl_ref, sg_ref,
         o_ref, k_s, v_s, lam_s) = refs
    else:
        q_ref, kc_ref, vc_ref, dl_ref, sg_ref, o_ref, k_s, v_s, lam_s = refs
    qi = pl.program_id(2)

    @pl.when(qi == 0)
    def _():
        k_s[0:n_c, :] = kc_ref[...]
        v_s[0:n_c, :] = vc_ref[...]
        if n_x:
            kx = _rope(kx_ref[...].astype(F32), cos_ref[...], sin_ref[...])
            k_s[n_c:n_c + n_x, :] = kx.astype(BF16)
            v_s[n_c:n_c + n_x, :] = vx_ref[...]
        dl = dl_ref[...]
        e1 = jnp.exp(jnp.sum(dl[0:1] * dl[1:2], axis=-1, keepdims=True))
        e2 = jnp.exp(jnp.sum(dl[2:3] * dl[3:4], axis=-1, keepdims=True))
        lam_s[...] = jnp.broadcast_to(e1 - e2 + lam_init, lam_s.shape)

    q = q_ref[...].astype(F32)
    if n_x:
        r0 = pl.multiple_of(qi * tq, tq)
        q = _rope(q, cos_ref[pl.ds(r0, tq), :], sin_ref[pl.ds(r0, tq), :])
    q = q * (HEAD_DIM ** -0.5)
    lane = lax.broadcasted_iota(jnp.int32, q.shape, 1)
    qq = jnp.concatenate([jnp.where(lane < HEAD_DIM, q, 0.0), jnp.where(lane >= HEAD_DIM, q, 0.0)], axis=0)
    s = lax.dot_general(qq.astype(BF16), k_s[...], (((1,), (1,)), ((), ())), preferred_element_type=F32)
    m = jnp.max(s, axis=-1, keepdims=True)
    e = jnp.exp(s - m)
    p = e / jnp.sum(e, axis=-1, keepdims=True)
    w = p[0:tq] - lam_s[0:1, 0:1] * p[tq:2 * tq]
    o = jnp.dot(w.astype(BF16), v_s[...], preferred_element_type=F32)
    o_ref[...] = (_rms(o) * sg_ref[...] * (1.0 - lam_init)).astype(o_ref.dtype)


def diff_attention(q_arr, q_col, n_q, kvx, kvc, cols, cos, sin_signed, diff_lambda, subln_g, *, lam_init, tq):
    kx_col, vx_col, kc_col, vc_col = cols
    qb, kxb, vxb, kcb, vcb = (v // LANES for v in (q_col, kx_col, vx_col, kc_col, vc_col))
    n_x = SEQ if kvx is not None else 0
    n_qb = n_q // tq
    in_specs = [pl.BlockSpec((tq, LANES), lambda b, h, i: (b * n_qb + i, qb + h))]
    args = [q_arr]
    if n_x:
        in_specs += [
            pl.BlockSpec((SEQ, LANES), lambda b, h, i: (b, kxb + h)),
            pl.BlockSpec((SEQ, LANES), lambda b, h, i: (b, vxb + h)),
        ]
        args += [kvx, kvx]
    in_specs += [
        pl.BlockSpec((CTX_LEN, LANES), lambda b, h, i: (b, kcb + h)),
        pl.BlockSpec((CTX_LEN, LANES), lambda b, h, i: (b, vcb + h)),
    ]
    args += [kvc, kvc]
    if n_x:
        in_specs += [
            pl.BlockSpec((SEQ, LANES), lambda b, h, i: (0, 0)),
            pl.BlockSpec((SEQ, LANES), lambda b, h, i: (0, 0)),
        ]
        args += [cos, sin_signed]
    in_specs += [
        pl.BlockSpec((4, HEAD_DIM), lambda b, h, i: (0, 0)),
        pl.BlockSpec((1, V_DIM), lambda b, h, i: (0, 0)),
    ]
    args += [diff_lambda, subln_g.reshape(1, V_DIM)]
    n_kv = CTX_LEN + n_x
    return pl.pallas_call(
        functools.partial(_attn_kernel, tq=tq, n_c=CTX_LEN, n_x=n_x, lam_init=lam_init),
        grid=(BATCH, N_HEADS, n_qb),
        in_specs=in_specs,
        out_specs=pl.BlockSpec((tq, LANES), lambda b, h, i: (b * n_qb + i, h)),
        out_shape=jax.ShapeDtypeStruct((BATCH * n_q, ATTN_W), BF16),
        scratch_shapes=[
            pltpu.VMEM((n_kv, LANES), BF16),
            pltpu.VMEM((n_kv, LANES), BF16),
            pltpu.VMEM((SUBLANES, LANES), F32),
        ],
        compiler_params=_params("parallel", "parallel", "arbitrary"),
        name="diff_attention",
    )(*args)


def _merge_kernel(yr_ref, ya_ref, gr_ref, ga_ref, x_ref, g1_ref, ng_ref, wr_ref, wa_ref, wo_ref, o_ref):
    y_rnn = jnp.dot(yr_ref[...], wr_ref[...], preferred_element_type=F32)
    y_attn = jnp.dot(ya_ref[...], wa_ref[...], preferred_element_type=F32)
    g_rnn = jax.nn.sigmoid(gr_ref[...].astype(F32))
    g_attn = jax.nn.sigmoid(ga_ref[...].astype(F32))
    m = (g_rnn * y_rnn + g_attn * y_attn).astype(BF16)
    mx = jnp.dot(m, wo_ref[...], preferred_element_type=F32)
    o_ref[...] = x_ref[...] + g1_ref[0] * (_rms(mx) * ng_ref[...])


def merge_out(y_rnn, y_attn, proj, x, g1, ng, w_proj_rnn, w_proj_attn, w_out, *, tm):
    t, d = x.shape
    nb = g1.shape[0]
    bpb = (t // tm) // nb
    grb, gab = COL_G_RNN // d, COL_G_ATTN // d
    return pl.pallas_call(
        _merge_kernel,
        grid=(t // tm,),
        in_specs=[
            pl.BlockSpec((tm, D_RNN), lambda i: (i, 0)),
            pl.BlockSpec((tm, ATTN_W), lambda i: (i, 0)),
            pl.BlockSpec((tm, d), lambda i: (i, grb)),
            pl.BlockSpec((tm, d), lambda i: (i, gab)),
            pl.BlockSpec((tm, d), lambda i: (i, 0)),
            pl.BlockSpec((1, 1, d), lambda i: (i // bpb, 0, 0)),
            pl.BlockSpec((1, d), lambda i: (0, 0)),
            pl.BlockSpec((D_RNN, d), lambda i: (0, 0)),
            pl.BlockSpec((ATTN_W, d), lambda i: (0, 0)),
            pl.BlockSpec((d, d), lambda i: (0, 0)),
        ],
        out_specs=pl.BlockSpec((tm, d), lambda i: (i, 0)),
        out_shape=jax.ShapeDtypeStruct((t, d), F32),
        compiler_params=_params("parallel"),
        name="merge_out",
    )(y_rnn, y_attn, proj, proj, x, g1, ng.reshape(1, d), w_proj_rnn, w_proj_attn, w_out)


def _ffn_kernel(*refs, n_e):
    if n_e > 1:
        x_ref, g_ref, sh_ref, sc_ref, g2_ref, ng_ref, cmb_ref, wg_ref, wu_ref, wd_ref, o_ref, h_ref, acc_ref = refs
    else:
        x_ref, g_ref, sh_ref, sc_ref, g2_ref, ng_ref, wg_ref, wu_ref, wd_ref, o_ref, h_ref, acc_ref = refs
    e = pl.program_id(1)
    f = pl.program_id(2)
    first = jnp.logical_and(e == 0, f == 0)
    last = jnp.logical_and(e == pl.num_programs(1) - 1, f == pl.num_programs(2) - 1)

    @pl.when(first)
    def _():
        h = _rms(x_ref[...]) * g_ref[...]
        h = h * (1.0 + sc_ref[0]) + sh_ref[0]
        h_ref[...] = h.astype(BF16)
        acc_ref[...] = jnp.zeros_like(acc_ref)

    h = h_ref[...]
    gate = jnp.dot(h, wg_ref[0], preferred_element_type=F32)
    up = jnp.dot(h, wu_ref[0], preferred_element_type=F32)
    act = (gate * jax.nn.sigmoid(gate) * up).astype(BF16)
    y = jnp.dot(act, wd_ref[0], preferred_element_type=F32)
    if n_e > 1:
        lane = lax.broadcasted_iota(jnp.int32, cmb_ref.shape, 1)
        y = y * jnp.sum(jnp.where(lane == e, cmb_ref[...], 0.0), axis=-1, keepdims=True)
    acc_ref[...] += y

    @pl.when(last)
    def _():
        o_ref[...] = x_ref[...] + g2_ref[0] * (_rms(acc_ref[...]) * ng_ref[...])


def ffn_residual(x, gain, shift, scale, g2, ng, w_gu, w_down, combine, *, tm, tf):
    t, d = x.shape
    n_e, _, two_f = w_gu.shape
    ff = two_f // 2
    nf = ff // tf
    nb = shift.shape[0]
    bpb = (t // tm) // nb
    mod_spec = pl.BlockSpec((1, 1, d), lambda i, e, f: (i // bpb, 0, 0))
    vec_spec = pl.BlockSpec((1, d), lambda i, e, f: (0, 0))
    in_specs = [pl.BlockSpec((tm, d), lambda i, e, f: (i, 0)), vec_spec, mod_spec, mod_spec, mod_spec, vec_spec]
    args = [x, gain.reshape(1, d), shift, scale, g2, ng.reshape(1, d)]
    if n_e > 1:
        in_specs.append(pl.BlockSpec((tm, LANES), lambda i, e, f: (i, 0)))
        args.append(combine)
    in_specs += [
        pl.BlockSpec((1, d, tf), lambda i, e, f: (e, 0, f)),
        pl.BlockSpec((1, d, tf), lambda i, e, f: (e, 0, nf + f)),
        pl.BlockSpec((1, tf, d), lambda i, e, f: (e, f, 0)),
    ]
    args += [w_gu, w_gu, w_down]
    return pl.pallas_call(
        functools.partial(_ffn_kernel, n_e=n_e),
        grid=(t // tm, n_e, nf),
        in_specs=in_specs,
        out_specs=pl.BlockSpec((tm, d), lambda i, e, f: (i, 0)),
        out_shape=jax.ShapeDtypeStruct((t, d), F32),
        scratch_shapes=[pltpu.VMEM((tm, d), BF16), pltpu.VMEM((tm, d), F32)],
        compiler_params=_params("parallel", "arbitrary", "arbitrary"),
        name="ffn_residual" if n_e == 1 else "moe_dense_residual",
    )(*args)


def _router_kernel(x_ref, g_ref, sh_ref, sc_ref, w_ref, b_ref, o_ref):
    h = _rms(x_ref[...]) * g_ref[...]
    h = h * (1.0 + sc_ref[0]) + sh_ref[0]
    logits = jnp.dot(h, w_ref[...], preferred_element_type=F32, precision=lax.Precision.HIGHEST) + b_ref[...]
    lane = lax.broadcasted_iota(jnp.int32, logits.shape, 1).astype(F32)
    neg = jnp.float32(-jnp.inf)
    logits = jnp.where(lane < N_EXPERTS, logits, neg)
    m1 = jnp.max(logits, axis=-1, keepdims=True)
    i1 = jnp.min(jnp.where(logits == m1, lane, float(LANES)), axis=-1, keepdims=True)
    rest = jnp.where(lane == i1, neg, logits)
    m2 = jnp.max(rest, axis=-1, keepdims=True)
    i2 = jnp.min(jnp.where(rest == m2, lane, float(LANES)), axis=-1, keepdims=True)
    e2 = jnp.exp(m2 - m1)
    denom = 1.0 + e2
    o_ref[...] = jnp.where(lane == i1, 1.0 / denom, 0.0) + jnp.where(lane == i2, e2 / denom, 0.0)


def router_combine(x, gain, shift, scale, router_w, router_b, *, tm):
    t, d = x.shape
    nb = shift.shape[0]
    bpb = (t // tm) // nb
    w = jnp.zeros((d, LANES), F32).at[:, :N_EXPERTS].set(router_w)
    b = jnp.zeros((1, LANES), F32).at[0, :N_EXPERTS].set(router_b)
    return pl.pallas_call(
        _router_kernel,
        grid=(t // tm,),
        in_specs=[
            pl.BlockSpec((tm, d), lambda i: (i, 0)),
            pl.BlockSpec((1, d), lambda i: (0, 0)),
            pl.BlockSpec((1, 1, d), lambda i: (i // bpb, 0, 0)),
            pl.BlockSpec((1, 1, d), lambda i: (i // bpb, 0, 0)),
            pl.BlockSpec((d, LANES), lambda i: (0, 0)),
            pl.BlockSpec((1, LANES), lambda i: (0, 0)),
        ],
        out_specs=pl.BlockSpec((tm, LANES), lambda i: (i, 0)),
        out_shape=jax.ShapeDtypeStruct((t, LANES), F32),
        compiler_params=_params("parallel"),
        name="router_combine",
    )(x, gain.reshape(1, d), shift, scale, w, b)


def _rope_tables():
    pos = jnp.arange(SEQ)
    inv_freq = jnp.power(ROPE_THETA, -jnp.arange(ROPE_FREQS, dtype=F32) / ROPE_FREQS)
    ang_r = (pos // GRID_W).astype(F32)[:, None] * inv_freq
    ang_c = (pos % GRID_W).astype(F32)[:, None] * inv_freq
    cos = jnp.concatenate([jnp.cos(ang_r)] * 2 + [jnp.cos(ang_c)] * 2, axis=-1)
    sin = jnp.concatenate([-jnp.sin(ang_r), jnp.sin(ang_r), -jnp.sin(ang_c), jnp.sin(ang_c)], axis=-1)
    return jnp.tile(cos, (1, 2)), jnp.tile(sin, (1, 2))


def _split_w_in(w):
    xr, k, v, gr, q, g = jnp.split(w, (D_RNN, D_RNN + QK_W, D_RNN + QK_W + ATTN_W, 2 * D_RNN + QK_W + ATTN_W,
                                       2 * D_RNN + 2 * QK_W + ATTN_W), axis=-1)
    return xr, k, v, gr, q, g


def _gate_layout(gate_w, gate_b, lam):
    c = RNN_CB
    n_cb = D_RNN // c
    per = c // RNN_BLOCK_W
    gw = gate_w.reshape(2, 2, n_cb, per, RNN_BLOCK_W, RNN_BLOCK_W)
    eye = jnp.eye(per, dtype=gate_w.dtype)
    bd = gw[:, :, :, :, :, None, :] * eye[None, None, None, :, None, :, None]
    bd = bd.reshape(2, 2, n_cb, c, c)
    wg = jnp.transpose(bd, (2, 3, 0, 1, 4)).reshape(n_cb, c, 4 * c).astype(BF16)
    gb = jnp.transpose(gate_b.reshape(2, 2, n_cb, c), (2, 0, 1, 3)).reshape(n_cb, 1, 4 * c)
    lm = jnp.transpose(lam.reshape(2, n_cb, c), (1, 0, 2)).reshape(n_cb, 1, 2 * c)
    return wg, gb, lm


def kernel(x, c, ctx, c_ctx, ada_w, ada_b, norm_g, w_in, conv_w, conv_b, lru_gate_w, lru_gate_b, lru_lambda,
           diff_lambda, subln_g, w_proj_rnn, w_proj_attn, w_out, ffn_w_gu, ffn_w_down, router_w, router_b,
           moe_w_gu, moe_w_down):
    xt = x.reshape(BATCH * SEQ, D_MODEL)
    ct = ctx.reshape(BATCH * CTX_LEN, D_MODEL)
    cos, sin_signed = _rope_tables()

    cvec = jnp.concatenate([c, c_ctx[None, :], jnp.zeros((2 * SUBLANES - BATCH - 1, D_MODEL), F32)], axis=0)
    mod = ada_modulation(cvec, ada_w, ada_b)

    for l in range(DEPTH):
        last = l == DEPTH - 1
        lam_init = 0.8 - 0.6 * math.exp(-0.3 * l)
        mx = mod[l, :BATCH].reshape(BATCH, 1, 6, D_MODEL)
        mc = mod[l, BATCH:BATCH + 1].reshape(1, 1, 6, D_MODEL)
        sh1x, sc1x, g1x, sh2x, sc2x, g2x = (mx[:, :, i] for i in range(6))
        sh1c, sc1c, g1c, sh2c, sc2c, g2c = (mc[:, :, i] for i in range(6))

        xr_w, k_w, v_w, gr_w, q_w, g_w = _split_w_in(w_in[l])
        w_full = jnp.concatenate([q_w, k_w, v_w, g_w, xr_w, gr_w], axis=-1).astype(BF16)
        wg, gb, lm = _gate_layout(lru_gate_w[l], lru_gate_b[l], lru_lambda[l])

        proj_x = norm_mod_matmul(xt, norm_g[l, 0], sh1x, sc1x, w_full, tm=1024, tn=1536)
        if last:
            w_ctx = jnp.concatenate([k_w, v_w, xr_w], axis=-1).astype(BF16)
            proj_c = norm_mod_matmul(ct, norm_g[l, 0], sh1c, sc1c, w_ctx, tm=1024, tn=CTX_STATE_W // 2)
            kc_col, vc_col, cxr_col, cgr_col = CCOL_K, CCOL_V, CCOL_XR, 0
        else:
            proj_c = norm_mod_matmul(ct, norm_g[l, 0], sh1c, sc1c, w_full, tm=1024, tn=1536)
            kc_col, vc_col, cxr_col, cgr_col = COL_K, COL_V, COL_XR, COL_GR

        y_rnn_x, y_rnn_c = rglru(proj_x, proj_c, conv_w[l], conv_b[l], wg, gb, lm, xr_col=COL_XR, gr_col=COL_GR,
                                 c_xr_col=cxr_col, c_gr_col=cgr_col, ctx_out=not last)
        y_attn_x = diff_attention(proj_x, COL_Q, SEQ, proj_x, proj_c, (COL_K, COL_V, kc_col, vc_col), cos,
                                  sin_signed, diff_lambda[l], subln_g[l], lam_init=lam_init, tq=256)
        wr = w_proj_rnn[l].astype(BF16)
        wa = w_proj_attn[l].astype(BF16)
        wo = w_out[l].astype(BF16)
        xt = merge_out(y_rnn_x, y_attn_x, proj_x, xt, g1x, norm_g[l, 1], wr, wa, wo, tm=512)
        if not last:
            y_attn_c = diff_attention(proj_c, COL_Q, CTX_LEN, None, proj_c, (0, 0, kc_col, vc_col), None, None,
                                      diff_lambda[l], subln_g[l], lam_init=lam_init, tq=CTX_LEN)
            ct = merge_out(y_rnn_c, y_attn_c, proj_c, ct, g1c, norm_g[l, 1], wr, wa, wo, tm=512)

        if l % 2 == 0:
            w_gu = ffn_w_gu[l // 2].astype(BF16)[None]
            w_dn = ffn_w_down[l // 2].astype(BF16)[None]
            xt = ffn_residual(xt, norm_g[l, 2], sh2x, sc2x, g2x, norm_g[l, 3], w_gu, w_dn, None, tm=512, tf=1408)
            if not last:
                ct = ffn_residual(ct, norm_g[l, 2], sh2c, sc2c, g2c, norm_g[l, 3], w_gu, w_dn, None, tm=512,
                                  tf=1408)
        else:
            w_gu = moe_w_gu[l // 2].astype(BF16)
            w_dn = moe_w_down[l // 2].astype(BF16)
            cmb_x = router_combine(xt, norm_g[l, 2], sh2x, sc2x, router_w[l // 2], router_b[l // 2], tm=512)
            xt = ffn_residual(xt, norm_g[l, 2], sh2x, sc2x, g2x, norm_g[l, 3], w_gu, w_dn, cmb_x, tm=512, tf=1408)
            if not last:
                cmb_c = router_combine(ct, norm_g[l, 2], sh2c, sc2c, router_w[l // 2], router_b[l // 2], tm=512)
                ct = ffn_residual(ct, norm_g[l, 2], sh2c, sc2c, g2c, norm_g[l, 3], w_gu, w_dn, cmb_c, tm=512,
                                  tf=1408)
    return xt.reshape(BATCH, SEQ, D_MODEL)
```

```python
import functools
import math

import jax
import jax.numpy as jnp
from jax import lax
from jax.experimental import pallas as pl
from jax.experimental.pallas import tpu as pltpu

F32 = jnp.float32
BF16 = jnp.bfloat16

D_MODEL = 1024
BATCH = 8
SEQ = 2048
DEPTH = 2
CTX_LEN = 256
GRID_W = 64
EPS = 1e-6
D_RNN = 1280
RNN_BLOCKS = 20
RNN_BLOCK_W = D_RNN // RNN_BLOCKS
CONV_W = 4
LRU_C = 8.0
N_HEADS = 8
HEAD_DIM = 64
V_DIM = 2 * HEAD_DIM
QK_W = N_HEADS * 2 * HEAD_DIM
ATTN_W = N_HEADS * V_DIM
ROPE_THETA = 10000.0
ROPE_FREQS = HEAD_DIM // 4
D_FF = 2816
N_EXPERTS = 8
TOP_K = 2

LANES = 128
SUBLANES = 8
VMEM_LIMIT_BYTES = 52 * 1024 * 1024

COL_Q = 0
COL_K = COL_Q + QK_W
COL_V = COL_K + QK_W
COL_G_RNN = COL_V + ATTN_W
COL_G_ATTN = COL_G_RNN + D_MODEL
COL_XR = COL_G_ATTN + D_MODEL
COL_GR = COL_XR + D_RNN
IN_W = COL_GR + D_RNN
CCOL_K = 0
CCOL_V = CCOL_K + QK_W
CCOL_XR = CCOL_V + ATTN_W
CTX_STATE_W = CCOL_XR + D_RNN

RNN_CB = 256
SCAN_CHUNK = 256


def _params(*sem):
    return pltpu.CompilerParams(dimension_semantics=sem, vmem_limit_bytes=VMEM_LIMIT_BYTES)


def _rms(x):
    return x * lax.rsqrt(jnp.mean(x * x, axis=-1, keepdims=True) + EPS)


def _ada_kernel(c_ref, w_ref, b_ref, o_ref):
    c = c_ref[...]
    s = c * jax.nn.sigmoid(c)
    o_ref[0] = jnp.dot(s, w_ref[0], preferred_element_type=F32, precision=lax.Precision.HIGHEST) + b_ref[0]


def ada_modulation(cvec, ada_w, ada_b):
    rows = cvec.shape[0]
    tn = 1536
    n = 6 * D_MODEL
    return pl.pallas_call(
        _ada_kernel,
        grid=(DEPTH, n // tn),
        in_specs=[
            pl.BlockSpec((rows, D_MODEL), lambda l, j: (0, 0)),
            pl.BlockSpec((1, D_MODEL, tn), lambda l, j: (l, 0, j)),
            pl.BlockSpec((1, 1, tn), lambda l, j: (l, 0, j)),
        ],
        out_specs=pl.BlockSpec((1, rows, tn), lambda l, j: (l, 0, j)),
        out_shape=jax.ShapeDtypeStruct((DEPTH, rows, n), F32),
        compiler_params=_params("parallel", "parallel"),
        name="ada_modulation",
    )(cvec, ada_w, ada_b.reshape(DEPTH, 1, n))


def _nmm_kernel(x_ref, g_ref, sh_ref, sc_ref, w_ref, o_ref, h_ref):
    @pl.when(pl.program_id(1) == 0)
    def _():
        h = _rms(x_ref[...]) * g_ref[...]
        h = h * (1.0 + sc_ref[0]) + sh_ref[0]
        h_ref[...] = h.astype(BF16)

    o_ref[...] = jnp.dot(h_ref[...], w_ref[...], preferred_element_type=F32).astype(o_ref.dtype)


def norm_mod_matmul(x, gain, shift, scale, w, *, tm, tn):
    t, d = x.shape
    n = w.shape[1]
    nb = shift.shape[0]
    bpb = (t // tm) // nb
    return pl.pallas_call(
        _nmm_kernel,
        grid=(t // tm, n // tn),
        in_specs=[
            pl.BlockSpec((tm, d), lambda i, j: (i, 0)),
            pl.BlockSpec((1, d), lambda i, j: (0, 0)),
            pl.BlockSpec((1, 1, d), lambda i, j: (i // bpb, 0, 0)),
            pl.BlockSpec((1, 1, d), lambda i, j: (i // bpb, 0, 0)),
            pl.BlockSpec((d, tn), lambda i, j: (0, j)),
        ],
        out_specs=pl.BlockSpec((tm, tn), lambda i, j: (i, j)),
        out_shape=jax.ShapeDtypeStruct((t, n), BF16),
        scratch_shapes=[pltpu.VMEM((tm, d), BF16)],
        compiler_params=_params("parallel", "arbitrary"),
        name="norm_mod_matmul",
    )(x, gain.reshape(1, d), shift, scale, w)


def _group_scan(a, b, reverse):
    rows, c = a.shape
    a = a.reshape(rows // SUBLANES, SUBLANES, c)
    b = b.reshape(rows // SUBLANES, SUBLANES, c)
    sub = lax.broadcasted_iota(jnp.int32, a.shape, 1)
    for sh in (1, 2, 4):
        if reverse:
            keep = sub < SUBLANES - sh
            amount = SUBLANES - sh
        else:
            keep = sub >= sh
            amount = sh
        a_sh = jnp.where(keep, pltpu.roll(a, amount, 1), 1.0)
        b_sh = jnp.where(keep, pltpu.roll(b, amount, 1), 0.0)
        b = a * b_sh + b
        a = a * a_sh
    return a.reshape(rows, c), b.reshape(rows, c)


def _dwconv(x, cw, cb):
    n = x.shape[0]
    rows = lax.broadcasted_iota(jnp.int32, x.shape, 0)
    xm2 = jnp.where(rows >= 2, pltpu.roll(x, 2, 0), 0.0)
    xm1 = jnp.where(rows >= 1, pltpu.roll(x, 1, 0), 0.0)
    xp1 = jnp.where(rows < n - 1, pltpu.roll(x, n - 1, 0), 0.0)
    return cb + xm2 * cw[0:1] + xm1 * cw[1:2] + x * cw[2:3] + xp1 * cw[3:4]


def _gelu_tanh(x):
    return 0.5 * x * (1.0 + jnp.tanh(math.sqrt(2.0 / math.pi) * (x + 0.044715 * (x * x * x))))


def _rglru_kernel(*refs, n_c, n_x, ctx_out):
    if ctx_out:
        (xrx_ref, grx_ref, xrc_ref, grc_ref, cw_ref, cb_ref, wg_ref, gb_ref, lam_ref,
         yx_ref, yc_ref, xc_s, a0_s, b0_s, a1_s, b1_s) = refs
    else:
        (xrx_ref, grx_ref, xrc_ref, cw_ref, cb_ref, wg_ref, gb_ref, lam_ref,
         yx_ref, xc_s, a0_s, b0_s, a1_s, b1_s) = refs
    c = RNN_CB
    n_all = n_c + n_x
    cw = cw_ref[...]
    cb = cb_ref[...]

    xc_s[0:n_c, :] = _dwconv(xrc_ref[...].astype(F32), cw, cb)
    xc_s[n_c:n_all, :] = _dwconv(xrx_ref[...].astype(F32), cw, cb)

    lam = lam_ref[0]
    neg_sp = -LRU_C * jax.nn.softplus(-lam)

    def coeff_chunk(ci, carry):
        r0 = pl.multiple_of(ci * SCAN_CHUNK, SCAN_CHUNK)
        xc = xc_s[pl.ds(r0, SCAN_CHUNK), :]
        gates = jnp.dot(xc.astype(BF16), wg_ref[0], preferred_element_type=F32) + gb_ref[0]
        gates = jax.nn.sigmoid(gates)
        for d, (a_s, b_s) in enumerate(((a0_s, b0_s), (a1_s, b1_s))):
            r = gates[:, (2 * d) * c:(2 * d + 1) * c]
            i = gates[:, (2 * d + 1) * c:(2 * d + 2) * c]
            log_a = r * neg_sp[:, d * c:(d + 1) * c]
            a = jnp.exp(log_a)
            mult = jnp.sqrt(1.0 - jnp.exp(2.0 * log_a))
            bb = mult * (i * xc)
            a_cum, b_cum = _group_scan(a, bb, reverse=(d == 1))
            a_s[pl.ds(r0, SCAN_CHUNK), :] = a_cum
            b_s[pl.ds(r0, SCAN_CHUNK), :] = b_cum
        return carry

    lax.fori_loop(0, n_all // SCAN_CHUNK, coeff_chunk, 0)

    g_c = n_c // SUBLANES
    g_all = n_all // SUBLANES

    def carry_step(k, carry):
        hf, hr = carry
        rf = pl.multiple_of(k * SUBLANES, SUBLANES)
        h = a0_s[pl.ds(rf, SUBLANES), :] * hf + b0_s[pl.ds(rf, SUBLANES), :]
        a0_s[pl.ds(rf, SUBLANES), :] = h
        hf = h[SUBLANES - 1:SUBLANES, :]
        kr = jnp.where(k < g_c, g_c - 1 - k, g_all + g_c - 1 - k)
        rr = pl.multiple_of(kr * SUBLANES, SUBLANES)
        h = a1_s[pl.ds(rr, SUBLANES), :] * hr + b1_s[pl.ds(rr, SUBLANES), :]
        a1_s[pl.ds(rr, SUBLANES), :] = h
        hr = h[0:1, :]
        return hf, hr

    zero = jnp.zeros((1, c), F32)
    lax.fori_loop(0, g_all, carry_step, (zero, zero), unroll=4)

    yx = (a0_s[n_c:n_all, :] + a1_s[n_c:n_all, :]) * _gelu_tanh(grx_ref[...].astype(F32))
    yx_ref[...] = yx.astype(yx_ref.dtype)
    if ctx_out:
        yc = (a0_s[0:n_c, :] + a1_s[0:n_c, :]) * _gelu_tanh(grc_ref[...].astype(F32))
        yc_ref[...] = yc.astype(yc_ref.dtype)


def rglru(proj_x, proj_c, conv_w, conv_b, wg, gb, lam, *, xr_col, gr_col, c_xr_col, c_gr_col, ctx_out):
    c = RNN_CB
    n_cb = D_RNN // c
    xr_b, gr_b, cxr_b = xr_col // c, gr_col // c, c_xr_col // c
    in_specs = [
        pl.BlockSpec((SEQ, c), lambda b, j: (b, xr_b + j)),
        pl.BlockSpec((SEQ, c), lambda b, j: (b, gr_b + j)),
        pl.BlockSpec((CTX_LEN, c), lambda b, j: (b, cxr_b + j)),
    ]
    args = [proj_x, proj_x, proj_c]
    if ctx_out:
        cgr_b = c_gr_col // c
        in_specs.append(pl.BlockSpec((CTX_LEN, c), lambda b, j: (b, cgr_b + j)))
        args.append(proj_c)
    in_specs += [
        pl.BlockSpec((CONV_W, c), lambda b, j: (0, j)),
        pl.BlockSpec((1, c), lambda b, j: (0, j)),
        pl.BlockSpec((1, c, 4 * c), lambda b, j: (j, 0, 0)),
        pl.BlockSpec((1, 1, 4 * c), lambda b, j: (j, 0, 0)),
        pl.BlockSpec((1, 1, 2 * c), lambda b, j: (j, 0, 0)),
    ]
    args += [conv_w, conv_b.reshape(1, D_RNN), wg, gb, lam]
    out_specs = [pl.BlockSpec((SEQ, c), lambda b, j: (b, j))]
    out_shape = [jax.ShapeDtypeStruct((BATCH * SEQ, D_RNN), BF16)]
    if ctx_out:
        out_specs.append(pl.BlockSpec((CTX_LEN, c), lambda b, j: (b, j)))
        out_shape.append(jax.ShapeDtypeStruct((BATCH * CTX_LEN, D_RNN), BF16))
    n_all = SEQ + CTX_LEN
    out = pl.pallas_call(
        functools.partial(_rglru_kernel, n_c=CTX_LEN, n_x=SEQ, ctx_out=ctx_out),
        grid=(BATCH, n_cb),
        in_specs=in_specs,
        out_specs=out_specs,
        out_shape=out_shape,
        scratch_shapes=[pltpu.VMEM((n_all, c), F32) for _ in range(5)],
        compiler_params=_params("parallel", "parallel"),
        name="rglru",
    )(*args)
    return out if ctx_out else (out[0], None)


def _rope(t, cos, sin_signed):
    lane = lax.broadcasted_iota(jnp.int32, t.shape, 1)
    first_half = (lane % (2 * ROPE_FREQS)) < ROPE_FREQS
    partner = jnp.where(first_half, pltpu.roll(t, LANES - ROPE_FREQS, 1), pltpu.roll(t, ROPE_FREQS, 1))
    return t * cos + partner * sin_signed


def _attn_kernel(*refs, tq, n_c, n_x, lam_init):
    if n_x:
        (q_ref, kx_ref, vx_ref, kc_ref, vc_ref, cos_ref, sin_ref, dl_ref, sg_ref,
         o_ref, k_s, v_s, lam_s) = refs
    else:
        q_ref, kc_ref, vc_ref, dl_ref, sg_ref, o_ref, k_s, v_s, lam_s = refs
    qi = pl.program_id(2)

    @pl.when(qi == 0)
    def _():
        k_s[0:n_c, :] = kc_ref[...]
        v_s[0:n_c, :] = vc_ref[...]
        if n_x:
            kx = _rope(kx_ref[...].astype(F32), cos_ref[...], sin_ref[...])
            k_s[n_c:n_c + n_x, :] = kx.astype(BF16)
            v_s[n_c:n_c + n_x, :] = vx_ref[...]
        dl = dl_ref[...]
        e1 = jnp.exp(jnp.sum(dl[0:1] * dl[1:2], axis=-1, keepdims=True))
        e2 = jnp.exp(jnp.sum(dl[2:3] * dl[3:4], axis=-1, keepdims=True))
        lam_s[...] = jnp.broadcast_to(e1 - e2 + lam_init, lam_s.shape)

    q = q_ref[...].astype(F32)
    if n_x:
        r0 = pl.multiple_of(qi * tq, tq)
        q = _rope(q, cos_ref[pl.ds(r0, tq), :], sin_ref[pl.ds(r0, tq), :])
    q = q * (HEAD_DIM ** -0.5)
    lane = lax.broadcasted_iota(jnp.int32, q.shape, 1)
    qq = jnp.concatenate([jnp.where(lane < HEAD_DIM, q, 0.0), jnp.where(lane >= HEAD_DIM, q, 0.0)], axis=0)
    s = lax.dot_general(qq.astype(BF16), k_s[...], (((1,), (1,)), ((), ())), preferred_element_type=F32)
    m = jnp.max(s, axis=-1, keepdims=True)
    e = jnp.exp(s - m)
    p = e / jnp.sum(e, axis=-1, keepdims=True)
    w = p[0:tq] - lam_s[0:1, 0:1] * p[tq:2 * tq]
    o = jnp.dot(w.astype(BF16), v_s[...], preferred_element_type=F32)
    o_ref[...] = (_rms(o) * sg_ref[...] * (1.0 - lam_init)).astype(o_ref.dtype)


def diff_attention(q_arr, q_col, n_q, kvx, kvc, cols, cos, sin_signed, diff_lambda, subln_g, *, lam_init, tq):
    kx_col, vx_col, kc_col, vc_col = cols
    qb, kxb, vxb, kcb, vcb = (v // LANES for v in (q_col, kx_col, vx_col, kc_col, vc_col))
    n_x = SEQ if kvx is not None else 0
    n_qb = n_q // tq
    in_specs = [pl.BlockSpec((tq, LANES), lambda b, h, i: (b * n_qb + i, qb + h))]
    args = [q_arr]
    if n_x:
        in_specs += [
            pl.BlockSpec((SEQ, LANES), lambda b, h, i: (b, kxb + h)),
            pl.BlockSpec((SEQ, LANES), lambda b, h, i: (b, vxb + h)),
        ]
        args += [kvx, kvx]
    in_specs += [
        pl.BlockSpec((CTX_LEN, LANES), lambda b, h, i: (b, kcb + h)),
        pl.BlockSpec((CTX_LEN, LANES), lambda b, h, i: (b, vcb + h)),
    ]
    args += [kvc, kvc]
    if n_x:
        in_specs += [
            pl.BlockSpec((SEQ, LANES), lambda b, h, i: (0, 0)),
            pl.BlockSpec((SEQ, LANES), lambda b, h, i: (0, 0)),
        ]
        args += [cos, sin_signed]
    in_specs += [
        pl.BlockSpec((4, HEAD_DIM), lambda b, h, i: (0, 0)),
        pl.BlockSpec((1, V_DIM), lambda b, h, i: (0, 0)),
    ]
    args += [diff_lambda, subln_g.reshape(1, V_DIM)]
    n_kv = CTX_LEN + n_x
    return pl.pallas_call(
        functools.partial(_attn_kernel, tq=tq, n_c=CTX_LEN, n_x=n_x, lam_init=lam_init),
        grid=(BATCH, N_HEADS, n_qb),
        in_specs=in_specs,
        out_specs=pl.BlockSpec((tq, LANES), lambda b, h, i: (b * n_qb + i, h)),
        out_shape=jax.ShapeDtypeStruct((BATCH * n_q, ATTN_W), BF16),
        scratch_shapes=[
            pltpu.VMEM((n_kv, LANES), BF16),
            pltpu.VMEM((n_kv, LANES), BF16),
            pltpu.VMEM((SUBLANES, LANES), F32),
        ],
        compiler_params=_params("parallel", "parallel", "arbitrary"),
        name="diff_attention",
    )(*args)


def _merge_kernel(yr_ref, ya_ref, gr_ref, ga_ref, x_ref, g1_ref, ng_ref, wr_ref, wa_ref, wo_ref, o_ref):
    y_rnn = jnp.dot(yr_ref[...], wr_ref[...], preferred_element_type=F32)
    y_attn = jnp.dot(ya_ref[...], wa_ref[...], preferred_element_type=F32)
    g_rnn = jax.nn.sigmoid(gr_ref[...].astype(F32))
    g_attn = jax.nn.sigmoid(ga_ref[...].astype(F32))
    m = (g_rnn * y_rnn + g_attn * y_attn).astype(BF16)
    mx = jnp.dot(m, wo_ref[...], preferred_element_type=F32)
    o_ref[...] = x_ref[...] + g1_ref[0] * (_rms(mx) * ng_ref[...])


def merge_out(y_rnn, y_attn, proj, x, g1, ng, w_proj_rnn, w_proj_attn, w_out, *, tm):
    t, d = x.shape
    nb = g1.shape[0]
    bpb = (t // tm) // nb
    grb, gab = COL_G_RNN // d, COL_G_ATTN // d
    return pl.pallas_call(
        _merge_kernel,
        grid=(t // tm,),
        in_specs=[
            pl.BlockSpec((tm, D_RNN), lambda i: (i, 0)),
            pl.BlockSpec((tm, ATTN_W), lambda i: (i, 0)),
            pl.BlockSpec((tm, d), lambda i: (i, grb)),
            pl.BlockSpec((tm, d), lambda i: (i, gab)),
            pl.BlockSpec((tm, d), lambda i: (i, 0)),
            pl.BlockSpec((1, 1, d), lambda i: (i // bpb, 0, 0)),
            pl.BlockSpec((1, d), lambda i: (0, 0)),
            pl.BlockSpec((D_RNN, d), lambda i: (0, 0)),
            pl.BlockSpec((ATTN_W, d), lambda i: (0, 0)),
            pl.BlockSpec((d, d), lambda i: (0, 0)),
        ],
        out_specs=pl.BlockSpec((tm, d), lambda i: (i, 0)),
        out_shape=jax.ShapeDtypeStruct((t, d), F32),
        compiler_params=_params("parallel"),
        name="merge_out",
    )(y_rnn, y_attn, proj, proj, x, g1, ng.reshape(1, d), w_proj_rnn, w_proj_attn, w_out)


def _swiglu_partial(h, wg, wu, wd):
    gate = jnp.dot(h, wg, preferred_element_type=F32)
    up = jnp.dot(h, wu, preferred_element_type=F32)
    act = (gate * jax.nn.sigmoid(gate) * up).astype(BF16)
    return jnp.dot(act, wd, preferred_element_type=F32)


def _ffn_kernel(x_ref, g_ref, sh_ref, sc_ref, g2_ref, ng_ref, wg_ref, wu_ref, wd_ref, o_ref, h_ref, acc_ref):
    f = pl.program_id(1)

    @pl.when(f == 0)
    def _():
        h = _rms(x_ref[...]) * g_ref[...]
        h = h * (1.0 + sc_ref[0]) + sh_ref[0]
        h_ref[...] = h.astype(BF16)
        acc_ref[...] = jnp.zeros_like(acc_ref)

    acc_ref[...] += _swiglu_partial(h_ref[...], wg_ref[...], wu_ref[...], wd_ref[...])

    @pl.when(f == pl.num_programs(1) - 1)
    def _():
        o_ref[...] = x_ref[...] + g2_ref[0] * (_rms(acc_ref[...]) * ng_ref[...])


def ffn_residual(x, gain, shift, scale, g2, ng, w_gu, w_down, *, tm, tf):
    t, d = x.shape
    ff = w_gu.shape[1] // 2
    nf = ff // tf
    nb = shift.shape[0]
    bpb = (t // tm) // nb
    mod_spec = pl.BlockSpec((1, 1, d), lambda i, f: (i // bpb, 0, 0))
    vec_spec = pl.BlockSpec((1, d), lambda i, f: (0, 0))
    return pl.pallas_call(
        _ffn_kernel,
        grid=(t // tm, nf),
        in_specs=[
            pl.BlockSpec((tm, d), lambda i, f: (i, 0)), vec_spec, mod_spec, mod_spec, mod_spec, vec_spec,
            pl.BlockSpec((d, tf), lambda i, f: (0, f)),
            pl.BlockSpec((d, tf), lambda i, f: (0, nf + f)),
            pl.BlockSpec((tf, d), lambda i, f: (f, 0)),
        ],
        out_specs=pl.BlockSpec((tm, d), lambda i, f: (i, 0)),
        out_shape=jax.ShapeDtypeStruct((t, d), F32),
        scratch_shapes=[pltpu.VMEM((tm, d), BF16), pltpu.VMEM((tm, d), F32)],
        compiler_params=_params("parallel", "arbitrary"),
        name="ffn_residual",
    )(x, gain.reshape(1, d), shift, scale, g2, ng.reshape(1, d), w_gu, w_gu, w_down)


MOE_TM = 512
MOE_TF = 1408
ROW_TILE = 256
META_E0, META_E1, META_W0, META_W1, META_R0, META_R1 = range(6)


def _lane_pick(rec, lane, k):
    return jnp.sum(jnp.where(lane == k, rec, 0.0), axis=-1, keepdims=True)


def _router_kernel(x_ref, g_ref, sh_ref, sc_ref, w_ref, b_ref, h_ref, meta_ref, cnt_ref, carry_s):
    @pl.when(pl.program_id(0) == 0)
    def _():
        carry_s[...] = jnp.zeros_like(carry_s)

    h = _rms(x_ref[...]) * g_ref[...]
    h = h * (1.0 + sc_ref[0]) + sh_ref[0]
    h_ref[...] = h
    logits = jnp.dot(h, w_ref[...], preferred_element_type=F32, precision=lax.Precision.HIGHEST) + b_ref[...]
    lane = lax.broadcasted_iota(jnp.int32, logits.shape, 1).astype(F32)
    neg = jnp.float32(-jnp.inf)
    logits = jnp.where(lane < N_EXPERTS, logits, neg)
    m1 = jnp.max(logits, axis=-1, keepdims=True)
    i1 = jnp.min(jnp.where(logits == m1, lane, float(LANES)), axis=-1, keepdims=True)
    rest = jnp.where(lane == i1, neg, logits)
    m2 = jnp.max(rest, axis=-1, keepdims=True)
    i2 = jnp.min(jnp.where(rest == m2, lane, float(LANES)), axis=-1, keepdims=True)
    e2 = jnp.exp(m2 - m1)
    denom = 1.0 + e2

    tm = logits.shape[0]
    sel1 = lane == i1
    sel2 = lane == i2
    member = jnp.where(jnp.logical_or(sel1, sel2), 1.0, 0.0)
    row = lax.broadcasted_iota(jnp.int32, (tm, tm), 0)
    col = lax.broadcasted_iota(jnp.int32, (tm, tm), 1)
    lower = jnp.where(col < row, 1.0, 0.0).astype(BF16)
    before = jnp.dot(lower, member.astype(BF16), preferred_element_type=F32) + carry_s[0:1, :]
    r1 = jnp.sum(jnp.where(sel1, before, 0.0), axis=-1, keepdims=True)
    r2 = jnp.sum(jnp.where(sel2, before, 0.0), axis=-1, keepdims=True)
    carry_s[0:1, :] = carry_s[0:1, :] + jnp.sum(member, axis=0, keepdims=True)
    cnt_ref[...] = jnp.broadcast_to(carry_s[0:1, :], cnt_ref.shape)

    meta = jnp.zeros_like(logits)
    for k, v in ((META_E0, i1), (META_E1, i2), (META_W0, 1.0 / denom), (META_W1, e2 / denom),
                 (META_R0, r1), (META_R1, r2)):
        meta = jnp.where(lane == k, v, meta)
    meta_ref[...] = meta


def moe_router(x, gain, shift, scale, router_w, router_b, *, tm):
    t, d = x.shape
    nb = shift.shape[0]
    bpb = (t // tm) // nb
    w = jnp.zeros((d, LANES), F32).at[:, :N_EXPERTS].set(router_w)
    b = jnp.zeros((1, LANES), F32).at[0, :N_EXPERTS].set(router_b)
    return pl.pallas_call(
        _router_kernel,
        grid=(t // tm,),
        in_specs=[
            pl.BlockSpec((tm, d), lambda i: (i, 0)),
            pl.BlockSpec((1, d), lambda i: (0, 0)),
            pl.BlockSpec((1, 1, d), lambda i: (i // bpb, 0, 0)),
            pl.BlockSpec((1, 1, d), lambda i: (i // bpb, 0, 0)),
            pl.BlockSpec((d, LANES), lambda i: (0, 0)),
            pl.BlockSpec((1, LANES), lambda i: (0, 0)),
        ],
        out_specs=[
            pl.BlockSpec((tm, d), lambda i: (i, 0)),
            pl.BlockSpec((tm, LANES), lambda i: (i, 0)),
            pl.BlockSpec((SUBLANES, LANES), lambda i: (0, 0)),
        ],
        out_shape=[
            jax.ShapeDtypeStruct((t, d), F32),
            jax.ShapeDtypeStruct((t, LANES), F32),
            jax.ShapeDtypeStruct((SUBLANES, LANES), F32),
        ],
        scratch_shapes=[pltpu.VMEM((SUBLANES, LANES), F32)],
        compiler_params=_params("arbitrary"),
        name="moe_router",
    )(x, gain.reshape(1, d), shift, scale, w, b)


def _row_copy(src_ref, src_row, dst_ref, dst_row, sem):
    return pltpu.make_async_copy(src_ref.at[pl.ds(src_row, 1), :], dst_ref.at[pl.ds(dst_row, 1), :], sem)


def _dispatch_kernel(pos_ref, h_ref, hs_in_ref, hs_ref, sem):
    del hs_in_ref
    n = h_ref.shape[0]

    def issue(r, carry):
        _row_copy(h_ref, r, hs_ref, pos_ref[0, 0, 2 * r], sem).start()
        _row_copy(h_ref, r, hs_ref, pos_ref[0, 0, 2 * r + 1], sem).start()
        return carry

    lax.fori_loop(0, n, issue, 0, unroll=8)

    def drain(r, carry):
        _row_copy(h_ref, r, hs_ref, 0, sem).wait()
        _row_copy(h_ref, r, hs_ref, 0, sem).wait()
        return carry

    lax.fori_loop(0, n, drain, 0, unroll=8)


def moe_dispatch(h, pos, n_rows):
    t, d = h.shape
    return pl.pallas_call(
        _dispatch_kernel,
        grid=(t // ROW_TILE,),
        in_specs=[
            pl.BlockSpec((1, 1, 2 * ROW_TILE), lambda i: (i, 0, 0), memory_space=pltpu.SMEM),
            pl.BlockSpec((ROW_TILE, d), lambda i: (i, 0)),
            pl.BlockSpec(memory_space=pl.ANY),
        ],
        out_specs=pl.BlockSpec(memory_space=pl.ANY),
        out_shape=jax.ShapeDtypeStruct((n_rows, d), F32),
        scratch_shapes=[pltpu.SemaphoreType.DMA(())],
        input_output_aliases={2: 0},
        compiler_params=_params("arbitrary"),
        name="moe_dispatch",
    )(pos, h, jnp.zeros((n_rows, d), F32))


def _grouped_ffn_kernel(te_ref, na_ref, hs_ref, wg_ref, wu_ref, wd_ref, ys_ref, h_s, acc_s):
    del te_ref
    f = pl.program_id(1)
    active = pl.program_id(0) < na_ref[0]

    @pl.when(jnp.logical_not(active))
    def _():
        ys_ref[...] = jnp.zeros_like(ys_ref)

    @pl.when(active)
    def _():
        @pl.when(f == 0)
        def _():
            h_s[...] = hs_ref[...].astype(BF16)
            acc_s[...] = jnp.zeros_like(acc_s)

        acc_s[...] += _swiglu_partial(h_s[...], wg_ref[0], wu_ref[0], wd_ref[0])

        @pl.when(f == pl.num_programs(1) - 1)
        def _():
            ys_ref[...] = acc_s[...]


def moe_grouped_ffn(hs, tile_expert, n_active, w_gu, w_down):
    n_rows, d = hs.shape
    ff = w_gu.shape[2] // 2
    nf = ff // MOE_TF
    n_tiles = n_rows // MOE_TM

    def row_map(t, f, te, na):
        return (jnp.minimum(t, na[0] - 1), 0)

    def f_eff(t, f, na):
        return jnp.where(t < na[0], f, nf - 1)

    return pl.pallas_call(
        _grouped_ffn_kernel,
        grid_spec=pltpu.PrefetchScalarGridSpec(
            num_scalar_prefetch=2,
            grid=(n_tiles, nf),
            in_specs=[
                pl.BlockSpec((MOE_TM, d), row_map),
                pl.BlockSpec((1, d, MOE_TF), lambda t, f, te, na: (te[t], 0, f_eff(t, f, na))),
                pl.BlockSpec((1, d, MOE_TF), lambda t, f, te, na: (te[t], 0, nf + f_eff(t, f, na))),
                pl.BlockSpec((1, MOE_TF, d), lambda t, f, te, na: (te[t], f_eff(t, f, na), 0)),
            ],
            out_specs=pl.BlockSpec((MOE_TM, d), lambda t, f, te, na: (t, 0)),
            scratch_shapes=[pltpu.VMEM((MOE_TM, d), BF16), pltpu.VMEM((MOE_TM, d), F32)],
        ),
        out_shape=jax.ShapeDtypeStruct((n_rows, d), F32),
        compiler_params=_params("arbitrary", "arbitrary"),
        name="moe_grouped_ffn",
    )(tile_expert, n_active, hs, w_gu, w_gu, w_down)


def _combine_kernel(pos_ref, ys_ref, x_ref, meta_ref, g2_ref, ng_ref, o_ref, a_s, b_s, sem):
    n = x_ref.shape[0]

    def issue(r, carry):
        _row_copy(ys_ref, pos_ref[0, 0, 2 * r], a_s, r, sem).start()
        _row_copy(ys_ref, pos_ref[0, 0, 2 * r + 1], b_s, r, sem).start()
        return carry

    lax.fori_loop(0, n, issue, 0, unroll=8)

    def drain(r, carry):
        _row_copy(ys_ref, 0, a_s, r, sem).wait()
        _row_copy(ys_ref, 0, b_s, r, sem).wait()
        return carry

    lax.fori_loop(0, n, drain, 0, unroll=8)

    meta = meta_ref[...]
    lane = lax.broadcasted_iota(jnp.int32, meta.shape, 1)
    mixed = _lane_pick(meta, lane, META_W0) * a_s[...] + _lane_pick(meta, lane, META_W1) * b_s[...]
    o_ref[...] = x_ref[...] + g2_ref[0] * (_rms(mixed) * ng_ref[...])


def moe_combine(ys, pos, meta, x, g2, ng):
    t, d = x.shape
    nb = g2.shape[0]
    bpb = (t // ROW_TILE) // nb
    return pl.pallas_call(
        _combine_kernel,
        grid=(t // ROW_TILE,),
        in_specs=[
            pl.BlockSpec((1, 1, 2 * ROW_TILE), lambda i: (i, 0, 0), memory_space=pltpu.SMEM),
            pl.BlockSpec(memory_space=pl.ANY),
            pl.BlockSpec((ROW_TILE, d), lambda i: (i, 0)),
            pl.BlockSpec((ROW_TILE, LANES), lambda i: (i, 0)),
            pl.BlockSpec((1, 1, d), lambda i: (i // bpb, 0, 0)),
            pl.BlockSpec((1, d), lambda i: (0, 0)),
        ],
        out_specs=pl.BlockSpec((ROW_TILE, d), lambda i: (i, 0)),
        out_shape=jax.ShapeDtypeStruct((t, d), F32),
        scratch_shapes=[pltpu.VMEM((ROW_TILE, d), F32), pltpu.VMEM((ROW_TILE, d), F32),
                        pltpu.SemaphoreType.DMA(())],
        compiler_params=_params("arbitrary"),
        name="moe_combine",
    )(pos, ys, x, meta, g2, ng.reshape(1, d))


def moe_residual(x, gain, shift, scale, g2, ng, router_w, router_b, w_gu, w_down):
    t, d = x.shape
    h, meta, counts = moe_router(x, gain, shift, scale, router_w, router_b, tm=512)
    n_rows = TOP_K * t + N_EXPERTS * MOE_TM
    n_tiles = n_rows // MOE_TM
    cnt = counts[0, :N_EXPERTS].astype(jnp.int32)
    padded = (cnt + MOE_TM - 1) // MOE_TM * MOE_TM
    seg_end = jnp.cumsum(padded)
    seg_start = seg_end - padded
    experts = jnp.arange(N_EXPERTS, dtype=jnp.int32)

    def position(e_lane, r_lane):
        e = meta[:, e_lane].astype(jnp.int32)
        start = jnp.sum(jnp.where(e[:, None] == experts[None, :], seg_start[None, :], 0), axis=-1)
        return start + meta[:, r_lane].astype(jnp.int32)

    pos = jnp.stack([position(META_E0, META_R0), position(META_E1, META_R1)], axis=-1)
    pos = pos.reshape(t // ROW_TILE, 1, 2 * ROW_TILE)
    n_active = seg_end[-1:] // MOE_TM
    tile_ids = jnp.arange(n_tiles, dtype=jnp.int32)
    tile_expert = jnp.sum((jnp.minimum(tile_ids, n_active - 1)[:, None] * MOE_TM >= seg_end[None, :]), axis=-1)
    tile_expert = jnp.minimum(tile_expert, N_EXPERTS - 1).astype(jnp.int32)

    hs = moe_dispatch(h, pos, n_rows)
    ys = moe_grouped_ffn(hs, tile_expert, n_active.astype(jnp.int32), w_gu, w_down)
    return moe_combine(ys, pos, meta, x, g2, ng)


def _rope_tables():
    pos = jnp.arange(SEQ)
    inv_freq = jnp.power(ROPE_THETA, -jnp.arange(ROPE_FREQS, dtype=F32) / ROPE_FREQS)
    ang_r = (pos // GRID_W).astype(F32)[:, None] * inv_freq
    ang_c = (pos % GRID_W).astype(F32)[:, None] * inv_freq
    cos = jnp.concatenate([jnp.cos(ang_r)] * 2 + [jnp.cos(ang_c)] * 2, axis=-1)
    sin = jnp.concatenate([-jnp.sin(ang_r), jnp.sin(ang_r), -jnp.sin(ang_c), jnp.sin(ang_c)], axis=-1)
    return jnp.tile(cos, (1, 2)), jnp.tile(sin, (1, 2))


def _split_w_in(w):
    xr, k, v, gr, q, g = jnp.split(w, (D_RNN, D_RNN + QK_W, D_RNN + QK_W + ATTN_W, 2 * D_RNN + QK_W + ATTN_W,
                                       2 * D_RNN + 2 * QK_W + ATTN_W), axis=-1)
    return xr, k, v, gr, q, g


def _gate_layout(gate_w, gate_b, lam):
    c = RNN_CB
    n_cb = D_RNN // c
    per = c // RNN_BLOCK_W
    gw = gate_w.reshape(2, 2, n_cb, per, RNN_BLOCK_W, RNN_BLOCK_W)
    eye = jnp.eye(per, dtype=gate_w.dtype)
    bd = gw[:, :, :, :, :, None, :] * eye[None, None, None, :, None, :, None]
    bd = bd.reshape(2, 2, n_cb, c, c)
    wg = jnp.transpose(bd, (2, 3, 0, 1, 4)).reshape(n_cb, c, 4 * c).astype(BF16)
    gb = jnp.transpose(gate_b.reshape(2, 2, n_cb, c), (2, 0, 1, 3)).reshape(n_cb, 1, 4 * c)
    lm = jnp.transpose(lam.reshape(2, n_cb, c), (1, 0, 2)).reshape(n_cb, 1, 2 * c)
    return wg, gb, lm


def kernel(x, c, ctx, c_ctx, ada_w, ada_b, norm_g, w_in, conv_w, conv_b, lru_gate_w, lru_gate_b, lru_lambda,
           diff_lambda, subln_g, w_proj_rnn, w_proj_attn, w_out, ffn_w_gu, ffn_w_down, router_w, router_b,
           moe_w_gu, moe_w_down):
    xt = x.reshape(BATCH * SEQ, D_MODEL)
    ct = ctx.reshape(BATCH * CTX_LEN, D_MODEL)
    cos, sin_signed = _rope_tables()

    cvec = jnp.concatenate([c, c_ctx[None, :], jnp.zeros((2 * SUBLANES - BATCH - 1, D_MODEL), F32)], axis=0)
    mod = ada_modulation(cvec, ada_w, ada_b)

    for l in range(DEPTH):
        last = l == DEPTH - 1
        lam_init = 0.8 - 0.6 * math.exp(-0.3 * l)
        mx = mod[l, :BATCH].reshape(BATCH, 1, 6, D_MODEL)
        mc = mod[l, BATCH:BATCH + 1].reshape(1, 1, 6, D_MODEL)
        sh1x, sc1x, g1x, sh2x, sc2x, g2x = (mx[:, :, i] for i in range(6))
        sh1c, sc1c, g1c, sh2c, sc2c, g2c = (mc[:, :, i] for i in range(6))

        xr_w, k_w, v_w, gr_w, q_w, g_w = _split_w_in(w_in[l])
        w_full = jnp.concatenate([q_w, k_w, v_w, g_w, xr_w, gr_w], axis=-1).astype(BF16)
        wg, gb, lm = _gate_layout(lru_gate_w[l], lru_gate_b[l], lru_lambda[l])

        proj_x = norm_mod_matmul(xt, norm_g[l, 0], sh1x, sc1x, w_full, tm=1024, tn=1536)
        if last:
            w_ctx = jnp.concatenate([k_w, v_w, xr_w], axis=-1).astype(BF16)
            proj_c = norm_mod_matmul(ct, norm_g[l, 0], sh1c, sc1c, w_ctx, tm=1024, tn=CTX_STATE_W // 2)
            kc_col, vc_col, cxr_col, cgr_col = CCOL_K, CCOL_V, CCOL_XR, 0
        else:
            proj_c = norm_mod_matmul(ct, norm_g[l, 0], sh1c, sc1c, w_full, tm=1024, tn=1536)
            kc_col, vc_col, cxr_col, cgr_col = COL_K, COL_V, COL_XR, COL_GR

        y_rnn_x, y_rnn_c = rglru(proj_x, proj_c, conv_w[l], conv_b[l], wg, gb, lm, xr_col=COL_XR, gr_col=COL_GR,
                                 c_xr_col=cxr_col, c_gr_col=cgr_col, ctx_out=not last)
        y_attn_x = diff_attention(proj_x, COL_Q, SEQ, proj_x, proj_c, (COL_K, COL_V, kc_col, vc_col), cos,
                                  sin_signed, diff_lambda[l], subln_g[l], lam_init=lam_init, tq=256)
        wr = w_proj_rnn[l].astype(BF16)
        wa = w_proj_attn[l].astype(BF16)
        wo = w_out[l].astype(BF16)
        xt = merge_out(y_rnn_x, y_attn_x, proj_x, xt, g1x, norm_g[l, 1], wr, wa, wo, tm=512)
        if not last:
            y_attn_c = diff_attention(proj_c, COL_Q, CTX_LEN, None, proj_c, (0, 0, kc_col, vc_col), None, None,
                                      diff_lambda[l], subln_g[l], lam_init=lam_init, tq=CTX_LEN)
            ct = merge_out(y_rnn_c, y_attn_c, proj_c, ct, g1c, norm_g[l, 1], wr, wa, wo, tm=512)

        if l % 2 == 0:
            w_gu = ffn_w_gu[l // 2].astype(BF16)
            w_dn = ffn_w_down[l // 2].astype(BF16)
            xt = ffn_residual(xt, norm_g[l, 2], sh2x, sc2x, g2x, norm_g[l, 3], w_gu, w_dn, tm=512, tf=1408)
            if not last:
                ct = ffn_residual(ct, norm_g[l, 2], sh2c, sc2c, g2c, norm_g[l, 3], w_gu, w_dn, tm=512, tf=1408)
        else:
            w_gu = moe_w_gu[l // 2].astype(BF16)
            w_dn = moe_w_down[l // 2].astype(BF16)
            rw, rb = router_w[l // 2], router_b[l // 2]
            xt = moe_residual(xt, norm_g[l, 2], sh2x, sc2x, g2x, norm_g[l, 3], rw, rb, w_gu, w_dn)
            if not last:
                ct = moe_residual(ct, norm_g[l, 2], sh2c, sc2c, g2c, norm_g[l, 3], rw, rb, w_gu, w_dn)
    return xt.reshape(BATCH, SEQ, D_MODEL)
```

```python
import functools
import math

import jax
import jax.numpy as jnp
from jax import lax
from jax.experimental import pallas as pl
from jax.experimental.pallas import tpu as pltpu

F32 = jnp.float32
BF16 = jnp.bfloat16

D_MODEL = 1024
BATCH = 8
SEQ = 2048
DEPTH = 2
CTX_LEN = 256
GRID_W = 64
EPS = 1e-6
D_RNN = 1280
RNN_BLOCKS = 20
RNN_BLOCK_W = D_RNN // RNN_BLOCKS
CONV_W = 4
LRU_C = 8.0
N_HEADS = 8
HEAD_DIM = 64
V_DIM = 2 * HEAD_DIM
QK_W = N_HEADS * 2 * HEAD_DIM
ATTN_W = N_HEADS * V_DIM
ROPE_THETA = 10000.0
ROPE_FREQS = HEAD_DIM // 4
D_FF = 2816
N_EXPERTS = 8
TOP_K = 2

LANES = 128
SUBLANES = 8
VMEM_LIMIT_BYTES = 52 * 1024 * 1024

COL_XR = 0
COL_K = COL_XR + D_RNN
COL_V = COL_K + QK_W
COL_GR = COL_V + ATTN_W
COL_Q = COL_GR + D_RNN
COL_G_RNN = COL_Q + QK_W
COL_G_ATTN = COL_G_RNN + D_MODEL
IN_W = COL_G_ATTN + D_MODEL
CTX_STATE_W = COL_GR
GATE_BLOCK_W = 512

RNN_CB = 256
SCAN_CHUNK = 256
ATTN_SUB = 128


def _params(*sem):
    return pltpu.CompilerParams(dimension_semantics=sem, vmem_limit_bytes=VMEM_LIMIT_BYTES)


def _rms(x):
    return x * lax.rsqrt(jnp.mean(x * x, axis=-1, keepdims=True) + EPS)


def _ada_kernel(c_ref, w_ref, b_ref, o_ref):
    c = c_ref[...]
    s = c * jax.nn.sigmoid(c)
    o_ref[0] = jnp.dot(s, w_ref[0], preferred_element_type=F32, precision=lax.Precision.HIGHEST) + b_ref[0]


def ada_modulation(cvec, ada_w, ada_b):
    rows = cvec.shape[0]
    tn = 1536
    n = 6 * D_MODEL
    return pl.pallas_call(
        _ada_kernel,
        grid=(DEPTH, n // tn),
        in_specs=[
            pl.BlockSpec((rows, D_MODEL), lambda l, j: (0, 0)),
            pl.BlockSpec((1, D_MODEL, tn), lambda l, j: (l, 0, j)),
            pl.BlockSpec((1, 1, tn), lambda l, j: (l, 0, j)),
        ],
        out_specs=pl.BlockSpec((1, rows, tn), lambda l, j: (l, 0, j)),
        out_shape=jax.ShapeDtypeStruct((DEPTH, rows, n), F32),
        compiler_params=_params("parallel", "parallel"),
        name="ada_modulation",
    )(cvec, ada_w, ada_b.reshape(DEPTH, 1, n))


def _nmm_kernel(x_ref, g_ref, sh_ref, sc_ref, w_ref, o_ref, h_ref):
    @pl.when(pl.program_id(1) == 0)
    def _():
        h = _rms(x_ref[...]) * g_ref[...]
        h = h * (1.0 + sc_ref[0]) + sh_ref[0]
        h_ref[...] = h.astype(BF16)

    w = w_ref[...].astype(BF16)
    o_ref[...] = jnp.dot(h_ref[...], w, preferred_element_type=F32).astype(o_ref.dtype)


def norm_mod_matmul(x, gain, shift, scale, w, n, *, tm, tn):
    t, d = x.shape
    nb = shift.shape[0]
    bpb = (t // tm) // nb
    return pl.pallas_call(
        _nmm_kernel,
        grid=(t // tm, n // tn),
        in_specs=[
            pl.BlockSpec((tm, d), lambda i, j: (i, 0)),
            pl.BlockSpec((1, d), lambda i, j: (0, 0)),
            pl.BlockSpec((1, 1, d), lambda i, j: (i // bpb, 0, 0)),
            pl.BlockSpec((1, 1, d), lambda i, j: (i // bpb, 0, 0)),
            pl.BlockSpec((d, tn), lambda i, j: (0, j)),
        ],
        out_specs=pl.BlockSpec((tm, tn), lambda i, j: (i, j)),
        out_shape=jax.ShapeDtypeStruct((t, n), BF16),
        scratch_shapes=[pltpu.VMEM((tm, d), BF16)],
        compiler_params=_params("parallel", "arbitrary"),
        name="norm_mod_matmul",
    )(x, gain.reshape(1, d), shift, scale, w)


def _group_scan(a, b, reverse):
    rows, c = a.shape
    a = a.reshape(rows // SUBLANES, SUBLANES, c)
    b = b.reshape(rows // SUBLANES, SUBLANES, c)
    sub = lax.broadcasted_iota(jnp.int32, a.shape, 1)
    for sh in (1, 2, 4):
        if reverse:
            keep = sub < SUBLANES - sh
            amount = SUBLANES - sh
        else:
            keep = sub >= sh
            amount = sh
        a_sh = jnp.where(keep, pltpu.roll(a, amount, 1), 1.0)
        b_sh = jnp.where(keep, pltpu.roll(b, amount, 1), 0.0)
        b = a * b_sh + b
        a = a * a_sh
    return a.reshape(rows, c), b.reshape(rows, c)


def _dwconv(x, cw, cb):
    n = x.shape[0]
    rows = lax.broadcasted_iota(jnp.int32, x.shape, 0)
    xm2 = jnp.where(rows >= 2, pltpu.roll(x, 2, 0), 0.0)
    xm1 = jnp.where(rows >= 1, pltpu.roll(x, 1, 0), 0.0)
    xp1 = jnp.where(rows < n - 1, pltpu.roll(x, n - 1, 0), 0.0)
    return cb + xm2 * cw[0:1] + xm1 * cw[1:2] + x * cw[2:3] + xp1 * cw[3:4]


def _gelu_tanh(x):
    return 0.5 * x * (1.0 + jnp.tanh(math.sqrt(2.0 / math.pi) * (x + 0.044715 * (x * x * x))))


def _rglru_kernel(*refs, n_c, n_x, ctx_out):
    if ctx_out:
        (xrx_ref, grx_ref, xrc_ref, grc_ref, cw_ref, cb_ref, wg_ref, gb_ref, lam_ref,
         yx_ref, yc_ref, xc_s, a0_s, b0_s, a1_s, b1_s) = refs
    else:
        (xrx_ref, grx_ref, xrc_ref, cw_ref, cb_ref, wg_ref, gb_ref, lam_ref,
         yx_ref, xc_s, a0_s, b0_s, a1_s, b1_s) = refs
    c = RNN_CB
    n_all = n_c + n_x
    cw = cw_ref[...]
    cb = cb_ref[...]

    xc_s[0:n_c, :] = _dwconv(xrc_ref[...].astype(F32), cw, cb)
    xc_s[n_c:n_all, :] = _dwconv(xrx_ref[...].astype(F32), cw, cb)

    lam = lam_ref[0]
    neg_sp = -LRU_C * jax.nn.softplus(-lam)

    def coeff_chunk(ci, carry):
        r0 = pl.multiple_of(ci * SCAN_CHUNK, SCAN_CHUNK)
        xc = xc_s[pl.ds(r0, SCAN_CHUNK), :]
        gates = jnp.dot(xc.astype(BF16), wg_ref[0], preferred_element_type=F32) + gb_ref[0]
        gates = 0.5 + 0.5 * jnp.tanh(0.5 * gates)
        for d, (a_s, b_s) in enumerate(((a0_s, b0_s), (a1_s, b1_s))):
            r = gates[:, (2 * d) * c:(2 * d + 1) * c]
            i = gates[:, (2 * d + 1) * c:(2 * d + 2) * c]
            a = jnp.exp(r * neg_sp[:, d * c:(d + 1) * c])
            one_m_a2 = 1.0 - a * a
            mult = jnp.where(one_m_a2 > 0.0, one_m_a2 * lax.rsqrt(one_m_a2), 0.0)
            bb = mult * (i * xc)
            a_cum, b_cum = _group_scan(a, bb, reverse=(d == 1))
            a_s[pl.ds(r0, SCAN_CHUNK), :] = a_cum
            b_s[pl.ds(r0, SCAN_CHUNK), :] = b_cum
        return carry

    lax.fori_loop(0, n_all // SCAN_CHUNK, coeff_chunk, 0)

    g_c = n_c // SUBLANES
    g_all = n_all // SUBLANES

    def carry_step(k, carry):
        hf, hr = carry
        rf = pl.multiple_of(k * SUBLANES, SUBLANES)
        h = a0_s[pl.ds(rf, SUBLANES), :] * hf + b0_s[pl.ds(rf, SUBLANES), :]
        a0_s[pl.ds(rf, SUBLANES), :] = h
        hf = h[SUBLANES - 1:SUBLANES, :]
        kr = jnp.where(k < g_c, g_c - 1 - k, g_all + g_c - 1 - k)
        rr = pl.multiple_of(kr * SUBLANES, SUBLANES)
        h = a1_s[pl.ds(rr, SUBLANES), :] * hr + b1_s[pl.ds(rr, SUBLANES), :]
        a1_s[pl.ds(rr, SUBLANES), :] = h
        hr = h[0:1, :]
        return hf, hr

    zero = jnp.zeros((1, c), F32)
    lax.fori_loop(0, g_all, carry_step, (zero, zero), unroll=4)

    yx = (a0_s[n_c:n_all, :] + a1_s[n_c:n_all, :]) * _gelu_tanh(grx_ref[...].astype(F32))
    yx_ref[...] = yx.astype(yx_ref.dtype)
    if ctx_out:
        yc = (a0_s[0:n_c, :] + a1_s[0:n_c, :]) * _gelu_tanh(grc_ref[...].astype(F32))
        yc_ref[...] = yc.astype(yc_ref.dtype)


def rglru(proj_x, proj_c, conv_w, conv_b, wg, gb, lam, *, xr_col, gr_col, c_xr_col, c_gr_col, ctx_out):
    c = RNN_CB
    n_cb = D_RNN // c
    xr_b, gr_b, cxr_b = xr_col // c, gr_col // c, c_xr_col // c
    in_specs = [
        pl.BlockSpec((SEQ, c), lambda b, j: (b, xr_b + j)),
        pl.BlockSpec((SEQ, c), lambda b, j: (b, gr_b + j)),
        pl.BlockSpec((CTX_LEN, c), lambda b, j: (b, cxr_b + j)),
    ]
    args = [proj_x, proj_x, proj_c]
    if ctx_out:
        cgr_b = c_gr_col // c
        in_specs.append(pl.BlockSpec((CTX_LEN, c), lambda b, j: (b, cgr_b + j)))
        args.append(proj_c)
    in_specs += [
        pl.BlockSpec((CONV_W, c), lambda b, j: (0, j)),
        pl.BlockSpec((1, c), lambda b, j: (0, j)),
        pl.BlockSpec((1, c, 4 * c), lambda b, j: (j, 0, 0)),
        pl.BlockSpec((1, 1, 4 * c), lambda b, j: (j, 0, 0)),
        pl.BlockSpec((1, 1, 2 * c), lambda b, j: (j, 0, 0)),
    ]
    args += [conv_w, conv_b.reshape(1, D_RNN), wg, gb, lam]
    out_specs = [pl.BlockSpec((SEQ, c), lambda b, j: (b, j))]
    out_shape = [jax.ShapeDtypeStruct((BATCH * SEQ, D_RNN), BF16)]
    if ctx_out:
        out_specs.append(pl.BlockSpec((CTX_LEN, c), lambda b, j: (b, j)))
        out_shape.append(jax.ShapeDtypeStruct((BATCH * CTX_LEN, D_RNN), BF16))
    n_all = SEQ + CTX_LEN
    out = pl.pallas_call(
        functools.partial(_rglru_kernel, n_c=CTX_LEN, n_x=SEQ, ctx_out=ctx_out),
        grid=(BATCH, n_cb),
        in_specs=in_specs,
        out_specs=out_specs,
        out_shape=out_shape,
        scratch_shapes=[pltpu.VMEM((n_all, c), F32) for _ in range(5)],
        compiler_params=_params("parallel", "parallel"),
        name="rglru",
    )(*args)
    return out if ctx_out else (out[0], None)


def _rope(t, cos, sin_signed):
    lane = lax.broadcasted_iota(jnp.int32, t.shape, 1)
    first_half = (lane % (2 * ROPE_FREQS)) < ROPE_FREQS
    partner = jnp.where(first_half, pltpu.roll(t, LANES - ROPE_FREQS, 1), pltpu.roll(t, ROPE_FREQS, 1))
    return t * cos + partner * sin_signed


def _attn_kernel(*refs, tq, n_c, n_x, lam_init):
    if n_x:
        (q_ref, kx_ref, vx_ref, kc_ref, vc_ref, cos_ref, sin_ref, dl_ref, sg_ref,
         o_ref, k_s, v_s, lam_s) = refs
    else:
        q_ref, kc_ref, vc_ref, dl_ref, sg_ref, o_ref, k_s, v_s, lam_s = refs
    qi = pl.program_id(2)

    @pl.when(qi == 0)
    def _():
        k_s[0:n_c, :] = kc_ref[...]
        v_s[0:n_c, 0:LANES] = vc_ref[...]
        if n_x:
            kx = _rope(kx_ref[...].astype(F32), cos_ref[...], sin_ref[...])
            k_s[n_c:n_c + n_x, :] = kx.astype(BF16)
            v_s[n_c:n_c + n_x, 0:LANES] = vx_ref[...]
        v_s[:, LANES:2 * LANES] = jnp.ones((n_c + n_x, LANES), BF16)
        dl = dl_ref[...]
        e1 = jnp.exp(jnp.sum(dl[0:1] * dl[1:2], axis=-1, keepdims=True))
        e2 = jnp.exp(jnp.sum(dl[2:3] * dl[3:4], axis=-1, keepdims=True))
        lam_s[...] = jnp.broadcast_to(e1 - e2 + lam_init, lam_s.shape)

    q = q_ref[...].astype(F32)
    if n_x:
        r0 = pl.multiple_of(qi * tq, tq)
        q = _rope(q, cos_ref[pl.ds(r0, tq), :], sin_ref[pl.ds(r0, tq), :])
    q = q * (HEAD_DIM ** -0.5 * math.log2(math.e))
    lane = lax.broadcasted_iota(jnp.int32, (ATTN_SUB, LANES), 1)
    lam = lam_s[0:1, 0:1]
    for sb in range(tq // ATTN_SUB):
        qs = q[sb * ATTN_SUB:(sb + 1) * ATTN_SUB]
        qq = jnp.concatenate([jnp.where(lane < HEAD_DIM, qs, 0.0), jnp.where(lane >= HEAD_DIM, qs, 0.0)], axis=0)
        s = lax.dot_general(qq.astype(BF16), k_s[...], (((1,), (1,)), ((), ())), preferred_element_type=F32)
        e = jnp.exp2(s - jnp.max(s, axis=-1, keepdims=True))
        ov = jnp.dot(e.astype(BF16), v_s[...], preferred_element_type=F32)
        ov = ov[:, 0:LANES] / ov[:, LANES:2 * LANES]
        o = ov[0:ATTN_SUB] - lam * ov[ATTN_SUB:2 * ATTN_SUB]
        o_ref[sb * ATTN_SUB:(sb + 1) * ATTN_SUB, :] = (_rms(o) * sg_ref[...] * (1.0 - lam_init)).astype(o_ref.dtype)


def diff_attention(q_arr, q_col, n_q, kvx, kvc, cols, cos, sin_signed, diff_lambda, subln_g, *, lam_init, tq):
    kx_col, vx_col, kc_col, vc_col = cols
    qb, kxb, vxb, kcb, vcb = (v // LANES for v in (q_col, kx_col, vx_col, kc_col, vc_col))
    n_x = SEQ if kvx is not None else 0
    n_qb = n_q // tq
    in_specs = [pl.BlockSpec((tq, LANES), lambda b, h, i: (b * n_qb + i, qb + h))]
    args = [q_arr]
    if n_x:
        in_specs += [
            pl.BlockSpec((SEQ, LANES), lambda b, h, i: (b, kxb + h)),
            pl.BlockSpec((SEQ, LANES), lambda b, h, i: (b, vxb + h)),
        ]
        args += [kvx, kvx]
    in_specs += [
        pl.BlockSpec((CTX_LEN, LANES), lambda b, h, i: (b, kcb + h)),
        pl.BlockSpec((CTX_LEN, LANES), lambda b, h, i: (b, vcb + h)),
    ]
    args += [kvc, kvc]
    if n_x:
        in_specs += [
            pl.BlockSpec((SEQ, LANES), lambda b, h, i: (0, 0)),
            pl.BlockSpec((SEQ, LANES), lambda b, h, i: (0, 0)),
        ]
        args += [cos, sin_signed]
    in_specs += [
        pl.BlockSpec((4, HEAD_DIM), lambda b, h, i: (0, 0)),
        pl.BlockSpec((1, V_DIM), lambda b, h, i: (0, 0)),
    ]
    args += [diff_lambda, subln_g.reshape(1, V_DIM)]
    n_kv = CTX_LEN + n_x
    return pl.pallas_call(
        functools.partial(_attn_kernel, tq=tq, n_c=CTX_LEN, n_x=n_x, lam_init=lam_init),
        grid=(BATCH, N_HEADS, n_qb),
        in_specs=in_specs,
        out_specs=pl.BlockSpec((tq, LANES), lambda b, h, i: (b * n_qb + i, h)),
        out_shape=jax.ShapeDtypeStruct((BATCH * n_q, ATTN_W), BF16),
        scratch_shapes=[
            pltpu.VMEM((n_kv, LANES), BF16),
            pltpu.VMEM((n_kv, 2 * LANES), BF16),
            pltpu.VMEM((SUBLANES, LANES), F32),
        ],
        compiler_params=_params("parallel", "parallel", "arbitrary"),
        name="diff_attention",
    )(*args)


def _merge_kernel(yr_ref, ya_ref, gr0_ref, gr1_ref, ga0_ref, ga1_ref, x_ref, g1_ref, ng_ref, wr_ref, wa_ref, wo_ref,
                  o_ref):
    y_rnn = jnp.dot(yr_ref[...], wr_ref[...], preferred_element_type=F32)
    y_attn = jnp.dot(ya_ref[...], wa_ref[...], preferred_element_type=F32)
    g_rnn = jax.nn.sigmoid(jnp.concatenate([gr0_ref[...], gr1_ref[...]], axis=1).astype(F32))
    g_attn = jax.nn.sigmoid(jnp.concatenate([ga0_ref[...], ga1_ref[...]], axis=1).astype(F32))
    m = (g_rnn * y_rnn + g_attn * y_attn).astype(BF16)
    mx = jnp.dot(m, wo_ref[...], preferred_element_type=F32)
    o_ref[...] = x_ref[...] + g1_ref[0] * (_rms(mx) * ng_ref[...])


def merge_out(y_rnn, y_attn, proj, x, g1, ng, w_proj_rnn, w_proj_attn, w_out, *, tm):
    t, d = x.shape
    nb = g1.shape[0]
    bpb = (t // tm) // nb
    gw = GATE_BLOCK_W
    grb, gab = COL_G_RNN // gw, COL_G_ATTN // gw
    return pl.pallas_call(
        _merge_kernel,
        grid=(t // tm,),
        in_specs=[
            pl.BlockSpec((tm, D_RNN), lambda i: (i, 0)),
            pl.BlockSpec((tm, ATTN_W), lambda i: (i, 0)),
            pl.BlockSpec((tm, gw), lambda i: (i, grb)),
            pl.BlockSpec((tm, gw), lambda i: (i, grb + 1)),
            pl.BlockSpec((tm, gw), lambda i: (i, gab)),
            pl.BlockSpec((tm, gw), lambda i: (i, gab + 1)),
            pl.BlockSpec((tm, d), lambda i: (i, 0)),
            pl.BlockSpec((1, 1, d), lambda i: (i // bpb, 0, 0)),
            pl.BlockSpec((1, d), lambda i: (0, 0)),
            pl.BlockSpec((D_RNN, d), lambda i: (0, 0)),
            pl.BlockSpec((ATTN_W, d), lambda i: (0, 0)),
            pl.BlockSpec((d, d), lambda i: (0, 0)),
        ],
        out_specs=pl.BlockSpec((tm, d), lambda i: (i, 0)),
        out_shape=jax.ShapeDtypeStruct((t, d), F32),
        compiler_params=_params("parallel"),
        name="merge_out",
    )(y_rnn, y_attn, proj, proj, proj, proj, x, g1, ng.reshape(1, d), w_proj_rnn, w_proj_attn, w_out)


def _swiglu_partial(h, wg, wu, wd):
    gate = jnp.dot(h, wg, preferred_element_type=F32)
    up = jnp.dot(h, wu, preferred_element_type=F32)
    act = (gate * jax.nn.sigmoid(gate) * up).astype(BF16)
    return jnp.dot(act, wd, preferred_element_type=F32)


def _ffn_kernel(x_ref, g_ref, sh_ref, sc_ref, g2_ref, ng_ref, wg_ref, wu_ref, wd_ref, o_ref, h_ref, acc_ref):
    f = pl.program_id(1)

    @pl.when(f == 0)
    def _():
        h = _rms(x_ref[...]) * g_ref[...]
        h = h * (1.0 + sc_ref[0]) + sh_ref[0]
        h_ref[...] = h.astype(BF16)
        acc_ref[...] = jnp.zeros_like(acc_ref)

    acc_ref[...] += _swiglu_partial(h_ref[...], wg_ref[...], wu_ref[...], wd_ref[...])

    @pl.when(f == pl.num_programs(1) - 1)
    def _():
        o_ref[...] = x_ref[...] + g2_ref[0] * (_rms(acc_ref[...]) * ng_ref[...])


def ffn_residual(x, gain, shift, scale, g2, ng, w_gu, w_down, *, tm, tf):
    t, d = x.shape
    ff = w_gu.shape[1] // 2
    nf = ff // tf
    nb = shift.shape[0]
    bpb = (t // tm) // nb
    mod_spec = pl.BlockSpec((1, 1, d), lambda i, f: (i // bpb, 0, 0))
    vec_spec = pl.BlockSpec((1, d), lambda i, f: (0, 0))
    return pl.pallas_call(
        _ffn_kernel,
        grid=(t // tm, nf),
        in_specs=[
            pl.BlockSpec((tm, d), lambda i, f: (i, 0)), vec_spec, mod_spec, mod_spec, mod_spec, vec_spec,
            pl.BlockSpec((d, tf), lambda i, f: (0, f)),
            pl.BlockSpec((d, tf), lambda i, f: (0, nf + f)),
            pl.BlockSpec((tf, d), lambda i, f: (f, 0)),
        ],
        out_specs=pl.BlockSpec((tm, d), lambda i, f: (i, 0)),
        out_shape=jax.ShapeDtypeStruct((t, d), F32),
        scratch_shapes=[pltpu.VMEM((tm, d), BF16), pltpu.VMEM((tm, d), F32)],
        compiler_params=_params("parallel", "arbitrary"),
        name="ffn_residual",
    )(x, gain.reshape(1, d), shift, scale, g2, ng.reshape(1, d), w_gu, w_gu, w_down)


MOE_TM = 512
MOE_TF = 1408
ROW_TILE = 256
META_E0, META_E1, META_W0, META_W1, META_R0, META_R1 = range(6)


def _lane_pick(rec, lane, k):
    return jnp.sum(jnp.where(lane == k, rec, 0.0), axis=-1, keepdims=True)


def _router_kernel(x_ref, g_ref, sh_ref, sc_ref, w_ref, b_ref, h_ref, meta_ref, cnt_ref, carry_s):
    @pl.when(pl.program_id(0) == 0)
    def _():
        carry_s[...] = jnp.zeros_like(carry_s)

    h = _rms(x_ref[...]) * g_ref[...]
    h = h * (1.0 + sc_ref[0]) + sh_ref[0]
    h_ref[...] = h
    logits = jnp.dot(h, w_ref[...], preferred_element_type=F32, precision=lax.Precision.HIGHEST) + b_ref[...]
    lane = lax.broadcasted_iota(jnp.int32, logits.shape, 1).astype(F32)
    neg = jnp.float32(-jnp.inf)
    logits = jnp.where(lane < N_EXPERTS, logits, neg)
    m1 = jnp.max(logits, axis=-1, keepdims=True)
    i1 = jnp.min(jnp.where(logits == m1, lane, float(LANES)), axis=-1, keepdims=True)
    rest = jnp.where(lane == i1, neg, logits)
    m2 = jnp.max(rest, axis=-1, keepdims=True)
    i2 = jnp.min(jnp.where(rest == m2, lane, float(LANES)), axis=-1, keepdims=True)
    e2 = jnp.exp(m2 - m1)
    denom = 1.0 + e2

    tm = logits.shape[0]
    sel1 = lane == i1
    sel2 = lane == i2
    member = jnp.where(jnp.logical_or(sel1, sel2), 1.0, 0.0)
    row = lax.broadcasted_iota(jnp.int32, (tm, tm), 0)
    col = lax.broadcasted_iota(jnp.int32, (tm, tm), 1)
    lower = jnp.where(col < row, 1.0, 0.0).astype(BF16)
    before = jnp.dot(lower, member.astype(BF16), preferred_element_type=F32) + carry_s[0:1, :]
    r1 = jnp.sum(jnp.where(sel1, before, 0.0), axis=-1, keepdims=True)
    r2 = jnp.sum(jnp.where(sel2, before, 0.0), axis=-1, keepdims=True)
    carry_s[0:1, :] = carry_s[0:1, :] + jnp.sum(member, axis=0, keepdims=True)
    cnt_ref[...] = jnp.broadcast_to(carry_s[0:1, :], cnt_ref.shape)

    meta = jnp.zeros_like(logits)
    for k, v in ((META_E0, i1), (META_E1, i2), (META_W0, 1.0 / denom), (META_W1, e2 / denom),
                 (META_R0, r1), (META_R1, r2)):
        meta = jnp.where(lane == k, v, meta)
    meta_ref[...] = meta


def moe_router(x, gain, shift, scale, router_w, router_b, *, tm):
    t, d = x.shape
    nb = shift.shape[0]
    bpb = (t // tm) // nb
    w = jnp.zeros((d, LANES), F32).at[:, :N_EXPERTS].set(router_w)
    b = jnp.zeros((1, LANES), F32).at[0, :N_EXPERTS].set(router_b)
    return pl.pallas_call(
        _router_kernel,
        grid=(t // tm,),
        in_specs=[
            pl.BlockSpec((tm, d), lambda i: (i, 0)),
            pl.BlockSpec((1, d), lambda i: (0, 0)),
            pl.BlockSpec((1, 1, d), lambda i: (i // bpb, 0, 0)),
            pl.BlockSpec((1, 1, d), lambda i: (i // bpb, 0, 0)),
            pl.BlockSpec((d, LANES), lambda i: (0, 0)),
            pl.BlockSpec((1, LANES), lambda i: (0, 0)),
        ],
        out_specs=[
            pl.BlockSpec((tm, d), lambda i: (i, 0)),
            pl.BlockSpec((tm, LANES), lambda i: (i, 0)),
            pl.BlockSpec((SUBLANES, LANES), lambda i: (0, 0)),
        ],
        out_shape=[
            jax.ShapeDtypeStruct((t, d), F32),
            jax.ShapeDtypeStruct((t, LANES), F32),
            jax.ShapeDtypeStruct((SUBLANES, LANES), F32),
        ],
        scratch_shapes=[pltpu.VMEM((SUBLANES, LANES), F32)],
        compiler_params=_params("arbitrary"),
        name="moe_router",
    )(x, gain.reshape(1, d), shift, scale, w, b)


def _row_copy(src_ref, src_row, dst_ref, dst_row, sem):
    return pltpu.make_async_copy(src_ref.at[pl.ds(src_row, 1), :], dst_ref.at[pl.ds(dst_row, 1), :], sem)


def _dispatch_kernel(pos_ref, h_ref, hs_in_ref, hs_ref, sem):
    del hs_in_ref
    n = h_ref.shape[0]

    def issue(r, carry):
        _row_copy(h_ref, r, hs_ref, pos_ref[0, 0, 2 * r], sem).start()
        _row_copy(h_ref, r, hs_ref, pos_ref[0, 0, 2 * r + 1], sem).start()
        return carry

    lax.fori_loop(0, n, issue, 0, unroll=8)

    def drain(r, carry):
        _row_copy(h_ref, r, hs_ref, 0, sem).wait()
        _row_copy(h_ref, r, hs_ref, 0, sem).wait()
        return carry

    lax.fori_loop(0, n, drain, 0, unroll=8)


def moe_dispatch(h, pos, n_rows):
    t, d = h.shape
    return pl.pallas_call(
        _dispatch_kernel,
        grid=(t // ROW_TILE,),
        in_specs=[
            pl.BlockSpec((1, 1, 2 * ROW_TILE), lambda i: (i, 0, 0), memory_space=pltpu.SMEM),
            pl.BlockSpec((ROW_TILE, d), lambda i: (i, 0)),
            pl.BlockSpec(memory_space=pl.ANY),
        ],
        out_specs=pl.BlockSpec(memory_space=pl.ANY),
        out_shape=jax.ShapeDtypeStruct((n_rows, d), F32),
        scratch_shapes=[pltpu.SemaphoreType.DMA(())],
        input_output_aliases={2: 0},
        compiler_params=_params("arbitrary"),
        name="moe_dispatch",
    )(pos, h, jnp.zeros((n_rows, d), F32))


def _grouped_ffn_kernel(te_ref, na_ref, hs_ref, wg_ref, wu_ref, wd_ref, ys_ref, h_s, acc_s):
    del te_ref
    f = pl.program_id(1)
    active = pl.program_id(0) < na_ref[0]

    @pl.when(jnp.logical_not(active))
    def _():
        ys_ref[...] = jnp.zeros_like(ys_ref)

    @pl.when(active)
    def _():
        @pl.when(f == 0)
        def _():
            h_s[...] = hs_ref[...].astype(BF16)
            acc_s[...] = jnp.zeros_like(acc_s)

        acc_s[...] += _swiglu_partial(h_s[...], wg_ref[0], wu_ref[0], wd_ref[0])

        @pl.when(f == pl.num_programs(1) - 1)
        def _():
            ys_ref[...] = acc_s[...]


def moe_grouped_ffn(hs, tile_expert, n_active, w_gu, w_down):
    n_rows, d = hs.shape
    ff = w_gu.shape[2] // 2
    nf = ff // MOE_TF
    n_tiles = n_rows // MOE_TM

    def row_map(t, f, te, na):
        return (jnp.minimum(t, na[0] - 1), 0)

    def f_eff(t, f, na):
        return jnp.where(t < na[0], f, nf - 1)

    return pl.pallas_call(
        _grouped_ffn_kernel,
        grid_spec=pltpu.PrefetchScalarGridSpec(
            num_scalar_prefetch=2,
            grid=(n_tiles, nf),
            in_specs=[
                pl.BlockSpec((MOE_TM, d), row_map),
                pl.BlockSpec((1, d, MOE_TF), lambda t, f, te, na: (te[t], 0, f_eff(t, f, na))),
                pl.BlockSpec((1, d, MOE_TF), lambda t, f, te, na: (te[t], 0, nf + f_eff(t, f, na))),
                pl.BlockSpec((1, MOE_TF, d), lambda t, f, te, na: (te[t], f_eff(t, f, na), 0)),
            ],
            out_specs=pl.BlockSpec((MOE_TM, d), lambda t, f, te, na: (t, 0)),
            scratch_shapes=[pltpu.VMEM((MOE_TM, d), BF16), pltpu.VMEM((MOE_TM, d), F32)],
        ),
        out_shape=jax.ShapeDtypeStruct((n_rows, d), F32),
        compiler_params=_params("arbitrary", "arbitrary"),
        name="moe_grouped_ffn",
    )(tile_expert, n_active, hs, w_gu, w_gu, w_down)


def _combine_kernel(pos_ref, ys_ref, x_ref, meta_ref, g2_ref, ng_ref, o_ref, a_s, b_s, sem):
    n = x_ref.shape[0]

    def issue(r, carry):
        _row_copy(ys_ref, pos_ref[0, 0, 2 * r], a_s, r, sem).start()
        _row_copy(ys_ref, pos_ref[0, 0, 2 * r + 1], b_s, r, sem).start()
        return carry

    lax.fori_loop(0, n, issue, 0, unroll=8)

    def drain(r, carry):
        _row_copy(ys_ref, 0, a_s, r, sem).wait()
        _row_copy(ys_ref, 0, b_s, r, sem).wait()
        return carry

    lax.fori_loop(0, n, drain, 0, unroll=8)

    meta = meta_ref[...]
    lane = lax.broadcasted_iota(jnp.int32, meta.shape, 1)
    mixed = _lane_pick(meta, lane, META_W0) * a_s[...] + _lane_pick(meta, lane, META_W1) * b_s[...]
    o_ref[...] = x_ref[...] + g2_ref[0] * (_rms(mixed) * ng_ref[...])


def moe_combine(ys, pos, meta, x, g2, ng):
    t, d = x.shape
    nb = g2.shape[0]
    bpb = (t // ROW_TILE) // nb
    return pl.pallas_call(
        _combine_kernel,
        grid=(t // ROW_TILE,),
        in_specs=[
            pl.BlockSpec((1, 1, 2 * ROW_TILE), lambda i: (i, 0, 0), memory_space=pltpu.SMEM),
            pl.BlockSpec(memory_space=pl.ANY),
            pl.BlockSpec((ROW_TILE, d), lambda i: (i, 0)),
            pl.BlockSpec((ROW_TILE, LANES), lambda i: (i, 0)),
            pl.BlockSpec((1, 1, d), lambda i: (i // bpb, 0, 0)),
            pl.BlockSpec((1, d), lambda i: (0, 0)),
        ],
        out_specs=pl.BlockSpec((ROW_TILE, d), lambda i: (i, 0)),
        out_shape=jax.ShapeDtypeStruct((t, d), F32),
        scratch_shapes=[pltpu.VMEM((ROW_TILE, d), F32), pltpu.VMEM((ROW_TILE, d), F32),
                        pltpu.SemaphoreType.DMA(())],
        compiler_params=_params("arbitrary"),
        name="moe_combine",
    )(pos, ys, x, meta, g2, ng.reshape(1, d))


def moe_residual(x, gain, shift, scale, g2, ng, router_w, router_b, w_gu, w_down):
    t, d = x.shape
    h, meta, counts = moe_router(x, gain, shift, scale, router_w, router_b, tm=512)
    n_rows = TOP_K * t + N_EXPERTS * MOE_TM
    n_tiles = n_rows // MOE_TM
    cnt = counts[0, :N_EXPERTS].astype(jnp.int32)
    padded = (cnt + MOE_TM - 1) // MOE_TM * MOE_TM
    seg_end = jnp.cumsum(padded)
    seg_start = seg_end - padded
    experts = jnp.arange(N_EXPERTS, dtype=jnp.int32)

    def position(e_lane, r_lane):
        e = meta[:, e_lane].astype(jnp.int32)
        start = jnp.sum(jnp.where(e[:, None] == experts[None, :], seg_start[None, :], 0), axis=-1)
        return start + meta[:, r_lane].astype(jnp.int32)

    pos = jnp.stack([position(META_E0, META_R0), position(META_E1, META_R1)], axis=-1)
    pos = pos.reshape(t // ROW_TILE, 1, 2 * ROW_TILE)
    n_active = seg_end[-1:] // MOE_TM
    tile_ids = jnp.arange(n_tiles, dtype=jnp.int32)
    tile_expert = jnp.sum((jnp.minimum(tile_ids, n_active - 1)[:, None] * MOE_TM >= seg_end[None, :]), axis=-1)
    tile_expert = jnp.minimum(tile_expert, N_EXPERTS - 1).astype(jnp.int32)

    hs = moe_dispatch(h, pos, n_rows)
    ys = moe_grouped_ffn(hs, tile_expert, n_active.astype(jnp.int32), w_gu, w_down)
    return moe_combine(ys, pos, meta, x, g2, ng)


def _rope_tables():
    pos = jnp.arange(SEQ)
    inv_freq = jnp.power(ROPE_THETA, -jnp.arange(ROPE_FREQS, dtype=F32) / ROPE_FREQS)
    ang_r = (pos // GRID_W).astype(F32)[:, None] * inv_freq
    ang_c = (pos % GRID_W).astype(F32)[:, None] * inv_freq
    cos = jnp.concatenate([jnp.cos(ang_r)] * 2 + [jnp.cos(ang_c)] * 2, axis=-1)
    sin = jnp.concatenate([-jnp.sin(ang_r), jnp.sin(ang_r), -jnp.sin(ang_c), jnp.sin(ang_c)], axis=-1)
    return jnp.tile(cos, (1, 2)), jnp.tile(sin, (1, 2))


def _gate_layout(gate_w, gate_b, lam):
    c = RNN_CB
    n_cb = D_RNN // c
    per = c // RNN_BLOCK_W
    gw = gate_w.reshape(2, 2, n_cb, per, RNN_BLOCK_W, RNN_BLOCK_W)
    eye = jnp.eye(per, dtype=gate_w.dtype)
    bd = gw[:, :, :, :, :, None, :] * eye[None, None, None, :, None, :, None]
    bd = bd.reshape(2, 2, n_cb, c, c)
    wg = jnp.transpose(bd, (2, 3, 0, 1, 4)).reshape(n_cb, c, 4 * c).astype(BF16)
    gb = jnp.transpose(gate_b.reshape(2, 2, n_cb, c), (2, 0, 1, 3)).reshape(n_cb, 1, 4 * c)
    lm = jnp.transpose(lam.reshape(2, n_cb, c), (1, 0, 2)).reshape(n_cb, 1, 2 * c)
    return wg, gb, lm


def kernel(x, c, ctx, c_ctx, ada_w, ada_b, norm_g, w_in, conv_w, conv_b, lru_gate_w, lru_gate_b, lru_lambda,
           diff_lambda, subln_g, w_proj_rnn, w_proj_attn, w_out, ffn_w_gu, ffn_w_down, router_w, router_b,
           moe_w_gu, moe_w_down):
    xt = x.reshape(BATCH * SEQ, D_MODEL)
    ct = ctx.reshape(BATCH * CTX_LEN, D_MODEL)
    cos, sin_signed = _rope_tables()

    cvec = jnp.concatenate([c, c_ctx[None, :], jnp.zeros((2 * SUBLANES - BATCH - 1, D_MODEL), F32)], axis=0)
    mod = ada_modulation(cvec, ada_w, ada_b)

    for l in range(DEPTH):
        last = l == DEPTH - 1
        lam_init = 0.8 - 0.6 * math.exp(-0.3 * l)
        mx = mod[l, :BATCH].reshape(BATCH, 1, 6, D_MODEL)
        mc = mod[l, BATCH:BATCH + 1].reshape(1, 1, 6, D_MODEL)
        sh1x, sc1x, g1x, sh2x, sc2x, g2x = (mx[:, :, i] for i in range(6))
        sh1c, sc1c, g1c, sh2c, sc2c, g2c = (mc[:, :, i] for i in range(6))

        wg, gb, lm = _gate_layout(lru_gate_w[l], lru_gate_b[l], lru_lambda[l])

        proj_x = norm_mod_matmul(xt, norm_g[l, 0], sh1x, sc1x, w_in[l], IN_W, tm=1024, tn=1536)
        if last:
            proj_c = norm_mod_matmul(ct, norm_g[l, 0], sh1c, sc1c, w_in[l], CTX_STATE_W, tm=1024,
                                     tn=CTX_STATE_W // 2)
        else:
            proj_c = norm_mod_matmul(ct, norm_g[l, 0], sh1c, sc1c, w_in[l], IN_W, tm=1024, tn=1536)

        y_rnn_x, y_rnn_c = rglru(proj_x, proj_c, conv_w[l], conv_b[l], wg, gb, lm, xr_col=COL_XR, gr_col=COL_GR,
                                 c_xr_col=COL_XR, c_gr_col=COL_GR, ctx_out=not last)
        y_attn_x = diff_attention(proj_x, COL_Q, SEQ, proj_x, proj_c, (COL_K, COL_V, COL_K, COL_V), cos,
                                  sin_signed, diff_lambda[l], subln_g[l], lam_init=lam_init, tq=512)
        wr = w_proj_rnn[l].astype(BF16)
        wa = w_proj_attn[l].astype(BF16)
        wo = w_out[l].astype(BF16)
        xt = merge_out(y_rnn_x, y_attn_x, proj_x, xt, g1x, norm_g[l, 1], wr, wa, wo, tm=512)
        if not last:
            y_attn_c = diff_attention(proj_c, COL_Q, CTX_LEN, None, proj_c, (0, 0, COL_K, COL_V), None, None,
                                      diff_lambda[l], subln_g[l], lam_init=lam_init, tq=CTX_LEN)
            ct = merge_out(y_rnn_c, y_attn_c, proj_c, ct, g1c, norm_g[l, 1], wr, wa, wo, tm=512)

        if l % 2 == 0:
            w_gu = ffn_w_gu[l // 2].astype(BF16)
            w_dn = ffn_w_down[l // 2].astype(BF16)
            xt = ffn_residual(xt, norm_g[l, 2], sh2x, sc2x, g2x, norm_g[l, 3], w_gu, w_dn, tm=512, tf=1408)
            if not last:
                ct = ffn_residual(ct, norm_g[l, 2], sh2c, sc2c, g2c, norm_g[l, 3], w_gu, w_dn, tm=512, tf=1408)
        else:
            w_gu = moe_w_gu[l // 2].astype(BF16)
            w_dn = moe_w_down[l // 2].astype(BF16)
            rw, rb = router_w[l // 2], router_b[l // 2]
            xt = moe_residual(xt, norm_g[l, 2], sh2x, sc2x, g2x, norm_g[l, 3], rw, rb, w_gu, w_dn)
            if not last:
                ct = moe_residual(ct, norm_g[l, 2], sh2c, sc2c, g2c, norm_g[l, 3], rw, rb, w_gu, w_dn)
    return xt.reshape(BATCH, SEQ, D_MODEL)
```

```python
import functools
import math

import jax
import jax.numpy as jnp
from jax import lax
from jax.experimental import pallas as pl
from jax.experimental.pallas import tpu as pltpu

F32 = jnp.float32
BF16 = jnp.bfloat16

D_MODEL = 1024
BATCH = 8
SEQ = 2048
DEPTH = 2
CTX_LEN = 256
GRID_W = 64
EPS = 1e-6
D_RNN = 1280
RNN_BLOCKS = 20
RNN_BLOCK_W = D_RNN // RNN_BLOCKS
CONV_W = 4
LRU_C = 8.0
N_HEADS = 8
HEAD_DIM = 64
V_DIM = 2 * HEAD_DIM
QK_W = N_HEADS * 2 * HEAD_DIM
ATTN_W = N_HEADS * V_DIM
ROPE_THETA = 10000.0
ROPE_FREQS = HEAD_DIM // 4
D_FF = 2816
N_EXPERTS = 8
TOP_K = 2

LANES = 128
SUBLANES = 8
VMEM_LIMIT_BYTES = 52 * 1024 * 1024

COL_XR = 0
COL_K = COL_XR + D_RNN
COL_V = COL_K + QK_W
COL_GR = COL_V + ATTN_W
COL_Q = COL_GR + D_RNN
COL_G_RNN = COL_Q + QK_W
COL_G_ATTN = COL_G_RNN + D_MODEL
IN_W = COL_G_ATTN + D_MODEL
CTX_STATE_W = COL_GR
GATE_BLOCK_W = 512

RNN_CB = 256
SCAN_CHUNK = 256
ATTN_SUB = 128


def _params(*sem):
    return pltpu.CompilerParams(dimension_semantics=sem, vmem_limit_bytes=VMEM_LIMIT_BYTES)


def _rms(x):
    return x * lax.rsqrt(jnp.mean(x * x, axis=-1, keepdims=True) + EPS)


def _ada_kernel(c_ref, w_ref, b_ref, o_ref):
    c = c_ref[...]
    s = c * jax.nn.sigmoid(c)
    o_ref[0] = jnp.dot(s, w_ref[0], preferred_element_type=F32, precision=lax.Precision.HIGHEST) + b_ref[0]


def ada_modulation(cvec, ada_w, ada_b):
    rows = cvec.shape[0]
    tn = 1536
    n = 6 * D_MODEL
    return pl.pallas_call(
        _ada_kernel,
        grid=(DEPTH, n // tn),
        in_specs=[
            pl.BlockSpec((rows, D_MODEL), lambda l, j: (0, 0)),
            pl.BlockSpec((1, D_MODEL, tn), lambda l, j: (l, 0, j)),
            pl.BlockSpec((1, 1, tn), lambda l, j: (l, 0, j)),
        ],
        out_specs=pl.BlockSpec((1, rows, tn), lambda l, j: (l, 0, j)),
        out_shape=jax.ShapeDtypeStruct((DEPTH, rows, n), F32),
        compiler_params=_params("parallel", "parallel"),
        name="ada_modulation",
    )(cvec, ada_w, ada_b.reshape(DEPTH, 1, n))


def _nmm_kernel(x_ref, g_ref, sh_ref, sc_ref, w_ref, o_ref, h_ref):
    @pl.when(pl.program_id(1) == 0)
    def _():
        h = _rms(x_ref[...]) * g_ref[...]
        h = h * (1.0 + sc_ref[0]) + sh_ref[0]
        h_ref[...] = h.astype(BF16)

    w = w_ref[...].astype(BF16)
    o_ref[...] = jnp.dot(h_ref[...], w, preferred_element_type=F32).astype(o_ref.dtype)


def norm_mod_matmul(x, gain, shift, scale, w, n, *, tm, tn):
    t, d = x.shape
    nb = shift.shape[0]
    bpb = (t // tm) // nb
    return pl.pallas_call(
        _nmm_kernel,
        grid=(t // tm, n // tn),
        in_specs=[
            pl.BlockSpec((tm, d), lambda i, j: (i, 0)),
            pl.BlockSpec((1, d), lambda i, j: (0, 0)),
            pl.BlockSpec((1, 1, d), lambda i, j: (i // bpb, 0, 0)),
            pl.BlockSpec((1, 1, d), lambda i, j: (i // bpb, 0, 0)),
            pl.BlockSpec((d, tn), lambda i, j: (0, j)),
        ],
        out_specs=pl.BlockSpec((tm, tn), lambda i, j: (i, j)),
        out_shape=jax.ShapeDtypeStruct((t, n), BF16),
        scratch_shapes=[pltpu.VMEM((tm, d), BF16)],
        compiler_params=_params("parallel", "arbitrary"),
        name="norm_mod_matmul",
    )(x, gain.reshape(1, d), shift, scale, w)


def _group_scan(a, b, reverse):
    rows, c = a.shape
    a = a.reshape(rows // SUBLANES, SUBLANES, c)
    b = b.reshape(rows // SUBLANES, SUBLANES, c)
    sub = lax.broadcasted_iota(jnp.int32, a.shape, 1)
    for sh in (1, 2, 4):
        if reverse:
            keep = sub < SUBLANES - sh
            amount = SUBLANES - sh
        else:
            keep = sub >= sh
            amount = sh
        a_sh = jnp.where(keep, pltpu.roll(a, amount, 1), 1.0)
        b_sh = jnp.where(keep, pltpu.roll(b, amount, 1), 0.0)
        b = a * b_sh + b
        a = a * a_sh
    return a.reshape(rows, c), b.reshape(rows, c)


def _dwconv(x, cw, cb):
    n = x.shape[0]
    rows = lax.broadcasted_iota(jnp.int32, x.shape, 0)
    xm2 = jnp.where(rows >= 2, pltpu.roll(x, 2, 0), 0.0)
    xm1 = jnp.where(rows >= 1, pltpu.roll(x, 1, 0), 0.0)
    xp1 = jnp.where(rows < n - 1, pltpu.roll(x, n - 1, 0), 0.0)
    return cb + xm2 * cw[0:1] + xm1 * cw[1:2] + x * cw[2:3] + xp1 * cw[3:4]


def _gelu_tanh(x):
    return 0.5 * x * (1.0 + jnp.tanh(math.sqrt(2.0 / math.pi) * (x + 0.044715 * (x * x * x))))


def _rglru_kernel(*refs, n_c, n_x, ctx_out):
    if ctx_out:
        (xrx_ref, grx_ref, xrc_ref, grc_ref, cw_ref, cb_ref, wg_ref, gb_ref, lam_ref,
         yx_ref, yc_ref, xc_s, a0_s, b0_s, a1_s, b1_s) = refs
    else:
        (xrx_ref, grx_ref, xrc_ref, cw_ref, cb_ref, wg_ref, gb_ref, lam_ref,
         yx_ref, xc_s, a0_s, b0_s, a1_s, b1_s) = refs
    c = RNN_CB
    n_all = n_c + n_x
    cw = cw_ref[...]
    cb = cb_ref[...]

    xc_s[0:n_c, :] = _dwconv(xrc_ref[...].astype(F32), cw, cb)
    xc_s[n_c:n_all, :] = _dwconv(xrx_ref[...].astype(F32), cw, cb)

    lam = lam_ref[0]
    neg_sp = -LRU_C * jax.nn.softplus(-lam)

    def coeff_chunk(ci, carry):
        r0 = pl.multiple_of(ci * SCAN_CHUNK, SCAN_CHUNK)
        xc = xc_s[pl.ds(r0, SCAN_CHUNK), :]
        gates = jnp.dot(xc.astype(BF16), wg_ref[0], preferred_element_type=F32) + gb_ref[0]
        gates = 0.5 + 0.5 * jnp.tanh(0.5 * gates)
        for d, (a_s, b_s) in enumerate(((a0_s, b0_s), (a1_s, b1_s))):
            r = gates[:, (2 * d) * c:(2 * d + 1) * c]
            i = gates[:, (2 * d + 1) * c:(2 * d + 2) * c]
            a = jnp.exp(r * neg_sp[:, d * c:(d + 1) * c])
            one_m_a2 = 1.0 - a * a
            mult = jnp.where(one_m_a2 > 0.0, one_m_a2 * lax.rsqrt(one_m_a2), 0.0)
            bb = mult * (i * xc)
            a_cum, b_cum = _group_scan(a, bb, reverse=(d == 1))
            a_s[pl.ds(r0, SCAN_CHUNK), :] = a_cum
            b_s[pl.ds(r0, SCAN_CHUNK), :] = b_cum
        return carry

    lax.fori_loop(0, n_all // SCAN_CHUNK, coeff_chunk, 0)

    g_c = n_c // SUBLANES
    g_all = n_all // SUBLANES

    def carry_step(k, carry):
        hf, hr = carry
        rf = pl.multiple_of(k * SUBLANES, SUBLANES)
        h = a0_s[pl.ds(rf, SUBLANES), :] * hf + b0_s[pl.ds(rf, SUBLANES), :]
        a0_s[pl.ds(rf, SUBLANES), :] = h
        hf = h[SUBLANES - 1:SUBLANES, :]
        kr = jnp.where(k < g_c, g_c - 1 - k, g_all + g_c - 1 - k)
        rr = pl.multiple_of(kr * SUBLANES, SUBLANES)
        h = a1_s[pl.ds(rr, SUBLANES), :] * hr + b1_s[pl.ds(rr, SUBLANES), :]
        a1_s[pl.ds(rr, SUBLANES), :] = h
        hr = h[0:1, :]
        return hf, hr

    zero = jnp.zeros((1, c), F32)
    lax.fori_loop(0, g_all, carry_step, (zero, zero), unroll=4)

    yx = (a0_s[n_c:n_all, :] + a1_s[n_c:n_all, :]) * _gelu_tanh(grx_ref[...].astype(F32))
    yx_ref[...] = yx.astype(yx_ref.dtype)
    if ctx_out:
        yc = (a0_s[0:n_c, :] + a1_s[0:n_c, :]) * _gelu_tanh(grc_ref[...].astype(F32))
        yc_ref[...] = yc.astype(yc_ref.dtype)


def rglru(proj_x, proj_c, conv_w, conv_b, wg, gb, lam, *, xr_col, gr_col, c_xr_col, c_gr_col, ctx_out):
    c = RNN_CB
    n_cb = D_RNN // c
    xr_b, gr_b, cxr_b = xr_col // c, gr_col // c, c_xr_col // c
    in_specs = [
        pl.BlockSpec((SEQ, c), lambda b, j: (b, xr_b + j)),
        pl.BlockSpec((SEQ, c), lambda b, j: (b, gr_b + j)),
        pl.BlockSpec((CTX_LEN, c), lambda b, j: (b, cxr_b + j)),
    ]
    args = [proj_x, proj_x, proj_c]
    if ctx_out:
        cgr_b = c_gr_col // c
        in_specs.append(pl.BlockSpec((CTX_LEN, c), lambda b, j: (b, cgr_b + j)))
        args.append(proj_c)
    in_specs += [
        pl.BlockSpec((CONV_W, c), lambda b, j: (0, j)),
        pl.BlockSpec((1, c), lambda b, j: (0, j)),
        pl.BlockSpec((1, c, 4 * c), lambda b, j: (j, 0, 0)),
        pl.BlockSpec((1, 1, 4 * c), lambda b, j: (j, 0, 0)),
        pl.BlockSpec((1, 1, 2 * c), lambda b, j: (j, 0, 0)),
    ]
    args += [conv_w, conv_b.reshape(1, D_RNN), wg, gb, lam]
    out_specs = [pl.BlockSpec((SEQ, c), lambda b, j: (b, j))]
    out_shape = [jax.ShapeDtypeStruct((BATCH * SEQ, D_RNN), BF16)]
    if ctx_out:
        out_specs.append(pl.BlockSpec((CTX_LEN, c), lambda b, j: (b, j)))
        out_shape.append(jax.ShapeDtypeStruct((BATCH * CTX_LEN, D_RNN), BF16))
    n_all = SEQ + CTX_LEN
    out = pl.pallas_call(
        functools.partial(_rglru_kernel, n_c=CTX_LEN, n_x=SEQ, ctx_out=ctx_out),
        grid=(BATCH, n_cb),
        in_specs=in_specs,
        out_specs=out_specs,
        out_shape=out_shape,
        scratch_shapes=[pltpu.VMEM((n_all, c), F32) for _ in range(5)],
        compiler_params=_params("parallel", "parallel"),
        name="rglru",
    )(*args)
    return out if ctx_out else (out[0], None)


def _rope(t, cos, sin_signed):
    lane = lax.broadcasted_iota(jnp.int32, t.shape, 1)
    first_half = (lane % (2 * ROPE_FREQS)) < ROPE_FREQS
    partner = jnp.where(first_half, pltpu.roll(t, LANES - ROPE_FREQS, 1), pltpu.roll(t, ROPE_FREQS, 1))
    return t * cos + partner * sin_signed


def _attn_kernel(*refs, tq, n_c, n_x, lam_init):
    if n_x:
        (q_ref, kx_ref, vx_ref, kc_ref, vc_ref, cos_ref, sin_ref, dl_ref, sg_ref,
         o_ref, k_s, v_s, lam_s) = refs
    else:
        q_ref, kc_ref, vc_ref, dl_ref, sg_ref, o_ref, k_s, v_s, lam_s = refs
    qi = pl.program_id(2)

    @pl.when(qi == 0)
    def _():
        k_s[0:n_c, :] = kc_ref[...]
        v_s[0:n_c, 0:LANES] = vc_ref[...]
        if n_x:
            kx = _rope(kx_ref[...].astype(F32), cos_ref[...], sin_ref[...])
            k_s[n_c:n_c + n_x, :] = kx.astype(BF16)
            v_s[n_c:n_c + n_x, 0:LANES] = vx_ref[...]
        v_s[:, LANES:2 * LANES] = jnp.ones((n_c + n_x, LANES), BF16)
        dl = dl_ref[...]
        e1 = jnp.exp(jnp.sum(dl[0:1] * dl[1:2], axis=-1, keepdims=True))
        e2 = jnp.exp(jnp.sum(dl[2:3] * dl[3:4], axis=-1, keepdims=True))
        lam_s[...] = jnp.broadcast_to(e1 - e2 + lam_init, lam_s.shape)

    q = q_ref[...].astype(F32)
    if n_x:
        r0 = pl.multiple_of(qi * tq, tq)
        q = _rope(q, cos_ref[pl.ds(r0, tq), :], sin_ref[pl.ds(r0, tq), :])
    q = q * (HEAD_DIM ** -0.5 * math.log2(math.e))
    lane = lax.broadcasted_iota(jnp.int32, (ATTN_SUB, LANES), 1)
    lam = lam_s[0:1, 0:1]
    scores = []
    for sb in range(tq // ATTN_SUB):
        qs = q[sb * ATTN_SUB:(sb + 1) * ATTN_SUB]
        qq = jnp.concatenate([jnp.where(lane < HEAD_DIM, qs, 0.0), jnp.where(lane >= HEAD_DIM, qs, 0.0)], axis=0)
        scores.append(lax.dot_general(qq.astype(BF16), k_s[...], (((1,), (1,)), ((), ())),
                                      preferred_element_type=F32))
    for sb, s in enumerate(scores):
        e = jnp.exp2(s - jnp.max(s, axis=-1, keepdims=True))
        ov = jnp.dot(e.astype(BF16), v_s[...], preferred_element_type=F32)
        ov = ov[:, 0:LANES] / ov[:, LANES:2 * LANES]
        o = ov[0:ATTN_SUB] - lam * ov[ATTN_SUB:2 * ATTN_SUB]
        o_ref[sb * ATTN_SUB:(sb + 1) * ATTN_SUB, :] = (_rms(o) * sg_ref[...] * (1.0 - lam_init)).astype(o_ref.dtype)


def diff_attention(q_arr, q_col, n_q, kvx, kvc, cols, cos, sin_signed, diff_lambda, subln_g, *, lam_init, tq):
    kx_col, vx_col, kc_col, vc_col = cols
    qb, kxb, vxb, kcb, vcb = (v // LANES for v in (q_col, kx_col, vx_col, kc_col, vc_col))
    n_x = SEQ if kvx is not None else 0
    n_qb = n_q // tq
    in_specs = [pl.BlockSpec((tq, LANES), lambda b, h, i: (b * n_qb + i, qb + h))]
    args = [q_arr]
    if n_x:
        in_specs += [
            pl.BlockSpec((SEQ, LANES), lambda b, h, i: (b, kxb + h)),
            pl.BlockSpec((SEQ, LANES), lambda b, h, i: (b, vxb + h)),
        ]
        args += [kvx, kvx]
    in_specs += [
        pl.BlockSpec((CTX_LEN, LANES), lambda b, h, i: (b, kcb + h)),
        pl.BlockSpec((CTX_LEN, LANES), lambda b, h, i: (b, vcb + h)),
    ]
    args += [kvc, kvc]
    if n_x:
        in_specs += [
            pl.BlockSpec((SEQ, LANES), lambda b, h, i: (0, 0)),
            pl.BlockSpec((SEQ, LANES), lambda b, h, i: (0, 0)),
        ]
        args += [cos, sin_signed]
    in_specs += [
        pl.BlockSpec((4, HEAD_DIM), lambda b, h, i: (0, 0)),
        pl.BlockSpec((1, V_DIM), lambda b, h, i: (0, 0)),
    ]
    args += [diff_lambda, subln_g.reshape(1, V_DIM)]
    n_kv = CTX_LEN + n_x
    return pl.pallas_call(
        functools.partial(_attn_kernel, tq=tq, n_c=CTX_LEN, n_x=n_x, lam_init=lam_init),
        grid=(BATCH, N_HEADS, n_qb),
        in_specs=in_specs,
        out_specs=pl.BlockSpec((tq, LANES), lambda b, h, i: (b * n_qb + i, h)),
        out_shape=jax.ShapeDtypeStruct((BATCH * n_q, ATTN_W), BF16),
        scratch_shapes=[
            pltpu.VMEM((n_kv, LANES), BF16),
            pltpu.VMEM((n_kv, 2 * LANES), BF16),
            pltpu.VMEM((SUBLANES, LANES), F32),
        ],
        compiler_params=_params("parallel", "parallel", "arbitrary"),
        name="diff_attention",
    )(*args)


def _merge_kernel(yr_ref, ya_ref, gr0_ref, gr1_ref, ga0_ref, ga1_ref, x_ref, g1_ref, ng_ref, wr_ref, wa_ref, wo_ref,
                  o_ref):
    y_rnn = jnp.dot(yr_ref[...], wr_ref[...], preferred_element_type=F32)
    y_attn = jnp.dot(ya_ref[...], wa_ref[...], preferred_element_type=F32)
    g_rnn = jax.nn.sigmoid(jnp.concatenate([gr0_ref[...], gr1_ref[...]], axis=1).astype(F32))
    g_attn = jax.nn.sigmoid(jnp.concatenate([ga0_ref[...], ga1_ref[...]], axis=1).astype(F32))
    m = (g_rnn * y_rnn + g_attn * y_attn).astype(BF16)
    mx = jnp.dot(m, wo_ref[...], preferred_element_type=F32)
    o_ref[...] = x_ref[...] + g1_ref[0] * (_rms(mx) * ng_ref[...])


def merge_out(y_rnn, y_attn, proj, x, g1, ng, w_proj_rnn, w_proj_attn, w_out, *, tm):
    t, d = x.shape
    nb = g1.shape[0]
    bpb = (t // tm) // nb
    gw = GATE_BLOCK_W
    grb, gab = COL_G_RNN // gw, COL_G_ATTN // gw
    return pl.pallas_call(
        _merge_kernel,
        grid=(t // tm,),
        in_specs=[
            pl.BlockSpec((tm, D_RNN), lambda i: (i, 0)),
            pl.BlockSpec((tm, ATTN_W), lambda i: (i, 0)),
            pl.BlockSpec((tm, gw), lambda i: (i, grb)),
            pl.BlockSpec((tm, gw), lambda i: (i, grb + 1)),
            pl.BlockSpec((tm, gw), lambda i: (i, gab)),
            pl.BlockSpec((tm, gw), lambda i: (i, gab + 1)),
            pl.BlockSpec((tm, d), lambda i: (i, 0)),
            pl.BlockSpec((1, 1, d), lambda i: (i // bpb, 0, 0)),
            pl.BlockSpec((1, d), lambda i: (0, 0)),
            pl.BlockSpec((D_RNN, d), lambda i: (0, 0)),
            pl.BlockSpec((ATTN_W, d), lambda i: (0, 0)),
            pl.BlockSpec((d, d), lambda i: (0, 0)),
        ],
        out_specs=pl.BlockSpec((tm, d), lambda i: (i, 0)),
        out_shape=jax.ShapeDtypeStruct((t, d), F32),
        compiler_params=_params("parallel"),
        name="merge_out",
    )(y_rnn, y_attn, proj, proj, proj, proj, x, g1, ng.reshape(1, d), w_proj_rnn, w_proj_attn, w_out)


def _swiglu_partial(h, wg, wu, wd):
    gate = jnp.dot(h, wg, preferred_element_type=F32)
    up = jnp.dot(h, wu, preferred_element_type=F32)
    act = (gate * jax.nn.sigmoid(gate) * up).astype(BF16)
    return jnp.dot(act, wd, preferred_element_type=F32)


def _ffn_kernel(x_ref, g_ref, sh_ref, sc_ref, g2_ref, ng_ref, wg_ref, wu_ref, wd_ref, o_ref, h_ref, acc_ref):
    f = pl.program_id(1)

    @pl.when(f == 0)
    def _():
        h = _rms(x_ref[...]) * g_ref[...]
        h = h * (1.0 + sc_ref[0]) + sh_ref[0]
        h_ref[...] = h.astype(BF16)
        acc_ref[...] = jnp.zeros_like(acc_ref)

    acc_ref[...] += _swiglu_partial(h_ref[...], wg_ref[...], wu_ref[...], wd_ref[...])

    @pl.when(f == pl.num_programs(1) - 1)
    def _():
        o_ref[...] = x_ref[...] + g2_ref[0] * (_rms(acc_ref[...]) * ng_ref[...])


def ffn_residual(x, gain, shift, scale, g2, ng, w_gu, w_down, *, tm, tf):
    t, d = x.shape
    ff = w_gu.shape[1] // 2
    nf = ff // tf
    nb = shift.shape[0]
    bpb = (t // tm) // nb
    mod_spec = pl.BlockSpec((1, 1, d), lambda i, f: (i // bpb, 0, 0))
    vec_spec = pl.BlockSpec((1, d), lambda i, f: (0, 0))
    return pl.pallas_call(
        _ffn_kernel,
        grid=(t // tm, nf),
        in_specs=[
            pl.BlockSpec((tm, d), lambda i, f: (i, 0)), vec_spec, mod_spec, mod_spec, mod_spec, vec_spec,
            pl.BlockSpec((d, tf), lambda i, f: (0, f)),
            pl.BlockSpec((d, tf), lambda i, f: (0, nf + f)),
            pl.BlockSpec((tf, d), lambda i, f: (f, 0)),
        ],
        out_specs=pl.BlockSpec((tm, d), lambda i, f: (i, 0)),
        out_shape=jax.ShapeDtypeStruct((t, d), F32),
        scratch_shapes=[pltpu.VMEM((tm, d), BF16), pltpu.VMEM((tm, d), F32)],
        compiler_params=_params("parallel", "arbitrary"),
        name="ffn_residual",
    )(x, gain.reshape(1, d), shift, scale, g2, ng.reshape(1, d), w_gu, w_gu, w_down)


MOE_TM = 512
MOE_TF = 1408
ROW_TILE = 256
DMA_UNROLL = 8
META_E0, META_E1, META_W0, META_W1, META_R0, META_R1 = range(6)


def _lane_pick(rec, lane, k):
    return jnp.sum(jnp.where(lane == k, rec, 0.0), axis=-1, keepdims=True)


def _router_kernel(x_ref, g_ref, sh_ref, sc_ref, w_ref, b_ref, h_ref, meta_ref, cnt_ref, carry_s):
    @pl.when(pl.program_id(0) == 0)
    def _():
        carry_s[...] = jnp.zeros_like(carry_s)

    h = _rms(x_ref[...]) * g_ref[...]
    h = h * (1.0 + sc_ref[0]) + sh_ref[0]
    h_ref[...] = h
    logits = jnp.dot(h, w_ref[...], preferred_element_type=F32, precision=lax.Precision.HIGHEST) + b_ref[...]
    lane = lax.broadcasted_iota(jnp.int32, logits.shape, 1).astype(F32)
    neg = jnp.float32(-jnp.inf)
    logits = jnp.where(lane < N_EXPERTS, logits, neg)
    m1 = jnp.max(logits, axis=-1, keepdims=True)
    i1 = jnp.min(jnp.where(logits == m1, lane, float(LANES)), axis=-1, keepdims=True)
    rest = jnp.where(lane == i1, neg, logits)
    m2 = jnp.max(rest, axis=-1, keepdims=True)
    i2 = jnp.min(jnp.where(rest == m2, lane, float(LANES)), axis=-1, keepdims=True)
    e2 = jnp.exp(m2 - m1)
    denom = 1.0 + e2

    tm = logits.shape[0]
    sel1 = lane == i1
    sel2 = lane == i2
    member = jnp.where(jnp.logical_or(sel1, sel2), 1.0, 0.0)
    row = lax.broadcasted_iota(jnp.int32, (tm, tm), 0)
    col = lax.broadcasted_iota(jnp.int32, (tm, tm), 1)
    lower = jnp.where(col < row, 1.0, 0.0).astype(BF16)
    before = jnp.dot(lower, member.astype(BF16), preferred_element_type=F32) + carry_s[0:1, :]
    r1 = jnp.sum(jnp.where(sel1, before, 0.0), axis=-1, keepdims=True)
    r2 = jnp.sum(jnp.where(sel2, before, 0.0), axis=-1, keepdims=True)
    carry_s[0:1, :] = carry_s[0:1, :] + jnp.sum(member, axis=0, keepdims=True)
    cnt_ref[...] = jnp.broadcast_to(carry_s[0:1, :], cnt_ref.shape)

    meta = jnp.zeros_like(logits)
    for k, v in ((META_E0, i1), (META_E1, i2), (META_W0, 1.0 / denom), (META_W1, e2 / denom),
                 (META_R0, r1), (META_R1, r2)):
        meta = jnp.where(lane == k, v, meta)
    meta_ref[...] = meta


def moe_router(x, gain, shift, scale, router_w, router_b, *, tm):
    t, d = x.shape
    nb = shift.shape[0]
    bpb = (t // tm) // nb
    w = jnp.zeros((d, LANES), F32).at[:, :N_EXPERTS].set(router_w)
    b = jnp.zeros((1, LANES), F32).at[0, :N_EXPERTS].set(router_b)
    return pl.pallas_call(
        _router_kernel,
        grid=(t // tm,),
        in_specs=[
            pl.BlockSpec((tm, d), lambda i: (i, 0)),
            pl.BlockSpec((1, d), lambda i: (0, 0)),
            pl.BlockSpec((1, 1, d), lambda i: (i // bpb, 0, 0)),
            pl.BlockSpec((1, 1, d), lambda i: (i // bpb, 0, 0)),
            pl.BlockSpec((d, LANES), lambda i: (0, 0)),
            pl.BlockSpec((1, LANES), lambda i: (0, 0)),
        ],
        out_specs=[
            pl.BlockSpec((tm, d), lambda i: (i, 0)),
            pl.BlockSpec((tm, LANES), lambda i: (i, 0)),
            pl.BlockSpec((SUBLANES, LANES), lambda i: (0, 0)),
        ],
        out_shape=[
            jax.ShapeDtypeStruct((t, d), F32),
            jax.ShapeDtypeStruct((t, LANES), F32),
            jax.ShapeDtypeStruct((SUBLANES, LANES), F32),
        ],
        scratch_shapes=[pltpu.VMEM((SUBLANES, LANES), F32)],
        compiler_params=_params("arbitrary"),
        name="moe_router",
    )(x, gain.reshape(1, d), shift, scale, w, b)


def _row_copy(src_ref, src_row, dst_ref, dst_row, sem):
    return pltpu.make_async_copy(src_ref.at[pl.ds(src_row, 1), :], dst_ref.at[pl.ds(dst_row, 1), :], sem)


def _dispatch_kernel(pos_ref, h_ref, hs_in_ref, hs_ref, sem):
    del hs_in_ref
    n = h_ref.shape[0]

    def issue(g, carry):
        for j in range(DMA_UNROLL):
            r = g * DMA_UNROLL + j
            _row_copy(h_ref, r, hs_ref, pos_ref[0, 0, 2 * r], sem).start(priority=0)
            _row_copy(h_ref, r, hs_ref, pos_ref[0, 0, 2 * r + 1], sem).start(priority=1)
        return carry

    lax.fori_loop(0, n // DMA_UNROLL, issue, 0)
    for _ in range(TOP_K):
        pltpu.make_async_copy(h_ref, hs_ref.at[pl.ds(0, n), :], sem).wait()


def moe_dispatch(h, pos, n_rows):
    t, d = h.shape
    return pl.pallas_call(
        _dispatch_kernel,
        grid=(t // ROW_TILE,),
        in_specs=[
            pl.BlockSpec((1, 1, 2 * ROW_TILE), lambda i: (i, 0, 0), memory_space=pltpu.SMEM),
            pl.BlockSpec((ROW_TILE, d), lambda i: (i, 0)),
            pl.BlockSpec(memory_space=pl.ANY),
        ],
        out_specs=pl.BlockSpec(memory_space=pl.ANY),
        out_shape=jax.ShapeDtypeStruct((n_rows, d), F32),
        scratch_shapes=[pltpu.SemaphoreType.DMA(())],
        input_output_aliases={2: 0},
        compiler_params=_params("arbitrary"),
        name="moe_dispatch",
    )(pos, h, jnp.zeros((n_rows, d), F32))


def _grouped_ffn_kernel(te_ref, na_ref, hs_ref, wg_ref, wu_ref, wd_ref, ys_ref, h_s, acc_s):
    del te_ref
    f = pl.program_id(1)
    active = pl.program_id(0) < na_ref[0]

    @pl.when(jnp.logical_not(active))
    def _():
        ys_ref[...] = jnp.zeros_like(ys_ref)

    @pl.when(active)
    def _():
        @pl.when(f == 0)
        def _():
            h_s[...] = hs_ref[...].astype(BF16)
            acc_s[...] = jnp.zeros_like(acc_s)

        acc_s[...] += _swiglu_partial(h_s[...], wg_ref[0], wu_ref[0], wd_ref[0])

        @pl.when(f == pl.num_programs(1) - 1)
        def _():
            ys_ref[...] = acc_s[...]


def moe_grouped_ffn(hs, tile_expert, n_active, w_gu, w_down):
    n_rows, d = hs.shape
    ff = w_gu.shape[2] // 2
    nf = ff // MOE_TF
    n_tiles = n_rows // MOE_TM

    def row_map(t, f, te, na):
        return (jnp.minimum(t, na[0] - 1), 0)

    def f_eff(t, f, na):
        return jnp.where(t < na[0], f, nf - 1)

    return pl.pallas_call(
        _grouped_ffn_kernel,
        grid_spec=pltpu.PrefetchScalarGridSpec(
            num_scalar_prefetch=2,
            grid=(n_tiles, nf),
            in_specs=[
                pl.BlockSpec((MOE_TM, d), row_map),
                pl.BlockSpec((1, d, MOE_TF), lambda t, f, te, na: (te[t], 0, f_eff(t, f, na))),
                pl.BlockSpec((1, d, MOE_TF), lambda t, f, te, na: (te[t], 0, nf + f_eff(t, f, na))),
                pl.BlockSpec((1, MOE_TF, d), lambda t, f, te, na: (te[t], f_eff(t, f, na), 0)),
            ],
            out_specs=pl.BlockSpec((MOE_TM, d), lambda t, f, te, na: (t, 0)),
            scratch_shapes=[pltpu.VMEM((MOE_TM, d), BF16), pltpu.VMEM((MOE_TM, d), F32)],
        ),
        out_shape=jax.ShapeDtypeStruct((n_rows, d), F32),
        compiler_params=_params("arbitrary", "arbitrary"),
        name="moe_grouped_ffn",
    )(tile_expert, n_active, hs, w_gu, w_gu, w_down)


def _combine_kernel(pos_ref, ys_ref, x_ref, meta_ref, g2_ref, ng_ref, o_ref, a_s, b_s, sem):
    n = x_ref.shape[0]

    def issue(g, carry):
        for j in range(DMA_UNROLL):
            r = g * DMA_UNROLL + j
            _row_copy(ys_ref, pos_ref[0, 0, 2 * r], a_s, r, sem).start(priority=0)
            _row_copy(ys_ref, pos_ref[0, 0, 2 * r + 1], b_s, r, sem).start(priority=1)
        return carry

    lax.fori_loop(0, n // DMA_UNROLL, issue, 0)
    pltpu.make_async_copy(ys_ref.at[pl.ds(0, n), :], a_s, sem).wait()
    pltpu.make_async_copy(ys_ref.at[pl.ds(0, n), :], b_s, sem).wait()

    meta = meta_ref[...]
    lane = lax.broadcasted_iota(jnp.int32, meta.shape, 1)
    mixed = _lane_pick(meta, lane, META_W0) * a_s[...] + _lane_pick(meta, lane, META_W1) * b_s[...]
    o_ref[...] = x_ref[...] + g2_ref[0] * (_rms(mixed) * ng_ref[...])


def moe_combine(ys, pos, meta, x, g2, ng):
    t, d = x.shape
    nb = g2.shape[0]
    bpb = (t // ROW_TILE) // nb
    return pl.pallas_call(
        _combine_kernel,
        grid=(t // ROW_TILE,),
        in_specs=[
            pl.BlockSpec((1, 1, 2 * ROW_TILE), lambda i: (i, 0, 0), memory_space=pltpu.SMEM),
            pl.BlockSpec(memory_space=pl.ANY),
            pl.BlockSpec((ROW_TILE, d), lambda i: (i, 0)),
            pl.BlockSpec((ROW_TILE, LANES), lambda i: (i, 0)),
            pl.BlockSpec((1, 1, d), lambda i: (i // bpb, 0, 0)),
            pl.BlockSpec((1, d), lambda i: (0, 0)),
        ],
        out_specs=pl.BlockSpec((ROW_TILE, d), lambda i: (i, 0)),
        out_shape=jax.ShapeDtypeStruct((t, d), F32),
        scratch_shapes=[pltpu.VMEM((ROW_TILE, d), F32), pltpu.VMEM((ROW_TILE, d), F32),
                        pltpu.SemaphoreType.DMA(())],
        compiler_params=_params("arbitrary"),
        name="moe_combine",
    )(pos, ys, x, meta, g2, ng.reshape(1, d))


def moe_residual(x, gain, shift, scale, g2, ng, router_w, router_b, w_gu, w_down):
    t, d = x.shape
    h, meta, counts = moe_router(x, gain, shift, scale, router_w, router_b, tm=512)
    n_rows = TOP_K * t + N_EXPERTS * MOE_TM
    n_tiles = n_rows // MOE_TM
    cnt = counts[0, :N_EXPERTS].astype(jnp.int32)
    padded = (cnt + MOE_TM - 1) // MOE_TM * MOE_TM
    seg_end = jnp.cumsum(padded)
    seg_start = seg_end - padded
    experts = jnp.arange(N_EXPERTS, dtype=jnp.int32)

    def position(e_lane, r_lane):
        e = meta[:, e_lane].astype(jnp.int32)
        start = jnp.sum(jnp.where(e[:, None] == experts[None, :], seg_start[None, :], 0), axis=-1)
        return start + meta[:, r_lane].astype(jnp.int32)

    pos = jnp.stack([position(META_E0, META_R0), position(META_E1, META_R1)], axis=-1)
    pos = pos.reshape(t // ROW_TILE, 1, 2 * ROW_TILE)
    n_active = seg_end[-1:] // MOE_TM
    tile_ids = jnp.arange(n_tiles, dtype=jnp.int32)
    tile_expert = jnp.sum((jnp.minimum(tile_ids, n_active - 1)[:, None] * MOE_TM >= seg_end[None, :]), axis=-1)
    tile_expert = jnp.minimum(tile_expert, N_EXPERTS - 1).astype(jnp.int32)

    hs = moe_dispatch(h, pos, n_rows)
    ys = moe_grouped_ffn(hs, tile_expert, n_active.astype(jnp.int32), w_gu, w_down)
    return moe_combine(ys, pos, meta, x, g2, ng)


def _rope_tables():
    pos = jnp.arange(SEQ)
    inv_freq = jnp.power(ROPE_THETA, -jnp.arange(ROPE_FREQS, dtype=F32) / ROPE_FREQS)
    ang_r = (pos // GRID_W).astype(F32)[:, None] * inv_freq
    ang_c = (pos % GRID_W).astype(F32)[:, None] * inv_freq
    cos = jnp.concatenate([jnp.cos(ang_r)] * 2 + [jnp.cos(ang_c)] * 2, axis=-1)
    sin = jnp.concatenate([-jnp.sin(ang_r), jnp.sin(ang_r), -jnp.sin(ang_c), jnp.sin(ang_c)], axis=-1)
    return jnp.tile(cos, (1, 2)), jnp.tile(sin, (1, 2))


def _gate_layout(gate_w, gate_b, lam):
    c = RNN_CB
    n_cb = D_RNN // c
    per = c // RNN_BLOCK_W
    gw = gate_w.reshape(2, 2, n_cb, per, RNN_BLOCK_W, RNN_BLOCK_W)
    eye = jnp.eye(per, dtype=gate_w.dtype)
    bd = gw[:, :, :, :, :, None, :] * eye[None, None, None, :, None, :, None]
    bd = bd.reshape(2, 2, n_cb, c, c)
    wg = jnp.transpose(bd, (2, 3, 0, 1, 4)).reshape(n_cb, c, 4 * c).astype(BF16)
    gb = jnp.transpose(gate_b.reshape(2, 2, n_cb, c), (2, 0, 1, 3)).reshape(n_cb, 1, 4 * c)
    lm = jnp.transpose(lam.reshape(2, n_cb, c), (1, 0, 2)).reshape(n_cb, 1, 2 * c)
    return wg, gb, lm


def kernel(x, c, ctx, c_ctx, ada_w, ada_b, norm_g, w_in, conv_w, conv_b, lru_gate_w, lru_gate_b, lru_lambda,
           diff_lambda, subln_g, w_proj_rnn, w_proj_attn, w_out, ffn_w_gu, ffn_w_down, router_w, router_b,
           moe_w_gu, moe_w_down):
    xt = x.reshape(BATCH * SEQ, D_MODEL)
    ct = ctx.reshape(BATCH * CTX_LEN, D_MODEL)
    cos, sin_signed = _rope_tables()

    cvec = jnp.concatenate([c, c_ctx[None, :], jnp.zeros((2 * SUBLANES - BATCH - 1, D_MODEL), F32)], axis=0)
    mod = ada_modulation(cvec, ada_w, ada_b)

    for l in range(DEPTH):
        last = l == DEPTH - 1
        lam_init = 0.8 - 0.6 * math.exp(-0.3 * l)
        mx = mod[l, :BATCH].reshape(BATCH, 1, 6, D_MODEL)
        mc = mod[l, BATCH:BATCH + 1].reshape(1, 1, 6, D_MODEL)
        sh1x, sc1x, g1x, sh2x, sc2x, g2x = (mx[:, :, i] for i in range(6))
        sh1c, sc1c, g1c, sh2c, sc2c, g2c = (mc[:, :, i] for i in range(6))

        wg, gb, lm = _gate_layout(lru_gate_w[l], lru_gate_b[l], lru_lambda[l])

        proj_x = norm_mod_matmul(xt, norm_g[l, 0], sh1x, sc1x, w_in[l], IN_W, tm=1024, tn=1536)
        if last:
            proj_c = norm_mod_matmul(ct, norm_g[l, 0], sh1c, sc1c, w_in[l], CTX_STATE_W, tm=1024,
                                     tn=CTX_STATE_W // 2)
        else:
            proj_c = norm_mod_matmul(ct, norm_g[l, 0], sh1c, sc1c, w_in[l], IN_W, tm=1024, tn=1536)

        y_rnn_x, y_rnn_c = rglru(proj_x, proj_c, conv_w[l], conv_b[l], wg, gb, lm, xr_col=COL_XR, gr_col=COL_GR,
                                 c_xr_col=COL_XR, c_gr_col=COL_GR, ctx_out=not last)
        y_attn_x = diff_attention(proj_x, COL_Q, SEQ, proj_x, proj_c, (COL_K, COL_V, COL_K, COL_V), cos,
                                  sin_signed, diff_lambda[l], subln_g[l], lam_init=lam_init, tq=1024)
        wr = w_proj_rnn[l].astype(BF16)
        wa = w_proj_attn[l].astype(BF16)
        wo = w_out[l].astype(BF16)
        xt = merge_out(y_rnn_x, y_attn_x, proj_x, xt, g1x, norm_g[l, 1], wr, wa, wo, tm=512)
        if not last:
            y_attn_c = diff_attention(proj_c, COL_Q, CTX_LEN, None, proj_c, (0, 0, COL_K, COL_V), None, None,
                                      diff_lambda[l], subln_g[l], lam_init=lam_init, tq=CTX_LEN)
            ct = merge_out(y_rnn_c, y_attn_c, proj_c, ct, g1c, norm_g[l, 1], wr, wa, wo, tm=512)

        if l % 2 == 0:
            w_gu = ffn_w_gu[l // 2].astype(BF16)
            w_dn = ffn_w_down[l // 2].astype(BF16)
            xt = ffn_residual(xt, norm_g[l, 2], sh2x, sc2x, g2x, norm_g[l, 3], w_gu, w_dn, tm=512, tf=1408)
            if not last:
                ct = ffn_residual(ct, norm_g[l, 2], sh2c, sc2c, g2c, norm_g[l, 3], w_gu, w_dn, tm=512, tf=1408)
        else:
            w_gu = moe_w_gu[l // 2].astype(BF16)
            w_dn = moe_w_down[l // 2].astype(BF16)
            rw, rb = router_w[l // 2], router_b[l // 2]
            xt = moe_residual(xt, norm_g[l, 2], sh2x, sc2x, g2x, norm_g[l, 3], rw, rb, w_gu, w_dn)
            if not last:
                ct = moe_residual(ct, norm_g[l, 2], sh2c, sc2c, g2c, norm_g[l, 3], rw, rb, w_gu, w_dn)
    return xt.reshape(BATCH, SEQ, D_MODEL)
```

```python
import functools
import math

import jax
import jax.numpy as jnp
from jax import lax
from jax.experimental import pallas as pl
from jax.experimental.pallas import tpu as pltpu

F32 = jnp.float32
BF16 = jnp.bfloat16

D_MODEL = 1024
BATCH = 8
SEQ = 2048
DEPTH = 2
CTX_LEN = 256
GRID_W = 64
EPS = 1e-6
D_RNN = 1280
RNN_BLOCKS = 20
RNN_BLOCK_W = D_RNN // RNN_BLOCKS
CONV_W = 4
LRU_C = 8.0
N_HEADS = 8
HEAD_DIM = 64
V_DIM = 2 * HEAD_DIM
QK_W = N_HEADS * 2 * HEAD_DIM
ATTN_W = N_HEADS * V_DIM
ROPE_THETA = 10000.0
ROPE_FREQS = HEAD_DIM // 4
D_FF = 2816
N_EXPERTS = 8
TOP_K = 2

LANES = 128
SUBLANES = 8
VMEM_LIMIT_BYTES = 52 * 1024 * 1024

COL_XR = 0
COL_K = COL_XR + D_RNN
COL_V = COL_K + QK_W
COL_GR = COL_V + ATTN_W
COL_Q = COL_GR + D_RNN
COL_G_RNN = COL_Q + QK_W
COL_G_ATTN = COL_G_RNN + D_MODEL
IN_W = COL_G_ATTN + D_MODEL
CTX_STATE_W = COL_GR
GATE_BLOCK_W = 512

RNN_CB = 256
SCAN_CHUNK = 256
ATTN_SUB = 128
NORM_ROWS = 512


def _params(*sem):
    return pltpu.CompilerParams(dimension_semantics=sem, vmem_limit_bytes=VMEM_LIMIT_BYTES)


def _rms(x):
    return x * lax.rsqrt(jnp.mean(x * x, axis=-1, keepdims=True) + EPS)


def _ada_kernel(c_ref, w_ref, b_ref, o_ref):
    c = c_ref[...]
    s = c * jax.nn.sigmoid(c)
    o_ref[0] = jnp.dot(s, w_ref[0], preferred_element_type=F32, precision=lax.Precision.HIGHEST) + b_ref[0]


def ada_modulation(cvec, ada_w, ada_b):
    rows = cvec.shape[0]
    tn = 1536
    n = 6 * D_MODEL
    return pl.pallas_call(
        _ada_kernel,
        grid=(DEPTH, n // tn),
        in_specs=[
            pl.BlockSpec((rows, D_MODEL), lambda l, j: (0, 0)),
            pl.BlockSpec((1, D_MODEL, tn), lambda l, j: (l, 0, j)),
            pl.BlockSpec((1, 1, tn), lambda l, j: (l, 0, j)),
        ],
        out_specs=pl.BlockSpec((1, rows, tn), lambda l, j: (l, 0, j)),
        out_shape=jax.ShapeDtypeStruct((DEPTH, rows, n), F32),
        compiler_params=_params("parallel", "parallel"),
        name="ada_modulation",
    )(cvec, ada_w, ada_b.reshape(DEPTH, 1, n))


def _nmm_kernel(x_ref, g_ref, sh_ref, sc_ref, w_ref, o_ref, h_ref):
    @pl.when(pl.program_id(1) == 0)
    def _():
        for r0 in range(0, x_ref.shape[0], NORM_ROWS):
            h = _rms(x_ref[r0:r0 + NORM_ROWS, :]) * g_ref[...]
            h = h * (1.0 + sc_ref[0]) + sh_ref[0]
            h_ref[r0:r0 + NORM_ROWS, :] = h.astype(BF16)

    w = w_ref[...].astype(BF16)
    o_ref[...] = jnp.dot(h_ref[...], w, preferred_element_type=F32).astype(o_ref.dtype)


def norm_mod_matmul(x, gain, shift, scale, w, n, *, tm, tn):
    t, d = x.shape
    nb = shift.shape[0]
    assert t % (tm * nb) == 0 and n % tn == 0 and tm % NORM_ROWS == 0, (t, tm, nb, n, tn)
    bpb = (t // tm) // nb
    return pl.pallas_call(
        _nmm_kernel,
        grid=(t // tm, n // tn),
        in_specs=[
            pl.BlockSpec((tm, d), lambda i, j: (i, 0)),
            pl.BlockSpec((1, d), lambda i, j: (0, 0)),
            pl.BlockSpec((1, 1, d), lambda i, j: (i // bpb, 0, 0)),
            pl.BlockSpec((1, 1, d), lambda i, j: (i // bpb, 0, 0)),
            pl.BlockSpec((d, tn), lambda i, j: (0, j)),
        ],
        out_specs=pl.BlockSpec((tm, tn), lambda i, j: (i, j)),
        out_shape=jax.ShapeDtypeStruct((t, n), BF16),
        scratch_shapes=[pltpu.VMEM((tm, d), BF16)],
        compiler_params=_params("parallel", "arbitrary"),
        name="norm_mod_matmul",
    )(x, gain.reshape(1, d), shift, scale, w)


def _group_scan(a, b, reverse):
    rows, c = a.shape
    a = a.reshape(rows // SUBLANES, SUBLANES, c)
    b = b.reshape(rows // SUBLANES, SUBLANES, c)
    sub = lax.broadcasted_iota(jnp.int32, a.shape, 1)
    for sh in (1, 2, 4):
        if reverse:
            keep = sub < SUBLANES - sh
            amount = SUBLANES - sh
        else:
            keep = sub >= sh
            amount = sh
        a_sh = jnp.where(keep, pltpu.roll(a, amount, 1), 1.0)
        b_sh = jnp.where(keep, pltpu.roll(b, amount, 1), 0.0)
        b = a * b_sh + b
        a = a * a_sh
    return a.reshape(rows, c), b.reshape(rows, c)


def _dwconv(x, cw, cb):
    n = x.shape[0]
    rows = lax.broadcasted_iota(jnp.int32, x.shape, 0)
    xm2 = jnp.where(rows >= 2, pltpu.roll(x, 2, 0), 0.0)
    xm1 = jnp.where(rows >= 1, pltpu.roll(x, 1, 0), 0.0)
    xp1 = jnp.where(rows < n - 1, pltpu.roll(x, n - 1, 0), 0.0)
    return cb + xm2 * cw[0:1] + xm1 * cw[1:2] + x * cw[2:3] + xp1 * cw[3:4]


def _gelu_tanh(x):
    return 0.5 * x * (1.0 + jnp.tanh(math.sqrt(2.0 / math.pi) * (x + 0.044715 * (x * x * x))))


def _rglru_kernel(*refs, n_c, n_x, ctx_out):
    if ctx_out:
        (xrx_ref, grx_ref, xrc_ref, grc_ref, cw_ref, cb_ref, wg_ref, gb_ref, lam_ref,
         yx_ref, yc_ref, xc_s, a0_s, b0_s, a1_s, b1_s) = refs
    else:
        (xrx_ref, grx_ref, xrc_ref, cw_ref, cb_ref, wg_ref, gb_ref, lam_ref,
         yx_ref, xc_s, a0_s, b0_s, a1_s, b1_s) = refs
    c = RNN_CB
    n_all = n_c + n_x
    cw = cw_ref[...]
    cb = cb_ref[...]

    xc_s[0:n_c, :] = _dwconv(xrc_ref[...].astype(F32), cw, cb)
    xc_s[n_c:n_all, :] = _dwconv(xrx_ref[...].astype(F32), cw, cb)

    lam = lam_ref[0]
    half_neg_sp = (-0.5 * LRU_C) * jax.nn.softplus(-lam)

    def coeff_chunk(ci, carry):
        r0 = pl.multiple_of(ci * SCAN_CHUNK, SCAN_CHUNK)
        xc = xc_s[pl.ds(r0, SCAN_CHUNK), :]
        t = jnp.tanh(jnp.dot(xc.astype(BF16), wg_ref[0], preferred_element_type=F32) + gb_ref[0])
        half_xc = 0.5 * xc
        for d, (a_s, b_s) in enumerate(((a0_s, b0_s), (a1_s, b1_s))):
            t_r = t[:, (2 * d) * c:(2 * d + 1) * c]
            t_i = t[:, (2 * d + 1) * c:(2 * d + 2) * c]
            half_sp = half_neg_sp[:, d * c:(d + 1) * c]
            a = jnp.exp(half_sp + half_sp * t_r)
            one_m_a2 = 1.0 - a * a
            mult = jnp.where(one_m_a2 > 0.0, one_m_a2 * lax.rsqrt(one_m_a2), 0.0)
            bb = mult * (half_xc + half_xc * t_i)
            a_cum, b_cum = _group_scan(a, bb, reverse=(d == 1))
            a_s[pl.ds(r0, SCAN_CHUNK), :] = a_cum
            b_s[pl.ds(r0, SCAN_CHUNK), :] = b_cum
        return carry

    lax.fori_loop(0, n_all // SCAN_CHUNK, coeff_chunk, 0)

    g_c = n_c // SUBLANES
    g_all = n_all // SUBLANES

    def carry_step(k, carry):
        hf, hr = carry
        rf = pl.multiple_of(k * SUBLANES, SUBLANES)
        h = a0_s[pl.ds(rf, SUBLANES), :] * hf + b0_s[pl.ds(rf, SUBLANES), :]
        a0_s[pl.ds(rf, SUBLANES), :] = h
        hf = h[SUBLANES - 1:SUBLANES, :]
        kr = jnp.where(k < g_c, g_c - 1 - k, g_all + g_c - 1 - k)
        rr = pl.multiple_of(kr * SUBLANES, SUBLANES)
        h = a1_s[pl.ds(rr, SUBLANES), :] * hr + b1_s[pl.ds(rr, SUBLANES), :]
        a1_s[pl.ds(rr, SUBLANES), :] = h
        hr = h[0:1, :]
        return hf, hr

    zero = jnp.zeros((1, c), F32)
    lax.fori_loop(0, g_all, carry_step, (zero, zero), unroll=4)

    yx = (a0_s[n_c:n_all, :] + a1_s[n_c:n_all, :]) * _gelu_tanh(grx_ref[...].astype(F32))
    yx_ref[...] = yx.astype(yx_ref.dtype)
    if ctx_out:
        yc = (a0_s[0:n_c, :] + a1_s[0:n_c, :]) * _gelu_tanh(grc_ref[...].astype(F32))
        yc_ref[...] = yc.astype(yc_ref.dtype)


def rglru(proj_x, proj_c, conv_w, conv_b, wg, gb, lam, *, xr_col, gr_col, c_xr_col, c_gr_col, ctx_out):
    c = RNN_CB
    n_cb = D_RNN // c
    xr_b, gr_b, cxr_b = xr_col // c, gr_col // c, c_xr_col // c
    in_specs = [
        pl.BlockSpec((SEQ, c), lambda b, j: (b, xr_b + j)),
        pl.BlockSpec((SEQ, c), lambda b, j: (b, gr_b + j)),
        pl.BlockSpec((CTX_LEN, c), lambda b, j: (b, cxr_b + j)),
    ]
    args = [proj_x, proj_x, proj_c]
    if ctx_out:
        cgr_b = c_gr_col // c
        in_specs.append(pl.BlockSpec((CTX_LEN, c), lambda b, j: (b, cgr_b + j)))
        args.append(proj_c)
    in_specs += [
        pl.BlockSpec((CONV_W, c), lambda b, j: (0, j)),
        pl.BlockSpec((1, c), lambda b, j: (0, j)),
        pl.BlockSpec((1, c, 4 * c), lambda b, j: (j, 0, 0)),
        pl.BlockSpec((1, 1, 4 * c), lambda b, j: (j, 0, 0)),
        pl.BlockSpec((1, 1, 2 * c), lambda b, j: (j, 0, 0)),
    ]
    args += [conv_w, conv_b.reshape(1, D_RNN), wg, gb, lam]
    out_specs = [pl.BlockSpec((SEQ, c), lambda b, j: (b, j))]
    out_shape = [jax.ShapeDtypeStruct((BATCH * SEQ, D_RNN), BF16)]
    if ctx_out:
        out_specs.append(pl.BlockSpec((CTX_LEN, c), lambda b, j: (b, j)))
        out_shape.append(jax.ShapeDtypeStruct((BATCH * CTX_LEN, D_RNN), BF16))
    n_all = SEQ + CTX_LEN
    out = pl.pallas_call(
        functools.partial(_rglru_kernel, n_c=CTX_LEN, n_x=SEQ, ctx_out=ctx_out),
        grid=(BATCH, n_cb),
        in_specs=in_specs,
        out_specs=out_specs,
        out_shape=out_shape,
        scratch_shapes=[pltpu.VMEM((n_all, c), F32) for _ in range(5)],
        compiler_params=_params("parallel", "parallel"),
        name="rglru",
    )(*args)
    return out if ctx_out else (out[0], None)


def _rope(t, cos, sin_signed):
    lane = lax.broadcasted_iota(jnp.int32, t.shape, 1)
    first_half = (lane % (2 * ROPE_FREQS)) < ROPE_FREQS
    partner = jnp.where(first_half, pltpu.roll(t, LANES - ROPE_FREQS, 1), pltpu.roll(t, ROPE_FREQS, 1))
    return t * cos + partner * sin_signed


def _attn_kernel(*refs, tq, n_c, n_x, lam_init):
    if n_x:
        (q_ref, kx_ref, vx_ref, kc_ref, vc_ref, cos_ref, sin_ref, dl_ref, sg_ref,
         o_ref, k_s, v_s, lam_s) = refs
    else:
        q_ref, kc_ref, vc_ref, dl_ref, sg_ref, o_ref, k_s, v_s, lam_s = refs
    qi = pl.program_id(2)

    @pl.when(qi == 0)
    def _():
        k_s[0:n_c, :] = kc_ref[...]
        v_s[0:n_c, 0:LANES] = vc_ref[...]
        if n_x:
            kx = _rope(kx_ref[...].astype(F32), cos_ref[...], sin_ref[...])
            k_s[n_c:n_c + n_x, :] = kx.astype(BF16)
            v_s[n_c:n_c + n_x, 0:LANES] = vx_ref[...]
        v_s[:, LANES:2 * LANES] = jnp.ones((n_c + n_x, LANES), BF16)
        dl = dl_ref[...]
        e1 = jnp.exp(jnp.sum(dl[0:1] * dl[1:2], axis=-1, keepdims=True))
        e2 = jnp.exp(jnp.sum(dl[2:3] * dl[3:4], axis=-1, keepdims=True))
        lam_s[...] = jnp.broadcast_to(e1 - e2 + lam_init, lam_s.shape)

    q = q_ref[...].astype(F32)
    if n_x:
        r0 = pl.multiple_of(qi * tq, tq)
        q = _rope(q, cos_ref[pl.ds(r0, tq), :], sin_ref[pl.ds(r0, tq), :])
    q = q * (HEAD_DIM ** -0.5 * math.log2(math.e))
    lane = lax.broadcasted_iota(jnp.int32, (ATTN_SUB, LANES), 1)
    lam = lam_s[0:1, 0:1]
    scores = []
    for sb in range(tq // ATTN_SUB):
        qs = q[sb * ATTN_SUB:(sb + 1) * ATTN_SUB]
        qq = jnp.concatenate([jnp.where(lane < HEAD_DIM, qs, 0.0), jnp.where(lane >= HEAD_DIM, qs, 0.0)], axis=0)
        scores.append(lax.dot_general(qq.astype(BF16), k_s[...], (((1,), (1,)), ((), ())),
                                      preferred_element_type=F32))
    for sb, s in enumerate(scores):
        e = jnp.exp2(s - jnp.max(s, axis=-1, keepdims=True))
        ov = jnp.dot(e.astype(BF16), v_s[...], preferred_element_type=F32)
        ov = ov[:, 0:LANES] / ov[:, LANES:2 * LANES]
        o = ov[0:ATTN_SUB] - lam * ov[ATTN_SUB:2 * ATTN_SUB]
        o_ref[sb * ATTN_SUB:(sb + 1) * ATTN_SUB, :] = (_rms(o) * sg_ref[...] * (1.0 - lam_init)).astype(o_ref.dtype)


def diff_attention(q_arr, q_col, n_q, kvx, kvc, cols, cos, sin_signed, diff_lambda, subln_g, *, lam_init, tq):
    kx_col, vx_col, kc_col, vc_col = cols
    qb, kxb, vxb, kcb, vcb = (v // LANES for v in (q_col, kx_col, vx_col, kc_col, vc_col))
    n_x = SEQ if kvx is not None else 0
    n_qb = n_q // tq
    in_specs = [pl.BlockSpec((tq, LANES), lambda b, h, i: (b * n_qb + i, qb + h))]
    args = [q_arr]
    if n_x:
        in_specs += [
            pl.BlockSpec((SEQ, LANES), lambda b, h, i: (b, kxb + h)),
            pl.BlockSpec((SEQ, LANES), lambda b, h, i: (b, vxb + h)),
        ]
        args += [kvx, kvx]
    in_specs += [
        pl.BlockSpec((CTX_LEN, LANES), lambda b, h, i: (b, kcb + h)),
        pl.BlockSpec((CTX_LEN, LANES), lambda b, h, i: (b, vcb + h)),
    ]
    args += [kvc, kvc]
    if n_x:
        in_specs += [
            pl.BlockSpec((SEQ, LANES), lambda b, h, i: (0, 0)),
            pl.BlockSpec((SEQ, LANES), lambda b, h, i: (0, 0)),
        ]
        args += [cos, sin_signed]
    in_specs += [
        pl.BlockSpec((4, HEAD_DIM), lambda b, h, i: (0, 0)),
        pl.BlockSpec((1, V_DIM), lambda b, h, i: (0, 0)),
    ]
    args += [diff_lambda, subln_g.reshape(1, V_DIM)]
    n_kv = CTX_LEN + n_x
    return pl.pallas_call(
        functools.partial(_attn_kernel, tq=tq, n_c=CTX_LEN, n_x=n_x, lam_init=lam_init),
        grid=(BATCH, N_HEADS, n_qb),
        in_specs=in_specs,
        out_specs=pl.BlockSpec((tq, LANES), lambda b, h, i: (b * n_qb + i, h)),
        out_shape=jax.ShapeDtypeStruct((BATCH * n_q, ATTN_W), BF16),
        scratch_shapes=[
            pltpu.VMEM((n_kv, LANES), BF16),
            pltpu.VMEM((n_kv, 2 * LANES), BF16),
            pltpu.VMEM((SUBLANES, LANES), F32),
        ],
        compiler_params=_params("parallel", "parallel", "arbitrary"),
        name="diff_attention",
    )(*args)


def _merge_kernel(yr_ref, ya_ref, gr0_ref, gr1_ref, ga0_ref, ga1_ref, x_ref, g1_ref, ng_ref, wr_ref, wa_ref, wo_ref,
                  o_ref):
    y_rnn = jnp.dot(yr_ref[...], wr_ref[...], preferred_element_type=F32)
    y_attn = jnp.dot(ya_ref[...], wa_ref[...], preferred_element_type=F32)
    g_rnn = jax.nn.sigmoid(jnp.concatenate([gr0_ref[...], gr1_ref[...]], axis=1).astype(F32))
    g_attn = jax.nn.sigmoid(jnp.concatenate([ga0_ref[...], ga1_ref[...]], axis=1).astype(F32))
    m = (g_rnn * y_rnn + g_attn * y_attn).astype(BF16)
    mx = jnp.dot(m, wo_ref[...], preferred_element_type=F32)
    o_ref[...] = x_ref[...] + g1_ref[0] * (_rms(mx) * ng_ref[...])


def merge_out(y_rnn, y_attn, proj, x, g1, ng, w_proj_rnn, w_proj_attn, w_out, *, tm):
    t, d = x.shape
    nb = g1.shape[0]
    assert t % (tm * nb) == 0, (t, tm, nb)
    bpb = (t // tm) // nb
    gw = GATE_BLOCK_W
    grb, gab = COL_G_RNN // gw, COL_G_ATTN // gw
    return pl.pallas_call(
        _merge_kernel,
        grid=(t // tm,),
        in_specs=[
            pl.BlockSpec((tm, D_RNN), lambda i: (i, 0)),
            pl.BlockSpec((tm, ATTN_W), lambda i: (i, 0)),
            pl.BlockSpec((tm, gw), lambda i: (i, grb)),
            pl.BlockSpec((tm, gw), lambda i: (i, grb + 1)),
            pl.BlockSpec((tm, gw), lambda i: (i, gab)),
            pl.BlockSpec((tm, gw), lambda i: (i, gab + 1)),
            pl.BlockSpec((tm, d), lambda i: (i, 0)),
            pl.BlockSpec((1, 1, d), lambda i: (i // bpb, 0, 0)),
            pl.BlockSpec((1, d), lambda i: (0, 0)),
            pl.BlockSpec((D_RNN, d), lambda i: (0, 0)),
            pl.BlockSpec((ATTN_W, d), lambda i: (0, 0)),
            pl.BlockSpec((d, d), lambda i: (0, 0)),
        ],
        out_specs=pl.BlockSpec((tm, d), lambda i: (i, 0)),
        out_shape=jax.ShapeDtypeStruct((t, d), F32),
        compiler_params=_params("parallel"),
        name="merge_out",
    )(y_rnn, y_attn, proj, proj, proj, proj, x, g1, ng.reshape(1, d), w_proj_rnn, w_proj_attn, w_out)


def _swiglu_partial(h, wg, wu, wd):
    gate = jnp.dot(h, wg, preferred_element_type=F32)
    up = jnp.dot(h, wu, preferred_element_type=F32)
    act = (gate * jax.nn.sigmoid(gate) * up).astype(BF16)
    return jnp.dot(act, wd, preferred_element_type=F32)


def _ffn_kernel(x_ref, g_ref, sh_ref, sc_ref, g2_ref, ng_ref, wg_ref, wu_ref, wd_ref, o_ref, h_ref, acc_ref):
    f = pl.program_id(1)

    @pl.when(f == 0)
    def _():
        h = _rms(x_ref[...]) * g_ref[...]
        h = h * (1.0 + sc_ref[0]) + sh_ref[0]
        h_ref[...] = h.astype(BF16)
        acc_ref[...] = jnp.zeros_like(acc_ref)

    acc_ref[...] += _swiglu_partial(h_ref[...], wg_ref[...], wu_ref[...], wd_ref[...])

    @pl.when(f == pl.num_programs(1) - 1)
    def _():
        o_ref[...] = x_ref[...] + g2_ref[0] * (_rms(acc_ref[...]) * ng_ref[...])


def ffn_residual(x, gain, shift, scale, g2, ng, w_gu, w_down, *, tm, tf):
    t, d = x.shape
    ff = w_gu.shape[1] // 2
    nf = ff // tf
    nb = shift.shape[0]
    assert t % (tm * nb) == 0 and ff % tf == 0, (t, tm, nb, ff, tf)
    bpb = (t // tm) // nb
    mod_spec = pl.BlockSpec((1, 1, d), lambda i, f: (i // bpb, 0, 0))
    vec_spec = pl.BlockSpec((1, d), lambda i, f: (0, 0))
    return pl.pallas_call(
        _ffn_kernel,
        grid=(t // tm, nf),
        in_specs=[
            pl.BlockSpec((tm, d), lambda i, f: (i, 0)), vec_spec, mod_spec, mod_spec, mod_spec, vec_spec,
            pl.BlockSpec((d, tf), lambda i, f: (0, f)),
            pl.BlockSpec((d, tf), lambda i, f: (0, nf + f)),
            pl.BlockSpec((tf, d), lambda i, f: (f, 0)),
        ],
        out_specs=pl.BlockSpec((tm, d), lambda i, f: (i, 0)),
        out_shape=jax.ShapeDtypeStruct((t, d), F32),
        scratch_shapes=[pltpu.VMEM((tm, d), BF16), pltpu.VMEM((tm, d), F32)],
        compiler_params=_params("parallel", "arbitrary"),
        name="ffn_residual",
    )(x, gain.reshape(1, d), shift, scale, g2, ng.reshape(1, d), w_gu, w_gu, w_down)


MOE_TM = 512
MOE_TF = 1408
ROW_TILE = 256
DMA_UNROLL = 8
META_E0, META_E1, META_W0, META_W1, META_R0, META_R1 = range(6)


def _lane_pick(rec, lane, k):
    return jnp.sum(jnp.where(lane == k, rec, 0.0), axis=-1, keepdims=True)


def _router_kernel(x_ref, g_ref, sh_ref, sc_ref, w_ref, b_ref, h_ref, meta_ref, cnt_ref, carry_s):
    @pl.when(pl.program_id(0) == 0)
    def _():
        carry_s[...] = jnp.zeros_like(carry_s)

    h = _rms(x_ref[...]) * g_ref[...]
    h = h * (1.0 + sc_ref[0]) + sh_ref[0]
    h_ref[...] = h
    logits = jnp.dot(h, w_ref[...], preferred_element_type=F32, precision=lax.Precision.HIGHEST) + b_ref[...]
    lane = lax.broadcasted_iota(jnp.int32, logits.shape, 1).astype(F32)
    neg = jnp.float32(-jnp.inf)
    logits = jnp.where(lane < N_EXPERTS, logits, neg)
    m1 = jnp.max(logits, axis=-1, keepdims=True)
    i1 = jnp.min(jnp.where(logits == m1, lane, float(LANES)), axis=-1, keepdims=True)
    rest = jnp.where(lane == i1, neg, logits)
    m2 = jnp.max(rest, axis=-1, keepdims=True)
    i2 = jnp.min(jnp.where(rest == m2, lane, float(LANES)), axis=-1, keepdims=True)
    e2 = jnp.exp(m2 - m1)
    denom = 1.0 + e2

    tm = logits.shape[0]
    sel1 = lane == i1
    sel2 = lane == i2
    member = jnp.where(jnp.logical_or(sel1, sel2), 1.0, 0.0)
    row = lax.broadcasted_iota(jnp.int32, (tm, tm), 0)
    col = lax.broadcasted_iota(jnp.int32, (tm, tm), 1)
    lower = jnp.where(col < row, 1.0, 0.0).astype(BF16)
    before = jnp.dot(lower, member.astype(BF16), preferred_element_type=F32) + carry_s[0:1, :]
    r1 = jnp.sum(jnp.where(sel1, before, 0.0), axis=-1, keepdims=True)
    r2 = jnp.sum(jnp.where(sel2, before, 0.0), axis=-1, keepdims=True)
    carry_s[0:1, :] = carry_s[0:1, :] + jnp.sum(member, axis=0, keepdims=True)
    cnt_ref[...] = jnp.broadcast_to(carry_s[0:1, :], cnt_ref.shape)

    meta = jnp.zeros_like(logits)
    for k, v in ((META_E0, i1), (META_E1, i2), (META_W0, 1.0 / denom), (META_W1, e2 / denom),
                 (META_R0, r1), (META_R1, r2)):
        meta = jnp.where(lane == k, v, meta)
    meta_ref[...] = meta


def moe_router(x, gain, shift, scale, router_w, router_b, *, tm):
    t, d = x.shape
    nb = shift.shape[0]
    bpb = (t // tm) // nb
    w = jnp.zeros((d, LANES), F32).at[:, :N_EXPERTS].set(router_w)
    b = jnp.zeros((1, LANES), F32).at[0, :N_EXPERTS].set(router_b)
    return pl.pallas_call(
        _router_kernel,
        grid=(t // tm,),
        in_specs=[
            pl.BlockSpec((tm, d), lambda i: (i, 0)),
            pl.BlockSpec((1, d), lambda i: (0, 0)),
            pl.BlockSpec((1, 1, d), lambda i: (i // bpb, 0, 0)),
            pl.BlockSpec((1, 1, d), lambda i: (i // bpb, 0, 0)),
            pl.BlockSpec((d, LANES), lambda i: (0, 0)),
            pl.BlockSpec((1, LANES), lambda i: (0, 0)),
        ],
        out_specs=[
            pl.BlockSpec((tm, d), lambda i: (i, 0)),
            pl.BlockSpec((tm, LANES), lambda i: (i, 0)),
            pl.BlockSpec((SUBLANES, LANES), lambda i: (0, 0)),
        ],
        out_shape=[
            jax.ShapeDtypeStruct((t, d), F32),
            jax.ShapeDtypeStruct((t, LANES), F32),
            jax.ShapeDtypeStruct((SUBLANES, LANES), F32),
        ],
        scratch_shapes=[pltpu.VMEM((SUBLANES, LANES), F32)],
        compiler_params=_params("arbitrary"),
        name="moe_router",
    )(x, gain.reshape(1, d), shift, scale, w, b)


def _row_copy(src_ref, src_row, dst_ref, dst_row, sem):
    return pltpu.make_async_copy(src_ref.at[pl.ds(src_row, 1), :], dst_ref.at[pl.ds(dst_row, 1), :], sem)


def _dispatch_kernel(pos_ref, h_ref, hs_in_ref, hs_ref, sem):
    del hs_in_ref
    n = h_ref.shape[0]

    def issue(g, carry):
        for j in range(DMA_UNROLL):
            r = g * DMA_UNROLL + j
            _row_copy(h_ref, r, hs_ref, pos_ref[0, 0, 2 * r], sem).start(priority=0)
            _row_copy(h_ref, r, hs_ref, pos_ref[0, 0, 2 * r + 1], sem).start(priority=1)
        return carry

    lax.fori_loop(0, n // DMA_UNROLL, issue, 0)
    for _ in range(TOP_K):
        pltpu.make_async_copy(h_ref, hs_ref.at[pl.ds(0, n), :], sem).wait()


def moe_dispatch(h, pos, n_rows):
    t, d = h.shape
    return pl.pallas_call(
        _dispatch_kernel,
        grid=(t // ROW_TILE,),
        in_specs=[
            pl.BlockSpec((1, 1, 2 * ROW_TILE), lambda i: (i, 0, 0), memory_space=pltpu.SMEM),
            pl.BlockSpec((ROW_TILE, d), lambda i: (i, 0)),
            pl.BlockSpec(memory_space=pl.ANY),
        ],
        out_specs=pl.BlockSpec(memory_space=pl.ANY),
        out_shape=jax.ShapeDtypeStruct((n_rows, d), F32),
        scratch_shapes=[pltpu.SemaphoreType.DMA(())],
        input_output_aliases={2: 0},
        compiler_params=_params("arbitrary"),
        name="moe_dispatch",
    )(pos, h, jnp.zeros((n_rows, d), F32))


def _grouped_ffn_kernel(te_ref, na_ref, hs_ref, wg_ref, wu_ref, wd_ref, ys_ref, h_s, acc_s):
    del te_ref
    f = pl.program_id(1)
    active = pl.program_id(0) < na_ref[0]

    @pl.when(jnp.logical_not(active))
    def _():
        ys_ref[...] = jnp.zeros_like(ys_ref)

    @pl.when(active)
    def _():
        @pl.when(f == 0)
        def _():
            h_s[...] = hs_ref[...].astype(BF16)
            acc_s[...] = jnp.zeros_like(acc_s)

        acc_s[...] += _swiglu_partial(h_s[...], wg_ref[0], wu_ref[0], wd_ref[0])

        @pl.when(f == pl.num_programs(1) - 1)
        def _():
            ys_ref[...] = acc_s[...]


def moe_grouped_ffn(hs, tile_expert, n_active, w_gu, w_down):
    n_rows, d = hs.shape
    ff = w_gu.shape[2] // 2
    nf = ff // MOE_TF
    n_tiles = n_rows // MOE_TM

    def row_map(t, f, te, na):
        return (jnp.maximum(jnp.minimum(t, na[0] - 1), 0), 0)

    def f_eff(t, f, na):
        return jnp.where(t < na[0], f, nf - 1)

    return pl.pallas_call(
        _grouped_ffn_kernel,
        grid_spec=pltpu.PrefetchScalarGridSpec(
            num_scalar_prefetch=2,
            grid=(n_tiles, nf),
            in_specs=[
                pl.BlockSpec((MOE_TM, d), row_map),
                pl.BlockSpec((1, d, MOE_TF), lambda t, f, te, na: (te[t], 0, f_eff(t, f, na))),
                pl.BlockSpec((1, d, MOE_TF), lambda t, f, te, na: (te[t], 0, nf + f_eff(t, f, na))),
                pl.BlockSpec((1, MOE_TF, d), lambda t, f, te, na: (te[t], f_eff(t, f, na), 0)),
            ],
            out_specs=pl.BlockSpec((MOE_TM, d), lambda t, f, te, na: (t, 0)),
            scratch_shapes=[pltpu.VMEM((MOE_TM, d), BF16), pltpu.VMEM((MOE_TM, d), F32)],
        ),
        out_shape=jax.ShapeDtypeStruct((n_rows, d), F32),
        compiler_params=_params("arbitrary", "arbitrary"),
        name="moe_grouped_ffn",
    )(tile_expert, n_active, hs, w_gu, w_gu, w_down)


def _combine_kernel(pos_ref, ys_ref, x_ref, meta_ref, g2_ref, ng_ref, o_ref, a_s, b_s, sem):
    n = x_ref.shape[0]

    def issue(g, carry):
        for j in range(DMA_UNROLL):
            r = g * DMA_UNROLL + j
            _row_copy(ys_ref, pos_ref[0, 0, 2 * r], a_s, r, sem).start(priority=0)
            _row_copy(ys_ref, pos_ref[0, 0, 2 * r + 1], b_s, r, sem).start(priority=1)
        return carry

    lax.fori_loop(0, n // DMA_UNROLL, issue, 0)
    pltpu.make_async_copy(ys_ref.at[pl.ds(0, n), :], a_s, sem).wait()
    pltpu.make_async_copy(ys_ref.at[pl.ds(0, n), :], b_s, sem).wait()

    meta = meta_ref[...]
    lane = lax.broadcasted_iota(jnp.int32, meta.shape, 1)
    mixed = _lane_pick(meta, lane, META_W0) * a_s[...] + _lane_pick(meta, lane, META_W1) * b_s[...]
    o_ref[...] = x_ref[...] + g2_ref[0] * (_rms(mixed) * ng_ref[...])


def moe_combine(ys, pos, meta, x, g2, ng):
    t, d = x.shape
    nb = g2.shape[0]
    bpb = (t // ROW_TILE) // nb
    return pl.pallas_call(
        _combine_kernel,
        grid=(t // ROW_TILE,),
        in_specs=[
            pl.BlockSpec((1, 1, 2 * ROW_TILE), lambda i: (i, 0, 0), memory_space=pltpu.SMEM),
            pl.BlockSpec(memory_space=pl.ANY),
            pl.BlockSpec((ROW_TILE, d), lambda i: (i, 0)),
            pl.BlockSpec((ROW_TILE, LANES), lambda i: (i, 0)),
            pl.BlockSpec((1, 1, d), lambda i: (i // bpb, 0, 0)),
            pl.BlockSpec((1, d), lambda i: (0, 0)),
        ],
        out_specs=pl.BlockSpec((ROW_TILE, d), lambda i: (i, 0)),
        out_shape=jax.ShapeDtypeStruct((t, d), F32),
        scratch_shapes=[pltpu.VMEM((ROW_TILE, d), F32), pltpu.VMEM((ROW_TILE, d), F32),
                        pltpu.SemaphoreType.DMA(())],
        compiler_params=_params("arbitrary"),
        name="moe_combine",
    )(pos, ys, x, meta, g2, ng.reshape(1, d))


def moe_residual(x, gain, shift, scale, g2, ng, router_w, router_b, w_gu, w_down):
    t, d = x.shape
    h, meta, counts = moe_router(x, gain, shift, scale, router_w, router_b, tm=512)
    n_rows = TOP_K * t + N_EXPERTS * MOE_TM
    n_tiles = n_rows // MOE_TM
    cnt = counts[0, :N_EXPERTS].astype(jnp.int32)
    padded = (cnt + MOE_TM - 1) // MOE_TM * MOE_TM
    seg_end = jnp.cumsum(padded)
    seg_start = seg_end - padded
    experts = jnp.arange(N_EXPERTS, dtype=jnp.int32)

    def position(e_lane, r_lane):
        e = meta[:, e_lane].astype(jnp.int32)
        start = jnp.sum(jnp.where(e[:, None] == experts[None, :], seg_start[None, :], 0), axis=-1)
        return start + meta[:, r_lane].astype(jnp.int32)

    pos = jnp.stack([position(META_E0, META_R0), position(META_E1, META_R1)], axis=-1)
    pos = pos.reshape(t // ROW_TILE, 1, 2 * ROW_TILE)
    n_active = seg_end[-1:] // MOE_TM
    tile_ids = jnp.arange(n_tiles, dtype=jnp.int32)
    tile_expert = jnp.sum((jnp.minimum(tile_ids, n_active - 1)[:, None] * MOE_TM >= seg_end[None, :]), axis=-1)
    tile_expert = jnp.minimum(tile_expert, N_EXPERTS - 1).astype(jnp.int32)

    hs = moe_dispatch(h, pos, n_rows)
    ys = moe_grouped_ffn(hs, tile_expert, n_active.astype(jnp.int32), w_gu, w_down)
    return moe_combine(ys, pos, meta, x, g2, ng)


def _rope_tables():
    pos = jnp.arange(SEQ)
    inv_freq = jnp.power(ROPE_THETA, -jnp.arange(ROPE_FREQS, dtype=F32) / ROPE_FREQS)
    ang_r = (pos // GRID_W).astype(F32)[:, None] * inv_freq
    ang_c = (pos % GRID_W).astype(F32)[:, None] * inv_freq
    cos = jnp.concatenate([jnp.cos(ang_r)] * 2 + [jnp.cos(ang_c)] * 2, axis=-1)
    sin = jnp.concatenate([-jnp.sin(ang_r), jnp.sin(ang_r), -jnp.sin(ang_c), jnp.sin(ang_c)], axis=-1)
    return jnp.tile(cos, (1, 2)), jnp.tile(sin, (1, 2))


def _gate_layout(gate_w, gate_b, lam):
    c = RNN_CB
    n_cb = D_RNN // c
    per = c // RNN_BLOCK_W
    gw = gate_w.reshape(2, 2, n_cb, per, RNN_BLOCK_W, RNN_BLOCK_W)
    eye = jnp.eye(per, dtype=gate_w.dtype)
    bd = gw[:, :, :, :, :, None, :] * eye[None, None, None, :, None, :, None]
    bd = bd.reshape(2, 2, n_cb, c, c)
    wg = (0.5 * jnp.transpose(bd, (2, 3, 0, 1, 4))).reshape(n_cb, c, 4 * c).astype(BF16)
    gb = 0.5 * jnp.transpose(gate_b.reshape(2, 2, n_cb, c), (2, 0, 1, 3)).reshape(n_cb, 1, 4 * c)
    lm = jnp.transpose(lam.reshape(2, n_cb, c), (1, 0, 2)).reshape(n_cb, 1, 2 * c)
    return wg, gb, lm


def kernel(x, c, ctx, c_ctx, ada_w, ada_b, norm_g, w_in, conv_w, conv_b, lru_gate_w, lru_gate_b, lru_lambda,
           diff_lambda, subln_g, w_proj_rnn, w_proj_attn, w_out, ffn_w_gu, ffn_w_down, router_w, router_b,
           moe_w_gu, moe_w_down):
    xt = x.reshape(BATCH * SEQ, D_MODEL)
    ct = ctx.reshape(BATCH * CTX_LEN, D_MODEL)
    cos, sin_signed = _rope_tables()

    cvec = jnp.concatenate([c, c_ctx[None, :], jnp.zeros((2 * SUBLANES - BATCH - 1, D_MODEL), F32)], axis=0)
    mod = ada_modulation(cvec, ada_w, ada_b)

    for l in range(DEPTH):
        last = l == DEPTH - 1
        lam_init = 0.8 - 0.6 * math.exp(-0.3 * l)
        mx = mod[l, :BATCH].reshape(BATCH, 1, 6, D_MODEL)
        mc = mod[l, BATCH:BATCH + 1].reshape(1, 1, 6, D_MODEL)
        sh1x, sc1x, g1x, sh2x, sc2x, g2x = (mx[:, :, i] for i in range(6))
        sh1c, sc1c, g1c, sh2c, sc2c, g2c = (mc[:, :, i] for i in range(6))

        wg, gb, lm = _gate_layout(lru_gate_w[l], lru_gate_b[l], lru_lambda[l])

        proj_x = norm_mod_matmul(xt, norm_g[l, 0], sh1x, sc1x, w_in[l], IN_W, tm=2048, tn=768)
        if last:
            proj_c = norm_mod_matmul(ct, norm_g[l, 0], sh1c, sc1c, w_in[l], CTX_STATE_W, tm=2048, tn=256)
        else:
            proj_c = norm_mod_matmul(ct, norm_g[l, 0], sh1c, sc1c, w_in[l], IN_W, tm=2048, tn=768)

        y_rnn_x, y_rnn_c = rglru(proj_x, proj_c, conv_w[l], conv_b[l], wg, gb, lm, xr_col=COL_XR, gr_col=COL_GR,
                                 c_xr_col=COL_XR, c_gr_col=COL_GR, ctx_out=not last)
        y_attn_x = diff_attention(proj_x, COL_Q, SEQ, proj_x, proj_c, (COL_K, COL_V, COL_K, COL_V), cos,
                                  sin_signed, diff_lambda[l], subln_g[l], lam_init=lam_init, tq=1024)
        wr = w_proj_rnn[l].astype(BF16)
        wa = w_proj_attn[l].astype(BF16)
        wo = w_out[l].astype(BF16)
        xt = merge_out(y_rnn_x, y_attn_x, proj_x, xt, g1x, norm_g[l, 1], wr, wa, wo, tm=512)
        if not last:
            y_attn_c = diff_attention(proj_c, COL_Q, CTX_LEN, None, proj_c, (0, 0, COL_K, COL_V), None, None,
                                      diff_lambda[l], subln_g[l], lam_init=lam_init, tq=CTX_LEN)
            ct = merge_out(y_rnn_c, y_attn_c, proj_c, ct, g1c, norm_g[l, 1], wr, wa, wo, tm=512)

        if l % 2 == 0:
            w_gu = ffn_w_gu[l // 2].astype(BF16)
            w_dn = ffn_w_down[l // 2].astype(BF16)
            xt = ffn_residual(xt, norm_g[l, 2], sh2x, sc2x, g2x, norm_g[l, 3], w_gu, w_dn, tm=512, tf=1408)
            if not last:
                ct = ffn_residual(ct, norm_g[l, 2], sh2c, sc2c, g2c, norm_g[l, 3], w_gu, w_dn, tm=512, tf=1408)
        else:
            w_gu = moe_w_gu[l // 2].astype(BF16)
            w_dn = moe_w_down[l // 2].astype(BF16)
            rw, rb = router_w[l // 2], router_b[l // 2]
            xt = moe_residual(xt, norm_g[l, 2], sh2x, sc2x, g2x, norm_g[l, 3], rw, rb, w_gu, w_dn)
            if not last:
                ct = moe_residual(ct, norm_g[l, 2], sh2c, sc2c, g2c, norm_g[l, 3], rw, rb, w_gu, w_dn)
    return xt.reshape(BATCH, SEQ, D_MODEL)
```

```python
import functools
import math

import jax
import jax.numpy as jnp
from jax import lax
from jax.experimental import pallas as pl
from jax.experimental.pallas import tpu as pltpu

F32 = jnp.float32
BF16 = jnp.bfloat16

D_MODEL = 1024
BATCH = 8
SEQ = 2048
DEPTH = 2
CTX_LEN = 256
GRID_W = 64
EPS = 1e-6
D_RNN = 1280
RNN_BLOCKS = 20
RNN_BLOCK_W = D_RNN // RNN_BLOCKS
CONV_W = 4
LRU_C = 8.0
N_HEADS = 8
HEAD_DIM = 64
V_DIM = 2 * HEAD_DIM
QK_W = N_HEADS * 2 * HEAD_DIM
ATTN_W = N_HEADS * V_DIM
ROPE_THETA = 10000.0
ROPE_FREQS = HEAD_DIM // 4
D_FF = 2816
N_EXPERTS = 8
TOP_K = 2

LANES = 128
SUBLANES = 8
VMEM_LIMIT_BYTES = 52 * 1024 * 1024

COL_XR = 0
COL_K = COL_XR + D_RNN
COL_V = COL_K + QK_W
COL_GR = COL_V + ATTN_W
COL_Q = COL_GR + D_RNN
COL_G_RNN = COL_Q + QK_W
COL_G_ATTN = COL_G_RNN + D_MODEL
IN_W = COL_G_ATTN + D_MODEL
CTX_STATE_W = COL_GR
GATE_BLOCK_W = 512

RNN_CB = 256
SCAN_CHUNK = 256
ATTN_SUB = 128
NORM_ROWS = 512


def _params(*sem):
    return pltpu.CompilerParams(dimension_semantics=sem, vmem_limit_bytes=VMEM_LIMIT_BYTES)


def _rms(x):
    return x * lax.rsqrt(jnp.mean(x * x, axis=-1, keepdims=True) + EPS)


def _ada_kernel(c_ref, w_ref, b_ref, o_ref):
    c = c_ref[...]
    s = c * jax.nn.sigmoid(c)
    o_ref[0] = jnp.dot(s, w_ref[0], preferred_element_type=F32, precision=lax.Precision.HIGHEST) + b_ref[0]


def ada_modulation(cvec, ada_w, ada_b):
    rows = cvec.shape[0]
    tn = 1536
    n = 6 * D_MODEL
    return pl.pallas_call(
        _ada_kernel,
        grid=(DEPTH, n // tn),
        in_specs=[
            pl.BlockSpec((rows, D_MODEL), lambda l, j: (0, 0)),
            pl.BlockSpec((1, D_MODEL, tn), lambda l, j: (l, 0, j)),
            pl.BlockSpec((1, 1, tn), lambda l, j: (l, 0, j)),
        ],
        out_specs=pl.BlockSpec((1, rows, tn), lambda l, j: (l, 0, j)),
        out_shape=jax.ShapeDtypeStruct((DEPTH, rows, n), F32),
        compiler_params=_params("parallel", "parallel"),
        name="ada_modulation",
    )(cvec, ada_w, ada_b.reshape(DEPTH, 1, n))


def _nmm_kernel(x_ref, g_ref, sh_ref, sc_ref, w_ref, o_ref, h_ref):
    @pl.when(pl.program_id(1) == 0)
    def _():
        for r0 in range(0, x_ref.shape[0], NORM_ROWS):
            h = _rms(x_ref[r0:r0 + NORM_ROWS, :]) * g_ref[...]
            h = h * (1.0 + sc_ref[0]) + sh_ref[0]
            h_ref[r0:r0 + NORM_ROWS, :] = h.astype(BF16)

    w = w_ref[...].astype(BF16)
    o_ref[...] = jnp.dot(h_ref[...], w, preferred_element_type=F32).astype(o_ref.dtype)


def norm_mod_matmul(x, gain, shift, scale, w, n, *, tm, tn):
    t, d = x.shape
    nb = shift.shape[0]
    assert t % (tm * nb) == 0 and n % tn == 0 and tm % NORM_ROWS == 0, (t, tm, nb, n, tn)
    bpb = (t // tm) // nb
    return pl.pallas_call(
        _nmm_kernel,
        grid=(t // tm, n // tn),
        in_specs=[
            pl.BlockSpec((tm, d), lambda i, j: (i, 0)),
            pl.BlockSpec((1, d), lambda i, j: (0, 0)),
            pl.BlockSpec((1, 1, d), lambda i, j: (i // bpb, 0, 0)),
            pl.BlockSpec((1, 1, d), lambda i, j: (i // bpb, 0, 0)),
            pl.BlockSpec((d, tn), lambda i, j: (0, j)),
        ],
        out_specs=pl.BlockSpec((tm, tn), lambda i, j: (i, j)),
        out_shape=jax.ShapeDtypeStruct((t, n), BF16),
        scratch_shapes=[pltpu.VMEM((tm, d), BF16)],
        compiler_params=_params("parallel", "arbitrary"),
        name="norm_mod_matmul",
    )(x, gain.reshape(1, d), shift, scale, w)


def _group_scan(a, b, reverse):
    rows, c = a.shape
    a = a.reshape(rows // SUBLANES, SUBLANES, c)
    b = b.reshape(rows // SUBLANES, SUBLANES, c)
    sub = lax.broadcasted_iota(jnp.int32, a.shape, 1)
    for sh in (1, 2, 4):
        if reverse:
            keep = sub < SUBLANES - sh
            amount = SUBLANES - sh
        else:
            keep = sub >= sh
            amount = sh
        a_sh = jnp.where(keep, pltpu.roll(a, amount, 1), 1.0)
        b_sh = jnp.where(keep, pltpu.roll(b, amount, 1), 0.0)
        b = a * b_sh + b
        a = a * a_sh
    return a.reshape(rows, c), b.reshape(rows, c)


def _dwconv(x, cw, cb):
    n = x.shape[0]
    rows = lax.broadcasted_iota(jnp.int32, x.shape, 0)
    xm2 = jnp.where(rows >= 2, pltpu.roll(x, 2, 0), 0.0)
    xm1 = jnp.where(rows >= 1, pltpu.roll(x, 1, 0), 0.0)
    xp1 = jnp.where(rows < n - 1, pltpu.roll(x, n - 1, 0), 0.0)
    return cb + xm2 * cw[0:1] + xm1 * cw[1:2] + x * cw[2:3] + xp1 * cw[3:4]


def _gelu_tanh(x):
    return 0.5 * x * (1.0 + jnp.tanh(math.sqrt(2.0 / math.pi) * (x + 0.044715 * (x * x * x))))


def _rglru_kernel(*refs, n_c, n_x, ctx_out):
    if ctx_out:
        (xrx_ref, grx_ref, xrc_ref, grc_ref, cw_ref, cb_ref, wg_ref, gb_ref, lam_ref,
         yx_ref, yc_ref, xc_s, a0_s, b0_s, a1_s, b1_s) = refs
    else:
        (xrx_ref, grx_ref, xrc_ref, cw_ref, cb_ref, wg_ref, gb_ref, lam_ref,
         yx_ref, xc_s, a0_s, b0_s, a1_s, b1_s) = refs
    c = RNN_CB
    n_all = n_c + n_x
    cw = cw_ref[...]
    cb = cb_ref[...]

    xc_s[0:n_c, :] = _dwconv(xrc_ref[...].astype(F32), cw, cb)
    xc_s[n_c:n_all, :] = _dwconv(xrx_ref[...].astype(F32), cw, cb)

    lam = lam_ref[0]
    half_neg_sp = (-0.5 * LRU_C) * jax.nn.softplus(-lam)

    def coeff_chunk(ci, carry):
        r0 = pl.multiple_of(ci * SCAN_CHUNK, SCAN_CHUNK)
        xc = xc_s[pl.ds(r0, SCAN_CHUNK), :]
        t = jnp.tanh(jnp.dot(xc.astype(BF16), wg_ref[0], preferred_element_type=F32) + gb_ref[0])
        half_xc = 0.5 * xc
        for d, (a_s, b_s) in enumerate(((a0_s, b0_s), (a1_s, b1_s))):
            t_r = t[:, (2 * d) * c:(2 * d + 1) * c]
            t_i = t[:, (2 * d + 1) * c:(2 * d + 2) * c]
            half_sp = half_neg_sp[:, d * c:(d + 1) * c]
            a = jnp.exp(half_sp + half_sp * t_r)
            one_m_a2 = 1.0 - a * a
            mult = jnp.where(one_m_a2 > 0.0, one_m_a2 * lax.rsqrt(one_m_a2), 0.0)
            bb = mult * (half_xc + half_xc * t_i)
            a_cum, b_cum = _group_scan(a, bb, reverse=(d == 1))
            a_s[pl.ds(r0, SCAN_CHUNK), :] = a_cum
            b_s[pl.ds(r0, SCAN_CHUNK), :] = b_cum
        return carry

    lax.fori_loop(0, n_all // SCAN_CHUNK, coeff_chunk, 0)

    g_c = n_c // SUBLANES
    g_all = n_all // SUBLANES

    def carry_step(k, carry):
        hf, hr = carry
        rf = pl.multiple_of(k * SUBLANES, SUBLANES)
        h = a0_s[pl.ds(rf, SUBLANES), :] * hf + b0_s[pl.ds(rf, SUBLANES), :]
        a0_s[pl.ds(rf, SUBLANES), :] = h
        hf = h[SUBLANES - 1:SUBLANES, :]
        kr = jnp.where(k < g_c, g_c - 1 - k, g_all + g_c - 1 - k)
        rr = pl.multiple_of(kr * SUBLANES, SUBLANES)
        h = a1_s[pl.ds(rr, SUBLANES), :] * hr + b1_s[pl.ds(rr, SUBLANES), :]
        a1_s[pl.ds(rr, SUBLANES), :] = h
        hr = h[0:1, :]
        return hf, hr

    zero = jnp.zeros((1, c), F32)
    lax.fori_loop(0, g_all, carry_step, (zero, zero), unroll=4)

    yx = (a0_s[n_c:n_all, :] + a1_s[n_c:n_all, :]) * _gelu_tanh(grx_ref[...].astype(F32))
    yx_ref[...] = yx.astype(yx_ref.dtype)
    if ctx_out:
        yc = (a0_s[0:n_c, :] + a1_s[0:n_c, :]) * _gelu_tanh(grc_ref[...].astype(F32))
        yc_ref[...] = yc.astype(yc_ref.dtype)


def rglru(proj_x, proj_c, conv_w, conv_b, wg, gb, lam, *, xr_col, gr_col, c_xr_col, c_gr_col, ctx_out):
    c = RNN_CB
    n_cb = D_RNN // c
    xr_b, gr_b, cxr_b = xr_col // c, gr_col // c, c_xr_col // c
    in_specs = [
        pl.BlockSpec((SEQ, c), lambda b, j: (b, xr_b + j)),
        pl.BlockSpec((SEQ, c), lambda b, j: (b, gr_b + j)),
        pl.BlockSpec((CTX_LEN, c), lambda b, j: (b, cxr_b + j)),
    ]
    args = [proj_x, proj_x, proj_c]
    if ctx_out:
        cgr_b = c_gr_col // c
        in_specs.append(pl.BlockSpec((CTX_LEN, c), lambda b, j: (b, cgr_b + j)))
        args.append(proj_c)
    in_specs += [
        pl.BlockSpec((CONV_W, c), lambda b, j: (0, j)),
        pl.BlockSpec((1, c), lambda b, j: (0, j)),
        pl.BlockSpec((1, c, 4 * c), lambda b, j: (j, 0, 0)),
        pl.BlockSpec((1, 1, 4 * c), lambda b, j: (j, 0, 0)),
        pl.BlockSpec((1, 1, 2 * c), lambda b, j: (j, 0, 0)),
    ]
    args += [conv_w, conv_b.reshape(1, D_RNN), wg, gb, lam]
    out_specs = [pl.BlockSpec((SEQ, c), lambda b, j: (b, j))]
    out_shape = [jax.ShapeDtypeStruct((BATCH * SEQ, D_RNN), BF16)]
    if ctx_out:
        out_specs.append(pl.BlockSpec((CTX_LEN, c), lambda b, j: (b, j)))
        out_shape.append(jax.ShapeDtypeStruct((BATCH * CTX_LEN, D_RNN), BF16))
    n_all = SEQ + CTX_LEN
    out = pl.pallas_call(
        functools.partial(_rglru_kernel, n_c=CTX_LEN, n_x=SEQ, ctx_out=ctx_out),
        grid=(BATCH, n_cb),
        in_specs=in_specs,
        out_specs=out_specs,
        out_shape=out_shape,
        scratch_shapes=[pltpu.VMEM((n_all, c), F32) for _ in range(5)],
        compiler_params=_params("parallel", "parallel"),
        name="rglru",
    )(*args)
    return out if ctx_out else (out[0], None)


def _rope(t, cos, sin_signed):
    lane = lax.broadcasted_iota(jnp.int32, t.shape, 1)
    first_half = (lane % (2 * ROPE_FREQS)) < ROPE_FREQS
    partner = jnp.where(first_half, pltpu.roll(t, LANES - ROPE_FREQS, 1), pltpu.roll(t, ROPE_FREQS, 1))
    return t * cos + partner * sin_signed


def _attn_kernel(*refs, tq, n_c, n_x, lam_init):
    if n_x:
        (q_ref, kx_ref, vx_ref, kc_ref, vc_ref, cos_ref, sin_ref, dl_ref, sg_ref,
         o_ref, k_s, v_s, lam_s) = refs
    else:
        q_ref, kc_ref, vc_ref, dl_ref, sg_ref, o_ref, k_s, v_s, lam_s = refs
    qi = pl.program_id(2)

    @pl.when(qi == 0)
    def _():
        k_s[0:n_c, :] = kc_ref[...]
        v_s[0:n_c, 0:LANES] = vc_ref[...]
        if n_x:
            kx = _rope(kx_ref[...].astype(F32), cos_ref[...], sin_ref[...])
            k_s[n_c:n_c + n_x, :] = kx.astype(BF16)
            v_s[n_c:n_c + n_x, 0:LANES] = vx_ref[...]
        v_s[:, LANES:2 * LANES] = jnp.ones((n_c + n_x, LANES), BF16)
        dl = dl_ref[...]
        e1 = jnp.exp(jnp.sum(dl[0:1] * dl[1:2], axis=-1, keepdims=True))
        e2 = jnp.exp(jnp.sum(dl[2:3] * dl[3:4], axis=-1, keepdims=True))
        lam_s[...] = jnp.broadcast_to(e1 - e2 + lam_init, lam_s.shape)

    q = q_ref[...].astype(F32)
    if n_x:
        r0 = pl.multiple_of(qi * tq, tq)
        q = _rope(q, cos_ref[pl.ds(r0, tq), :], sin_ref[pl.ds(r0, tq), :])
    q = q * (HEAD_DIM ** -0.5 * math.log2(math.e))
    lane = lax.broadcasted_iota(jnp.int32, (ATTN_SUB, LANES), 1)
    lam = lam_s[0:1, 0:1]
    scores = []
    for sb in range(tq // ATTN_SUB):
        qs = q[sb * ATTN_SUB:(sb + 1) * ATTN_SUB]
        qq = jnp.concatenate([jnp.where(lane < HEAD_DIM, qs, 0.0), jnp.where(lane >= HEAD_DIM, qs, 0.0)], axis=0)
        scores.append(lax.dot_general(qq.astype(BF16), k_s[...], (((1,), (1,)), ((), ())),
                                      preferred_element_type=F32))
    for sb, s in enumerate(scores):
        e = jnp.exp2(s - jnp.max(s, axis=-1, keepdims=True))
        ov = jnp.dot(e.astype(BF16), v_s[...], preferred_element_type=F32)
        ov = ov[:, 0:LANES] / ov[:, LANES:2 * LANES]
        o = ov[0:ATTN_SUB] - lam * ov[ATTN_SUB:2 * ATTN_SUB]
        o_ref[sb * ATTN_SUB:(sb + 1) * ATTN_SUB, :] = (_rms(o) * sg_ref[...] * (1.0 - lam_init)).astype(o_ref.dtype)


def diff_attention(q_arr, q_col, n_q, kvx, kvc, cols, cos, sin_signed, diff_lambda, subln_g, *, lam_init, tq):
    kx_col, vx_col, kc_col, vc_col = cols
    qb, kxb, vxb, kcb, vcb = (v // LANES for v in (q_col, kx_col, vx_col, kc_col, vc_col))
    n_x = SEQ if kvx is not None else 0
    n_qb = n_q // tq
    in_specs = [pl.BlockSpec((tq, LANES), lambda b, h, i: (b * n_qb + i, qb + h))]
    args = [q_arr]
    if n_x:
        in_specs += [
            pl.BlockSpec((SEQ, LANES), lambda b, h, i: (b, kxb + h)),
            pl.BlockSpec((SEQ, LANES), lambda b, h, i: (b, vxb + h)),
        ]
        args += [kvx, kvx]
    in_specs += [
        pl.BlockSpec((CTX_LEN, LANES), lambda b, h, i: (b, kcb + h)),
        pl.BlockSpec((CTX_LEN, LANES), lambda b, h, i: (b, vcb + h)),
    ]
    args += [kvc, kvc]
    if n_x:
        in_specs += [
            pl.BlockSpec((SEQ, LANES), lambda b, h, i: (0, 0)),
            pl.BlockSpec((SEQ, LANES), lambda b, h, i: (0, 0)),
        ]
        args += [cos, sin_signed]
    in_specs += [
        pl.BlockSpec((4, HEAD_DIM), lambda b, h, i: (0, 0)),
        pl.BlockSpec((1, V_DIM), lambda b, h, i: (0, 0)),
    ]
    args += [diff_lambda, subln_g.reshape(1, V_DIM)]
    n_kv = CTX_LEN + n_x
    return pl.pallas_call(
        functools.partial(_attn_kernel, tq=tq, n_c=CTX_LEN, n_x=n_x, lam_init=lam_init),
        grid=(BATCH, N_HEADS, n_qb),
        in_specs=in_specs,
        out_specs=pl.BlockSpec((tq, LANES), lambda b, h, i: (b * n_qb + i, h)),
        out_shape=jax.ShapeDtypeStruct((BATCH * n_q, ATTN_W), BF16),
        scratch_shapes=[
            pltpu.VMEM((n_kv, LANES), BF16),
            pltpu.VMEM((n_kv, 2 * LANES), BF16),
            pltpu.VMEM((SUBLANES, LANES), F32),
        ],
        compiler_params=_params("parallel", "parallel", "arbitrary"),
        name="diff_attention",
    )(*args)


def _merge_kernel(yr_ref, ya_ref, gr0_ref, gr1_ref, ga0_ref, ga1_ref, x_ref, g1_ref, ng_ref, wr_ref, wa_ref, wo_ref,
                  o_ref):
    y_rnn = jnp.dot(yr_ref[...], wr_ref[...], preferred_element_type=F32)
    y_attn = jnp.dot(ya_ref[...], wa_ref[...], preferred_element_type=F32)
    g_rnn = jax.nn.sigmoid(jnp.concatenate([gr0_ref[...], gr1_ref[...]], axis=1).astype(F32))
    g_attn = jax.nn.sigmoid(jnp.concatenate([ga0_ref[...], ga1_ref[...]], axis=1).astype(F32))
    m = (g_rnn * y_rnn + g_attn * y_attn).astype(BF16)
    mx = jnp.dot(m, wo_ref[...], preferred_element_type=F32)
    o_ref[...] = x_ref[...] + g1_ref[0] * (_rms(mx) * ng_ref[...])


def merge_out(y_rnn, y_attn, proj, x, g1, ng, w_proj_rnn, w_proj_attn, w_out, *, tm):
    t, d = x.shape
    nb = g1.shape[0]
    assert t % (tm * nb) == 0, (t, tm, nb)
    bpb = (t // tm) // nb
    gw = GATE_BLOCK_W
    grb, gab = COL_G_RNN // gw, COL_G_ATTN // gw
    return pl.pallas_call(
        _merge_kernel,
        grid=(t // tm,),
        in_specs=[
            pl.BlockSpec((tm, D_RNN), lambda i: (i, 0)),
            pl.BlockSpec((tm, ATTN_W), lambda i: (i, 0)),
            pl.BlockSpec((tm, gw), lambda i: (i, grb)),
            pl.BlockSpec((tm, gw), lambda i: (i, grb + 1)),
            pl.BlockSpec((tm, gw), lambda i: (i, gab)),
            pl.BlockSpec((tm, gw), lambda i: (i, gab + 1)),
            pl.BlockSpec((tm, d), lambda i: (i, 0)),
            pl.BlockSpec((1, 1, d), lambda i: (i // bpb, 0, 0)),
            pl.BlockSpec((1, d), lambda i: (0, 0)),
            pl.BlockSpec((D_RNN, d), lambda i: (0, 0)),
            pl.BlockSpec((ATTN_W, d), lambda i: (0, 0)),
            pl.BlockSpec((d, d), lambda i: (0, 0)),
        ],
        out_specs=pl.BlockSpec((tm, d), lambda i: (i, 0)),
        out_shape=jax.ShapeDtypeStruct((t, d), F32),
        compiler_params=_params("parallel"),
        name="merge_out",
    )(y_rnn, y_attn, proj, proj, proj, proj, x, g1, ng.reshape(1, d), w_proj_rnn, w_proj_attn, w_out)


def _swiglu_partial(h, wg, wu, wd):
    gate = jnp.dot(h, wg, preferred_element_type=F32)
    up = jnp.dot(h, wu, preferred_element_type=F32)
    act = (gate * jax.nn.sigmoid(gate) * up).astype(BF16)
    return jnp.dot(act, wd, preferred_element_type=F32)


def _ffn_kernel(x_ref, g_ref, sh_ref, sc_ref, g2_ref, ng_ref, wg_ref, wu_ref, wd_ref, o_ref, h_ref, acc_ref):
    f = pl.program_id(1)

    @pl.when(f == 0)
    def _():
        h = _rms(x_ref[...]) * g_ref[...]
        h = h * (1.0 + sc_ref[0]) + sh_ref[0]
        h_ref[...] = h.astype(BF16)
        acc_ref[...] = jnp.zeros_like(acc_ref)

    acc_ref[...] += _swiglu_partial(h_ref[...], wg_ref[...], wu_ref[...], wd_ref[...])

    @pl.when(f == pl.num_programs(1) - 1)
    def _():
        o_ref[...] = x_ref[...] + g2_ref[0] * (_rms(acc_ref[...]) * ng_ref[...])


def ffn_residual(x, gain, shift, scale, g2, ng, w_gu, w_down, *, tm, tf):
    t, d = x.shape
    ff = w_gu.shape[1] // 2
    nf = ff // tf
    nb = shift.shape[0]
    assert t % (tm * nb) == 0 and ff % tf == 0, (t, tm, nb, ff, tf)
    bpb = (t // tm) // nb
    mod_spec = pl.BlockSpec((1, 1, d), lambda i, f: (i // bpb, 0, 0))
    vec_spec = pl.BlockSpec((1, d), lambda i, f: (0, 0))
    return pl.pallas_call(
        _ffn_kernel,
        grid=(t // tm, nf),
        in_specs=[
            pl.BlockSpec((tm, d), lambda i, f: (i, 0)), vec_spec, mod_spec, mod_spec, mod_spec, vec_spec,
            pl.BlockSpec((d, tf), lambda i, f: (0, f)),
            pl.BlockSpec((d, tf), lambda i, f: (0, nf + f)),
            pl.BlockSpec((tf, d), lambda i, f: (f, 0)),
        ],
        out_specs=pl.BlockSpec((tm, d), lambda i, f: (i, 0)),
        out_shape=jax.ShapeDtypeStruct((t, d), F32),
        scratch_shapes=[pltpu.VMEM((tm, d), BF16), pltpu.VMEM((tm, d), F32)],
        compiler_params=_params("parallel", "arbitrary"),
        name="ffn_residual",
    )(x, gain.reshape(1, d), shift, scale, g2, ng.reshape(1, d), w_gu, w_gu, w_down)


MOE_TM = 512
MOE_TF = 1408
ROW_TILE = 256
DMA_UNROLL = 8
META_E0, META_E1, META_W0, META_W1, META_R0, META_R1 = range(6)


def _lane_pick(rec, lane, k):
    return jnp.sum(jnp.where(lane == k, rec, 0.0), axis=-1, keepdims=True)


def _router_kernel(x_ref, g_ref, sh_ref, sc_ref, w_ref, b_ref, h_ref, meta_ref, cnt_ref, carry_s):
    @pl.when(pl.program_id(0) == 0)
    def _():
        carry_s[...] = jnp.zeros_like(carry_s)

    h = _rms(x_ref[...]) * g_ref[...]
    h = h * (1.0 + sc_ref[0]) + sh_ref[0]
    h_ref[...] = h
    logits = jnp.dot(h, w_ref[...], preferred_element_type=F32, precision=lax.Precision.HIGHEST) + b_ref[...]
    lane = lax.broadcasted_iota(jnp.int32, logits.shape, 1).astype(F32)
    neg = jnp.float32(-jnp.inf)
    logits = jnp.where(lane < N_EXPERTS, logits, neg)
    m1 = jnp.max(logits, axis=-1, keepdims=True)
    i1 = jnp.min(jnp.where(logits == m1, lane, float(LANES)), axis=-1, keepdims=True)
    rest = jnp.where(lane == i1, neg, logits)
    m2 = jnp.max(rest, axis=-1, keepdims=True)
    i2 = jnp.min(jnp.where(rest == m2, lane, float(LANES)), axis=-1, keepdims=True)
    e2 = jnp.exp(m2 - m1)
    denom = 1.0 + e2

    tm = logits.shape[0]
    sel1 = lane == i1
    sel2 = lane == i2
    member = jnp.where(jnp.logical_or(sel1, sel2), 1.0, 0.0)
    row = lax.broadcasted_iota(jnp.int32, (tm, tm), 0)
    col = lax.broadcasted_iota(jnp.int32, (tm, tm), 1)
    lower = jnp.where(col < row, 1.0, 0.0).astype(BF16)
    before = jnp.dot(lower, member.astype(BF16), preferred_element_type=F32) + carry_s[0:1, :]
    r1 = jnp.sum(jnp.where(sel1, before, 0.0), axis=-1, keepdims=True)
    r2 = jnp.sum(jnp.where(sel2, before, 0.0), axis=-1, keepdims=True)
    carry_s[0:1, :] = carry_s[0:1, :] + jnp.sum(member, axis=0, keepdims=True)
    cnt_ref[...] = jnp.broadcast_to(carry_s[0:1, :], cnt_ref.shape)

    meta = jnp.zeros_like(logits)
    for k, v in ((META_E0, i1), (META_E1, i2), (META_W0, 1.0 / denom), (META_W1, e2 / denom),
                 (META_R0, r1), (META_R1, r2)):
        meta = jnp.where(lane == k, v, meta)
    meta_ref[...] = meta


def moe_router(x, gain, shift, scale, router_w, router_b, *, tm):
    t, d = x.shape
    nb = shift.shape[0]
    bpb = (t // tm) // nb
    w = jnp.zeros((d, LANES), F32).at[:, :N_EXPERTS].set(router_w)
    b = jnp.zeros((1, LANES), F32).at[0, :N_EXPERTS].set(router_b)
    return pl.pallas_call(
        _router_kernel,
        grid=(t // tm,),
        in_specs=[
            pl.BlockSpec((tm, d), lambda i: (i, 0)),
            pl.BlockSpec((1, d), lambda i: (0, 0)),
            pl.BlockSpec((1, 1, d), lambda i: (i // bpb, 0, 0)),
            pl.BlockSpec((1, 1, d), lambda i: (i // bpb, 0, 0)),
            pl.BlockSpec((d, LANES), lambda i: (0, 0)),
            pl.BlockSpec((1, LANES), lambda i: (0, 0)),
        ],
        out_specs=[
            pl.BlockSpec((tm, d), lambda i: (i, 0)),
            pl.BlockSpec((tm, LANES), lambda i: (i, 0)),
            pl.BlockSpec((SUBLANES, LANES), lambda i: (0, 0)),
        ],
        out_shape=[
            jax.ShapeDtypeStruct((t, d), F32),
            jax.ShapeDtypeStruct((t, LANES), F32),
            jax.ShapeDtypeStruct((SUBLANES, LANES), F32),
        ],
        scratch_shapes=[pltpu.VMEM((SUBLANES, LANES), F32)],
        compiler_params=_params("arbitrary"),
        name="moe_router",
    )(x, gain.reshape(1, d), shift, scale, w, b)


def _row_copy(src_ref, src_row, dst_ref, dst_row, sem):
    return pltpu.make_async_copy(src_ref.at[pl.ds(src_row, 1), :], dst_ref.at[pl.ds(dst_row, 1), :], sem)


FILL_START, FILL_COUNT = 0, 1


def _dispatch_kernel(fill_ref, pos_ref, h_ref, hs_ref, buf_s, zero_s, sems):
    i = pl.program_id(0)
    n_steps = pl.num_programs(0)
    n = h_ref.shape[0]
    slot = i % 2
    zero_sem = sems.at[2]

    @pl.when(i == 0)
    def _():
        zero_s[...] = jnp.zeros_like(zero_s)

        def pad_copy(e, k):
            return _row_copy(zero_s, 0, hs_ref, fill_ref[FILL_START, e] + k, zero_sem)

        def tail_copy(k):
            row = pl.multiple_of(fill_ref[FILL_START, N_EXPERTS] + k * SUBLANES, SUBLANES)
            return pltpu.make_async_copy(zero_s, hs_ref.at[pl.ds(row, SUBLANES), :], zero_sem)

        def for_each(copy, count, op):
            def body(k, carry):
                getattr(copy(k), op)()
                return carry

            lax.fori_loop(0, count, body, 0)

        for op in ("start", "wait"):
            for e in range(N_EXPERTS):
                for_each(functools.partial(pad_copy, e), fill_ref[FILL_COUNT, e], op)
            for_each(tail_copy, fill_ref[FILL_COUNT, N_EXPERTS], op)

    buf_s[slot] = h_ref[...]
    src = buf_s.at[slot]
    sem = sems.at[slot]

    def issue(g, carry):
        for j in range(DMA_UNROLL):
            r = g * DMA_UNROLL + j
            _row_copy(src, r, hs_ref, pos_ref[0, 0, 2 * r], sem).start(priority=0)
            _row_copy(src, r, hs_ref, pos_ref[0, 0, 2 * r + 1], sem).start(priority=1)
        return carry

    lax.fori_loop(0, n // DMA_UNROLL, issue, 0)

    def drain(s):
        for _ in range(TOP_K):
            pltpu.make_async_copy(buf_s.at[s], hs_ref.at[pl.ds(0, n), :], sems.at[s]).wait()

    @pl.when(i > 0)
    def _():
        drain(1 - slot)

    @pl.when(i == n_steps - 1)
    def _():
        drain(slot)


def moe_dispatch(h, pos, fill, n_rows):
    t, d = h.shape
    return pl.pallas_call(
        _dispatch_kernel,
        grid_spec=pltpu.PrefetchScalarGridSpec(
            num_scalar_prefetch=1,
            grid=(t // ROW_TILE,),
            in_specs=[
                pl.BlockSpec((1, 1, 2 * ROW_TILE), lambda i, fill: (i, 0, 0), memory_space=pltpu.SMEM),
                pl.BlockSpec((ROW_TILE, d), lambda i, fill: (i, 0)),
            ],
            out_specs=pl.BlockSpec(memory_space=pl.ANY),
            scratch_shapes=[pltpu.VMEM((2, ROW_TILE, d), F32), pltpu.VMEM((SUBLANES, d), F32),
                            pltpu.SemaphoreType.DMA((3,))],
        ),
        out_shape=jax.ShapeDtypeStruct((n_rows, d), F32),
        compiler_params=_params("arbitrary"),
        name="moe_dispatch",
    )(fill, pos, h)


def _grouped_ffn_kernel(te_ref, na_ref, hs_ref, wg_ref, wu_ref, wd_ref, ys_ref, h_s, acc_s):
    del te_ref
    f = pl.program_id(1)
    active = pl.program_id(0) < na_ref[0]

    @pl.when(jnp.logical_not(active))
    def _():
        ys_ref[...] = jnp.zeros_like(ys_ref)

    @pl.when(active)
    def _():
        @pl.when(f == 0)
        def _():
            h_s[...] = hs_ref[...].astype(BF16)
            acc_s[...] = jnp.zeros_like(acc_s)

        acc_s[...] += _swiglu_partial(h_s[...], wg_ref[0], wu_ref[0], wd_ref[0])

        @pl.when(f == pl.num_programs(1) - 1)
        def _():
            ys_ref[...] = acc_s[...]


def moe_grouped_ffn(hs, tile_expert, n_active, w_gu, w_down):
    n_rows, d = hs.shape
    ff = w_gu.shape[2] // 2
    nf = ff // MOE_TF
    n_tiles = n_rows // MOE_TM

    def row_map(t, f, te, na):
        return (jnp.maximum(jnp.minimum(t, na[0] - 1), 0), 0)

    def f_eff(t, f, na):
        return jnp.where(t < na[0], f, nf - 1)

    return pl.pallas_call(
        _grouped_ffn_kernel,
        grid_spec=pltpu.PrefetchScalarGridSpec(
            num_scalar_prefetch=2,
            grid=(n_tiles, nf),
            in_specs=[
                pl.BlockSpec((MOE_TM, d), row_map),
                pl.BlockSpec((1, d, MOE_TF), lambda t, f, te, na: (te[t], 0, f_eff(t, f, na))),
                pl.BlockSpec((1, d, MOE_TF), lambda t, f, te, na: (te[t], 0, nf + f_eff(t, f, na))),
                pl.BlockSpec((1, MOE_TF, d), lambda t, f, te, na: (te[t], f_eff(t, f, na), 0)),
            ],
            out_specs=pl.BlockSpec((MOE_TM, d), lambda t, f, te, na: (t, 0)),
            scratch_shapes=[pltpu.VMEM((MOE_TM, d), BF16), pltpu.VMEM((MOE_TM, d), F32)],
        ),
        out_shape=jax.ShapeDtypeStruct((n_rows, d), F32),
        compiler_params=_params("arbitrary", "arbitrary"),
        name="moe_grouped_ffn",
    )(tile_expert, n_active, hs, w_gu, w_gu, w_down)


def _combine_kernel(pos_ref, pos_next_ref, ys_ref, x_ref, meta_ref, g2_ref, ng_ref, o_ref, a_s, b_s, sems):
    i = pl.program_id(0)
    n_steps = pl.num_programs(0)
    n = x_ref.shape[0]
    slot = i % 2

    def gather(p_ref, s):
        def issue(g, carry):
            for j in range(DMA_UNROLL):
                r = g * DMA_UNROLL + j
                _row_copy(ys_ref, p_ref[0, 0, 2 * r], a_s.at[s], r, sems.at[s]).start(priority=0)
                _row_copy(ys_ref, p_ref[0, 0, 2 * r + 1], b_s.at[s], r, sems.at[s]).start(priority=1)
            return carry

        lax.fori_loop(0, n // DMA_UNROLL, issue, 0)

    @pl.when(i == 0)
    def _():
        gather(pos_ref, 0)

    @pl.when(i + 1 < n_steps)
    def _():
        gather(pos_next_ref, 1 - slot)

    pltpu.make_async_copy(ys_ref.at[pl.ds(0, n), :], a_s.at[slot], sems.at[slot]).wait()
    pltpu.make_async_copy(ys_ref.at[pl.ds(0, n), :], b_s.at[slot], sems.at[slot]).wait()

    meta = meta_ref[...]
    lane = lax.broadcasted_iota(jnp.int32, meta.shape, 1)
    mixed = _lane_pick(meta, lane, META_W0) * a_s[slot] + _lane_pick(meta, lane, META_W1) * b_s[slot]
    o_ref[...] = x_ref[...] + g2_ref[0] * (_rms(mixed) * ng_ref[...])


def moe_combine(ys, pos, meta, x, g2, ng):
    t, d = x.shape
    nb = g2.shape[0]
    n_steps = t // ROW_TILE
    bpb = n_steps // nb
    return pl.pallas_call(
        _combine_kernel,
        grid=(n_steps,),
        in_specs=[
            pl.BlockSpec((1, 1, 2 * ROW_TILE), lambda i: (i, 0, 0), memory_space=pltpu.SMEM),
            pl.BlockSpec((1, 1, 2 * ROW_TILE), lambda i: (jnp.minimum(i + 1, n_steps - 1), 0, 0),
                         memory_space=pltpu.SMEM),
            pl.BlockSpec(memory_space=pl.ANY),
            pl.BlockSpec((ROW_TILE, d), lambda i: (i, 0)),
            pl.BlockSpec((ROW_TILE, LANES), lambda i: (i, 0)),
            pl.BlockSpec((1, 1, d), lambda i: (i // bpb, 0, 0)),
            pl.BlockSpec((1, d), lambda i: (0, 0)),
        ],
        out_specs=pl.BlockSpec((ROW_TILE, d), lambda i: (i, 0)),
        out_shape=jax.ShapeDtypeStruct((t, d), F32),
        scratch_shapes=[pltpu.VMEM((2, ROW_TILE, d), F32), pltpu.VMEM((2, ROW_TILE, d), F32),
                        pltpu.SemaphoreType.DMA((2,))],
        compiler_params=_params("arbitrary"),
        name="moe_combine",
    )(pos, pos, ys, x, meta, g2, ng.reshape(1, d))


def moe_residual(x, gain, shift, scale, g2, ng, router_w, router_b, w_gu, w_down):
    t, d = x.shape
    h, meta, counts = moe_router(x, gain, shift, scale, router_w, router_b, tm=512)
    n_rows = TOP_K * t + N_EXPERTS * MOE_TM
    n_tiles = n_rows // MOE_TM
    cnt = counts[0, :N_EXPERTS].astype(jnp.int32)
    padded = (cnt + MOE_TM - 1) // MOE_TM * MOE_TM
    seg_end = jnp.cumsum(padded)
    seg_start = seg_end - padded
    experts = jnp.arange(N_EXPERTS, dtype=jnp.int32)

    def position(e_lane, r_lane):
        e = meta[:, e_lane].astype(jnp.int32)
        start = jnp.sum(jnp.where(e[:, None] == experts[None, :], seg_start[None, :], 0), axis=-1)
        return start + meta[:, r_lane].astype(jnp.int32)

    pos = jnp.stack([position(META_E0, META_R0), position(META_E1, META_R1)], axis=-1)
    pos = pos.reshape(t // ROW_TILE, 1, 2 * ROW_TILE)
    n_active = seg_end[-1:] // MOE_TM
    tile_ids = jnp.arange(n_tiles, dtype=jnp.int32)
    tile_expert = jnp.sum((jnp.minimum(tile_ids, n_active - 1)[:, None] * MOE_TM >= seg_end[None, :]), axis=-1)
    tile_expert = jnp.minimum(tile_expert, N_EXPERTS - 1).astype(jnp.int32)

    fill = jnp.stack([
        jnp.concatenate([seg_start + cnt, seg_end[-1:], jnp.zeros((N_EXPERTS - 1,), jnp.int32)]),
        jnp.concatenate([padded - cnt, (n_rows - seg_end[-1:]) // SUBLANES, jnp.zeros((N_EXPERTS - 1,), jnp.int32)]),
    ]).astype(jnp.int32)
    hs = moe_dispatch(h, pos, fill, n_rows)
    ys = moe_grouped_ffn(hs, tile_expert, n_active.astype(jnp.int32), w_gu, w_down)
    return moe_combine(ys, pos, meta, x, g2, ng)


def _rope_tables():
    pos = jnp.arange(SEQ)
    inv_freq = jnp.power(ROPE_THETA, -jnp.arange(ROPE_FREQS, dtype=F32) / ROPE_FREQS)
    ang_r = (pos // GRID_W).astype(F32)[:, None] * inv_freq
    ang_c = (pos % GRID_W).astype(F32)[:, None] * inv_freq
    cos = jnp.concatenate([jnp.cos(ang_r)] * 2 + [jnp.cos(ang_c)] * 2, axis=-1)
    sin = jnp.concatenate([-jnp.sin(ang_r), jnp.sin(ang_r), -jnp.sin(ang_c), jnp.sin(ang_c)], axis=-1)
    return jnp.tile(cos, (1, 2)), jnp.tile(sin, (1, 2))


def _gate_layout(gate_w, gate_b, lam):
    c = RNN_CB
    n_cb = D_RNN // c
    per = c // RNN_BLOCK_W
    gw = gate_w.reshape(2, 2, n_cb, per, RNN_BLOCK_W, RNN_BLOCK_W)
    eye = jnp.eye(per, dtype=gate_w.dtype)
    bd = gw[:, :, :, :, :, None, :] * eye[None, None, None, :, None, :, None]
    bd = bd.reshape(2, 2, n_cb, c, c)
    wg = (0.5 * jnp.transpose(bd, (2, 3, 0, 1, 4))).reshape(n_cb, c, 4 * c).astype(BF16)
    gb = 0.5 * jnp.transpose(gate_b.reshape(2, 2, n_cb, c), (2, 0, 1, 3)).reshape(n_cb, 1, 4 * c)
    lm = jnp.transpose(lam.reshape(2, n_cb, c), (1, 0, 2)).reshape(n_cb, 1, 2 * c)
    return wg, gb, lm


def kernel(x, c, ctx, c_ctx, ada_w, ada_b, norm_g, w_in, conv_w, conv_b, lru_gate_w, lru_gate_b, lru_lambda,
           diff_lambda, subln_g, w_proj_rnn, w_proj_attn, w_out, ffn_w_gu, ffn_w_down, router_w, router_b,
           moe_w_gu, moe_w_down):
    xt = x.reshape(BATCH * SEQ, D_MODEL)
    ct = ctx.reshape(BATCH * CTX_LEN, D_MODEL)
    cos, sin_signed = _rope_tables()

    cvec = jnp.concatenate([c, c_ctx[None, :], jnp.zeros((2 * SUBLANES - BATCH - 1, D_MODEL), F32)], axis=0)
    mod = ada_modulation(cvec, ada_w, ada_b)

    for l in range(DEPTH):
        last = l == DEPTH - 1
        lam_init = 0.8 - 0.6 * math.exp(-0.3 * l)
        mx = mod[l, :BATCH].reshape(BATCH, 1, 6, D_MODEL)
        mc = mod[l, BATCH:BATCH + 1].reshape(1, 1, 6, D_MODEL)
        sh1x, sc1x, g1x, sh2x, sc2x, g2x = (mx[:, :, i] for i in range(6))
        sh1c, sc1c, g1c, sh2c, sc2c, g2c = (mc[:, :, i] for i in range(6))

        wg, gb, lm = _gate_layout(lru_gate_w[l], lru_gate_b[l], lru_lambda[l])

        proj_x = norm_mod_matmul(xt, norm_g[l, 0], sh1x, sc1x, w_in[l], IN_W, tm=2048, tn=768)
        if last:
            proj_c = norm_mod_matmul(ct, norm_g[l, 0], sh1c, sc1c, w_in[l], CTX_STATE_W, tm=2048, tn=256)
        else:
            proj_c = norm_mod_matmul(ct, norm_g[l, 0], sh1c, sc1c, w_in[l], IN_W, tm=2048, tn=768)

        y_rnn_x, y_rnn_c = rglru(proj_x, proj_c, conv_w[l], conv_b[l], wg, gb, lm, xr_col=COL_XR, gr_col=COL_GR,
                                 c_xr_col=COL_XR, c_gr_col=COL_GR, ctx_out=not last)
        y_attn_x = diff_attention(proj_x, COL_Q, SEQ, proj_x, proj_c, (COL_K, COL_V, COL_K, COL_V), cos,
                                  sin_signed, diff_lambda[l], subln_g[l], lam_init=lam_init, tq=1024)
        wr = w_proj_rnn[l].astype(BF16)
        wa = w_proj_attn[l].astype(BF16)
        wo = w_out[l].astype(BF16)
        xt = merge_out(y_rnn_x, y_attn_x, proj_x, xt, g1x, norm_g[l, 1], wr, wa, wo, tm=512)
        if not last:
            y_attn_c = diff_attention(proj_c, COL_Q, CTX_LEN, None, proj_c, (0, 0, COL_K, COL_V), None, None,
                                      diff_lambda[l], subln_g[l], lam_init=lam_init, tq=CTX_LEN)
            ct = merge_out(y_rnn_c, y_attn_c, proj_c, ct, g1c, norm_g[l, 1], wr, wa, wo, tm=512)

        if l % 2 == 0:
            w_gu = ffn_w_gu[l // 2].astype(BF16)
            w_dn = ffn_w_down[l // 2].astype(BF16)
            xt = ffn_residual(xt, norm_g[l, 2], sh2x, sc2x, g2x, norm_g[l, 3], w_gu, w_dn, tm=512, tf=1408)
            if not last:
                ct = ffn_residual(ct, norm_g[l, 2], sh2c, sc2c, g2c, norm_g[l, 3], w_gu, w_dn, tm=512, tf=1408)
        else:
            w_gu = moe_w_gu[l // 2].astype(BF16)
            w_dn = moe_w_down[l // 2].astype(BF16)
            rw, rb = router_w[l // 2], router_b[l // 2]
            xt = moe_residual(xt, norm_g[l, 2], sh2x, sc2x, g2x, norm_g[l, 3], rw, rb, w_gu, w_dn)
            if not last:
                ct = moe_residual(ct, norm_g[l, 2], sh2c, sc2c, g2c, norm_g[l, 3], rw, rb, w_gu, w_dn)
    return xt.reshape(BATCH, SEQ, D_MODEL)
```

```python
import functools
import math

import jax
import jax.numpy as jnp
from jax import lax
from jax.experimental import pallas as pl
from jax.experimental.pallas import tpu as pltpu

F32 = jnp.float32
BF16 = jnp.bfloat16

D_MODEL = 1024
BATCH = 8
SEQ = 2048
DEPTH = 2
CTX_LEN = 256
GRID_W = 64
EPS = 1e-6
D_RNN = 1280
RNN_BLOCKS = 20
RNN_BLOCK_W = D_RNN // RNN_BLOCKS
CONV_W = 4
LRU_C = 8.0
N_HEADS = 8
HEAD_DIM = 64
V_DIM = 2 * HEAD_DIM
QK_W = N_HEADS * 2 * HEAD_DIM
ATTN_W = N_HEADS * V_DIM
ROPE_THETA = 10000.0
ROPE_FREQS = HEAD_DIM // 4
D_FF = 2816
N_EXPERTS = 8
TOP_K = 2

LANES = 128
SUBLANES = 8
VMEM_LIMIT_BYTES = 52 * 1024 * 1024

COL_XR = 0
COL_K = COL_XR + D_RNN
COL_V = COL_K + QK_W
COL_GR = COL_V + ATTN_W
COL_Q = COL_GR + D_RNN
COL_G_RNN = COL_Q + QK_W
COL_G_ATTN = COL_G_RNN + D_MODEL
IN_W = COL_G_ATTN + D_MODEL
CTX_STATE_W = COL_GR
GATE_BLOCK_W = 512

RNN_CB = 256
SCAN_CHUNK = 256
ATTN_SUB = 128
NORM_ROWS = 512


def _params(*sem):
    return pltpu.CompilerParams(dimension_semantics=sem, vmem_limit_bytes=VMEM_LIMIT_BYTES)


def _rms(x):
    return x * lax.rsqrt(jnp.mean(x * x, axis=-1, keepdims=True) + EPS)


def _ada_kernel(c_ref, w_ref, b_ref, o_ref):
    c = c_ref[...]
    s = c * jax.nn.sigmoid(c)
    o_ref[0] = jnp.dot(s, w_ref[0], preferred_element_type=F32, precision=lax.Precision.HIGHEST) + b_ref[0]


def ada_modulation(cvec, ada_w, ada_b):
    rows = cvec.shape[0]
    tn = 1536
    n = 6 * D_MODEL
    return pl.pallas_call(
        _ada_kernel,
        grid=(DEPTH, n // tn),
        in_specs=[
            pl.BlockSpec((rows, D_MODEL), lambda l, j: (0, 0)),
            pl.BlockSpec((1, D_MODEL, tn), lambda l, j: (l, 0, j)),
            pl.BlockSpec((1, 1, tn), lambda l, j: (l, 0, j)),
        ],
        out_specs=pl.BlockSpec((1, rows, tn), lambda l, j: (l, 0, j)),
        out_shape=jax.ShapeDtypeStruct((DEPTH, rows, n), F32),
        compiler_params=_params("parallel", "parallel"),
        name="ada_modulation",
    )(cvec, ada_w, ada_b.reshape(DEPTH, 1, n))


def _nmm_kernel(x_ref, g_ref, sh_ref, sc_ref, w_ref, o_ref, h_ref):
    @pl.when(pl.program_id(1) == 0)
    def _():
        for r0 in range(0, x_ref.shape[0], NORM_ROWS):
            h = _rms(x_ref[r0:r0 + NORM_ROWS, :]) * g_ref[...]
            h = h * (1.0 + sc_ref[0]) + sh_ref[0]
            h_ref[r0:r0 + NORM_ROWS, :] = h.astype(BF16)

    w = w_ref[0].astype(BF16)
    o_ref[...] = jnp.dot(h_ref[...], w, preferred_element_type=F32).astype(o_ref.dtype)


def norm_mod_matmul(x, gain, shift, scale, w, layer, n, *, tm, tn):
    t, d = x.shape
    nb = shift.shape[0]
    assert t % (tm * nb) == 0 and n % tn == 0 and tm % NORM_ROWS == 0, (t, tm, nb, n, tn)
    bpb = (t // tm) // nb
    return pl.pallas_call(
        _nmm_kernel,
        grid=(t // tm, n // tn),
        in_specs=[
            pl.BlockSpec((tm, d), lambda i, j: (i, 0)),
            pl.BlockSpec((1, d), lambda i, j: (0, 0)),
            pl.BlockSpec((1, 1, d), lambda i, j: (i // bpb, 0, 0)),
            pl.BlockSpec((1, 1, d), lambda i, j: (i // bpb, 0, 0)),
            pl.BlockSpec((1, d, tn), lambda i, j: (layer, 0, j)),
        ],
        out_specs=pl.BlockSpec((tm, tn), lambda i, j: (i, j)),
        out_shape=jax.ShapeDtypeStruct((t, n), BF16),
        scratch_shapes=[pltpu.VMEM((tm, d), BF16)],
        compiler_params=_params("parallel", "arbitrary"),
        name="norm_mod_matmul",
    )(x, gain.reshape(1, d), shift, scale, w)


def _group_scan(a, b, reverse):
    rows, c = a.shape
    a = a.reshape(rows // SUBLANES, SUBLANES, c)
    b = b.reshape(rows // SUBLANES, SUBLANES, c)
    sub = lax.broadcasted_iota(jnp.int32, a.shape, 1)
    for sh in (1, 2, 4):
        if reverse:
            keep = sub < SUBLANES - sh
            amount = SUBLANES - sh
        else:
            keep = sub >= sh
            amount = sh
        a_sh = jnp.where(keep, pltpu.roll(a, amount, 1), 1.0)
        b_sh = jnp.where(keep, pltpu.roll(b, amount, 1), 0.0)
        b = a * b_sh + b
        a = a * a_sh
    return a.reshape(rows, c), b.reshape(rows, c)


def _dwconv(x, cw, cb):
    n = x.shape[0]
    rows = lax.broadcasted_iota(jnp.int32, x.shape, 0)
    xm2 = jnp.where(rows >= 2, pltpu.roll(x, 2, 0), 0.0)
    xm1 = jnp.where(rows >= 1, pltpu.roll(x, 1, 0), 0.0)
    xp1 = jnp.where(rows < n - 1, pltpu.roll(x, n - 1, 0), 0.0)
    return cb + xm2 * cw[0:1] + xm1 * cw[1:2] + x * cw[2:3] + xp1 * cw[3:4]


def _gelu_tanh(x):
    return 0.5 * x * (1.0 + jnp.tanh(math.sqrt(2.0 / math.pi) * (x + 0.044715 * (x * x * x))))


def _rglru_kernel(*refs, n_c, n_x, ctx_out):
    if ctx_out:
        (xrx_ref, grx_ref, xrc_ref, grc_ref, cw_ref, cb_ref, wg_ref, gb_ref, lam_ref,
         yx_ref, yc_ref, xc_s, a0_s, b0_s, a1_s, b1_s) = refs
    else:
        (xrx_ref, grx_ref, xrc_ref, cw_ref, cb_ref, wg_ref, gb_ref, lam_ref,
         yx_ref, xc_s, a0_s, b0_s, a1_s, b1_s) = refs
    c = RNN_CB
    n_all = n_c + n_x
    cw = cw_ref[...]
    cb = cb_ref[...]

    xc_s[0:n_c, :] = _dwconv(xrc_ref[...].astype(F32), cw, cb)
    xc_s[n_c:n_all, :] = _dwconv(xrx_ref[...].astype(F32), cw, cb)

    lam = lam_ref[0]
    half_neg_sp = (-0.5 * LRU_C) * jax.nn.softplus(-lam)

    def coeff_chunk(ci, carry):
        r0 = pl.multiple_of(ci * SCAN_CHUNK, SCAN_CHUNK)
        xc = xc_s[pl.ds(r0, SCAN_CHUNK), :]
        t = jnp.tanh(jnp.dot(xc.astype(BF16), wg_ref[0], preferred_element_type=F32) + gb_ref[0])
        half_xc = 0.5 * xc
        for d, (a_s, b_s) in enumerate(((a0_s, b0_s), (a1_s, b1_s))):
            t_r = t[:, (2 * d) * c:(2 * d + 1) * c]
            t_i = t[:, (2 * d + 1) * c:(2 * d + 2) * c]
            half_sp = half_neg_sp[:, d * c:(d + 1) * c]
            a = jnp.exp(half_sp + half_sp * t_r)
            one_m_a2 = 1.0 - a * a
            mult = jnp.where(one_m_a2 > 0.0, one_m_a2 * lax.rsqrt(one_m_a2), 0.0)
            bb = mult * (half_xc + half_xc * t_i)
            a_cum, b_cum = _group_scan(a, bb, reverse=(d == 1))
            a_s[pl.ds(r0, SCAN_CHUNK), :] = a_cum
            b_s[pl.ds(r0, SCAN_CHUNK), :] = b_cum
        return carry

    lax.fori_loop(0, n_all // SCAN_CHUNK, coeff_chunk, 0)

    g_c = n_c // SUBLANES
    g_all = n_all // SUBLANES

    def carry_step(k, carry):
        hf, hr = carry
        rf = pl.multiple_of(k * SUBLANES, SUBLANES)
        h = a0_s[pl.ds(rf, SUBLANES), :] * hf + b0_s[pl.ds(rf, SUBLANES), :]
        a0_s[pl.ds(rf, SUBLANES), :] = h
        hf = h[SUBLANES - 1:SUBLANES, :]
        kr = jnp.where(k < g_c, g_c - 1 - k, g_all + g_c - 1 - k)
        rr = pl.multiple_of(kr * SUBLANES, SUBLANES)
        h = a1_s[pl.ds(rr, SUBLANES), :] * hr + b1_s[pl.ds(rr, SUBLANES), :]
        a1_s[pl.ds(rr, SUBLANES), :] = h
        hr = h[0:1, :]
        return hf, hr

    zero = jnp.zeros((1, c), F32)
    lax.fori_loop(0, g_all, carry_step, (zero, zero), unroll=4)

    yx = (a0_s[n_c:n_all, :] + a1_s[n_c:n_all, :]) * _gelu_tanh(grx_ref[...].astype(F32))
    yx_ref[...] = yx.astype(yx_ref.dtype)
    if ctx_out:
        yc = (a0_s[0:n_c, :] + a1_s[0:n_c, :]) * _gelu_tanh(grc_ref[...].astype(F32))
        yc_ref[...] = yc.astype(yc_ref.dtype)


def rglru(proj_x, proj_c, conv_w, conv_b, wg, gb, lam, *, xr_col, gr_col, c_xr_col, c_gr_col, ctx_out):
    c = RNN_CB
    n_cb = D_RNN // c
    xr_b, gr_b, cxr_b = xr_col // c, gr_col // c, c_xr_col // c
    in_specs = [
        pl.BlockSpec((SEQ, c), lambda b, j: (b, xr_b + j)),
        pl.BlockSpec((SEQ, c), lambda b, j: (b, gr_b + j)),
        pl.BlockSpec((CTX_LEN, c), lambda b, j: (b, cxr_b + j)),
    ]
    args = [proj_x, proj_x, proj_c]
    if ctx_out:
        cgr_b = c_gr_col // c
        in_specs.append(pl.BlockSpec((CTX_LEN, c), lambda b, j: (b, cgr_b + j)))
        args.append(proj_c)
    in_specs += [
        pl.BlockSpec((CONV_W, c), lambda b, j: (0, j)),
        pl.BlockSpec((1, c), lambda b, j: (0, j)),
        pl.BlockSpec((1, c, 4 * c), lambda b, j: (j, 0, 0)),
        pl.BlockSpec((1, 1, 4 * c), lambda b, j: (j, 0, 0)),
        pl.BlockSpec((1, 1, 2 * c), lambda b, j: (j, 0, 0)),
    ]
    args += [conv_w, conv_b.reshape(1, D_RNN), wg, gb, lam]
    out_specs = [pl.BlockSpec((SEQ, c), lambda b, j: (b, j))]
    out_shape = [jax.ShapeDtypeStruct((BATCH * SEQ, D_RNN), BF16)]
    if ctx_out:
        out_specs.append(pl.BlockSpec((CTX_LEN, c), lambda b, j: (b, j)))
        out_shape.append(jax.ShapeDtypeStruct((BATCH * CTX_LEN, D_RNN), BF16))
    n_all = SEQ + CTX_LEN
    out = pl.pallas_call(
        functools.partial(_rglru_kernel, n_c=CTX_LEN, n_x=SEQ, ctx_out=ctx_out),
        grid=(BATCH, n_cb),
        in_specs=in_specs,
        out_specs=out_specs,
        out_shape=out_shape,
        scratch_shapes=[pltpu.VMEM((n_all, c), F32) for _ in range(5)],
        compiler_params=_params("parallel", "parallel"),
        name="rglru",
    )(*args)
    return out if ctx_out else (out[0], None)


def _rope(t, cos, sin_signed):
    lane = lax.broadcasted_iota(jnp.int32, t.shape, 1)
    first_half = (lane % (2 * ROPE_FREQS)) < ROPE_FREQS
    partner = jnp.where(first_half, pltpu.roll(t, LANES - ROPE_FREQS, 1), pltpu.roll(t, ROPE_FREQS, 1))
    return t * cos + partner * sin_signed


def _attn_kernel(*refs, tq, n_c, n_x, lam_init):
    if n_x:
        (q_ref, kx_ref, vx_ref, kc_ref, vc_ref, cos_ref, sin_ref, dl_ref, sg_ref,
         o_ref, k_s, v_s, lam_s) = refs
    else:
        q_ref, kc_ref, vc_ref, dl_ref, sg_ref, o_ref, k_s, v_s, lam_s = refs
    qi = pl.program_id(2)

    @pl.when(qi == 0)
    def _():
        k_s[0:n_c, :] = kc_ref[...]
        v_s[0:n_c, 0:LANES] = vc_ref[...]
        if n_x:
            kx = _rope(kx_ref[...].astype(F32), cos_ref[...], sin_ref[...])
            k_s[n_c:n_c + n_x, :] = kx.astype(BF16)
            v_s[n_c:n_c + n_x, 0:LANES] = vx_ref[...]
        v_s[:, LANES:2 * LANES] = jnp.ones((n_c + n_x, LANES), BF16)
        dl = dl_ref[...]
        e1 = jnp.exp(jnp.sum(dl[0:1] * dl[1:2], axis=-1, keepdims=True))
        e2 = jnp.exp(jnp.sum(dl[2:3] * dl[3:4], axis=-1, keepdims=True))
        lam_s[...] = jnp.broadcast_to(e1 - e2 + lam_init, lam_s.shape)

    q = q_ref[...].astype(F32)
    if n_x:
        r0 = pl.multiple_of(qi * tq, tq)
        q = _rope(q, cos_ref[pl.ds(r0, tq), :], sin_ref[pl.ds(r0, tq), :])
    q = q * (HEAD_DIM ** -0.5 * math.log2(math.e))
    lane = lax.broadcasted_iota(jnp.int32, (ATTN_SUB, LANES), 1)
    lam = lam_s[0:1, 0:1]
    scores = []
    for sb in range(tq // ATTN_SUB):
        qs = q[sb * ATTN_SUB:(sb + 1) * ATTN_SUB]
        qq = jnp.concatenate([jnp.where(lane < HEAD_DIM, qs, 0.0), jnp.where(lane >= HEAD_DIM, qs, 0.0)], axis=0)
        scores.append(lax.dot_general(qq.astype(BF16), k_s[...], (((1,), (1,)), ((), ())),
                                      preferred_element_type=F32))
    for sb, s in enumerate(scores):
        e = jnp.exp2(s - jnp.max(s, axis=-1, keepdims=True))
        ov = jnp.dot(e.astype(BF16), v_s[...], preferred_element_type=F32)
        ov = ov[:, 0:LANES] / ov[:, LANES:2 * LANES]
        o = ov[0:ATTN_SUB] - lam * ov[ATTN_SUB:2 * ATTN_SUB]
        o_ref[sb * ATTN_SUB:(sb + 1) * ATTN_SUB, :] = (_rms(o) * sg_ref[...] * (1.0 - lam_init)).astype(o_ref.dtype)


def diff_attention(q_arr, q_col, n_q, kvx, kvc, cols, cos, sin_signed, diff_lambda, subln_g, *, lam_init, tq):
    kx_col, vx_col, kc_col, vc_col = cols
    qb, kxb, vxb, kcb, vcb = (v // LANES for v in (q_col, kx_col, vx_col, kc_col, vc_col))
    n_x = SEQ if kvx is not None else 0
    n_qb = n_q // tq
    in_specs = [pl.BlockSpec((tq, LANES), lambda b, h, i: (b * n_qb + i, qb + h))]
    args = [q_arr]
    if n_x:
        in_specs += [
            pl.BlockSpec((SEQ, LANES), lambda b, h, i: (b, kxb + h)),
            pl.BlockSpec((SEQ, LANES), lambda b, h, i: (b, vxb + h)),
        ]
        args += [kvx, kvx]
    in_specs += [
        pl.BlockSpec((CTX_LEN, LANES), lambda b, h, i: (b, kcb + h)),
        pl.BlockSpec((CTX_LEN, LANES), lambda b, h, i: (b, vcb + h)),
    ]
    args += [kvc, kvc]
    if n_x:
        in_specs += [
            pl.BlockSpec((SEQ, LANES), lambda b, h, i: (0, 0)),
            pl.BlockSpec((SEQ, LANES), lambda b, h, i: (0, 0)),
        ]
        args += [cos, sin_signed]
    in_specs += [
        pl.BlockSpec((4, HEAD_DIM), lambda b, h, i: (0, 0)),
        pl.BlockSpec((1, V_DIM), lambda b, h, i: (0, 0)),
    ]
    args += [diff_lambda, subln_g.reshape(1, V_DIM)]
    n_kv = CTX_LEN + n_x
    return pl.pallas_call(
        functools.partial(_attn_kernel, tq=tq, n_c=CTX_LEN, n_x=n_x, lam_init=lam_init),
        grid=(BATCH, N_HEADS, n_qb),
        in_specs=in_specs,
        out_specs=pl.BlockSpec((tq, LANES), lambda b, h, i: (b * n_qb + i, h)),
        out_shape=jax.ShapeDtypeStruct((BATCH * n_q, ATTN_W), BF16),
        scratch_shapes=[
            pltpu.VMEM((n_kv, LANES), BF16),
            pltpu.VMEM((n_kv, 2 * LANES), BF16),
            pltpu.VMEM((SUBLANES, LANES), F32),
        ],
        compiler_params=_params("parallel", "parallel", "arbitrary"),
        name="diff_attention",
    )(*args)


def _merge_kernel(yr_ref, ya_ref, gr0_ref, gr1_ref, ga0_ref, ga1_ref, x_ref, g1_ref, ng_ref, wr_ref, wa_ref, wo_ref,
                  o_ref):
    y_rnn = jnp.dot(yr_ref[...], wr_ref[...], preferred_element_type=F32)
    y_attn = jnp.dot(ya_ref[...], wa_ref[...], preferred_element_type=F32)
    g_rnn = jax.nn.sigmoid(jnp.concatenate([gr0_ref[...], gr1_ref[...]], axis=1).astype(F32))
    g_attn = jax.nn.sigmoid(jnp.concatenate([ga0_ref[...], ga1_ref[...]], axis=1).astype(F32))
    m = (g_rnn * y_rnn + g_attn * y_attn).astype(BF16)
    mx = jnp.dot(m, wo_ref[...], preferred_element_type=F32)
    o_ref[...] = x_ref[...] + g1_ref[0] * (_rms(mx) * ng_ref[...])


def merge_out(y_rnn, y_attn, proj, x, g1, ng, w_proj_rnn, w_proj_attn, w_out, *, tm):
    t, d = x.shape
    nb = g1.shape[0]
    assert t % (tm * nb) == 0, (t, tm, nb)
    bpb = (t // tm) // nb
    gw = GATE_BLOCK_W
    grb, gab = COL_G_RNN // gw, COL_G_ATTN // gw
    return pl.pallas_call(
        _merge_kernel,
        grid=(t // tm,),
        in_specs=[
            pl.BlockSpec((tm, D_RNN), lambda i: (i, 0)),
            pl.BlockSpec((tm, ATTN_W), lambda i: (i, 0)),
            pl.BlockSpec((tm, gw), lambda i: (i, grb)),
            pl.BlockSpec((tm, gw), lambda i: (i, grb + 1)),
            pl.BlockSpec((tm, gw), lambda i: (i, gab)),
            pl.BlockSpec((tm, gw), lambda i: (i, gab + 1)),
            pl.BlockSpec((tm, d), lambda i: (i, 0)),
            pl.BlockSpec((1, 1, d), lambda i: (i // bpb, 0, 0)),
            pl.BlockSpec((1, d), lambda i: (0, 0)),
            pl.BlockSpec((D_RNN, d), lambda i: (0, 0)),
            pl.BlockSpec((ATTN_W, d), lambda i: (0, 0)),
            pl.BlockSpec((d, d), lambda i: (0, 0)),
        ],
        out_specs=pl.BlockSpec((tm, d), lambda i: (i, 0)),
        out_shape=jax.ShapeDtypeStruct((t, d), F32),
        compiler_params=_params("parallel"),
        name="merge_out",
    )(y_rnn, y_attn, proj, proj, proj, proj, x, g1, ng.reshape(1, d), w_proj_rnn, w_proj_attn, w_out)


def _swiglu_partial(h, wg, wu, wd):
    gate = jnp.dot(h, wg, preferred_element_type=F32)
    up = jnp.dot(h, wu, preferred_element_type=F32)
    act = (gate * jax.nn.sigmoid(gate) * up).astype(BF16)
    return jnp.dot(act, wd, preferred_element_type=F32)


def _ffn_kernel(x_ref, g_ref, sh_ref, sc_ref, g2_ref, ng_ref, wg_ref, wu_ref, wd_ref, o_ref, h_ref, acc_ref):
    f = pl.program_id(1)

    @pl.when(f == 0)
    def _():
        h = _rms(x_ref[...]) * g_ref[...]
        h = h * (1.0 + sc_ref[0]) + sh_ref[0]
        h_ref[...] = h.astype(BF16)
        acc_ref[...] = jnp.zeros_like(acc_ref)

    acc_ref[...] += _swiglu_partial(h_ref[...], wg_ref[...], wu_ref[...], wd_ref[...])

    @pl.when(f == pl.num_programs(1) - 1)
    def _():
        o_ref[...] = x_ref[...] + g2_ref[0] * (_rms(acc_ref[...]) * ng_ref[...])


def ffn_residual(x, gain, shift, scale, g2, ng, w_gu, w_down, *, tm, tf):
    t, d = x.shape
    ff = w_gu.shape[1] // 2
    nf = ff // tf
    nb = shift.shape[0]
    assert t % (tm * nb) == 0 and ff % tf == 0, (t, tm, nb, ff, tf)
    bpb = (t // tm) // nb
    mod_spec = pl.BlockSpec((1, 1, d), lambda i, f: (i // bpb, 0, 0))
    vec_spec = pl.BlockSpec((1, d), lambda i, f: (0, 0))
    return pl.pallas_call(
        _ffn_kernel,
        grid=(t // tm, nf),
        in_specs=[
            pl.BlockSpec((tm, d), lambda i, f: (i, 0)), vec_spec, mod_spec, mod_spec, mod_spec, vec_spec,
            pl.BlockSpec((d, tf), lambda i, f: (0, f)),
            pl.BlockSpec((d, tf), lambda i, f: (0, nf + f)),
            pl.BlockSpec((tf, d), lambda i, f: (f, 0)),
        ],
        out_specs=pl.BlockSpec((tm, d), lambda i, f: (i, 0)),
        out_shape=jax.ShapeDtypeStruct((t, d), F32),
        scratch_shapes=[pltpu.VMEM((tm, d), BF16), pltpu.VMEM((tm, d), F32)],
        compiler_params=_params("parallel", "arbitrary"),
        name="ffn_residual",
    )(x, gain.reshape(1, d), shift, scale, g2, ng.reshape(1, d), w_gu, w_gu, w_down)


MOE_TM = 512
MOE_TF = 1408
ROW_TILE = 256
DMA_UNROLL = 8
META_E0, META_E1, META_W0, META_W1, META_R0, META_R1 = range(6)


def _lane_pick(rec, lane, k):
    return jnp.sum(jnp.where(lane == k, rec, 0.0), axis=-1, keepdims=True)


def _router_kernel(x_ref, g_ref, sh_ref, sc_ref, w_ref, b_ref, h_ref, meta_ref, cnt_ref, carry_s):
    @pl.when(pl.program_id(0) == 0)
    def _():
        carry_s[...] = jnp.zeros_like(carry_s)

    h = _rms(x_ref[...]) * g_ref[...]
    h = h * (1.0 + sc_ref[0]) + sh_ref[0]
    h_ref[...] = h
    logits = jnp.dot(h, w_ref[...], preferred_element_type=F32, precision=lax.Precision.HIGHEST) + b_ref[...]
    lane = lax.broadcasted_iota(jnp.int32, logits.shape, 1).astype(F32)
    neg = jnp.float32(-jnp.inf)
    logits = jnp.where(lane < N_EXPERTS, logits, neg)
    m1 = jnp.max(logits, axis=-1, keepdims=True)
    i1 = jnp.min(jnp.where(logits == m1, lane, float(LANES)), axis=-1, keepdims=True)
    rest = jnp.where(lane == i1, neg, logits)
    m2 = jnp.max(rest, axis=-1, keepdims=True)
    i2 = jnp.min(jnp.where(rest == m2, lane, float(LANES)), axis=-1, keepdims=True)
    e2 = jnp.exp(m2 - m1)
    denom = 1.0 + e2

    tm = logits.shape[0]
    sel1 = lane == i1
    sel2 = lane == i2
    member = jnp.where(jnp.logical_or(sel1, sel2), 1.0, 0.0)
    row = lax.broadcasted_iota(jnp.int32, (tm, tm), 0)
    col = lax.broadcasted_iota(jnp.int32, (tm, tm), 1)
    lower = jnp.where(col < row, 1.0, 0.0).astype(BF16)
    before = jnp.dot(lower, member.astype(BF16), preferred_element_type=F32) + carry_s[0:1, :]
    r1 = jnp.sum(jnp.where(sel1, before, 0.0), axis=-1, keepdims=True)
    r2 = jnp.sum(jnp.where(sel2, before, 0.0), axis=-1, keepdims=True)
    carry_s[0:1, :] = carry_s[0:1, :] + jnp.sum(member, axis=0, keepdims=True)
    cnt_ref[...] = jnp.broadcast_to(carry_s[0:1, :], cnt_ref.shape)

    meta = jnp.zeros_like(logits)
    for k, v in ((META_E0, i1), (META_E1, i2), (META_W0, 1.0 / denom), (META_W1, e2 / denom),
                 (META_R0, r1), (META_R1, r2)):
        meta = jnp.where(lane == k, v, meta)
    meta_ref[...] = meta


def moe_router(x, gain, shift, scale, router_w, router_b, *, tm):
    t, d = x.shape
    nb = shift.shape[0]
    bpb = (t // tm) // nb
    w = jnp.zeros((d, LANES), F32).at[:, :N_EXPERTS].set(router_w)
    b = jnp.zeros((1, LANES), F32).at[0, :N_EXPERTS].set(router_b)
    return pl.pallas_call(
        _router_kernel,
        grid=(t // tm,),
        in_specs=[
            pl.BlockSpec((tm, d), lambda i: (i, 0)),
            pl.BlockSpec((1, d), lambda i: (0, 0)),
            pl.BlockSpec((1, 1, d), lambda i: (i // bpb, 0, 0)),
            pl.BlockSpec((1, 1, d), lambda i: (i // bpb, 0, 0)),
            pl.BlockSpec((d, LANES), lambda i: (0, 0)),
            pl.BlockSpec((1, LANES), lambda i: (0, 0)),
        ],
        out_specs=[
            pl.BlockSpec((tm, d), lambda i: (i, 0)),
            pl.BlockSpec((tm, LANES), lambda i: (i, 0)),
            pl.BlockSpec((SUBLANES, LANES), lambda i: (0, 0)),
        ],
        out_shape=[
            jax.ShapeDtypeStruct((t, d), F32),
            jax.ShapeDtypeStruct((t, LANES), F32),
            jax.ShapeDtypeStruct((SUBLANES, LANES), F32),
        ],
        scratch_shapes=[pltpu.VMEM((SUBLANES, LANES), F32)],
        compiler_params=_params("arbitrary"),
        name="moe_router",
    )(x, gain.reshape(1, d), shift, scale, w, b)


def _row_copy(src_ref, src_row, dst_ref, dst_row, sem):
    return pltpu.make_async_copy(src_ref.at[pl.ds(src_row, 1), :], dst_ref.at[pl.ds(dst_row, 1), :], sem)


FILL_START, FILL_COUNT = 0, 1


def _dispatch_kernel(fill_ref, pos_ref, h_ref, hs_ref, buf_s, zero_s, sems):
    i = pl.program_id(0)
    n_steps = pl.num_programs(0)
    n = h_ref.shape[0]
    slot = i % 2
    zero_sem = sems.at[2]

    @pl.when(i == 0)
    def _():
        zero_s[...] = jnp.zeros_like(zero_s)

        def pad_copy(e, k):
            return _row_copy(zero_s, 0, hs_ref, fill_ref[FILL_START, e] + k, zero_sem)

        def tail_copy(k):
            row = pl.multiple_of(fill_ref[FILL_START, N_EXPERTS] + k * SUBLANES, SUBLANES)
            return pltpu.make_async_copy(zero_s, hs_ref.at[pl.ds(row, SUBLANES), :], zero_sem)

        def for_each(copy, count, op):
            def body(k, carry):
                getattr(copy(k), op)()
                return carry

            lax.fori_loop(0, count, body, 0)

        for op in ("start", "wait"):
            for e in range(N_EXPERTS):
                for_each(functools.partial(pad_copy, e), fill_ref[FILL_COUNT, e], op)
            for_each(tail_copy, fill_ref[FILL_COUNT, N_EXPERTS], op)

    buf_s[slot] = h_ref[...]
    src = buf_s.at[slot]
    sem = sems.at[slot]

    def issue(g, carry):
        for j in range(DMA_UNROLL):
            r = g * DMA_UNROLL + j
            _row_copy(src, r, hs_ref, pos_ref[0, 0, 2 * r], sem).start(priority=0)
            _row_copy(src, r, hs_ref, pos_ref[0, 0, 2 * r + 1], sem).start(priority=1)
        return carry

    lax.fori_loop(0, n // DMA_UNROLL, issue, 0)

    def drain(s):
        for _ in range(TOP_K):
            pltpu.make_async_copy(buf_s.at[s], hs_ref.at[pl.ds(0, n), :], sems.at[s]).wait()

    @pl.when(i > 0)
    def _():
        drain(1 - slot)

    @pl.when(i == n_steps - 1)
    def _():
        drain(slot)


def moe_dispatch(h, pos, fill, n_rows):
    t, d = h.shape
    return pl.pallas_call(
        _dispatch_kernel,
        grid_spec=pltpu.PrefetchScalarGridSpec(
            num_scalar_prefetch=1,
            grid=(t // ROW_TILE,),
            in_specs=[
                pl.BlockSpec((1, 1, 2 * ROW_TILE), lambda i, fill: (i, 0, 0), memory_space=pltpu.SMEM),
                pl.BlockSpec((ROW_TILE, d), lambda i, fill: (i, 0)),
            ],
            out_specs=pl.BlockSpec(memory_space=pl.ANY),
            scratch_shapes=[pltpu.VMEM((2, ROW_TILE, d), F32), pltpu.VMEM((SUBLANES, d), F32),
                            pltpu.SemaphoreType.DMA((3,))],
        ),
        out_shape=jax.ShapeDtypeStruct((n_rows, d), F32),
        compiler_params=_params("arbitrary"),
        name="moe_dispatch",
    )(fill, pos, h)


def _grouped_ffn_kernel(te_ref, na_ref, hs_ref, wg_ref, wu_ref, wd_ref, ys_ref, h_s, acc_s):
    del te_ref
    f = pl.program_id(1)
    active = pl.program_id(0) < na_ref[0]

    @pl.when(jnp.logical_not(active))
    def _():
        ys_ref[...] = jnp.zeros_like(ys_ref)

    @pl.when(active)
    def _():
        @pl.when(f == 0)
        def _():
            h_s[...] = hs_ref[...].astype(BF16)
            acc_s[...] = jnp.zeros_like(acc_s)

        acc_s[...] += _swiglu_partial(h_s[...], wg_ref[0], wu_ref[0], wd_ref[0])

        @pl.when(f == pl.num_programs(1) - 1)
        def _():
            ys_ref[...] = acc_s[...]


def moe_grouped_ffn(hs, tile_expert, n_active, w_gu, w_down):
    n_rows, d = hs.shape
    ff = w_gu.shape[2] // 2
    nf = ff // MOE_TF
    n_tiles = n_rows // MOE_TM

    def row_map(t, f, te, na):
        return (jnp.maximum(jnp.minimum(t, na[0] - 1), 0), 0)

    def f_eff(t, f, na):
        return jnp.where(t < na[0], f, nf - 1)

    return pl.pallas_call(
        _grouped_ffn_kernel,
        grid_spec=pltpu.PrefetchScalarGridSpec(
            num_scalar_prefetch=2,
            grid=(n_tiles, nf),
            in_specs=[
                pl.BlockSpec((MOE_TM, d), row_map),
                pl.BlockSpec((1, d, MOE_TF), lambda t, f, te, na: (te[t], 0, f_eff(t, f, na))),
                pl.BlockSpec((1, d, MOE_TF), lambda t, f, te, na: (te[t], 0, nf + f_eff(t, f, na))),
                pl.BlockSpec((1, MOE_TF, d), lambda t, f, te, na: (te[t], f_eff(t, f, na), 0)),
            ],
            out_specs=pl.BlockSpec((MOE_TM, d), lambda t, f, te, na: (t, 0)),
            scratch_shapes=[pltpu.VMEM((MOE_TM, d), BF16), pltpu.VMEM((MOE_TM, d), F32)],
        ),
        out_shape=jax.ShapeDtypeStruct((n_rows, d), F32),
        compiler_params=_params("arbitrary", "arbitrary"),
        name="moe_grouped_ffn",
    )(tile_expert, n_active, hs, w_gu, w_gu, w_down)


def _combine_kernel(pos_ref, pos_next_ref, ys_ref, x_ref, meta_ref, g2_ref, ng_ref, o_ref, a_s, b_s, sems):
    i = pl.program_id(0)
    n_steps = pl.num_programs(0)
    n = x_ref.shape[0]
    slot = i % 2

    def gather(p_ref, s):
        def issue(g, carry):
            for j in range(DMA_UNROLL):
                r = g * DMA_UNROLL + j
                _row_copy(ys_ref, p_ref[0, 0, 2 * r], a_s.at[s], r, sems.at[s]).start(priority=0)
                _row_copy(ys_ref, p_ref[0, 0, 2 * r + 1], b_s.at[s], r, sems.at[s]).start(priority=1)
            return carry

        lax.fori_loop(0, n // DMA_UNROLL, issue, 0)

    @pl.when(i == 0)
    def _():
        gather(pos_ref, 0)

    @pl.when(i + 1 < n_steps)
    def _():
        gather(pos_next_ref, 1 - slot)

    pltpu.make_async_copy(ys_ref.at[pl.ds(0, n), :], a_s.at[slot], sems.at[slot]).wait()
    pltpu.make_async_copy(ys_ref.at[pl.ds(0, n), :], b_s.at[slot], sems.at[slot]).wait()

    meta = meta_ref[...]
    lane = lax.broadcasted_iota(jnp.int32, meta.shape, 1)
    mixed = _lane_pick(meta, lane, META_W0) * a_s[slot] + _lane_pick(meta, lane, META_W1) * b_s[slot]
    o_ref[...] = x_ref[...] + g2_ref[0] * (_rms(mixed) * ng_ref[...])


def moe_combine(ys, pos, meta, x, g2, ng):
    t, d = x.shape
    nb = g2.shape[0]
    n_steps = t // ROW_TILE
    bpb = n_steps // nb
    return pl.pallas_call(
        _combine_kernel,
        grid=(n_steps,),
        in_specs=[
            pl.BlockSpec((1, 1, 2 * ROW_TILE), lambda i: (i, 0, 0), memory_space=pltpu.SMEM),
            pl.BlockSpec((1, 1, 2 * ROW_TILE), lambda i: (jnp.minimum(i + 1, n_steps - 1), 0, 0),
                         memory_space=pltpu.SMEM),
            pl.BlockSpec(memory_space=pl.ANY),
            pl.BlockSpec((ROW_TILE, d), lambda i: (i, 0)),
            pl.BlockSpec((ROW_TILE, LANES), lambda i: (i, 0)),
            pl.BlockSpec((1, 1, d), lambda i: (i // bpb, 0, 0)),
            pl.BlockSpec((1, d), lambda i: (0, 0)),
        ],
        out_specs=pl.BlockSpec((ROW_TILE, d), lambda i: (i, 0)),
        out_shape=jax.ShapeDtypeStruct((t, d), F32),
        scratch_shapes=[pltpu.VMEM((2, ROW_TILE, d), F32), pltpu.VMEM((2, ROW_TILE, d), F32),
                        pltpu.SemaphoreType.DMA((2,))],
        compiler_params=_params("arbitrary"),
        name="moe_combine",
    )(pos, pos, ys, x, meta, g2, ng.reshape(1, d))


def moe_residual(x, gain, shift, scale, g2, ng, router_w, router_b, w_gu, w_down):
    t, d = x.shape
    h, meta, counts = moe_router(x, gain, shift, scale, router_w, router_b, tm=512)
    n_rows = TOP_K * t + N_EXPERTS * MOE_TM
    n_tiles = n_rows // MOE_TM
    cnt = counts[0, :N_EXPERTS].astype(jnp.int32)
    padded = (cnt + MOE_TM - 1) // MOE_TM * MOE_TM
    seg_end = jnp.cumsum(padded)
    seg_start = seg_end - padded
    experts = jnp.arange(N_EXPERTS, dtype=jnp.int32)

    def position(e_lane, r_lane):
        e = meta[:, e_lane].astype(jnp.int32)
        start = jnp.sum(jnp.where(e[:, None] == experts[None, :], seg_start[None, :], 0), axis=-1)
        return start + meta[:, r_lane].astype(jnp.int32)

    pos = jnp.stack([position(META_E0, META_R0), position(META_E1, META_R1)], axis=-1)
    pos = pos.reshape(t // ROW_TILE, 1, 2 * ROW_TILE)
    n_active = seg_end[-1:] // MOE_TM
    tile_ids = jnp.arange(n_tiles, dtype=jnp.int32)
    tile_expert = jnp.sum((jnp.minimum(tile_ids, n_active - 1)[:, None] * MOE_TM >= seg_end[None, :]), axis=-1)
    tile_expert = jnp.minimum(tile_expert, N_EXPERTS - 1).astype(jnp.int32)

    fill = jnp.stack([
        jnp.concatenate([seg_start + cnt, seg_end[-1:], jnp.zeros((N_EXPERTS - 1,), jnp.int32)]),
        jnp.concatenate([padded - cnt, (n_rows - seg_end[-1:]) // SUBLANES, jnp.zeros((N_EXPERTS - 1,), jnp.int32)]),
    ]).astype(jnp.int32)
    hs = moe_dispatch(h, pos, fill, n_rows)
    ys = moe_grouped_ffn(hs, tile_expert, n_active.astype(jnp.int32), w_gu, w_down)
    return moe_combine(ys, pos, meta, x, g2, ng)


def _rope_tables():
    pos = jnp.arange(SEQ)
    inv_freq = jnp.power(ROPE_THETA, -jnp.arange(ROPE_FREQS, dtype=F32) / ROPE_FREQS)
    ang_r = (pos // GRID_W).astype(F32)[:, None] * inv_freq
    ang_c = (pos % GRID_W).astype(F32)[:, None] * inv_freq
    cos = jnp.concatenate([jnp.cos(ang_r)] * 2 + [jnp.cos(ang_c)] * 2, axis=-1)
    sin = jnp.concatenate([-jnp.sin(ang_r), jnp.sin(ang_r), -jnp.sin(ang_c), jnp.sin(ang_c)], axis=-1)
    return jnp.tile(cos, (1, 2)), jnp.tile(sin, (1, 2))


def _gate_layout(gate_w, gate_b, lam):
    c = RNN_CB
    n_cb = D_RNN // c
    per = c // RNN_BLOCK_W
    gw = gate_w.reshape(2, 2, n_cb, per, RNN_BLOCK_W, RNN_BLOCK_W)
    eye = jnp.eye(per, dtype=gate_w.dtype)
    bd = gw[:, :, :, :, :, None, :] * eye[None, None, None, :, None, :, None]
    bd = bd.reshape(2, 2, n_cb, c, c)
    wg = (0.5 * jnp.transpose(bd, (2, 3, 0, 1, 4))).reshape(n_cb, c, 4 * c).astype(BF16)
    gb = 0.5 * jnp.transpose(gate_b.reshape(2, 2, n_cb, c), (2, 0, 1, 3)).reshape(n_cb, 1, 4 * c)
    lm = jnp.transpose(lam.reshape(2, n_cb, c), (1, 0, 2)).reshape(n_cb, 1, 2 * c)
    return wg, gb, lm


def kernel(x, c, ctx, c_ctx, ada_w, ada_b, norm_g, w_in, conv_w, conv_b, lru_gate_w, lru_gate_b, lru_lambda,
           diff_lambda, subln_g, w_proj_rnn, w_proj_attn, w_out, ffn_w_gu, ffn_w_down, router_w, router_b,
           moe_w_gu, moe_w_down):
    xt = x.reshape(BATCH * SEQ, D_MODEL)
    ct = ctx.reshape(BATCH * CTX_LEN, D_MODEL)
    cos, sin_signed = _rope_tables()

    cvec = jnp.concatenate([c, c_ctx[None, :], jnp.zeros((2 * SUBLANES - BATCH - 1, D_MODEL), F32)], axis=0)
    mod = ada_modulation(cvec, ada_w, ada_b)

    for l in range(DEPTH):
        last = l == DEPTH - 1
        lam_init = 0.8 - 0.6 * math.exp(-0.3 * l)
        mx = mod[l, :BATCH].reshape(BATCH, 1, 6, D_MODEL)
        mc = mod[l, BATCH:BATCH + 1].reshape(1, 1, 6, D_MODEL)
        sh1x, sc1x, g1x, sh2x, sc2x, g2x = (mx[:, :, i] for i in range(6))
        sh1c, sc1c, g1c, sh2c, sc2c, g2c = (mc[:, :, i] for i in range(6))

        wg, gb, lm = _gate_layout(lru_gate_w[l], lru_gate_b[l], lru_lambda[l])

        proj_x = norm_mod_matmul(xt, norm_g[l, 0], sh1x, sc1x, w_in, l, IN_W, tm=2048, tn=768)
        if last:
            proj_c = norm_mod_matmul(ct, norm_g[l, 0], sh1c, sc1c, w_in, l, CTX_STATE_W, tm=2048, tn=256)
        else:
            proj_c = norm_mod_matmul(ct, norm_g[l, 0], sh1c, sc1c, w_in, l, IN_W, tm=2048, tn=768)

        y_rnn_x, y_rnn_c = rglru(proj_x, proj_c, conv_w[l], conv_b[l], wg, gb, lm, xr_col=COL_XR, gr_col=COL_GR,
                                 c_xr_col=COL_XR, c_gr_col=COL_GR, ctx_out=not last)
        y_attn_x = diff_attention(proj_x, COL_Q, SEQ, proj_x, proj_c, (COL_K, COL_V, COL_K, COL_V), cos,
                                  sin_signed, diff_lambda[l], subln_g[l], lam_init=lam_init, tq=1024)
        wr = w_proj_rnn[l].astype(BF16)
        wa = w_proj_attn[l].astype(BF16)
        wo = w_out[l].astype(BF16)
        xt = merge_out(y_rnn_x, y_attn_x, proj_x, xt, g1x, norm_g[l, 1], wr, wa, wo, tm=512)
        if not last:
            y_attn_c = diff_attention(proj_c, COL_Q, CTX_LEN, None, proj_c, (0, 0, COL_K, COL_V), None, None,
                                      diff_lambda[l], subln_g[l], lam_init=lam_init, tq=CTX_LEN)
            ct = merge_out(y_rnn_c, y_attn_c, proj_c, ct, g1c, norm_g[l, 1], wr, wa, wo, tm=512)

        if l % 2 == 0:
            w_gu = ffn_w_gu[l // 2].astype(BF16)
            w_dn = ffn_w_down[l // 2].astype(BF16)
            xt = ffn_residual(xt, norm_g[l, 2], sh2x, sc2x, g2x, norm_g[l, 3], w_gu, w_dn, tm=512, tf=1408)
            if not last:
                ct = ffn_residual(ct, norm_g[l, 2], sh2c, sc2c, g2c, norm_g[l, 3], w_gu, w_dn, tm=512, tf=1408)
        else:
            w_gu = moe_w_gu[l // 2].astype(BF16)
            w_dn = moe_w_down[l // 2].astype(BF16)
            rw, rb = router_w[l // 2], router_b[l // 2]
            xt = moe_residual(xt, norm_g[l, 2], sh2x, sc2x, g2x, norm_g[l, 3], rw, rb, w_gu, w_dn)
            if not last:
                ct = moe_residual(ct, norm_g[l, 2], sh2c, sc2c, g2c, norm_g[l, 3], rw, rb, w_gu, w_dn)
    return xt.reshape(BATCH, SEQ, D_MODEL)
```

```python
import functools
import math

import jax
import jax.numpy as jnp
from jax import lax
from jax.experimental import pallas as pl
from jax.experimental.pallas import tpu as pltpu

F32 = jnp.float32
BF16 = jnp.bfloat16

D_MODEL = 1024
BATCH = 8
SEQ = 2048
DEPTH = 2
CTX_LEN = 256
GRID_W = 64
EPS = 1e-6
D_RNN = 1280
RNN_BLOCKS = 20
RNN_BLOCK_W = D_RNN // RNN_BLOCKS
CONV_W = 4
LRU_C = 8.0
N_HEADS = 8
HEAD_DIM = 64
V_DIM = 2 * HEAD_DIM
QK_W = N_HEADS * 2 * HEAD_DIM
ATTN_W = N_HEADS * V_DIM
ROPE_THETA = 10000.0
ROPE_FREQS = HEAD_DIM // 4
D_FF = 2816
N_EXPERTS = 8
TOP_K = 2

LANES = 128
SUBLANES = 8
VMEM_LIMIT_BYTES = 52 * 1024 * 1024

COL_XR = 0
COL_K = COL_XR + D_RNN
COL_V = COL_K + QK_W
COL_GR = COL_V + ATTN_W
COL_Q = COL_GR + D_RNN
COL_G_RNN = COL_Q + QK_W
COL_G_ATTN = COL_G_RNN + D_MODEL
IN_W = COL_G_ATTN + D_MODEL
CTX_STATE_W = COL_GR
GATE_BLOCK_W = 512

RNN_CB = 256
SCAN_CHUNK = 256
ATTN_SUB = 128
NORM_ROWS = 512


def _params(*sem):
    return pltpu.CompilerParams(dimension_semantics=sem, vmem_limit_bytes=VMEM_LIMIT_BYTES)


def _rms(x):
    return x * lax.rsqrt(jnp.mean(x * x, axis=-1, keepdims=True) + EPS)


def _ada_kernel(c_ref, w_ref, b_ref, o_ref):
    c = c_ref[...]
    s = c * jax.nn.sigmoid(c)
    o_ref[0] = jnp.dot(s, w_ref[0], preferred_element_type=F32, precision=lax.Precision.HIGHEST) + b_ref[0]


def ada_modulation(cvec, ada_w, ada_b):
    rows = cvec.shape[0]
    tn = 1536
    n = 6 * D_MODEL
    return pl.pallas_call(
        _ada_kernel,
        grid=(DEPTH, n // tn),
        in_specs=[
            pl.BlockSpec((rows, D_MODEL), lambda l, j: (0, 0)),
            pl.BlockSpec((1, D_MODEL, tn), lambda l, j: (l, 0, j)),
            pl.BlockSpec((1, 1, tn), lambda l, j: (l, 0, j)),
        ],
        out_specs=pl.BlockSpec((1, rows, tn), lambda l, j: (l, 0, j)),
        out_shape=jax.ShapeDtypeStruct((DEPTH, rows, n), F32),
        compiler_params=_params("parallel", "parallel"),
        name="ada_modulation",
    )(cvec, ada_w, ada_b.reshape(DEPTH, 1, n))


def _nmm_kernel(x_ref, g_ref, sh_ref, sc_ref, w_ref, o_ref, h_ref):
    @pl.when(pl.program_id(1) == 0)
    def _():
        for r0 in range(0, x_ref.shape[0], NORM_ROWS):
            h = _rms(x_ref[r0:r0 + NORM_ROWS, :]) * g_ref[...]
            h = h * (1.0 + sc_ref[0]) + sh_ref[0]
            h_ref[r0:r0 + NORM_ROWS, :] = h.astype(BF16)

    w = w_ref[0].astype(BF16)
    o_ref[...] = jnp.dot(h_ref[...], w, preferred_element_type=F32).astype(o_ref.dtype)


def norm_mod_matmul(x, gain, shift, scale, w, layer, n, *, tm, tn):
    t, d = x.shape
    nb = shift.shape[0]
    assert t % (tm * nb) == 0 and n % tn == 0 and tm % NORM_ROWS == 0, (t, tm, nb, n, tn)
    bpb = (t // tm) // nb
    return pl.pallas_call(
        _nmm_kernel,
        grid=(t // tm, n // tn),
        in_specs=[
            pl.BlockSpec((tm, d), lambda i, j: (i, 0)),
            pl.BlockSpec((1, d), lambda i, j: (0, 0)),
            pl.BlockSpec((1, 1, d), lambda i, j: (i // bpb, 0, 0)),
            pl.BlockSpec((1, 1, d), lambda i, j: (i // bpb, 0, 0)),
            pl.BlockSpec((1, d, tn), lambda i, j: (layer, 0, j)),
        ],
        out_specs=pl.BlockSpec((tm, tn), lambda i, j: (i, j)),
        out_shape=jax.ShapeDtypeStruct((t, n), BF16),
        scratch_shapes=[pltpu.VMEM((tm, d), BF16)],
        compiler_params=_params("parallel", "arbitrary"),
        name="norm_mod_matmul",
    )(x, gain.reshape(1, d), shift, scale, w)


def _group_scan(a, b, reverse):
    rows, c = a.shape
    a = a.reshape(rows // SUBLANES, SUBLANES, c)
    b = b.reshape(rows // SUBLANES, SUBLANES, c)
    sub = lax.broadcasted_iota(jnp.int32, a.shape, 1)
    for sh in (1, 2, 4):
        if reverse:
            keep = sub < SUBLANES - sh
            amount = SUBLANES - sh
        else:
            keep = sub >= sh
            amount = sh
        a_sh = jnp.where(keep, pltpu.roll(a, amount, 1), 1.0)
        b_sh = jnp.where(keep, pltpu.roll(b, amount, 1), 0.0)
        b = a * b_sh + b
        a = a * a_sh
    return a.reshape(rows, c), b.reshape(rows, c)


def _dwconv(x, cw, cb):
    n = x.shape[0]
    rows = lax.broadcasted_iota(jnp.int32, x.shape, 0)
    xm2 = jnp.where(rows >= 2, pltpu.roll(x, 2, 0), 0.0)
    xm1 = jnp.where(rows >= 1, pltpu.roll(x, 1, 0), 0.0)
    xp1 = jnp.where(rows < n - 1, pltpu.roll(x, n - 1, 0), 0.0)
    return cb + xm2 * cw[0:1] + xm1 * cw[1:2] + x * cw[2:3] + xp1 * cw[3:4]


def _gelu_tanh(x):
    return 0.5 * x * (1.0 + jnp.tanh(math.sqrt(2.0 / math.pi) * (x + 0.044715 * (x * x * x))))


def _rglru_kernel(*refs, n_c, n_x, ctx_out):
    if ctx_out:
        (xrx_ref, grx_ref, xrc_ref, grc_ref, cw_ref, cb_ref, wg_ref, gb_ref, lam_ref,
         yx_ref, yc_ref, xc_s, a0_s, b0_s, a1_s, b1_s) = refs
    else:
        (xrx_ref, grx_ref, xrc_ref, cw_ref, cb_ref, wg_ref, gb_ref, lam_ref,
         yx_ref, xc_s, a0_s, b0_s, a1_s, b1_s) = refs
    c = RNN_CB
    n_all = n_c + n_x
    cw = cw_ref[...]
    cb = cb_ref[...]

    xc_s[0:n_c, :] = _dwconv(xrc_ref[...].astype(F32), cw, cb)
    xc_s[n_c:n_all, :] = _dwconv(xrx_ref[...].astype(F32), cw, cb)

    lam = lam_ref[0]
    half_neg_sp = (-0.5 * LRU_C) * jax.nn.softplus(-lam)

    def coeff_chunk(ci, carry):
        r0 = pl.multiple_of(ci * SCAN_CHUNK, SCAN_CHUNK)
        xc = xc_s[pl.ds(r0, SCAN_CHUNK), :]
        t = jnp.tanh(jnp.dot(xc.astype(BF16), wg_ref[0], preferred_element_type=F32) + gb_ref[0])
        half_xc = 0.5 * xc
        for d, (a_s, b_s) in enumerate(((a0_s, b0_s), (a1_s, b1_s))):
            t_r = t[:, (2 * d) * c:(2 * d + 1) * c]
            t_i = t[:, (2 * d + 1) * c:(2 * d + 2) * c]
            half_sp = half_neg_sp[:, d * c:(d + 1) * c]
            a = jnp.exp(half_sp + half_sp * t_r)
            one_m_a2 = 1.0 - a * a
            mult = jnp.where(one_m_a2 > 0.0, one_m_a2 * lax.rsqrt(one_m_a2), 0.0)
            bb = mult * (half_xc + half_xc * t_i)
            a_cum, b_cum = _group_scan(a, bb, reverse=(d == 1))
            a_s[pl.ds(r0, SCAN_CHUNK), :] = a_cum
            b_s[pl.ds(r0, SCAN_CHUNK), :] = b_cum
        return carry

    lax.fori_loop(0, n_all // SCAN_CHUNK, coeff_chunk, 0)

    g_c = n_c // SUBLANES
    g_all = n_all // SUBLANES

    def carry_step(k, carry):
        hf, hr = carry
        rf = pl.multiple_of(k * SUBLANES, SUBLANES)
        h = a0_s[pl.ds(rf, SUBLANES), :] * hf + b0_s[pl.ds(rf, SUBLANES), :]
        a0_s[pl.ds(rf, SUBLANES), :] = h
        hf = h[SUBLANES - 1:SUBLANES, :]
        kr = jnp.where(k < g_c, g_c - 1 - k, g_all + g_c - 1 - k)
        rr = pl.multiple_of(kr * SUBLANES, SUBLANES)
        h = a1_s[pl.ds(rr, SUBLANES), :] * hr + b1_s[pl.ds(rr, SUBLANES), :]
        a1_s[pl.ds(rr, SUBLANES), :] = h
        hr = h[0:1, :]
        return hf, hr

    zero = jnp.zeros((1, c), F32)
    lax.fori_loop(0, g_all, carry_step, (zero, zero), unroll=4)

    yx = (a0_s[n_c:n_all, :] + a1_s[n_c:n_all, :]) * _gelu_tanh(grx_ref[...].astype(F32))
    yx_ref[...] = yx.astype(yx_ref.dtype)
    if ctx_out:
        yc = (a0_s[0:n_c, :] + a1_s[0:n_c, :]) * _gelu_tanh(grc_ref[...].astype(F32))
        yc_ref[...] = yc.astype(yc_ref.dtype)


def rglru(proj_x, proj_c, conv_w, conv_b, wg, gb, lam, *, xr_col, gr_col, c_xr_col, c_gr_col, ctx_out):
    c = RNN_CB
    n_cb = D_RNN // c
    xr_b, gr_b, cxr_b = xr_col // c, gr_col // c, c_xr_col // c
    in_specs = [
        pl.BlockSpec((SEQ, c), lambda b, j: (b, xr_b + j)),
        pl.BlockSpec((SEQ, c), lambda b, j: (b, gr_b + j)),
        pl.BlockSpec((CTX_LEN, c), lambda b, j: (b, cxr_b + j)),
    ]
    args = [proj_x, proj_x, proj_c]
    if ctx_out:
        cgr_b = c_gr_col // c
        in_specs.append(pl.BlockSpec((CTX_LEN, c), lambda b, j: (b, cgr_b + j)))
        args.append(proj_c)
    in_specs += [
        pl.BlockSpec((CONV_W, c), lambda b, j: (0, j)),
        pl.BlockSpec((1, c), lambda b, j: (0, j)),
        pl.BlockSpec((1, c, 4 * c), lambda b, j: (j, 0, 0)),
        pl.BlockSpec((1, 1, 4 * c), lambda b, j: (j, 0, 0)),
        pl.BlockSpec((1, 1, 2 * c), lambda b, j: (j, 0, 0)),
    ]
    args += [conv_w, conv_b.reshape(1, D_RNN), wg, gb, lam]
    out_specs = [pl.BlockSpec((SEQ, c), lambda b, j: (b, j))]
    out_shape = [jax.ShapeDtypeStruct((BATCH * SEQ, D_RNN), BF16)]
    if ctx_out:
        out_specs.append(pl.BlockSpec((CTX_LEN, c), lambda b, j: (b, j)))
        out_shape.append(jax.ShapeDtypeStruct((BATCH * CTX_LEN, D_RNN), BF16))
    n_all = SEQ + CTX_LEN
    out = pl.pallas_call(
        functools.partial(_rglru_kernel, n_c=CTX_LEN, n_x=SEQ, ctx_out=ctx_out),
        grid=(BATCH, n_cb),
        in_specs=in_specs,
        out_specs=out_specs,
        out_shape=out_shape,
        scratch_shapes=[pltpu.VMEM((n_all, c), F32) for _ in range(5)],
        compiler_params=_params("parallel", "parallel"),
        name="rglru",
    )(*args)
    return out if ctx_out else (out[0], None)


def _rope(t, cos, sin_signed):
    lane = lax.broadcasted_iota(jnp.int32, t.shape, 1)
    first_half = (lane % (2 * ROPE_FREQS)) < ROPE_FREQS
    partner = jnp.where(first_half, pltpu.roll(t, LANES - ROPE_FREQS, 1), pltpu.roll(t, ROPE_FREQS, 1))
    return t * cos + partner * sin_signed


def _attn_kernel(*refs, tq, n_c, n_x, lam_init):
    if n_x:
        (q_ref, kx_ref, vx_ref, kc_ref, vc_ref, cos_ref, sin_ref, dl_ref, sg_ref,
         o_ref, k_s, v_s, lam_s) = refs
    else:
        q_ref, kc_ref, vc_ref, dl_ref, sg_ref, o_ref, k_s, v_s, lam_s = refs
    qi = pl.program_id(2)

    @pl.when(qi == 0)
    def _():
        k_s[0:n_c, :] = kc_ref[...]
        v_s[0:n_c, 0:LANES] = vc_ref[...]
        if n_x:
            kx = _rope(kx_ref[...].astype(F32), cos_ref[...], sin_ref[...])
            k_s[n_c:n_c + n_x, :] = kx.astype(BF16)
            v_s[n_c:n_c + n_x, 0:LANES] = vx_ref[...]
        v_s[:, LANES:2 * LANES] = jnp.ones((n_c + n_x, LANES), BF16)
        dl = dl_ref[...]
        e1 = jnp.exp(jnp.sum(dl[0:1] * dl[1:2], axis=-1, keepdims=True))
        e2 = jnp.exp(jnp.sum(dl[2:3] * dl[3:4], axis=-1, keepdims=True))
        lam_s[...] = jnp.broadcast_to(e1 - e2 + lam_init, lam_s.shape)

    q = q_ref[...].astype(F32)
    if n_x:
        r0 = pl.multiple_of(qi * tq, tq)
        q = _rope(q, cos_ref[pl.ds(r0, tq), :], sin_ref[pl.ds(r0, tq), :])
    q = q * (HEAD_DIM ** -0.5 * math.log2(math.e))
    lane = lax.broadcasted_iota(jnp.int32, (ATTN_SUB, LANES), 1)
    lam = lam_s[0:1, 0:1]
    scores = []
    for sb in range(tq // ATTN_SUB):
        qs = q[sb * ATTN_SUB:(sb + 1) * ATTN_SUB]
        qq = jnp.concatenate([jnp.where(lane < HEAD_DIM, qs, 0.0), jnp.where(lane >= HEAD_DIM, qs, 0.0)], axis=0)
        scores.append(lax.dot_general(qq.astype(BF16), k_s[...], (((1,), (1,)), ((), ())),
                                      preferred_element_type=F32))
    for sb, s in enumerate(scores):
        e = jnp.exp2(s - jnp.max(s, axis=-1, keepdims=True))
        ov = jnp.dot(e.astype(BF16), v_s[...], preferred_element_type=F32)
        ov = ov[:, 0:LANES] / ov[:, LANES:2 * LANES]
        o = ov[0:ATTN_SUB] - lam * ov[ATTN_SUB:2 * ATTN_SUB]
        o_ref[sb * ATTN_SUB:(sb + 1) * ATTN_SUB, :] = (_rms(o) * sg_ref[...] * (1.0 - lam_init)).astype(o_ref.dtype)


def diff_attention(q_arr, q_col, n_q, kvx, kvc, cols, cos, sin_signed, diff_lambda, subln_g, *, lam_init, tq):
    kx_col, vx_col, kc_col, vc_col = cols
    qb, kxb, vxb, kcb, vcb = (v // LANES for v in (q_col, kx_col, vx_col, kc_col, vc_col))
    n_x = SEQ if kvx is not None else 0
    n_qb = n_q // tq
    in_specs = [pl.BlockSpec((tq, LANES), lambda b, h, i: (b * n_qb + i, qb + h))]
    args = [q_arr]
    if n_x:
        in_specs += [
            pl.BlockSpec((SEQ, LANES), lambda b, h, i: (b, kxb + h)),
            pl.BlockSpec((SEQ, LANES), lambda b, h, i: (b, vxb + h)),
        ]
        args += [kvx, kvx]
    in_specs += [
        pl.BlockSpec((CTX_LEN, LANES), lambda b, h, i: (b, kcb + h)),
        pl.BlockSpec((CTX_LEN, LANES), lambda b, h, i: (b, vcb + h)),
    ]
    args += [kvc, kvc]
    if n_x:
        in_specs += [
            pl.BlockSpec((SEQ, LANES), lambda b, h, i: (0, 0)),
            pl.BlockSpec((SEQ, LANES), lambda b, h, i: (0, 0)),
        ]
        args += [cos, sin_signed]
    in_specs += [
        pl.BlockSpec((4, HEAD_DIM), lambda b, h, i: (0, 0)),
        pl.BlockSpec((1, V_DIM), lambda b, h, i: (0, 0)),
    ]
    args += [diff_lambda, subln_g.reshape(1, V_DIM)]
    n_kv = CTX_LEN + n_x
    return pl.pallas_call(
        functools.partial(_attn_kernel, tq=tq, n_c=CTX_LEN, n_x=n_x, lam_init=lam_init),
        grid=(BATCH, N_HEADS, n_qb),
        in_specs=in_specs,
        out_specs=pl.BlockSpec((tq, LANES), lambda b, h, i: (b * n_qb + i, h)),
        out_shape=jax.ShapeDtypeStruct((BATCH * n_q, ATTN_W), BF16),
        scratch_shapes=[
            pltpu.VMEM((n_kv, LANES), BF16),
            pltpu.VMEM((n_kv, 2 * LANES), BF16),
            pltpu.VMEM((SUBLANES, LANES), F32),
        ],
        compiler_params=_params("parallel", "parallel", "arbitrary"),
        name="diff_attention",
    )(*args)


def _merge_kernel(yr_ref, ya_ref, gr0_ref, gr1_ref, ga0_ref, ga1_ref, x_ref, g1_ref, ng_ref, wr_ref, wa_ref, wo_ref,
                  o_ref):
    y_rnn = jnp.dot(yr_ref[...], wr_ref[...], preferred_element_type=F32)
    y_attn = jnp.dot(ya_ref[...], wa_ref[...], preferred_element_type=F32)
    g_rnn = jax.nn.sigmoid(jnp.concatenate([gr0_ref[...], gr1_ref[...]], axis=1).astype(F32))
    g_attn = jax.nn.sigmoid(jnp.concatenate([ga0_ref[...], ga1_ref[...]], axis=1).astype(F32))
    m = (g_rnn * y_rnn + g_attn * y_attn).astype(BF16)
    mx = jnp.dot(m, wo_ref[...], preferred_element_type=F32)
    o_ref[...] = x_ref[...] + g1_ref[0] * (_rms(mx) * ng_ref[...])


def merge_out(y_rnn, y_attn, proj, x, g1, ng, w_proj_rnn, w_proj_attn, w_out, *, tm):
    t, d = x.shape
    nb = g1.shape[0]
    assert t % (tm * nb) == 0, (t, tm, nb)
    bpb = (t // tm) // nb
    gw = GATE_BLOCK_W
    grb, gab = COL_G_RNN // gw, COL_G_ATTN // gw
    return pl.pallas_call(
        _merge_kernel,
        grid=(t // tm,),
        in_specs=[
            pl.BlockSpec((tm, D_RNN), lambda i: (i, 0)),
            pl.BlockSpec((tm, ATTN_W), lambda i: (i, 0)),
            pl.BlockSpec((tm, gw), lambda i: (i, grb)),
            pl.BlockSpec((tm, gw), lambda i: (i, grb + 1)),
            pl.BlockSpec((tm, gw), lambda i: (i, gab)),
            pl.BlockSpec((tm, gw), lambda i: (i, gab + 1)),
            pl.BlockSpec((tm, d), lambda i: (i, 0)),
            pl.BlockSpec((1, 1, d), lambda i: (i // bpb, 0, 0)),
            pl.BlockSpec((1, d), lambda i: (0, 0)),
            pl.BlockSpec((D_RNN, d), lambda i: (0, 0)),
            pl.BlockSpec((ATTN_W, d), lambda i: (0, 0)),
            pl.BlockSpec((d, d), lambda i: (0, 0)),
        ],
        out_specs=pl.BlockSpec((tm, d), lambda i: (i, 0)),
        out_shape=jax.ShapeDtypeStruct((t, d), F32),
        compiler_params=_params("parallel"),
        name="merge_out",
    )(y_rnn, y_attn, proj, proj, proj, proj, x, g1, ng.reshape(1, d), w_proj_rnn, w_proj_attn, w_out)


def _swiglu_partial(h, wg, wu, wd):
    gate = jnp.dot(h, wg, preferred_element_type=F32)
    up = jnp.dot(h, wu, preferred_element_type=F32)
    act = (gate * jax.nn.sigmoid(gate) * up).astype(BF16)
    return jnp.dot(act, wd, preferred_element_type=F32)


def _ffn_kernel(x_ref, g_ref, sh_ref, sc_ref, g2_ref, ng_ref, wg_ref, wu_ref, wd_ref, o_ref, h_ref, acc_ref):
    f = pl.program_id(1)

    @pl.when(f == 0)
    def _():
        h = _rms(x_ref[...]) * g_ref[...]
        h = h * (1.0 + sc_ref[0]) + sh_ref[0]
        h_ref[...] = h.astype(BF16)
        acc_ref[...] = jnp.zeros_like(acc_ref)

    acc_ref[...] += _swiglu_partial(h_ref[...], wg_ref[...], wu_ref[...], wd_ref[...])

    @pl.when(f == pl.num_programs(1) - 1)
    def _():
        o_ref[...] = x_ref[...] + g2_ref[0] * (_rms(acc_ref[...]) * ng_ref[...])


def ffn_residual(x, gain, shift, scale, g2, ng, w_gu, w_down, *, tm, tf):
    t, d = x.shape
    ff = w_gu.shape[1] // 2
    nf = ff // tf
    nb = shift.shape[0]
    assert t % (tm * nb) == 0 and ff % tf == 0, (t, tm, nb, ff, tf)
    bpb = (t // tm) // nb
    mod_spec = pl.BlockSpec((1, 1, d), lambda i, f: (i // bpb, 0, 0))
    vec_spec = pl.BlockSpec((1, d), lambda i, f: (0, 0))
    return pl.pallas_call(
        _ffn_kernel,
        grid=(t // tm, nf),
        in_specs=[
            pl.BlockSpec((tm, d), lambda i, f: (i, 0)), vec_spec, mod_spec, mod_spec, mod_spec, vec_spec,
            pl.BlockSpec((d, tf), lambda i, f: (0, f)),
            pl.BlockSpec((d, tf), lambda i, f: (0, nf + f)),
            pl.BlockSpec((tf, d), lambda i, f: (f, 0)),
        ],
        out_specs=pl.BlockSpec((tm, d), lambda i, f: (i, 0)),
        out_shape=jax.ShapeDtypeStruct((t, d), F32),
        scratch_shapes=[pltpu.VMEM((tm, d), BF16), pltpu.VMEM((tm, d), F32)],
        compiler_params=_params("parallel", "arbitrary"),
        name="ffn_residual",
    )(x, gain.reshape(1, d), shift, scale, g2, ng.reshape(1, d), w_gu, w_gu, w_down)


MOE_TM = 512
MOE_TF = 1408
ROW_TILE = 256
DMA_UNROLL = 8
META_E0, META_E1, META_W0, META_W1, META_R0, META_R1 = range(6)


def _lane_pick(rec, lane, k):
    return jnp.sum(jnp.where(lane == k, rec, 0.0), axis=-1, keepdims=True)


def _router_kernel(x_ref, g_ref, sh_ref, sc_ref, w_ref, b_ref, h_ref, meta_ref, cnt_ref, carry_s):
    @pl.when(pl.program_id(0) == 0)
    def _():
        carry_s[...] = jnp.zeros_like(carry_s)

    h = _rms(x_ref[...]) * g_ref[...]
    h = h * (1.0 + sc_ref[0]) + sh_ref[0]
    h_ref[...] = h
    logits = jnp.dot(h, w_ref[...], preferred_element_type=F32, precision=lax.Precision.HIGHEST) + b_ref[...]
    lane = lax.broadcasted_iota(jnp.int32, logits.shape, 1).astype(F32)
    neg = jnp.float32(-jnp.inf)
    logits = jnp.where(lane < N_EXPERTS, logits, neg)
    m1 = jnp.max(logits, axis=-1, keepdims=True)
    i1 = jnp.min(jnp.where(logits == m1, lane, float(LANES)), axis=-1, keepdims=True)
    rest = jnp.where(lane == i1, neg, logits)
    m2 = jnp.max(rest, axis=-1, keepdims=True)
    i2 = jnp.min(jnp.where(rest == m2, lane, float(LANES)), axis=-1, keepdims=True)
    e2 = jnp.exp(m2 - m1)
    denom = 1.0 + e2

    tm = logits.shape[0]
    sel1 = lane == i1
    sel2 = lane == i2
    member = jnp.where(jnp.logical_or(sel1, sel2), 1.0, 0.0)
    row = lax.broadcasted_iota(jnp.int32, (tm, tm), 0)
    col = lax.broadcasted_iota(jnp.int32, (tm, tm), 1)
    lower = jnp.where(col < row, 1.0, 0.0).astype(BF16)
    before = jnp.dot(lower, member.astype(BF16), preferred_element_type=F32) + carry_s[0:1, :]
    r1 = jnp.sum(jnp.where(sel1, before, 0.0), axis=-1, keepdims=True)
    r2 = jnp.sum(jnp.where(sel2, before, 0.0), axis=-1, keepdims=True)
    carry_s[0:1, :] = carry_s[0:1, :] + jnp.sum(member, axis=0, keepdims=True)
    cnt_ref[...] = jnp.broadcast_to(carry_s[0:1, :], cnt_ref.shape)

    meta = jnp.zeros_like(logits)
    for k, v in ((META_E0, i1), (META_E1, i2), (META_W0, 1.0 / denom), (META_W1, e2 / denom),
                 (META_R0, r1), (META_R1, r2)):
        meta = jnp.where(lane == k, v, meta)
    meta_ref[...] = meta


def moe_router(x, gain, shift, scale, router_w, router_b, *, tm):
    t, d = x.shape
    nb = shift.shape[0]
    bpb = (t // tm) // nb
    w = jnp.zeros((d, LANES), F32).at[:, :N_EXPERTS].set(router_w)
    b = jnp.zeros((1, LANES), F32).at[0, :N_EXPERTS].set(router_b)
    return pl.pallas_call(
        _router_kernel,
        grid=(t // tm,),
        in_specs=[
            pl.BlockSpec((tm, d), lambda i: (i, 0)),
            pl.BlockSpec((1, d), lambda i: (0, 0)),
            pl.BlockSpec((1, 1, d), lambda i: (i // bpb, 0, 0)),
            pl.BlockSpec((1, 1, d), lambda i: (i // bpb, 0, 0)),
            pl.BlockSpec((d, LANES), lambda i: (0, 0)),
            pl.BlockSpec((1, LANES), lambda i: (0, 0)),
        ],
        out_specs=[
            pl.BlockSpec((tm, d), lambda i: (i, 0)),
            pl.BlockSpec((tm, LANES), lambda i: (i, 0)),
            pl.BlockSpec((SUBLANES, LANES), lambda i: (0, 0)),
        ],
        out_shape=[
            jax.ShapeDtypeStruct((t, d), F32),
            jax.ShapeDtypeStruct((t, LANES), F32),
            jax.ShapeDtypeStruct((SUBLANES, LANES), F32),
        ],
        scratch_shapes=[pltpu.VMEM((SUBLANES, LANES), F32)],
        compiler_params=_params("arbitrary"),
        name="moe_router",
    )(x, gain.reshape(1, d), shift, scale, w, b)


def _row_copy(src_ref, src_row, dst_ref, dst_row, sem):
    return pltpu.make_async_copy(src_ref.at[pl.ds(src_row, 1), :], dst_ref.at[pl.ds(dst_row, 1), :], sem)


FILL_START, FILL_COUNT = 0, 1


def _dispatch_kernel(fill_ref, pos_ref, h_ref, wgu_ref, wdn_ref, hs_ref, wgu_out_ref, wdn_out_ref, buf_s, zero_s,
                     sems):
    wgu_out_ref[...] = wgu_ref[...].astype(BF16)
    wdn_out_ref[...] = wdn_ref[...].astype(BF16)

    i = pl.program_id(0)
    n_steps = pl.num_programs(0)
    n = h_ref.shape[0]
    slot = i % 2
    zero_sem = sems.at[2]

    @pl.when(i == 0)
    def _():
        zero_s[...] = jnp.zeros_like(zero_s)

        def pad_copy(e, k):
            return _row_copy(zero_s, 0, hs_ref, fill_ref[FILL_START, e] + k, zero_sem)

        def tail_copy(k):
            row = pl.multiple_of(fill_ref[FILL_START, N_EXPERTS] + k * SUBLANES, SUBLANES)
            return pltpu.make_async_copy(zero_s, hs_ref.at[pl.ds(row, SUBLANES), :], zero_sem)

        def for_each(copy, count, op):
            def body(k, carry):
                getattr(copy(k), op)()
                return carry

            lax.fori_loop(0, count, body, 0)

        for op in ("start", "wait"):
            for e in range(N_EXPERTS):
                for_each(functools.partial(pad_copy, e), fill_ref[FILL_COUNT, e], op)
            for_each(tail_copy, fill_ref[FILL_COUNT, N_EXPERTS], op)

    buf_s[slot] = h_ref[...]
    src = buf_s.at[slot]
    sem = sems.at[slot]

    def issue(g, carry):
        for j in range(DMA_UNROLL):
            r = g * DMA_UNROLL + j
            _row_copy(src, r, hs_ref, pos_ref[0, 0, 2 * r], sem).start(priority=0)
            _row_copy(src, r, hs_ref, pos_ref[0, 0, 2 * r + 1], sem).start(priority=1)
        return carry

    lax.fori_loop(0, n // DMA_UNROLL, issue, 0)

    def drain(s):
        for _ in range(TOP_K):
            pltpu.make_async_copy(buf_s.at[s], hs_ref.at[pl.ds(0, n), :], sems.at[s]).wait()

    @pl.when(i > 0)
    def _():
        drain(1 - slot)

    @pl.when(i == n_steps - 1)
    def _():
        drain(slot)


def moe_dispatch(h, pos, fill, n_rows, w_gu, w_down):
    t, d = h.shape
    n_steps = t // ROW_TILE
    wgu2 = w_gu.reshape(-1, w_gu.shape[-1])
    wdn2 = w_down.reshape(-1, w_down.shape[-1])
    gu_rows, dn_rows = wgu2.shape[0] // n_steps, wdn2.shape[0] // n_steps
    bf16_rows = 2 * SUBLANES
    assert wgu2.shape[0] % n_steps == 0 and wdn2.shape[0] % n_steps == 0, (wgu2.shape, wdn2.shape, n_steps)
    assert gu_rows % bf16_rows == 0 and dn_rows % bf16_rows == 0, (gu_rows, dn_rows)
    hs, wgu_bf, wdn_bf = pl.pallas_call(
        _dispatch_kernel,
        grid_spec=pltpu.PrefetchScalarGridSpec(
            num_scalar_prefetch=1,
            grid=(n_steps,),
            in_specs=[
                pl.BlockSpec((1, 1, 2 * ROW_TILE), lambda i, fill: (i, 0, 0), memory_space=pltpu.SMEM),
                pl.BlockSpec((ROW_TILE, d), lambda i, fill: (i, 0)),
                pl.BlockSpec((gu_rows, wgu2.shape[1]), lambda i, fill: (i, 0)),
                pl.BlockSpec((dn_rows, wdn2.shape[1]), lambda i, fill: (i, 0)),
            ],
            out_specs=[
                pl.BlockSpec(memory_space=pl.ANY),
                pl.BlockSpec((gu_rows, wgu2.shape[1]), lambda i, fill: (i, 0)),
                pl.BlockSpec((dn_rows, wdn2.shape[1]), lambda i, fill: (i, 0)),
            ],
            scratch_shapes=[pltpu.VMEM((2, ROW_TILE, d), F32), pltpu.VMEM((SUBLANES, d), F32),
                            pltpu.SemaphoreType.DMA((3,))],
        ),
        out_shape=[
            jax.ShapeDtypeStruct((n_rows, d), F32),
            jax.ShapeDtypeStruct(wgu2.shape, BF16),
            jax.ShapeDtypeStruct(wdn2.shape, BF16),
        ],
        compiler_params=_params("arbitrary"),
        name="moe_dispatch",
    )(fill, pos, h, wgu2, wdn2)
    return hs, wgu_bf.reshape(w_gu.shape), wdn_bf.reshape(w_down.shape)


def _grouped_ffn_kernel(te_ref, na_ref, hs_ref, wg_ref, wu_ref, wd_ref, ys_ref, h_s, acc_s):
    del te_ref
    f = pl.program_id(1)
    active = pl.program_id(0) < na_ref[0]

    @pl.when(jnp.logical_not(active))
    def _():
        ys_ref[...] = jnp.zeros_like(ys_ref)

    @pl.when(active)
    def _():
        @pl.when(f == 0)
        def _():
            h_s[...] = hs_ref[...].astype(BF16)
            acc_s[...] = jnp.zeros_like(acc_s)

        acc_s[...] += _swiglu_partial(h_s[...], wg_ref[0], wu_ref[0], wd_ref[0])

        @pl.when(f == pl.num_programs(1) - 1)
        def _():
            ys_ref[...] = acc_s[...]


def moe_grouped_ffn(hs, tile_expert, n_active, w_gu, w_down):
    n_rows, d = hs.shape
    ff = w_gu.shape[2] // 2
    nf = ff // MOE_TF
    n_tiles = n_rows // MOE_TM

    def row_map(t, f, te, na):
        return (jnp.maximum(jnp.minimum(t, na[0] - 1), 0), 0)

    def f_eff(t, f, na):
        return jnp.where(t < na[0], f, nf - 1)

    return pl.pallas_call(
        _grouped_ffn_kernel,
        grid_spec=pltpu.PrefetchScalarGridSpec(
            num_scalar_prefetch=2,
            grid=(n_tiles, nf),
            in_specs=[
                pl.BlockSpec((MOE_TM, d), row_map),
                pl.BlockSpec((1, d, MOE_TF), lambda t, f, te, na: (te[t], 0, f_eff(t, f, na))),
                pl.BlockSpec((1, d, MOE_TF), lambda t, f, te, na: (te[t], 0, nf + f_eff(t, f, na))),
                pl.BlockSpec((1, MOE_TF, d), lambda t, f, te, na: (te[t], f_eff(t, f, na), 0)),
            ],
            out_specs=pl.BlockSpec((MOE_TM, d), lambda t, f, te, na: (t, 0)),
            scratch_shapes=[pltpu.VMEM((MOE_TM, d), BF16), pltpu.VMEM((MOE_TM, d), F32)],
        ),
        out_shape=jax.ShapeDtypeStruct((n_rows, d), F32),
        compiler_params=_params("arbitrary", "arbitrary"),
        name="moe_grouped_ffn",
    )(tile_expert, n_active, hs, w_gu, w_gu, w_down)


def _combine_kernel(pos_ref, pos_next_ref, ys_ref, x_ref, meta_ref, g2_ref, ng_ref, o_ref, a_s, b_s, sems):
    i = pl.program_id(0)
    n_steps = pl.num_programs(0)
    n = x_ref.shape[0]
    slot = i % 2

    def gather(p_ref, s):
        def issue(g, carry):
            for j in range(DMA_UNROLL):
                r = g * DMA_UNROLL + j
                _row_copy(ys_ref, p_ref[0, 0, 2 * r], a_s.at[s], r, sems.at[s]).start(priority=0)
                _row_copy(ys_ref, p_ref[0, 0, 2 * r + 1], b_s.at[s], r, sems.at[s]).start(priority=1)
            return carry

        lax.fori_loop(0, n // DMA_UNROLL, issue, 0)

    @pl.when(i == 0)
    def _():
        gather(pos_ref, 0)

    @pl.when(i + 1 < n_steps)
    def _():
        gather(pos_next_ref, 1 - slot)

    pltpu.make_async_copy(ys_ref.at[pl.ds(0, n), :], a_s.at[slot], sems.at[slot]).wait()
    pltpu.make_async_copy(ys_ref.at[pl.ds(0, n), :], b_s.at[slot], sems.at[slot]).wait()

    meta = meta_ref[...]
    lane = lax.broadcasted_iota(jnp.int32, meta.shape, 1)
    mixed = _lane_pick(meta, lane, META_W0) * a_s[slot] + _lane_pick(meta, lane, META_W1) * b_s[slot]
    o_ref[...] = x_ref[...] + g2_ref[0] * (_rms(mixed) * ng_ref[...])


def moe_combine(ys, pos, meta, x, g2, ng):
    t, d = x.shape
    nb = g2.shape[0]
    n_steps = t // ROW_TILE
    bpb = n_steps // nb
    return pl.pallas_call(
        _combine_kernel,
        grid=(n_steps,),
        in_specs=[
            pl.BlockSpec((1, 1, 2 * ROW_TILE), lambda i: (i, 0, 0), memory_space=pltpu.SMEM),
            pl.BlockSpec((1, 1, 2 * ROW_TILE), lambda i: (jnp.minimum(i + 1, n_steps - 1), 0, 0),
                         memory_space=pltpu.SMEM),
            pl.BlockSpec(memory_space=pl.ANY),
            pl.BlockSpec((ROW_TILE, d), lambda i: (i, 0)),
            pl.BlockSpec((ROW_TILE, LANES), lambda i: (i, 0)),
            pl.BlockSpec((1, 1, d), lambda i: (i // bpb, 0, 0)),
            pl.BlockSpec((1, d), lambda i: (0, 0)),
        ],
        out_specs=pl.BlockSpec((ROW_TILE, d), lambda i: (i, 0)),
        out_shape=jax.ShapeDtypeStruct((t, d), F32),
        scratch_shapes=[pltpu.VMEM((2, ROW_TILE, d), F32), pltpu.VMEM((2, ROW_TILE, d), F32),
                        pltpu.SemaphoreType.DMA((2,))],
        compiler_params=_params("arbitrary"),
        name="moe_combine",
    )(pos, pos, ys, x, meta, g2, ng.reshape(1, d))


def moe_residual(x, gain, shift, scale, g2, ng, router_w, router_b, w_gu, w_down):
    t, d = x.shape
    h, meta, counts = moe_router(x, gain, shift, scale, router_w, router_b, tm=512)
    n_rows = TOP_K * t + N_EXPERTS * MOE_TM
    n_tiles = n_rows // MOE_TM
    cnt = counts[0, :N_EXPERTS].astype(jnp.int32)
    padded = (cnt + MOE_TM - 1) // MOE_TM * MOE_TM
    seg_end = jnp.cumsum(padded)
    seg_start = seg_end - padded
    experts = jnp.arange(N_EXPERTS, dtype=jnp.int32)

    def position(e_lane, r_lane):
        e = meta[:, e_lane].astype(jnp.int32)
        start = jnp.sum(jnp.where(e[:, None] == experts[None, :], seg_start[None, :], 0), axis=-1)
        return start + meta[:, r_lane].astype(jnp.int32)

    pos = jnp.stack([position(META_E0, META_R0), position(META_E1, META_R1)], axis=-1)
    pos = pos.reshape(t // ROW_TILE, 1, 2 * ROW_TILE)
    n_active = seg_end[-1:] // MOE_TM
    tile_ids = jnp.arange(n_tiles, dtype=jnp.int32)
    tile_expert = jnp.sum((jnp.minimum(tile_ids, n_active - 1)[:, None] * MOE_TM >= seg_end[None, :]), axis=-1)
    tile_expert = jnp.minimum(tile_expert, N_EXPERTS - 1).astype(jnp.int32)

    fill = jnp.stack([
        jnp.concatenate([seg_start + cnt, seg_end[-1:], jnp.zeros((N_EXPERTS - 1,), jnp.int32)]),
        jnp.concatenate([padded - cnt, (n_rows - seg_end[-1:]) // SUBLANES, jnp.zeros((N_EXPERTS - 1,), jnp.int32)]),
    ]).astype(jnp.int32)
    hs, w_gu_bf, w_down_bf = moe_dispatch(h, pos, fill, n_rows, w_gu, w_down)
    ys = moe_grouped_ffn(hs, tile_expert, n_active.astype(jnp.int32), w_gu_bf, w_down_bf)
    return moe_combine(ys, pos, meta, x, g2, ng)


def _rope_tables():
    pos = jnp.arange(SEQ)
    inv_freq = jnp.power(ROPE_THETA, -jnp.arange(ROPE_FREQS, dtype=F32) / ROPE_FREQS)
    ang_r = (pos // GRID_W).astype(F32)[:, None] * inv_freq
    ang_c = (pos % GRID_W).astype(F32)[:, None] * inv_freq
    cos = jnp.concatenate([jnp.cos(ang_r)] * 2 + [jnp.cos(ang_c)] * 2, axis=-1)
    sin = jnp.concatenate([-jnp.sin(ang_r), jnp.sin(ang_r), -jnp.sin(ang_c), jnp.sin(ang_c)], axis=-1)
    return jnp.tile(cos, (1, 2)), jnp.tile(sin, (1, 2))


def _gate_layout(gate_w, gate_b, lam):
    c = RNN_CB
    n_cb = D_RNN // c
    per = c // RNN_BLOCK_W
    gw = gate_w.reshape(2, 2, n_cb, per, RNN_BLOCK_W, RNN_BLOCK_W)
    eye = jnp.eye(per, dtype=gate_w.dtype)
    bd = gw[:, :, :, :, :, None, :] * eye[None, None, None, :, None, :, None]
    bd = bd.reshape(2, 2, n_cb, c, c)
    wg = (0.5 * jnp.transpose(bd, (2, 3, 0, 1, 4))).reshape(n_cb, c, 4 * c).astype(BF16)
    gb = 0.5 * jnp.transpose(gate_b.reshape(2, 2, n_cb, c), (2, 0, 1, 3)).reshape(n_cb, 1, 4 * c)
    lm = jnp.transpose(lam.reshape(2, n_cb, c), (1, 0, 2)).reshape(n_cb, 1, 2 * c)
    return wg, gb, lm


def kernel(x, c, ctx, c_ctx, ada_w, ada_b, norm_g, w_in, conv_w, conv_b, lru_gate_w, lru_gate_b, lru_lambda,
           diff_lambda, subln_g, w_proj_rnn, w_proj_attn, w_out, ffn_w_gu, ffn_w_down, router_w, router_b,
           moe_w_gu, moe_w_down):
    xt = x.reshape(BATCH * SEQ, D_MODEL)
    ct = ctx.reshape(BATCH * CTX_LEN, D_MODEL)
    cos, sin_signed = _rope_tables()

    cvec = jnp.concatenate([c, c_ctx[None, :], jnp.zeros((2 * SUBLANES - BATCH - 1, D_MODEL), F32)], axis=0)
    mod = ada_modulation(cvec, ada_w, ada_b)

    for l in range(DEPTH):
        last = l == DEPTH - 1
        lam_init = 0.8 - 0.6 * math.exp(-0.3 * l)
        mx = mod[l, :BATCH].reshape(BATCH, 1, 6, D_MODEL)
        mc = mod[l, BATCH:BATCH + 1].reshape(1, 1, 6, D_MODEL)
        sh1x, sc1x, g1x, sh2x, sc2x, g2x = (mx[:, :, i] for i in range(6))
        sh1c, sc1c, g1c, sh2c, sc2c, g2c = (mc[:, :, i] for i in range(6))

        wg, gb, lm = _gate_layout(lru_gate_w[l], lru_gate_b[l], lru_lambda[l])

        proj_x = norm_mod_matmul(xt, norm_g[l, 0], sh1x, sc1x, w_in, l, IN_W, tm=2048, tn=768)
        if last:
            proj_c = norm_mod_matmul(ct, norm_g[l, 0], sh1c, sc1c, w_in, l, CTX_STATE_W, tm=2048, tn=256)
        else:
            proj_c = norm_mod_matmul(ct, norm_g[l, 0], sh1c, sc1c, w_in, l, IN_W, tm=2048, tn=768)

        y_rnn_x, y_rnn_c = rglru(proj_x, proj_c, conv_w[l], conv_b[l], wg, gb, lm, xr_col=COL_XR, gr_col=COL_GR,
                                 c_xr_col=COL_XR, c_gr_col=COL_GR, ctx_out=not last)
        y_attn_x = diff_attention(proj_x, COL_Q, SEQ, proj_x, proj_c, (COL_K, COL_V, COL_K, COL_V), cos,
                                  sin_signed, diff_lambda[l], subln_g[l], lam_init=lam_init, tq=1024)
        wr = w_proj_rnn[l].astype(BF16)
        wa = w_proj_attn[l].astype(BF16)
        wo = w_out[l].astype(BF16)
        xt = merge_out(y_rnn_x, y_attn_x, proj_x, xt, g1x, norm_g[l, 1], wr, wa, wo, tm=512)
        if not last:
            y_attn_c = diff_attention(proj_c, COL_Q, CTX_LEN, None, proj_c, (0, 0, COL_K, COL_V), None, None,
                                      diff_lambda[l], subln_g[l], lam_init=lam_init, tq=CTX_LEN)
            ct = merge_out(y_rnn_c, y_attn_c, proj_c, ct, g1c, norm_g[l, 1], wr, wa, wo, tm=512)

        if l % 2 == 0:
            w_gu = ffn_w_gu[l // 2].astype(BF16)
            w_dn = ffn_w_down[l // 2].astype(BF16)
            xt = ffn_residual(xt, norm_g[l, 2], sh2x, sc2x, g2x, norm_g[l, 3], w_gu, w_dn, tm=512, tf=1408)
            if not last:
                ct = ffn_residual(ct, norm_g[l, 2], sh2c, sc2c, g2c, norm_g[l, 3], w_gu, w_dn, tm=512, tf=1408)
        else:
            w_gu, w_dn = moe_w_gu[l // 2], moe_w_down[l // 2]
            rw, rb = router_w[l // 2], router_b[l // 2]
            xt = moe_residual(xt, norm_g[l, 2], sh2x, sc2x, g2x, norm_g[l, 3], rw, rb, w_gu, w_dn)
            if not last:
                ct = moe_residual(ct, norm_g[l, 2], sh2c, sc2c, g2c, norm_g[l, 3], rw, rb, w_gu, w_dn)
    return xt.reshape(BATCH, SEQ, D_MODEL)
```

```python
import functools
import math

import jax
import jax.numpy as jnp
from jax import lax
from jax.experimental import pallas as pl
from jax.experimental.pallas import tpu as pltpu

F32 = jnp.float32
BF16 = jnp.bfloat16

D_MODEL = 1024
BATCH = 8
SEQ = 2048
DEPTH = 2
CTX_LEN = 256
GRID_W = 64
EPS = 1e-6
D_RNN = 1280
RNN_BLOCKS = 20
RNN_BLOCK_W = D_RNN // RNN_BLOCKS
CONV_W = 4
LRU_C = 8.0
N_HEADS = 8
HEAD_DIM = 64
V_DIM = 2 * HEAD_DIM
QK_W = N_HEADS * 2 * HEAD_DIM
ATTN_W = N_HEADS * V_DIM
ROPE_THETA = 10000.0
ROPE_FREQS = HEAD_DIM // 4
D_FF = 2816
N_EXPERTS = 8
TOP_K = 2

LANES = 128
SUBLANES = 8
VMEM_LIMIT_BYTES = 52 * 1024 * 1024

COL_XR = 0
COL_K = COL_XR + D_RNN
COL_V = COL_K + QK_W
COL_GR = COL_V + ATTN_W
COL_Q = COL_GR + D_RNN
COL_G_RNN = COL_Q + QK_W
COL_G_ATTN = COL_G_RNN + D_MODEL
IN_W = COL_G_ATTN + D_MODEL
CTX_STATE_W = COL_GR
GATE_BLOCK_W = 512

RNN_CB = 256
SCAN_CHUNK = 256
ATTN_SUB = 128
NORM_ROWS = 512


def _params(*sem):
    return pltpu.CompilerParams(dimension_semantics=sem, vmem_limit_bytes=VMEM_LIMIT_BYTES)


def _rms(x):
    return x * lax.rsqrt(jnp.mean(x * x, axis=-1, keepdims=True) + EPS)


def _ada_kernel(c_ref, w_ref, b_ref, o_ref):
    c = c_ref[...]
    s = c * jax.nn.sigmoid(c)
    o_ref[0] = jnp.dot(s, w_ref[0], preferred_element_type=F32, precision=lax.Precision.HIGHEST) + b_ref[0]


def ada_modulation(cvec, ada_w, ada_b):
    rows = cvec.shape[0]
    tn = 1536
    n = 6 * D_MODEL
    return pl.pallas_call(
        _ada_kernel,
        grid=(DEPTH, n // tn),
        in_specs=[
            pl.BlockSpec((rows, D_MODEL), lambda l, j: (0, 0)),
            pl.BlockSpec((1, D_MODEL, tn), lambda l, j: (l, 0, j)),
            pl.BlockSpec((1, 1, tn), lambda l, j: (l, 0, j)),
        ],
        out_specs=pl.BlockSpec((1, rows, tn), lambda l, j: (l, 0, j)),
        out_shape=jax.ShapeDtypeStruct((DEPTH, rows, n), F32),
        compiler_params=_params("parallel", "parallel"),
        name="ada_modulation",
    )(cvec, ada_w, ada_b.reshape(DEPTH, 1, n))


def _nmm_kernel(x_ref, g_ref, sh_ref, sc_ref, w_ref, o_ref, h_ref):
    @pl.when(pl.program_id(1) == 0)
    def _():
        for r0 in range(0, x_ref.shape[0], NORM_ROWS):
            h = _rms(x_ref[r0:r0 + NORM_ROWS, :]) * g_ref[...]
            h = h * (1.0 + sc_ref[0]) + sh_ref[0]
            h_ref[r0:r0 + NORM_ROWS, :] = h.astype(BF16)

    w = w_ref[0].astype(BF16)
    o_ref[...] = jnp.dot(h_ref[...], w, preferred_element_type=F32).astype(o_ref.dtype)


def norm_mod_matmul(x, gain, shift, scale, w, layer, n, *, tm, tn):
    t, d = x.shape
    nb = shift.shape[0]
    assert t % (tm * nb) == 0 and n % tn == 0 and tm % NORM_ROWS == 0, (t, tm, nb, n, tn)
    bpb = (t // tm) // nb
    return pl.pallas_call(
        _nmm_kernel,
        grid=(t // tm, n // tn),
        in_specs=[
            pl.BlockSpec((tm, d), lambda i, j: (i, 0)),
            pl.BlockSpec((1, d), lambda i, j: (0, 0)),
            pl.BlockSpec((1, 1, d), lambda i, j: (i // bpb, 0, 0)),
            pl.BlockSpec((1, 1, d), lambda i, j: (i // bpb, 0, 0)),
            pl.BlockSpec((1, d, tn), lambda i, j: (layer, 0, j)),
        ],
        out_specs=pl.BlockSpec((tm, tn), lambda i, j: (i, j)),
        out_shape=jax.ShapeDtypeStruct((t, n), BF16),
        scratch_shapes=[pltpu.VMEM((tm, d), BF16)],
        compiler_params=_params("parallel", "arbitrary"),
        name="norm_mod_matmul",
    )(x, gain.reshape(1, d), shift, scale, w)


def _group_scan(a, b, reverse):
    rows, c = a.shape
    a = a.reshape(rows // SUBLANES, SUBLANES, c)
    b = b.reshape(rows // SUBLANES, SUBLANES, c)
    sub = lax.broadcasted_iota(jnp.int32, a.shape, 1)
    for sh in (1, 2, 4):
        if reverse:
            keep = sub < SUBLANES - sh
            amount = SUBLANES - sh
        else:
            keep = sub >= sh
            amount = sh
        a_sh = jnp.where(keep, pltpu.roll(a, amount, 1), 1.0)
        b_sh = jnp.where(keep, pltpu.roll(b, amount, 1), 0.0)
        b = a * b_sh + b
        a = a * a_sh
    return a.reshape(rows, c), b.reshape(rows, c)


def _dwconv(x, cw, cb):
    n = x.shape[0]
    rows = lax.broadcasted_iota(jnp.int32, x.shape, 0)
    xm2 = jnp.where(rows >= 2, pltpu.roll(x, 2, 0), 0.0)
    xm1 = jnp.where(rows >= 1, pltpu.roll(x, 1, 0), 0.0)
    xp1 = jnp.where(rows < n - 1, pltpu.roll(x, n - 1, 0), 0.0)
    return cb + xm2 * cw[0:1] + xm1 * cw[1:2] + x * cw[2:3] + xp1 * cw[3:4]


def _gelu_tanh(x):
    return 0.5 * x * (1.0 + jnp.tanh(math.sqrt(2.0 / math.pi) * (x + 0.044715 * (x * x * x))))


def _rglru_kernel(*refs, n_c, n_x, ctx_out):
    if ctx_out:
        (xrx_ref, grx_ref, xrc_ref, grc_ref, cw_ref, cb_ref, wg_ref, gb_ref, lam_ref,
         yx_ref, yc_ref, xc_s, a0_s, b0_s, a1_s, b1_s) = refs
    else:
        (xrx_ref, grx_ref, xrc_ref, cw_ref, cb_ref, wg_ref, gb_ref, lam_ref,
         yx_ref, xc_s, a0_s, b0_s, a1_s, b1_s) = refs
    c = RNN_CB
    n_all = n_c + n_x
    cw = cw_ref[...]
    cb = cb_ref[...]

    xc_s[0:n_c, :] = _dwconv(xrc_ref[...].astype(F32), cw, cb)
    xc_s[n_c:n_all, :] = _dwconv(xrx_ref[...].astype(F32), cw, cb)

    lam = lam_ref[0]
    half_neg_sp = (-0.5 * LRU_C) * jax.nn.softplus(-lam)

    def coeff_chunk(ci, carry):
        r0 = pl.multiple_of(ci * SCAN_CHUNK, SCAN_CHUNK)
        xc = xc_s[pl.ds(r0, SCAN_CHUNK), :]
        t = jnp.tanh(jnp.dot(xc.astype(BF16), wg_ref[0], preferred_element_type=F32) + gb_ref[0])
        half_xc = 0.5 * xc
        for d, (a_s, b_s) in enumerate(((a0_s, b0_s), (a1_s, b1_s))):
            t_r = t[:, (2 * d) * c:(2 * d + 1) * c]
            t_i = t[:, (2 * d + 1) * c:(2 * d + 2) * c]
            half_sp = half_neg_sp[:, d * c:(d + 1) * c]
            a = jnp.exp(half_sp + half_sp * t_r)
            one_m_a2 = 1.0 - a * a
            mult = jnp.where(one_m_a2 > 0.0, one_m_a2 * lax.rsqrt(one_m_a2), 0.0)
            bb = mult * (half_xc + half_xc * t_i)
            a_cum, b_cum = _group_scan(a, bb, reverse=(d == 1))
            a_s[pl.ds(r0, SCAN_CHUNK), :] = a_cum
            b_s[pl.ds(r0, SCAN_CHUNK), :] = b_cum
        return carry

    lax.fori_loop(0, n_all // SCAN_CHUNK, coeff_chunk, 0)

    g_c = n_c // SUBLANES
    g_all = n_all // SUBLANES

    def carry_step(k, carry):
        hf, hr = carry
        rf = pl.multiple_of(k * SUBLANES, SUBLANES)
        h = a0_s[pl.ds(rf, SUBLANES), :] * hf + b0_s[pl.ds(rf, SUBLANES), :]
        a0_s[pl.ds(rf, SUBLANES), :] = h
        hf = h[SUBLANES - 1:SUBLANES, :]
        kr = jnp.where(k < g_c, g_c - 1 - k, g_all + g_c - 1 - k)
        rr = pl.multiple_of(kr * SUBLANES, SUBLANES)
        h = a1_s[pl.ds(rr, SUBLANES), :] * hr + b1_s[pl.ds(rr, SUBLANES), :]
        a1_s[pl.ds(rr, SUBLANES), :] = h
        hr = h[0:1, :]
        return hf, hr

    zero = jnp.zeros((1, c), F32)
    lax.fori_loop(0, g_all, carry_step, (zero, zero), unroll=4)

    yx = (a0_s[n_c:n_all, :] + a1_s[n_c:n_all, :]) * _gelu_tanh(grx_ref[...].astype(F32))
    yx_ref[...] = yx.astype(yx_ref.dtype)
    if ctx_out:
        yc = (a0_s[0:n_c, :] + a1_s[0:n_c, :]) * _gelu_tanh(grc_ref[...].astype(F32))
        yc_ref[...] = yc.astype(yc_ref.dtype)


def rglru(proj_x, proj_c, conv_w, conv_b, wg, gb, lam, *, xr_col, gr_col, c_xr_col, c_gr_col, ctx_out):
    c = RNN_CB
    n_cb = D_RNN // c
    xr_b, gr_b, cxr_b = xr_col // c, gr_col // c, c_xr_col // c
    in_specs = [
        pl.BlockSpec((SEQ, c), lambda b, j: (b, xr_b + j)),
        pl.BlockSpec((SEQ, c), lambda b, j: (b, gr_b + j)),
        pl.BlockSpec((CTX_LEN, c), lambda b, j: (b, cxr_b + j)),
    ]
    args = [proj_x, proj_x, proj_c]
    if ctx_out:
        cgr_b = c_gr_col // c
        in_specs.append(pl.BlockSpec((CTX_LEN, c), lambda b, j: (b, cgr_b + j)))
        args.append(proj_c)
    in_specs += [
        pl.BlockSpec((CONV_W, c), lambda b, j: (0, j)),
        pl.BlockSpec((1, c), lambda b, j: (0, j)),
        pl.BlockSpec((1, c, 4 * c), lambda b, j: (j, 0, 0)),
        pl.BlockSpec((1, 1, 4 * c), lambda b, j: (j, 0, 0)),
        pl.BlockSpec((1, 1, 2 * c), lambda b, j: (j, 0, 0)),
    ]
    args += [conv_w, conv_b.reshape(1, D_RNN), wg, gb, lam]
    out_specs = [pl.BlockSpec((SEQ, c), lambda b, j: (b, j))]
    out_shape = [jax.ShapeDtypeStruct((BATCH * SEQ, D_RNN), BF16)]
    if ctx_out:
        out_specs.append(pl.BlockSpec((CTX_LEN, c), lambda b, j: (b, j)))
        out_shape.append(jax.ShapeDtypeStruct((BATCH * CTX_LEN, D_RNN), BF16))
    n_all = SEQ + CTX_LEN
    out = pl.pallas_call(
        functools.partial(_rglru_kernel, n_c=CTX_LEN, n_x=SEQ, ctx_out=ctx_out),
        grid=(BATCH, n_cb),
        in_specs=in_specs,
        out_specs=out_specs,
        out_shape=out_shape,
        scratch_shapes=[pltpu.VMEM((n_all, c), F32) for _ in range(5)],
        compiler_params=_params("parallel", "parallel"),
        name="rglru",
    )(*args)
    return out if ctx_out else (out[0], None)


def _rope(t, cos, sin_signed):
    lane = lax.broadcasted_iota(jnp.int32, t.shape, 1)
    first_half = (lane % (2 * ROPE_FREQS)) < ROPE_FREQS
    partner = jnp.where(first_half, pltpu.roll(t, LANES - ROPE_FREQS, 1), pltpu.roll(t, ROPE_FREQS, 1))
    return t * cos + partner * sin_signed


def _attn_kernel(*refs, tq, n_c, n_x, lam_init):
    if n_x:
        (q_ref, kx_ref, vx_ref, kc_ref, vc_ref, cos_ref, sin_ref, dl_ref, sg_ref,
         o_ref, k_s, v_s, lam_s) = refs
    else:
        q_ref, kc_ref, vc_ref, dl_ref, sg_ref, o_ref, k_s, v_s, lam_s = refs
    qi = pl.program_id(2)

    @pl.when(qi == 0)
    def _():
        k_s[0:n_c, :] = kc_ref[...]
        v_s[0:n_c, 0:LANES] = vc_ref[...]
        if n_x:
            kx = _rope(kx_ref[...].astype(F32), cos_ref[...], sin_ref[...])
            k_s[n_c:n_c + n_x, :] = kx.astype(BF16)
            v_s[n_c:n_c + n_x, 0:LANES] = vx_ref[...]
        v_s[:, LANES:2 * LANES] = jnp.ones((n_c + n_x, LANES), BF16)
        dl = dl_ref[...]
        e1 = jnp.exp(jnp.sum(dl[0:1] * dl[1:2], axis=-1, keepdims=True))
        e2 = jnp.exp(jnp.sum(dl[2:3] * dl[3:4], axis=-1, keepdims=True))
        lam_s[...] = jnp.broadcast_to(e1 - e2 + lam_init, lam_s.shape)

    q = q_ref[...].astype(F32)
    if n_x:
        r0 = pl.multiple_of(qi * tq, tq)
        q = _rope(q, cos_ref[pl.ds(r0, tq), :], sin_ref[pl.ds(r0, tq), :])
    q = q * (HEAD_DIM ** -0.5 * math.log2(math.e))
    lane = lax.broadcasted_iota(jnp.int32, (ATTN_SUB, LANES), 1)
    lam = lam_s[0:1, 0:1]
    scores = []
    for sb in range(tq // ATTN_SUB):
        qs = q[sb * ATTN_SUB:(sb + 1) * ATTN_SUB]
        qq = jnp.concatenate([jnp.where(lane < HEAD_DIM, qs, 0.0), jnp.where(lane >= HEAD_DIM, qs, 0.0)], axis=0)
        scores.append(lax.dot_general(qq.astype(BF16), k_s[...], (((1,), (1,)), ((), ())),
                                      preferred_element_type=F32))
    for sb, s in enumerate(scores):
        e = jnp.exp2(s - jnp.max(s, axis=-1, keepdims=True))
        ov = jnp.dot(e.astype(BF16), v_s[...], preferred_element_type=F32)
        ov = ov[:, 0:LANES] / ov[:, LANES:2 * LANES]
        o = ov[0:ATTN_SUB] - lam * ov[ATTN_SUB:2 * ATTN_SUB]
        o_ref[sb * ATTN_SUB:(sb + 1) * ATTN_SUB, :] = (_rms(o) * sg_ref[...] * (1.0 - lam_init)).astype(o_ref.dtype)


def diff_attention(q_arr, q_col, n_q, kvx, kvc, cols, cos, sin_signed, diff_lambda, subln_g, *, lam_init, tq):
    kx_col, vx_col, kc_col, vc_col = cols
    qb, kxb, vxb, kcb, vcb = (v // LANES for v in (q_col, kx_col, vx_col, kc_col, vc_col))
    n_x = SEQ if kvx is not None else 0
    n_qb = n_q // tq
    in_specs = [pl.BlockSpec((tq, LANES), lambda b, h, i: (b * n_qb + i, qb + h))]
    args = [q_arr]
    if n_x:
        in_specs += [
            pl.BlockSpec((SEQ, LANES), lambda b, h, i: (b, kxb + h)),
            pl.BlockSpec((SEQ, LANES), lambda b, h, i: (b, vxb + h)),
        ]
        args += [kvx, kvx]
    in_specs += [
        pl.BlockSpec((CTX_LEN, LANES), lambda b, h, i: (b, kcb + h)),
        pl.BlockSpec((CTX_LEN, LANES), lambda b, h, i: (b, vcb + h)),
    ]
    args += [kvc, kvc]
    if n_x:
        in_specs += [
            pl.BlockSpec((SEQ, LANES), lambda b, h, i: (0, 0)),
            pl.BlockSpec((SEQ, LANES), lambda b, h, i: (0, 0)),
        ]
        args += [cos, sin_signed]
    in_specs += [
        pl.BlockSpec((4, HEAD_DIM), lambda b, h, i: (0, 0)),
        pl.BlockSpec((1, V_DIM), lambda b, h, i: (0, 0)),
    ]
    args += [diff_lambda, subln_g.reshape(1, V_DIM)]
    n_kv = CTX_LEN + n_x
    return pl.pallas_call(
        functools.partial(_attn_kernel, tq=tq, n_c=CTX_LEN, n_x=n_x, lam_init=lam_init),
        grid=(BATCH, N_HEADS, n_qb),
        in_specs=in_specs,
        out_specs=pl.BlockSpec((tq, LANES), lambda b, h, i: (b * n_qb + i, h)),
        out_shape=jax.ShapeDtypeStruct((BATCH * n_q, ATTN_W), BF16),
        scratch_shapes=[
            pltpu.VMEM((n_kv, LANES), BF16),
            pltpu.VMEM((n_kv, 2 * LANES), BF16),
            pltpu.VMEM((SUBLANES, LANES), F32),
        ],
        compiler_params=_params("parallel", "parallel", "arbitrary"),
        name="diff_attention",
    )(*args)


def _merge_kernel(yr_ref, ya_ref, gr0_ref, gr1_ref, ga0_ref, ga1_ref, x_ref, g1_ref, ng_ref, wr_ref, wa_ref, wo_ref,
                  o_ref):
    y_rnn = jnp.dot(yr_ref[...], wr_ref[...], preferred_element_type=F32)
    y_attn = jnp.dot(ya_ref[...], wa_ref[...], preferred_element_type=F32)
    g_rnn = jax.nn.sigmoid(jnp.concatenate([gr0_ref[...], gr1_ref[...]], axis=1).astype(F32))
    g_attn = jax.nn.sigmoid(jnp.concatenate([ga0_ref[...], ga1_ref[...]], axis=1).astype(F32))
    m = (g_rnn * y_rnn + g_attn * y_attn).astype(BF16)
    mx = jnp.dot(m, wo_ref[...], preferred_element_type=F32)
    o_ref[...] = x_ref[...] + g1_ref[0] * (_rms(mx) * ng_ref[...])


def merge_out(y_rnn, y_attn, proj, x, g1, ng, w_proj_rnn, w_proj_attn, w_out, *, tm):
    t, d = x.shape
    nb = g1.shape[0]
    assert t % (tm * nb) == 0, (t, tm, nb)
    bpb = (t // tm) // nb
    gw = GATE_BLOCK_W
    grb, gab = COL_G_RNN // gw, COL_G_ATTN // gw
    return pl.pallas_call(
        _merge_kernel,
        grid=(t // tm,),
        in_specs=[
            pl.BlockSpec((tm, D_RNN), lambda i: (i, 0)),
            pl.BlockSpec((tm, ATTN_W), lambda i: (i, 0)),
            pl.BlockSpec((tm, gw), lambda i: (i, grb)),
            pl.BlockSpec((tm, gw), lambda i: (i, grb + 1)),
            pl.BlockSpec((tm, gw), lambda i: (i, gab)),
            pl.BlockSpec((tm, gw), lambda i: (i, gab + 1)),
            pl.BlockSpec((tm, d), lambda i: (i, 0)),
            pl.BlockSpec((1, 1, d), lambda i: (i // bpb, 0, 0)),
            pl.BlockSpec((1, d), lambda i: (0, 0)),
            pl.BlockSpec((D_RNN, d), lambda i: (0, 0)),
            pl.BlockSpec((ATTN_W, d), lambda i: (0, 0)),
            pl.BlockSpec((d, d), lambda i: (0, 0)),
        ],
        out_specs=pl.BlockSpec((tm, d), lambda i: (i, 0)),
        out_shape=jax.ShapeDtypeStruct((t, d), F32),
        compiler_params=_params("parallel"),
        name="merge_out",
    )(y_rnn, y_attn, proj, proj, proj, proj, x, g1, ng.reshape(1, d), w_proj_rnn, w_proj_attn, w_out)


def _swiglu_partial(h, wg, wu, wd):
    gate = jnp.dot(h, wg, preferred_element_type=F32)
    up = jnp.dot(h, wu, preferred_element_type=F32)
    act = (gate * jax.nn.sigmoid(gate) * up).astype(BF16)
    return jnp.dot(act, wd, preferred_element_type=F32)


def _ffn_kernel(x_ref, g_ref, sh_ref, sc_ref, g2_ref, ng_ref, wg_ref, wu_ref, wd_ref, o_ref, h_ref, acc_ref):
    f = pl.program_id(1)

    @pl.when(f == 0)
    def _():
        h = _rms(x_ref[...]) * g_ref[...]
        h = h * (1.0 + sc_ref[0]) + sh_ref[0]
        h_ref[...] = h.astype(BF16)
        acc_ref[...] = jnp.zeros_like(acc_ref)

    acc_ref[...] += _swiglu_partial(h_ref[...], wg_ref[...], wu_ref[...], wd_ref[...])

    @pl.when(f == pl.num_programs(1) - 1)
    def _():
        o_ref[...] = x_ref[...] + g2_ref[0] * (_rms(acc_ref[...]) * ng_ref[...])


def ffn_residual(x, gain, shift, scale, g2, ng, w_gu, w_down, *, tm, tf):
    t, d = x.shape
    ff = w_gu.shape[1] // 2
    nf = ff // tf
    nb = shift.shape[0]
    assert t % (tm * nb) == 0 and ff % tf == 0, (t, tm, nb, ff, tf)
    bpb = (t // tm) // nb
    mod_spec = pl.BlockSpec((1, 1, d), lambda i, f: (i // bpb, 0, 0))
    vec_spec = pl.BlockSpec((1, d), lambda i, f: (0, 0))
    return pl.pallas_call(
        _ffn_kernel,
        grid=(t // tm, nf),
        in_specs=[
            pl.BlockSpec((tm, d), lambda i, f: (i, 0)), vec_spec, mod_spec, mod_spec, mod_spec, vec_spec,
            pl.BlockSpec((d, tf), lambda i, f: (0, f)),
            pl.BlockSpec((d, tf), lambda i, f: (0, nf + f)),
            pl.BlockSpec((tf, d), lambda i, f: (f, 0)),
        ],
        out_specs=pl.BlockSpec((tm, d), lambda i, f: (i, 0)),
        out_shape=jax.ShapeDtypeStruct((t, d), F32),
        scratch_shapes=[pltpu.VMEM((tm, d), BF16), pltpu.VMEM((tm, d), F32)],
        compiler_params=_params("parallel", "arbitrary"),
        name="ffn_residual",
    )(x, gain.reshape(1, d), shift, scale, g2, ng.reshape(1, d), w_gu, w_gu, w_down)


MOE_TM = 512
MOE_TF = 1408
ROW_TILE = 256
DMA_UNROLL = 8
META_E0, META_E1, META_W0, META_W1, META_R0, META_R1 = range(6)


def _lane_pick(rec, lane, k):
    return jnp.sum(jnp.where(lane == k, rec, 0.0), axis=-1, keepdims=True)


def _router_kernel(x_ref, g_ref, sh_ref, sc_ref, w_ref, b_ref, h_ref, meta_ref, meta_t_ref, cnt_ref, carry_s):
    @pl.when(pl.program_id(0) == 0)
    def _():
        carry_s[...] = jnp.zeros_like(carry_s)

    h = _rms(x_ref[...]) * g_ref[...]
    h = h * (1.0 + sc_ref[0]) + sh_ref[0]
    h_ref[...] = h
    logits = jnp.dot(h, w_ref[...], preferred_element_type=F32, precision=lax.Precision.HIGHEST) + b_ref[...]
    lane = lax.broadcasted_iota(jnp.int32, logits.shape, 1).astype(F32)
    neg = jnp.float32(-jnp.inf)
    logits = jnp.where(lane < N_EXPERTS, logits, neg)
    m1 = jnp.max(logits, axis=-1, keepdims=True)
    i1 = jnp.min(jnp.where(logits == m1, lane, float(LANES)), axis=-1, keepdims=True)
    rest = jnp.where(lane == i1, neg, logits)
    m2 = jnp.max(rest, axis=-1, keepdims=True)
    i2 = jnp.min(jnp.where(rest == m2, lane, float(LANES)), axis=-1, keepdims=True)
    e2 = jnp.exp(m2 - m1)
    denom = 1.0 + e2

    tm = logits.shape[0]
    sel1 = lane == i1
    sel2 = lane == i2
    member = jnp.where(jnp.logical_or(sel1, sel2), 1.0, 0.0)
    row = lax.broadcasted_iota(jnp.int32, (tm, tm), 0)
    col = lax.broadcasted_iota(jnp.int32, (tm, tm), 1)
    lower = jnp.where(col < row, 1.0, 0.0).astype(BF16)
    before = jnp.dot(lower, member.astype(BF16), preferred_element_type=F32) + carry_s[0:1, :]
    r1 = jnp.sum(jnp.where(sel1, before, 0.0), axis=-1, keepdims=True)
    r2 = jnp.sum(jnp.where(sel2, before, 0.0), axis=-1, keepdims=True)
    carry_s[0:1, :] = carry_s[0:1, :] + jnp.sum(member, axis=0, keepdims=True)
    cnt_ref[...] = jnp.broadcast_to(carry_s[0:1, :], cnt_ref.shape)

    meta = jnp.zeros_like(logits)
    for k, v in ((META_E0, i1), (META_E1, i2), (META_W0, 1.0 / denom), (META_W1, e2 / denom),
                 (META_R0, r1), (META_R1, r2)):
        meta = jnp.where(lane == k, v, meta)
    meta_ref[...] = meta
    meta_t_ref[...] = meta.T[0:SUBLANES, :]


def moe_router(x, gain, shift, scale, router_w, router_b, *, tm):
    t, d = x.shape
    nb = shift.shape[0]
    bpb = (t // tm) // nb
    w = jnp.zeros((d, LANES), F32).at[:, :N_EXPERTS].set(router_w)
    b = jnp.zeros((1, LANES), F32).at[0, :N_EXPERTS].set(router_b)
    return pl.pallas_call(
        _router_kernel,
        grid=(t // tm,),
        in_specs=[
            pl.BlockSpec((tm, d), lambda i: (i, 0)),
            pl.BlockSpec((1, d), lambda i: (0, 0)),
            pl.BlockSpec((1, 1, d), lambda i: (i // bpb, 0, 0)),
            pl.BlockSpec((1, 1, d), lambda i: (i // bpb, 0, 0)),
            pl.BlockSpec((d, LANES), lambda i: (0, 0)),
            pl.BlockSpec((1, LANES), lambda i: (0, 0)),
        ],
        out_specs=[
            pl.BlockSpec((tm, d), lambda i: (i, 0)),
            pl.BlockSpec((tm, LANES), lambda i: (i, 0)),
            pl.BlockSpec((SUBLANES, tm), lambda i: (0, i)),
            pl.BlockSpec((SUBLANES, LANES), lambda i: (0, 0)),
        ],
        out_shape=[
            jax.ShapeDtypeStruct((t, d), F32),
            jax.ShapeDtypeStruct((t, LANES), F32),
            jax.ShapeDtypeStruct((SUBLANES, t), F32),
            jax.ShapeDtypeStruct((SUBLANES, LANES), F32),
        ],
        scratch_shapes=[pltpu.VMEM((SUBLANES, LANES), F32)],
        compiler_params=_params("arbitrary"),
        name="moe_router",
    )(x, gain.reshape(1, d), shift, scale, w, b)


def _row_copy(src_ref, src_row, dst_ref, dst_row, sem):
    return pltpu.make_async_copy(src_ref.at[pl.ds(src_row, 1), :], dst_ref.at[pl.ds(dst_row, 1), :], sem)


FILL_START, FILL_COUNT = 0, 1


def _dispatch_kernel(fill_ref, pos_ref, h_ref, wgu_ref, wdn_ref, hs_ref, wgu_out_ref, wdn_out_ref, buf_s, zero_s,
                     sems):
    wgu_out_ref[...] = wgu_ref[...].astype(BF16)
    wdn_out_ref[...] = wdn_ref[...].astype(BF16)

    i = pl.program_id(0)
    n_steps = pl.num_programs(0)
    n = h_ref.shape[0]
    slot = i % 2
    zero_sem = sems.at[2]

    @pl.when(i == 0)
    def _():
        zero_s[...] = jnp.zeros_like(zero_s)

        def pad_copy(e, k):
            return _row_copy(zero_s, 0, hs_ref, fill_ref[FILL_START, e] + k, zero_sem)

        def tail_copy(k):
            row = pl.multiple_of(fill_ref[FILL_START, N_EXPERTS] + k * SUBLANES, SUBLANES)
            return pltpu.make_async_copy(zero_s, hs_ref.at[pl.ds(row, SUBLANES), :], zero_sem)

        def for_each(copy, count, op):
            def body(k, carry):
                getattr(copy(k), op)()
                return carry

            lax.fori_loop(0, count, body, 0)

        for op in ("start", "wait"):
            for e in range(N_EXPERTS):
                for_each(functools.partial(pad_copy, e), fill_ref[FILL_COUNT, e], op)
            for_each(tail_copy, fill_ref[FILL_COUNT, N_EXPERTS], op)

    buf_s[slot] = h_ref[...]
    src = buf_s.at[slot]
    sem = sems.at[slot]

    def issue(g, carry):
        for j in range(DMA_UNROLL):
            r = g * DMA_UNROLL + j
            _row_copy(src, r, hs_ref, pos_ref[0, 0, 2 * r], sem).start(priority=0)
            _row_copy(src, r, hs_ref, pos_ref[0, 0, 2 * r + 1], sem).start(priority=1)
        return carry

    lax.fori_loop(0, n // DMA_UNROLL, issue, 0)

    def drain(s):
        for _ in range(TOP_K):
            pltpu.make_async_copy(buf_s.at[s], hs_ref.at[pl.ds(0, n), :], sems.at[s]).wait()

    @pl.when(i > 0)
    def _():
        drain(1 - slot)

    @pl.when(i == n_steps - 1)
    def _():
        drain(slot)


def moe_dispatch(h, pos, fill, n_rows, w_gu, w_down):
    t, d = h.shape
    n_steps = t // ROW_TILE
    wgu2 = w_gu.reshape(-1, w_gu.shape[-1])
    wdn2 = w_down.reshape(-1, w_down.shape[-1])
    gu_rows, dn_rows = wgu2.shape[0] // n_steps, wdn2.shape[0] // n_steps
    bf16_rows = 2 * SUBLANES
    assert wgu2.shape[0] % n_steps == 0 and wdn2.shape[0] % n_steps == 0, (wgu2.shape, wdn2.shape, n_steps)
    assert gu_rows % bf16_rows == 0 and dn_rows % bf16_rows == 0, (gu_rows, dn_rows)
    hs, wgu_bf, wdn_bf = pl.pallas_call(
        _dispatch_kernel,
        grid_spec=pltpu.PrefetchScalarGridSpec(
            num_scalar_prefetch=1,
            grid=(n_steps,),
            in_specs=[
                pl.BlockSpec((1, 1, 2 * ROW_TILE), lambda i, fill: (i, 0, 0), memory_space=pltpu.SMEM),
                pl.BlockSpec((ROW_TILE, d), lambda i, fill: (i, 0)),
                pl.BlockSpec((gu_rows, wgu2.shape[1]), lambda i, fill: (i, 0)),
                pl.BlockSpec((dn_rows, wdn2.shape[1]), lambda i, fill: (i, 0)),
            ],
            out_specs=[
                pl.BlockSpec(memory_space=pl.ANY),
                pl.BlockSpec((gu_rows, wgu2.shape[1]), lambda i, fill: (i, 0)),
                pl.BlockSpec((dn_rows, wdn2.shape[1]), lambda i, fill: (i, 0)),
            ],
            scratch_shapes=[pltpu.VMEM((2, ROW_TILE, d), F32), pltpu.VMEM((SUBLANES, d), F32),
                            pltpu.SemaphoreType.DMA((3,))],
        ),
        out_shape=[
            jax.ShapeDtypeStruct((n_rows, d), F32),
            jax.ShapeDtypeStruct(wgu2.shape, BF16),
            jax.ShapeDtypeStruct(wdn2.shape, BF16),
        ],
        compiler_params=_params("arbitrary"),
        name="moe_dispatch",
    )(fill, pos, h, wgu2, wdn2)
    return hs, wgu_bf.reshape(w_gu.shape), wdn_bf.reshape(w_down.shape)


def _grouped_ffn_kernel(te_ref, na_ref, hs_ref, wg_ref, wu_ref, wd_ref, ys_ref, h_s, acc_s):
    del te_ref
    f = pl.program_id(1)
    active = pl.program_id(0) < na_ref[0]

    @pl.when(jnp.logical_not(active))
    def _():
        ys_ref[...] = jnp.zeros_like(ys_ref)

    @pl.when(active)
    def _():
        @pl.when(f == 0)
        def _():
            h_s[...] = hs_ref[...].astype(BF16)
            acc_s[...] = jnp.zeros_like(acc_s)

        acc_s[...] += _swiglu_partial(h_s[...], wg_ref[0], wu_ref[0], wd_ref[0])

        @pl.when(f == pl.num_programs(1) - 1)
        def _():
            ys_ref[...] = acc_s[...]


def moe_grouped_ffn(hs, tile_expert, n_active, w_gu, w_down):
    n_rows, d = hs.shape
    ff = w_gu.shape[2] // 2
    nf = ff // MOE_TF
    n_tiles = n_rows // MOE_TM

    def row_map(t, f, te, na):
        return (jnp.maximum(jnp.minimum(t, na[0] - 1), 0), 0)

    def f_eff(t, f, na):
        return jnp.where(t < na[0], f, nf - 1)

    return pl.pallas_call(
        _grouped_ffn_kernel,
        grid_spec=pltpu.PrefetchScalarGridSpec(
            num_scalar_prefetch=2,
            grid=(n_tiles, nf),
            in_specs=[
                pl.BlockSpec((MOE_TM, d), row_map),
                pl.BlockSpec((1, d, MOE_TF), lambda t, f, te, na: (te[t], 0, f_eff(t, f, na))),
                pl.BlockSpec((1, d, MOE_TF), lambda t, f, te, na: (te[t], 0, nf + f_eff(t, f, na))),
                pl.BlockSpec((1, MOE_TF, d), lambda t, f, te, na: (te[t], f_eff(t, f, na), 0)),
            ],
            out_specs=pl.BlockSpec((MOE_TM, d), lambda t, f, te, na: (t, 0)),
            scratch_shapes=[pltpu.VMEM((MOE_TM, d), BF16), pltpu.VMEM((MOE_TM, d), F32)],
        ),
        out_shape=jax.ShapeDtypeStruct((n_rows, d), F32),
        compiler_params=_params("arbitrary", "arbitrary"),
        name="moe_grouped_ffn",
    )(tile_expert, n_active, hs, w_gu, w_gu, w_down)


def _combine_kernel(pos_ref, pos_next_ref, ys_ref, x_ref, meta_ref, g2_ref, ng_ref, o_ref, a_s, b_s, sems):
    i = pl.program_id(0)
    n_steps = pl.num_programs(0)
    n = x_ref.shape[0]
    slot = i % 2

    def gather(p_ref, s):
        def issue(g, carry):
            for j in range(DMA_UNROLL):
                r = g * DMA_UNROLL + j
                _row_copy(ys_ref, p_ref[0, 0, 2 * r], a_s.at[s], r, sems.at[s]).start(priority=0)
                _row_copy(ys_ref, p_ref[0, 0, 2 * r + 1], b_s.at[s], r, sems.at[s]).start(priority=1)
            return carry

        lax.fori_loop(0, n // DMA_UNROLL, issue, 0)

    @pl.when(i == 0)
    def _():
        gather(pos_ref, 0)

    @pl.when(i + 1 < n_steps)
    def _():
        gather(pos_next_ref, 1 - slot)

    pltpu.make_async_copy(ys_ref.at[pl.ds(0, n), :], a_s.at[slot], sems.at[slot]).wait()
    pltpu.make_async_copy(ys_ref.at[pl.ds(0, n), :], b_s.at[slot], sems.at[slot]).wait()

    meta = meta_ref[...]
    lane = lax.broadcasted_iota(jnp.int32, meta.shape, 1)
    mixed = _lane_pick(meta, lane, META_W0) * a_s[slot] + _lane_pick(meta, lane, META_W1) * b_s[slot]
    o_ref[...] = x_ref[...] + g2_ref[0] * (_rms(mixed) * ng_ref[...])


def moe_combine(ys, pos, meta, x, g2, ng):
    t, d = x.shape
    nb = g2.shape[0]
    n_steps = t // ROW_TILE
    bpb = n_steps // nb
    return pl.pallas_call(
        _combine_kernel,
        grid=(n_steps,),
        in_specs=[
            pl.BlockSpec((1, 1, 2 * ROW_TILE), lambda i: (i, 0, 0), memory_space=pltpu.SMEM),
            pl.BlockSpec((1, 1, 2 * ROW_TILE), lambda i: (jnp.minimum(i + 1, n_steps - 1), 0, 0),
                         memory_space=pltpu.SMEM),
            pl.BlockSpec(memory_space=pl.ANY),
            pl.BlockSpec((ROW_TILE, d), lambda i: (i, 0)),
            pl.BlockSpec((ROW_TILE, LANES), lambda i: (i, 0)),
            pl.BlockSpec((1, 1, d), lambda i: (i // bpb, 0, 0)),
            pl.BlockSpec((1, d), lambda i: (0, 0)),
        ],
        out_specs=pl.BlockSpec((ROW_TILE, d), lambda i: (i, 0)),
        out_shape=jax.ShapeDtypeStruct((t, d), F32),
        scratch_shapes=[pltpu.VMEM((2, ROW_TILE, d), F32), pltpu.VMEM((2, ROW_TILE, d), F32),
                        pltpu.SemaphoreType.DMA((2,))],
        compiler_params=_params("arbitrary"),
        name="moe_combine",
    )(pos, pos, ys, x, meta, g2, ng.reshape(1, d))


def moe_residual(x, gain, shift, scale, g2, ng, router_w, router_b, w_gu, w_down):
    t, d = x.shape
    h, meta, meta_t, counts = moe_router(x, gain, shift, scale, router_w, router_b, tm=512)
    n_rows = TOP_K * t + N_EXPERTS * MOE_TM
    n_tiles = n_rows // MOE_TM
    cnt = counts[0, :N_EXPERTS].astype(jnp.int32)
    padded = (cnt + MOE_TM - 1) // MOE_TM * MOE_TM
    seg_end = jnp.cumsum(padded)
    seg_start = seg_end - padded
    experts = jnp.arange(N_EXPERTS, dtype=jnp.int32)

    def position(e_field, r_field):
        e = meta_t[e_field].astype(jnp.int32)
        start = jnp.sum(jnp.where(e[None, :] == experts[:, None], seg_start[:, None], 0), axis=0)
        return start + meta_t[r_field].astype(jnp.int32)

    pos = jnp.stack([position(META_E0, META_R0), position(META_E1, META_R1)], axis=-1)
    pos = pos.reshape(t // ROW_TILE, 1, 2 * ROW_TILE)
    n_active = seg_end[-1:] // MOE_TM
    tile_ids = jnp.arange(n_tiles, dtype=jnp.int32)
    tile_expert = jnp.sum((jnp.minimum(tile_ids, n_active - 1)[:, None] * MOE_TM >= seg_end[None, :]), axis=-1)
    tile_expert = jnp.minimum(tile_expert, N_EXPERTS - 1).astype(jnp.int32)

    fill = jnp.stack([
        jnp.concatenate([seg_start + cnt, seg_end[-1:], jnp.zeros((N_EXPERTS - 1,), jnp.int32)]),
        jnp.concatenate([padded - cnt, (n_rows - seg_end[-1:]) // SUBLANES, jnp.zeros((N_EXPERTS - 1,), jnp.int32)]),
    ]).astype(jnp.int32)
    hs, w_gu_bf, w_down_bf = moe_dispatch(h, pos, fill, n_rows, w_gu, w_down)
    ys = moe_grouped_ffn(hs, tile_expert, n_active.astype(jnp.int32), w_gu_bf, w_down_bf)
    return moe_combine(ys, pos, meta, x, g2, ng)


def _rope_tables():
    pos = jnp.arange(SEQ)
    inv_freq = jnp.power(ROPE_THETA, -jnp.arange(ROPE_FREQS, dtype=F32) / ROPE_FREQS)
    ang_r = (pos // GRID_W).astype(F32)[:, None] * inv_freq
    ang_c = (pos % GRID_W).astype(F32)[:, None] * inv_freq
    cos = jnp.concatenate([jnp.cos(ang_r)] * 2 + [jnp.cos(ang_c)] * 2, axis=-1)
    sin = jnp.concatenate([-jnp.sin(ang_r), jnp.sin(ang_r), -jnp.sin(ang_c), jnp.sin(ang_c)], axis=-1)
    return jnp.tile(cos, (1, 2)), jnp.tile(sin, (1, 2))


def _gate_layout(gate_w, gate_b, lam):
    c = RNN_CB
    n_cb = D_RNN // c
    per = c // RNN_BLOCK_W
    gw = gate_w.reshape(2, 2, n_cb, per, RNN_BLOCK_W, RNN_BLOCK_W)
    gw = jnp.transpose(gw, (2, 3, 4, 0, 1, 5))
    half_eye = 0.5 * jnp.eye(per, dtype=gate_w.dtype)
    bd = gw[:, :, :, :, :, None, :] * half_eye[None, :, None, None, None, :, None]
    wg = bd.reshape(n_cb, c, 4 * c).astype(BF16)
    gb = 0.5 * jnp.transpose(gate_b.reshape(2, 2, n_cb, c), (2, 0, 1, 3)).reshape(n_cb, 1, 4 * c)
    lm = jnp.transpose(lam.reshape(2, n_cb, c), (1, 0, 2)).reshape(n_cb, 1, 2 * c)
    return wg, gb, lm


def kernel(x, c, ctx, c_ctx, ada_w, ada_b, norm_g, w_in, conv_w, conv_b, lru_gate_w, lru_gate_b, lru_lambda,
           diff_lambda, subln_g, w_proj_rnn, w_proj_attn, w_out, ffn_w_gu, ffn_w_down, router_w, router_b,
           moe_w_gu, moe_w_down):
    xt = x.reshape(BATCH * SEQ, D_MODEL)
    ct = ctx.reshape(BATCH * CTX_LEN, D_MODEL)
    cos, sin_signed = _rope_tables()

    cvec = jnp.concatenate([c, c_ctx[None, :], jnp.zeros((2 * SUBLANES - BATCH - 1, D_MODEL), F32)], axis=0)
    mod = ada_modulation(cvec, ada_w, ada_b)

    for l in range(DEPTH):
        last = l == DEPTH - 1
        lam_init = 0.8 - 0.6 * math.exp(-0.3 * l)
        mx = mod[l, :BATCH].reshape(BATCH, 1, 6, D_MODEL)
        mc = mod[l, BATCH:BATCH + 1].reshape(1, 1, 6, D_MODEL)
        sh1x, sc1x, g1x, sh2x, sc2x, g2x = (mx[:, :, i] for i in range(6))
        sh1c, sc1c, g1c, sh2c, sc2c, g2c = (mc[:, :, i] for i in range(6))

        wg, gb, lm = _gate_layout(lru_gate_w[l], lru_gate_b[l], lru_lambda[l])

        proj_x = norm_mod_matmul(xt, norm_g[l, 0], sh1x, sc1x, w_in, l, IN_W, tm=2048, tn=768)
        if last:
            proj_c = norm_mod_matmul(ct, norm_g[l, 0], sh1c, sc1c, w_in, l, CTX_STATE_W, tm=2048, tn=256)
        else:
            proj_c = norm_mod_matmul(ct, norm_g[l, 0], sh1c, sc1c, w_in, l, IN_W, tm=2048, tn=768)

        y_rnn_x, y_rnn_c = rglru(proj_x, proj_c, conv_w[l], conv_b[l], wg, gb, lm, xr_col=COL_XR, gr_col=COL_GR,
                                 c_xr_col=COL_XR, c_gr_col=COL_GR, ctx_out=not last)
        y_attn_x = diff_attention(proj_x, COL_Q, SEQ, proj_x, proj_c, (COL_K, COL_V, COL_K, COL_V), cos,
                                  sin_signed, diff_lambda[l], subln_g[l], lam_init=lam_init, tq=1024)
        wr = w_proj_rnn[l].astype(BF16)
        wa = w_proj_attn[l].astype(BF16)
        wo = w_out[l].astype(BF16)
        xt = merge_out(y_rnn_x, y_attn_x, proj_x, xt, g1x, norm_g[l, 1], wr, wa, wo, tm=512)
        if not last:
            y_attn_c = diff_attention(proj_c, COL_Q, CTX_LEN, None, proj_c, (0, 0, COL_K, COL_V), None, None,
                                      diff_lambda[l], subln_g[l], lam_init=lam_init, tq=CTX_LEN)
            ct = merge_out(y_rnn_c, y_attn_c, proj_c, ct, g1c, norm_g[l, 1], wr, wa, wo, tm=512)

        if l % 2 == 0:
            w_gu = ffn_w_gu[l // 2].astype(BF16)
            w_dn = ffn_w_down[l // 2].astype(BF16)
            xt = ffn_residual(xt, norm_g[l, 2], sh2x, sc2x, g2x, norm_g[l, 3], w_gu, w_dn, tm=512, tf=1408)
            if not last:
                ct = ffn_residual(ct, norm_g[l, 2], sh2c, sc2c, g2c, norm_g[l, 3], w_gu, w_dn, tm=512, tf=1408)
        else:
            w_gu, w_dn = moe_w_gu[l // 2], moe_w_down[l // 2]
            rw, rb = router_w[l // 2], router_b[l // 2]
            xt = moe_residual(xt, norm_g[l, 2], sh2x, sc2x, g2x, norm_g[l, 3], rw, rb, w_gu, w_dn)
            if not last:
                ct = moe_residual(ct, norm_g[l, 2], sh2c, sc2c, g2c, norm_g[l, 3], rw, rb, w_gu, w_dn)
    return xt.reshape(BATCH, SEQ, D_MODEL)
```

```python
import functools
import math

import jax
import jax.numpy as jnp
from jax import lax
from jax.experimental import pallas as pl
from jax.experimental.pallas import tpu as pltpu

F32 = jnp.float32
BF16 = jnp.bfloat16

D_MODEL = 1024
BATCH = 8
SEQ = 2048
DEPTH = 2
CTX_LEN = 256
GRID_W = 64
EPS = 1e-6
D_RNN = 1280
RNN_BLOCKS = 20
RNN_BLOCK_W = D_RNN // RNN_BLOCKS
CONV_W = 4
LRU_C = 8.0
N_HEADS = 8
HEAD_DIM = 64
V_DIM = 2 * HEAD_DIM
QK_W = N_HEADS * 2 * HEAD_DIM
ATTN_W = N_HEADS * V_DIM
ROPE_THETA = 10000.0
ROPE_FREQS = HEAD_DIM // 4
D_FF = 2816
N_EXPERTS = 8
TOP_K = 2

LANES = 128
SUBLANES = 8
VMEM_LIMIT_BYTES = 52 * 1024 * 1024

COL_XR = 0
COL_K = COL_XR + D_RNN
COL_V = COL_K + QK_W
COL_GR = COL_V + ATTN_W
COL_Q = COL_GR + D_RNN
COL_G_RNN = COL_Q + QK_W
COL_G_ATTN = COL_G_RNN + D_MODEL
IN_W = COL_G_ATTN + D_MODEL
CTX_STATE_W = COL_GR
GATE_BLOCK_W = 512

RNN_CB = 256
SCAN_CHUNK = 256
ATTN_SUB = 128
NORM_ROWS = 512


def _params(*sem):
    return pltpu.CompilerParams(dimension_semantics=sem, vmem_limit_bytes=VMEM_LIMIT_BYTES)


def _rms(x):
    return x * lax.rsqrt(jnp.mean(x * x, axis=-1, keepdims=True) + EPS)


def _ada_kernel(c_ref, w_ref, b_ref, o_ref):
    c = c_ref[...]
    s = c * jax.nn.sigmoid(c)
    o_ref[0] = jnp.dot(s, w_ref[0], preferred_element_type=F32, precision=lax.Precision.HIGHEST) + b_ref[0]


def ada_modulation(cvec, ada_w, ada_b):
    rows = cvec.shape[0]
    tn = 1536
    n = 6 * D_MODEL
    return pl.pallas_call(
        _ada_kernel,
        grid=(DEPTH, n // tn),
        in_specs=[
            pl.BlockSpec((rows, D_MODEL), lambda l, j: (0, 0)),
            pl.BlockSpec((1, D_MODEL, tn), lambda l, j: (l, 0, j)),
            pl.BlockSpec((1, 1, tn), lambda l, j: (l, 0, j)),
        ],
        out_specs=pl.BlockSpec((1, rows, tn), lambda l, j: (l, 0, j)),
        out_shape=jax.ShapeDtypeStruct((DEPTH, rows, n), F32),
        compiler_params=_params("parallel", "parallel"),
        name="ada_modulation",
    )(cvec, ada_w, ada_b.reshape(DEPTH, 1, n))


def _nmm_kernel(x_ref, g_ref, sh_ref, sc_ref, w_ref, o_ref, h_ref):
    @pl.when(pl.program_id(1) == 0)
    def _():
        for r0 in range(0, x_ref.shape[0], NORM_ROWS):
            h = _rms(x_ref[r0:r0 + NORM_ROWS, :]) * g_ref[...]
            h = h * (1.0 + sc_ref[0]) + sh_ref[0]
            h_ref[r0:r0 + NORM_ROWS, :] = h.astype(BF16)

    w = w_ref[0].astype(BF16)
    o_ref[...] = jnp.dot(h_ref[...], w, preferred_element_type=F32).astype(o_ref.dtype)


def norm_mod_matmul(x, gain, shift, scale, w, layer, n, *, tm, tn):
    t, d = x.shape
    nb = shift.shape[0]
    assert t % (tm * nb) == 0 and n % tn == 0 and tm % NORM_ROWS == 0, (t, tm, nb, n, tn)
    bpb = (t // tm) // nb
    return pl.pallas_call(
        _nmm_kernel,
        grid=(t // tm, n // tn),
        in_specs=[
            pl.BlockSpec((tm, d), lambda i, j: (i, 0)),
            pl.BlockSpec((1, d), lambda i, j: (0, 0)),
            pl.BlockSpec((1, 1, d), lambda i, j: (i // bpb, 0, 0)),
            pl.BlockSpec((1, 1, d), lambda i, j: (i // bpb, 0, 0)),
            pl.BlockSpec((1, d, tn), lambda i, j: (layer, 0, j)),
        ],
        out_specs=pl.BlockSpec((tm, tn), lambda i, j: (i, j)),
        out_shape=jax.ShapeDtypeStruct((t, n), BF16),
        scratch_shapes=[pltpu.VMEM((tm, d), BF16)],
        compiler_params=_params("parallel", "arbitrary"),
        name="norm_mod_matmul",
    )(x, gain.reshape(1, d), shift, scale, w)


def _group_scan(a, b, reverse):
    rows, c = a.shape
    a = a.reshape(rows // SUBLANES, SUBLANES, c)
    b = b.reshape(rows // SUBLANES, SUBLANES, c)
    sub = lax.broadcasted_iota(jnp.int32, a.shape, 1)
    for sh in (1, 2, 4):
        if reverse:
            keep = sub < SUBLANES - sh
            amount = SUBLANES - sh
        else:
            keep = sub >= sh
            amount = sh
        a_sh = jnp.where(keep, pltpu.roll(a, amount, 1), 1.0)
        b_sh = jnp.where(keep, pltpu.roll(b, amount, 1), 0.0)
        b = a * b_sh + b
        a = a * a_sh
    return a.reshape(rows, c), b.reshape(rows, c)


def _dwconv(x, cw, cb):
    n = x.shape[0]
    rows = lax.broadcasted_iota(jnp.int32, x.shape, 0)
    xm2 = jnp.where(rows >= 2, pltpu.roll(x, 2, 0), 0.0)
    xm1 = jnp.where(rows >= 1, pltpu.roll(x, 1, 0), 0.0)
    xp1 = jnp.where(rows < n - 1, pltpu.roll(x, n - 1, 0), 0.0)
    return cb + xm2 * cw[0:1] + xm1 * cw[1:2] + x * cw[2:3] + xp1 * cw[3:4]


def _gelu_tanh(x):
    return 0.5 * x * (1.0 + jnp.tanh(math.sqrt(2.0 / math.pi) * (x + 0.044715 * (x * x * x))))


def _rglru_kernel(*refs, n_c, n_x, ctx_out):
    if ctx_out:
        (xrx_ref, grx_ref, xrc_ref, grc_ref, cw_ref, cb_ref, wg_ref, gb_ref, lam_ref,
         yx_ref, yc_ref, xc_s, a0_s, b0_s, a1_s, b1_s) = refs
    else:
        (xrx_ref, grx_ref, xrc_ref, cw_ref, cb_ref, wg_ref, gb_ref, lam_ref,
         yx_ref, xc_s, a0_s, b0_s, a1_s, b1_s) = refs
    c = RNN_CB
    n_all = n_c + n_x
    cw = cw_ref[...]
    cb = cb_ref[...]

    xc_s[0:n_c, :] = _dwconv(xrc_ref[...].astype(F32), cw, cb)
    xc_s[n_c:n_all, :] = _dwconv(xrx_ref[...].astype(F32), cw, cb)

    lam = lam_ref[0]
    half_neg_sp = (-0.5 * LRU_C) * jax.nn.softplus(-lam)

    def coeff_chunk(ci, carry):
        r0 = pl.multiple_of(ci * SCAN_CHUNK, SCAN_CHUNK)
        xc = xc_s[pl.ds(r0, SCAN_CHUNK), :]
        t = jnp.tanh(jnp.dot(xc.astype(BF16), wg_ref[0], preferred_element_type=F32) + gb_ref[0])
        half_xc = 0.5 * xc
        for d, (a_s, b_s) in enumerate(((a0_s, b0_s), (a1_s, b1_s))):
            t_r = t[:, (2 * d) * c:(2 * d + 1) * c]
            t_i = t[:, (2 * d + 1) * c:(2 * d + 2) * c]
            half_sp = half_neg_sp[:, d * c:(d + 1) * c]
            a = jnp.exp(half_sp + half_sp * t_r)
            one_m_a2 = 1.0 - a * a
            mult = jnp.where(one_m_a2 > 0.0, one_m_a2 * lax.rsqrt(one_m_a2), 0.0)
            bb = mult * (half_xc + half_xc * t_i)
            a_cum, b_cum = _group_scan(a, bb, reverse=(d == 1))
            a_s[pl.ds(r0, SCAN_CHUNK), :] = a_cum
            b_s[pl.ds(r0, SCAN_CHUNK), :] = b_cum
        return carry

    lax.fori_loop(0, n_all // SCAN_CHUNK, coeff_chunk, 0)

    g_c = n_c // SUBLANES
    g_all = n_all // SUBLANES

    def carry_step(k, carry):
        hf, hr = carry
        rf = pl.multiple_of(k * SUBLANES, SUBLANES)
        h = a0_s[pl.ds(rf, SUBLANES), :] * hf + b0_s[pl.ds(rf, SUBLANES), :]
        a0_s[pl.ds(rf, SUBLANES), :] = h
        hf = h[SUBLANES - 1:SUBLANES, :]
        kr = jnp.where(k < g_c, g_c - 1 - k, g_all + g_c - 1 - k)
        rr = pl.multiple_of(kr * SUBLANES, SUBLANES)
        h = a1_s[pl.ds(rr, SUBLANES), :] * hr + b1_s[pl.ds(rr, SUBLANES), :]
        a1_s[pl.ds(rr, SUBLANES), :] = h
        hr = h[0:1, :]
        return hf, hr

    zero = jnp.zeros((1, c), F32)
    lax.fori_loop(0, g_all, carry_step, (zero, zero), unroll=4)

    yx = (a0_s[n_c:n_all, :] + a1_s[n_c:n_all, :]) * _gelu_tanh(grx_ref[...].astype(F32))
    yx_ref[...] = yx.astype(yx_ref.dtype)
    if ctx_out:
        yc = (a0_s[0:n_c, :] + a1_s[0:n_c, :]) * _gelu_tanh(grc_ref[...].astype(F32))
        yc_ref[...] = yc.astype(yc_ref.dtype)


def rglru(proj_x, proj_c, conv_w, conv_b, wg, gb, lam, *, xr_col, gr_col, c_xr_col, c_gr_col, ctx_out):
    c = RNN_CB
    n_cb = D_RNN // c
    xr_b, gr_b, cxr_b = xr_col // c, gr_col // c, c_xr_col // c
    in_specs = [
        pl.BlockSpec((SEQ, c), lambda b, j: (b, xr_b + j)),
        pl.BlockSpec((SEQ, c), lambda b, j: (b, gr_b + j)),
        pl.BlockSpec((CTX_LEN, c), lambda b, j: (b, cxr_b + j)),
    ]
    args = [proj_x, proj_x, proj_c]
    if ctx_out:
        cgr_b = c_gr_col // c
        in_specs.append(pl.BlockSpec((CTX_LEN, c), lambda b, j: (b, cgr_b + j)))
        args.append(proj_c)
    in_specs += [
        pl.BlockSpec((CONV_W, c), lambda b, j: (0, j)),
        pl.BlockSpec((1, c), lambda b, j: (0, j)),
        pl.BlockSpec((1, c, 4 * c), lambda b, j: (j, 0, 0)),
        pl.BlockSpec((1, 1, 4 * c), lambda b, j: (j, 0, 0)),
        pl.BlockSpec((1, 1, 2 * c), lambda b, j: (j, 0, 0)),
    ]
    args += [conv_w, conv_b.reshape(1, D_RNN), wg, gb, lam]
    out_specs = [pl.BlockSpec((SEQ, c), lambda b, j: (b, j))]
    out_shape = [jax.ShapeDtypeStruct((BATCH * SEQ, D_RNN), BF16)]
    if ctx_out:
        out_specs.append(pl.BlockSpec((CTX_LEN, c), lambda b, j: (b, j)))
        out_shape.append(jax.ShapeDtypeStruct((BATCH * CTX_LEN, D_RNN), BF16))
    n_all = SEQ + CTX_LEN
    out = pl.pallas_call(
        functools.partial(_rglru_kernel, n_c=CTX_LEN, n_x=SEQ, ctx_out=ctx_out),
        grid=(BATCH, n_cb),
        in_specs=in_specs,
        out_specs=out_specs,
        out_shape=out_shape,
        scratch_shapes=[pltpu.VMEM((n_all, c), F32) for _ in range(5)],
        compiler_params=_params("parallel", "parallel"),
        name="rglru",
    )(*args)
    return out if ctx_out else (out[0], None)


def _rope(t, cos, sin_signed):
    lane = lax.broadcasted_iota(jnp.int32, t.shape, 1)
    first_half = (lane % (2 * ROPE_FREQS)) < ROPE_FREQS
    partner = jnp.where(first_half, pltpu.roll(t, LANES - ROPE_FREQS, 1), pltpu.roll(t, ROPE_FREQS, 1))
    return t * cos + partner * sin_signed


def _attn_kernel(*refs, tq, n_c, n_x, lam_init):
    if n_x:
        (q_ref, kx_ref, vx_ref, kc_ref, vc_ref, cos_ref, sin_ref, dl_ref, sg_ref,
         o_ref, k_s, v_s, lam_s) = refs
    else:
        q_ref, kc_ref, vc_ref, dl_ref, sg_ref, o_ref, k_s, v_s, lam_s = refs
    qi = pl.program_id(2)

    @pl.when(qi == 0)
    def _():
        k_s[0:n_c, :] = kc_ref[...]
        v_s[0:n_c, 0:LANES] = vc_ref[...]
        if n_x:
            kx = _rope(kx_ref[...].astype(F32), cos_ref[...], sin_ref[...])
            k_s[n_c:n_c + n_x, :] = kx.astype(BF16)
            v_s[n_c:n_c + n_x, 0:LANES] = vx_ref[...]
        v_s[:, LANES:2 * LANES] = jnp.ones((n_c + n_x, LANES), BF16)
        dl = dl_ref[...]
        e1 = jnp.exp(jnp.sum(dl[0:1] * dl[1:2], axis=-1, keepdims=True))
        e2 = jnp.exp(jnp.sum(dl[2:3] * dl[3:4], axis=-1, keepdims=True))
        lam_s[...] = jnp.broadcast_to(e1 - e2 + lam_init, lam_s.shape)

    q = q_ref[...].astype(F32)
    if n_x:
        r0 = pl.multiple_of(qi * tq, tq)
        q = _rope(q, cos_ref[pl.ds(r0, tq), :], sin_ref[pl.ds(r0, tq), :])
    q = q * (HEAD_DIM ** -0.5 * math.log2(math.e))
    lane = lax.broadcasted_iota(jnp.int32, (ATTN_SUB, LANES), 1)
    lam = lam_s[0:1, 0:1]
    scores = []
    for sb in range(tq // ATTN_SUB):
        qs = q[sb * ATTN_SUB:(sb + 1) * ATTN_SUB]
        qq = jnp.concatenate([jnp.where(lane < HEAD_DIM, qs, 0.0), jnp.where(lane >= HEAD_DIM, qs, 0.0)], axis=0)
        scores.append(lax.dot_general(qq.astype(BF16), k_s[...], (((1,), (1,)), ((), ())),
                                      preferred_element_type=F32))
    for sb, s in enumerate(scores):
        e = jnp.exp2(s - jnp.max(s, axis=-1, keepdims=True))
        ov = jnp.dot(e.astype(BF16), v_s[...], preferred_element_type=F32)
        ov = ov[:, 0:LANES] / ov[:, LANES:2 * LANES]
        o = ov[0:ATTN_SUB] - lam * ov[ATTN_SUB:2 * ATTN_SUB]
        o_ref[sb * ATTN_SUB:(sb + 1) * ATTN_SUB, :] = (_rms(o) * sg_ref[...] * (1.0 - lam_init)).astype(o_ref.dtype)


def diff_attention(q_arr, q_col, n_q, kvx, kvc, cols, cos, sin_signed, diff_lambda, subln_g, *, lam_init, tq):
    kx_col, vx_col, kc_col, vc_col = cols
    qb, kxb, vxb, kcb, vcb = (v // LANES for v in (q_col, kx_col, vx_col, kc_col, vc_col))
    n_x = SEQ if kvx is not None else 0
    n_qb = n_q // tq
    in_specs = [pl.BlockSpec((tq, LANES), lambda b, h, i: (b * n_qb + i, qb + h))]
    args = [q_arr]
    if n_x:
        in_specs += [
            pl.BlockSpec((SEQ, LANES), lambda b, h, i: (b, kxb + h)),
            pl.BlockSpec((SEQ, LANES), lambda b, h, i: (b, vxb + h)),
        ]
        args += [kvx, kvx]
    in_specs += [
        pl.BlockSpec((CTX_LEN, LANES), lambda b, h, i: (b, kcb + h)),
        pl.BlockSpec((CTX_LEN, LANES), lambda b, h, i: (b, vcb + h)),
    ]
    args += [kvc, kvc]
    if n_x:
        in_specs += [
            pl.BlockSpec((SEQ, LANES), lambda b, h, i: (0, 0)),
            pl.BlockSpec((SEQ, LANES), lambda b, h, i: (0, 0)),
        ]
        args += [cos, sin_signed]
    in_specs += [
        pl.BlockSpec((4, HEAD_DIM), lambda b, h, i: (0, 0)),
        pl.BlockSpec((1, V_DIM), lambda b, h, i: (0, 0)),
    ]
    args += [diff_lambda, subln_g.reshape(1, V_DIM)]
    n_kv = CTX_LEN + n_x
    return pl.pallas_call(
        functools.partial(_attn_kernel, tq=tq, n_c=CTX_LEN, n_x=n_x, lam_init=lam_init),
        grid=(BATCH, N_HEADS, n_qb),
        in_specs=in_specs,
        out_specs=pl.BlockSpec((tq, LANES), lambda b, h, i: (b * n_qb + i, h)),
        out_shape=jax.ShapeDtypeStruct((BATCH * n_q, ATTN_W), BF16),
        scratch_shapes=[
            pltpu.VMEM((n_kv, LANES), BF16),
            pltpu.VMEM((n_kv, 2 * LANES), BF16),
            pltpu.VMEM((SUBLANES, LANES), F32),
        ],
        compiler_params=_params("parallel", "parallel", "arbitrary"),
        name="diff_attention",
    )(*args)


def _merge_kernel(yr_ref, ya_ref, gr0_ref, gr1_ref, ga0_ref, ga1_ref, x_ref, g1_ref, ng_ref, wr_ref, wa_ref, wo_ref,
                  o_ref):
    y_rnn = jnp.dot(yr_ref[...], wr_ref[...], preferred_element_type=F32)
    y_attn = jnp.dot(ya_ref[...], wa_ref[...], preferred_element_type=F32)
    g_rnn = jax.nn.sigmoid(jnp.concatenate([gr0_ref[...], gr1_ref[...]], axis=1).astype(F32))
    g_attn = jax.nn.sigmoid(jnp.concatenate([ga0_ref[...], ga1_ref[...]], axis=1).astype(F32))
    m = (g_rnn * y_rnn + g_attn * y_attn).astype(BF16)
    mx = jnp.dot(m, wo_ref[...], preferred_element_type=F32)
    o_ref[...] = x_ref[...] + g1_ref[0] * (_rms(mx) * ng_ref[...])


def merge_out(y_rnn, y_attn, proj, x, g1, ng, w_proj_rnn, w_proj_attn, w_out, *, tm):
    t, d = x.shape
    nb = g1.shape[0]
    assert t % (tm * nb) == 0, (t, tm, nb)
    bpb = (t // tm) // nb
    gw = GATE_BLOCK_W
    grb, gab = COL_G_RNN // gw, COL_G_ATTN // gw
    return pl.pallas_call(
        _merge_kernel,
        grid=(t // tm,),
        in_specs=[
            pl.BlockSpec((tm, D_RNN), lambda i: (i, 0)),
            pl.BlockSpec((tm, ATTN_W), lambda i: (i, 0)),
            pl.BlockSpec((tm, gw), lambda i: (i, grb)),
            pl.BlockSpec((tm, gw), lambda i: (i, grb + 1)),
            pl.BlockSpec((tm, gw), lambda i: (i, gab)),
            pl.BlockSpec((tm, gw), lambda i: (i, gab + 1)),
            pl.BlockSpec((tm, d), lambda i: (i, 0)),
            pl.BlockSpec((1, 1, d), lambda i: (i // bpb, 0, 0)),
            pl.BlockSpec((1, d), lambda i: (0, 0)),
            pl.BlockSpec((D_RNN, d), lambda i: (0, 0)),
            pl.BlockSpec((ATTN_W, d), lambda i: (0, 0)),
            pl.BlockSpec((d, d), lambda i: (0, 0)),
        ],
        out_specs=pl.BlockSpec((tm, d), lambda i: (i, 0)),
        out_shape=jax.ShapeDtypeStruct((t, d), F32),
        compiler_params=_params("parallel"),
        name="merge_out",
    )(y_rnn, y_attn, proj, proj, proj, proj, x, g1, ng.reshape(1, d), w_proj_rnn, w_proj_attn, w_out)


def _swiglu_partial(h, wg, wu, wd):
    gate = jnp.dot(h, wg, preferred_element_type=F32)
    up = jnp.dot(h, wu, preferred_element_type=F32)
    act = (gate * jax.nn.sigmoid(gate) * up).astype(BF16)
    return jnp.dot(act, wd, preferred_element_type=F32)


def _ffn_kernel(x_ref, g_ref, sh_ref, sc_ref, g2_ref, ng_ref, wg_ref, wu_ref, wd_ref, o_ref, h_ref, acc_ref):
    f = pl.program_id(1)

    @pl.when(f == 0)
    def _():
        h = _rms(x_ref[...]) * g_ref[...]
        h = h * (1.0 + sc_ref[0]) + sh_ref[0]
        h_ref[...] = h.astype(BF16)
        acc_ref[...] = jnp.zeros_like(acc_ref)

    acc_ref[...] += _swiglu_partial(h_ref[...], wg_ref[...], wu_ref[...], wd_ref[...])

    @pl.when(f == pl.num_programs(1) - 1)
    def _():
        o_ref[...] = x_ref[...] + g2_ref[0] * (_rms(acc_ref[...]) * ng_ref[...])


def ffn_residual(x, gain, shift, scale, g2, ng, w_gu, w_down, *, tm, tf):
    t, d = x.shape
    ff = w_gu.shape[1] // 2
    nf = ff // tf
    nb = shift.shape[0]
    assert t % (tm * nb) == 0 and ff % tf == 0, (t, tm, nb, ff, tf)
    bpb = (t // tm) // nb
    mod_spec = pl.BlockSpec((1, 1, d), lambda i, f: (i // bpb, 0, 0))
    vec_spec = pl.BlockSpec((1, d), lambda i, f: (0, 0))
    return pl.pallas_call(
        _ffn_kernel,
        grid=(t // tm, nf),
        in_specs=[
            pl.BlockSpec((tm, d), lambda i, f: (i, 0)), vec_spec, mod_spec, mod_spec, mod_spec, vec_spec,
            pl.BlockSpec((d, tf), lambda i, f: (0, f)),
            pl.BlockSpec((d, tf), lambda i, f: (0, nf + f)),
            pl.BlockSpec((tf, d), lambda i, f: (f, 0)),
        ],
        out_specs=pl.BlockSpec((tm, d), lambda i, f: (i, 0)),
        out_shape=jax.ShapeDtypeStruct((t, d), F32),
        scratch_shapes=[pltpu.VMEM((tm, d), BF16), pltpu.VMEM((tm, d), F32)],
        compiler_params=_params("parallel", "arbitrary"),
        name="ffn_residual",
    )(x, gain.reshape(1, d), shift, scale, g2, ng.reshape(1, d), w_gu, w_gu, w_down)


MOE_TM = 512
MOE_TF = 1408
ROW_TILE = 512
DMA_UNROLL = 8
META_E0, META_E1, META_W0, META_W1, META_R0, META_R1 = range(6)


def _lane_pick(rec, lane, k):
    return jnp.sum(jnp.where(lane == k, rec, 0.0), axis=-1, keepdims=True)


def _router_kernel(x_ref, g_ref, sh_ref, sc_ref, w_ref, b_ref, h_ref, meta_ref, meta_t_ref, cnt_ref, carry_s):
    @pl.when(pl.program_id(0) == 0)
    def _():
        carry_s[...] = jnp.zeros_like(carry_s)

    h = _rms(x_ref[...]) * g_ref[...]
    h = h * (1.0 + sc_ref[0]) + sh_ref[0]
    h_ref[...] = h
    logits = jnp.dot(h, w_ref[...], preferred_element_type=F32, precision=lax.Precision.HIGHEST) + b_ref[...]
    lane = lax.broadcasted_iota(jnp.int32, logits.shape, 1).astype(F32)
    neg = jnp.float32(-jnp.inf)
    logits = jnp.where(lane < N_EXPERTS, logits, neg)
    m1 = jnp.max(logits, axis=-1, keepdims=True)
    i1 = jnp.min(jnp.where(logits == m1, lane, float(LANES)), axis=-1, keepdims=True)
    rest = jnp.where(lane == i1, neg, logits)
    m2 = jnp.max(rest, axis=-1, keepdims=True)
    i2 = jnp.min(jnp.where(rest == m2, lane, float(LANES)), axis=-1, keepdims=True)
    e2 = jnp.exp(m2 - m1)
    denom = 1.0 + e2

    tm = logits.shape[0]
    sel1 = lane == i1
    sel2 = lane == i2
    member = jnp.where(jnp.logical_or(sel1, sel2), 1.0, 0.0)
    row = lax.broadcasted_iota(jnp.int32, (tm, tm), 0)
    col = lax.broadcasted_iota(jnp.int32, (tm, tm), 1)
    lower = jnp.where(col < row, 1.0, 0.0).astype(BF16)
    before = jnp.dot(lower, member.astype(BF16), preferred_element_type=F32) + carry_s[0:1, :]
    r1 = jnp.sum(jnp.where(sel1, before, 0.0), axis=-1, keepdims=True)
    r2 = jnp.sum(jnp.where(sel2, before, 0.0), axis=-1, keepdims=True)
    carry_s[0:1, :] = carry_s[0:1, :] + jnp.sum(member, axis=0, keepdims=True)
    cnt_ref[...] = jnp.broadcast_to(carry_s[0:1, :], cnt_ref.shape)

    meta = jnp.zeros_like(logits)
    for k, v in ((META_E0, i1), (META_E1, i2), (META_W0, 1.0 / denom), (META_W1, e2 / denom),
                 (META_R0, r1), (META_R1, r2)):
        meta = jnp.where(lane == k, v, meta)
    meta_ref[...] = meta
    meta_t_ref[...] = meta.T[0:SUBLANES, :]


def moe_router(x, gain, shift, scale, router_w, router_b, *, tm):
    t, d = x.shape
    nb = shift.shape[0]
    bpb = (t // tm) // nb
    w = jnp.zeros((d, LANES), F32).at[:, :N_EXPERTS].set(router_w)
    b = jnp.zeros((1, LANES), F32).at[0, :N_EXPERTS].set(router_b)
    return pl.pallas_call(
        _router_kernel,
        grid=(t // tm,),
        in_specs=[
            pl.BlockSpec((tm, d), lambda i: (i, 0)),
            pl.BlockSpec((1, d), lambda i: (0, 0)),
            pl.BlockSpec((1, 1, d), lambda i: (i // bpb, 0, 0)),
            pl.BlockSpec((1, 1, d), lambda i: (i // bpb, 0, 0)),
            pl.BlockSpec((d, LANES), lambda i: (0, 0)),
            pl.BlockSpec((1, LANES), lambda i: (0, 0)),
        ],
        out_specs=[
            pl.BlockSpec((tm, d), lambda i: (i, 0)),
            pl.BlockSpec((tm, LANES), lambda i: (i, 0)),
            pl.BlockSpec((SUBLANES, tm), lambda i: (0, i)),
            pl.BlockSpec((SUBLANES, LANES), lambda i: (0, 0)),
        ],
        out_shape=[
            jax.ShapeDtypeStruct((t, d), F32),
            jax.ShapeDtypeStruct((t, LANES), F32),
            jax.ShapeDtypeStruct((SUBLANES, t), F32),
            jax.ShapeDtypeStruct((SUBLANES, LANES), F32),
        ],
        scratch_shapes=[pltpu.VMEM((SUBLANES, LANES), F32)],
        compiler_params=_params("arbitrary"),
        name="moe_router",
    )(x, gain.reshape(1, d), shift, scale, w, b)


def _row_copy(src_ref, src_row, dst_ref, dst_row, sem):
    return pltpu.make_async_copy(src_ref.at[pl.ds(src_row, 1), :], dst_ref.at[pl.ds(dst_row, 1), :], sem)


FILL_START, FILL_COUNT = 0, 1


def _dispatch_kernel(fill_ref, pos_ref, h_ref, wgu_ref, wdn_ref, hs_ref, wgu_out_ref, wdn_out_ref, buf_s, zero_s,
                     sems):
    wgu_out_ref[...] = wgu_ref[...].astype(BF16)
    wdn_out_ref[...] = wdn_ref[...].astype(BF16)

    i = pl.program_id(0)
    n_steps = pl.num_programs(0)
    n = h_ref.shape[0]
    slot = i % 2
    zero_sem = sems.at[2]

    @pl.when(i == 0)
    def _():
        zero_s[...] = jnp.zeros_like(zero_s)

        def pad_copy(e, k):
            return _row_copy(zero_s, 0, hs_ref, fill_ref[FILL_START, e] + k, zero_sem)

        def tail_copy(k):
            row = pl.multiple_of(fill_ref[FILL_START, N_EXPERTS] + k * SUBLANES, SUBLANES)
            return pltpu.make_async_copy(zero_s, hs_ref.at[pl.ds(row, SUBLANES), :], zero_sem)

        def for_each(copy, count, op):
            def body(k, carry):
                getattr(copy(k), op)()
                return carry

            lax.fori_loop(0, count, body, 0)

        for op in ("start", "wait"):
            for e in range(N_EXPERTS):
                for_each(functools.partial(pad_copy, e), fill_ref[FILL_COUNT, e], op)
            for_each(tail_copy, fill_ref[FILL_COUNT, N_EXPERTS], op)

    buf_s[slot] = h_ref[...]
    src = buf_s.at[slot]
    sem = sems.at[slot]

    def issue(g, carry):
        for j in range(DMA_UNROLL):
            r = g * DMA_UNROLL + j
            _row_copy(src, r, hs_ref, pos_ref[0, 0, 2 * r], sem).start(priority=0)
            _row_copy(src, r, hs_ref, pos_ref[0, 0, 2 * r + 1], sem).start(priority=1)
        return carry

    lax.fori_loop(0, n // DMA_UNROLL, issue, 0)

    def drain(s):
        for _ in range(TOP_K):
            pltpu.make_async_copy(buf_s.at[s], hs_ref.at[pl.ds(0, n), :], sems.at[s]).wait()

    @pl.when(i > 0)
    def _():
        drain(1 - slot)

    @pl.when(i == n_steps - 1)
    def _():
        drain(slot)


def moe_dispatch(h, pos, fill, n_rows, w_gu, w_down):
    t, d = h.shape
    n_steps = t // ROW_TILE
    wgu2 = w_gu.reshape(-1, w_gu.shape[-1])
    wdn2 = w_down.reshape(-1, w_down.shape[-1])
    gu_rows, dn_rows = wgu2.shape[0] // n_steps, wdn2.shape[0] // n_steps
    bf16_rows = 2 * SUBLANES
    assert wgu2.shape[0] % n_steps == 0 and wdn2.shape[0] % n_steps == 0, (wgu2.shape, wdn2.shape, n_steps)
    assert gu_rows % bf16_rows == 0 and dn_rows % bf16_rows == 0, (gu_rows, dn_rows)
    hs, wgu_bf, wdn_bf = pl.pallas_call(
        _dispatch_kernel,
        grid_spec=pltpu.PrefetchScalarGridSpec(
            num_scalar_prefetch=1,
            grid=(n_steps,),
            in_specs=[
                pl.BlockSpec((1, 1, 2 * ROW_TILE), lambda i, fill: (i, 0, 0), memory_space=pltpu.SMEM),
                pl.BlockSpec((ROW_TILE, d), lambda i, fill: (i, 0)),
                pl.BlockSpec((gu_rows, wgu2.shape[1]), lambda i, fill: (i, 0)),
                pl.BlockSpec((dn_rows, wdn2.shape[1]), lambda i, fill: (i, 0)),
            ],
            out_specs=[
                pl.BlockSpec(memory_space=pl.ANY),
                pl.BlockSpec((gu_rows, wgu2.shape[1]), lambda i, fill: (i, 0)),
                pl.BlockSpec((dn_rows, wdn2.shape[1]), lambda i, fill: (i, 0)),
            ],
            scratch_shapes=[pltpu.VMEM((2, ROW_TILE, d), F32), pltpu.VMEM((SUBLANES, d), F32),
                            pltpu.SemaphoreType.DMA((3,))],
        ),
        out_shape=[
            jax.ShapeDtypeStruct((n_rows, d), F32),
            jax.ShapeDtypeStruct(wgu2.shape, BF16),
            jax.ShapeDtypeStruct(wdn2.shape, BF16),
        ],
        compiler_params=_params("arbitrary"),
        name="moe_dispatch",
    )(fill, pos, h, wgu2, wdn2)
    return hs, wgu_bf.reshape(w_gu.shape), wdn_bf.reshape(w_down.shape)


def _grouped_ffn_kernel(te_ref, na_ref, hs_ref, wg_ref, wu_ref, wd_ref, ys_ref, h_s, acc_s):
    del te_ref
    f = pl.program_id(1)
    active = pl.program_id(0) < na_ref[0]

    @pl.when(jnp.logical_not(active))
    def _():
        ys_ref[...] = jnp.zeros_like(ys_ref)

    @pl.when(active)
    def _():
        @pl.when(f == 0)
        def _():
            h_s[...] = hs_ref[...].astype(BF16)
            acc_s[...] = jnp.zeros_like(acc_s)

        acc_s[...] += _swiglu_partial(h_s[...], wg_ref[0], wu_ref[0], wd_ref[0])

        @pl.when(f == pl.num_programs(1) - 1)
        def _():
            ys_ref[...] = acc_s[...]


def moe_grouped_ffn(hs, tile_expert, n_active, w_gu, w_down):
    n_rows, d = hs.shape
    ff = w_gu.shape[2] // 2
    nf = ff // MOE_TF
    n_tiles = n_rows // MOE_TM

    def row_map(t, f, te, na):
        return (jnp.maximum(jnp.minimum(t, na[0] - 1), 0), 0)

    def f_eff(t, f, na):
        return jnp.where(t < na[0], f, nf - 1)

    return pl.pallas_call(
        _grouped_ffn_kernel,
        grid_spec=pltpu.PrefetchScalarGridSpec(
            num_scalar_prefetch=2,
            grid=(n_tiles, nf),
            in_specs=[
                pl.BlockSpec((MOE_TM, d), row_map),
                pl.BlockSpec((1, d, MOE_TF), lambda t, f, te, na: (te[t], 0, f_eff(t, f, na))),
                pl.BlockSpec((1, d, MOE_TF), lambda t, f, te, na: (te[t], 0, nf + f_eff(t, f, na))),
                pl.BlockSpec((1, MOE_TF, d), lambda t, f, te, na: (te[t], f_eff(t, f, na), 0)),
            ],
            out_specs=pl.BlockSpec((MOE_TM, d), lambda t, f, te, na: (t, 0)),
            scratch_shapes=[pltpu.VMEM((MOE_TM, d), BF16), pltpu.VMEM((MOE_TM, d), F32)],
        ),
        out_shape=jax.ShapeDtypeStruct((n_rows, d), F32),
        compiler_params=_params("arbitrary", "arbitrary"),
        name="moe_grouped_ffn",
    )(tile_expert, n_active, hs, w_gu, w_gu, w_down)


def _combine_kernel(pos_ref, pos_next_ref, ys_ref, x_ref, meta_ref, g2_ref, ng_ref, o_ref, a_s, b_s, sems):
    i = pl.program_id(0)
    n_steps = pl.num_programs(0)
    n = x_ref.shape[0]
    slot = i % 2

    def gather(p_ref, s):
        def issue(g, carry):
            for j in range(DMA_UNROLL):
                r = g * DMA_UNROLL + j
                _row_copy(ys_ref, p_ref[0, 0, 2 * r], a_s.at[s], r, sems.at[s]).start(priority=0)
                _row_copy(ys_ref, p_ref[0, 0, 2 * r + 1], b_s.at[s], r, sems.at[s]).start(priority=1)
            return carry

        lax.fori_loop(0, n // DMA_UNROLL, issue, 0)

    @pl.when(i == 0)
    def _():
        gather(pos_ref, 0)

    @pl.when(i + 1 < n_steps)
    def _():
        gather(pos_next_ref, 1 - slot)

    pltpu.make_async_copy(ys_ref.at[pl.ds(0, n), :], a_s.at[slot], sems.at[slot]).wait()
    pltpu.make_async_copy(ys_ref.at[pl.ds(0, n), :], b_s.at[slot], sems.at[slot]).wait()

    meta = meta_ref[...]
    lane = lax.broadcasted_iota(jnp.int32, meta.shape, 1)
    mixed = _lane_pick(meta, lane, META_W0) * a_s[slot] + _lane_pick(meta, lane, META_W1) * b_s[slot]
    o_ref[...] = x_ref[...] + g2_ref[0] * (_rms(mixed) * ng_ref[...])


def moe_combine(ys, pos, meta, x, g2, ng):
    t, d = x.shape
    nb = g2.shape[0]
    n_steps = t // ROW_TILE
    bpb = n_steps // nb
    return pl.pallas_call(
        _combine_kernel,
        grid=(n_steps,),
        in_specs=[
            pl.BlockSpec((1, 1, 2 * ROW_TILE), lambda i: (i, 0, 0), memory_space=pltpu.SMEM),
            pl.BlockSpec((1, 1, 2 * ROW_TILE), lambda i: (jnp.minimum(i + 1, n_steps - 1), 0, 0),
                         memory_space=pltpu.SMEM),
            pl.BlockSpec(memory_space=pl.ANY),
            pl.BlockSpec((ROW_TILE, d), lambda i: (i, 0)),
            pl.BlockSpec((ROW_TILE, LANES), lambda i: (i, 0)),
            pl.BlockSpec((1, 1, d), lambda i: (i // bpb, 0, 0)),
            pl.BlockSpec((1, d), lambda i: (0, 0)),
        ],
        out_specs=pl.BlockSpec((ROW_TILE, d), lambda i: (i, 0)),
        out_shape=jax.ShapeDtypeStruct((t, d), F32),
        scratch_shapes=[pltpu.VMEM((2, ROW_TILE, d), F32), pltpu.VMEM((2, ROW_TILE, d), F32),
                        pltpu.SemaphoreType.DMA((2,))],
        compiler_params=_params("arbitrary"),
        name="moe_combine",
    )(pos, pos, ys, x, meta, g2, ng.reshape(1, d))


def moe_residual(x, gain, shift, scale, g2, ng, router_w, router_b, w_gu, w_down):
    t, d = x.shape
    h, meta, meta_t, counts = moe_router(x, gain, shift, scale, router_w, router_b, tm=512)
    n_rows = TOP_K * t + N_EXPERTS * MOE_TM
    n_tiles = n_rows // MOE_TM
    cnt = counts[0, :N_EXPERTS].astype(jnp.int32)
    padded = (cnt + MOE_TM - 1) // MOE_TM * MOE_TM
    seg_end = jnp.cumsum(padded)
    seg_start = seg_end - padded
    experts = jnp.arange(N_EXPERTS, dtype=jnp.int32)

    def position(e_field, r_field):
        e = meta_t[e_field].astype(jnp.int32)
        start = jnp.sum(jnp.where(e[None, :] == experts[:, None], seg_start[:, None], 0), axis=0)
        return start + meta_t[r_field].astype(jnp.int32)

    pos = jnp.stack([position(META_E0, META_R0), position(META_E1, META_R1)], axis=-1)
    pos = pos.reshape(t // ROW_TILE, 1, 2 * ROW_TILE)
    n_active = seg_end[-1:] // MOE_TM
    tile_ids = jnp.arange(n_tiles, dtype=jnp.int32)
    tile_expert = jnp.sum((jnp.minimum(tile_ids, n_active - 1)[:, None] * MOE_TM >= seg_end[None, :]), axis=-1)
    tile_expert = jnp.minimum(tile_expert, N_EXPERTS - 1).astype(jnp.int32)

    fill = jnp.stack([
        jnp.concatenate([seg_start + cnt, seg_end[-1:], jnp.zeros((N_EXPERTS - 1,), jnp.int32)]),
        jnp.concatenate([padded - cnt, (n_rows - seg_end[-1:]) // SUBLANES, jnp.zeros((N_EXPERTS - 1,), jnp.int32)]),
    ]).astype(jnp.int32)
    hs, w_gu_bf, w_down_bf = moe_dispatch(h, pos, fill, n_rows, w_gu, w_down)
    ys = moe_grouped_ffn(hs, tile_expert, n_active.astype(jnp.int32), w_gu_bf, w_down_bf)
    return moe_combine(ys, pos, meta, x, g2, ng)


def _rope_tables():
    pos = jnp.arange(SEQ)
    inv_freq = jnp.power(ROPE_THETA, -jnp.arange(ROPE_FREQS, dtype=F32) / ROPE_FREQS)
    ang_r = (pos // GRID_W).astype(F32)[:, None] * inv_freq
    ang_c = (pos % GRID_W).astype(F32)[:, None] * inv_freq
    cos = jnp.concatenate([jnp.cos(ang_r)] * 2 + [jnp.cos(ang_c)] * 2, axis=-1)
    sin = jnp.concatenate([-jnp.sin(ang_r), jnp.sin(ang_r), -jnp.sin(ang_c), jnp.sin(ang_c)], axis=-1)
    return jnp.tile(cos, (1, 2)), jnp.tile(sin, (1, 2))


def _gate_layout(gate_w, gate_b, lam):
    c = RNN_CB
    n_cb = D_RNN // c
    per = c // RNN_BLOCK_W
    gw = gate_w.reshape(2, 2, n_cb, per, RNN_BLOCK_W, RNN_BLOCK_W)
    gw = jnp.transpose(gw, (2, 3, 4, 0, 1, 5))
    half_eye = 0.5 * jnp.eye(per, dtype=gate_w.dtype)
    bd = gw[:, :, :, :, :, None, :] * half_eye[None, :, None, None, None, :, None]
    wg = bd.reshape(n_cb, c, 4 * c).astype(BF16)
    gb = 0.5 * jnp.transpose(gate_b.reshape(2, 2, n_cb, c), (2, 0, 1, 3)).reshape(n_cb, 1, 4 * c)
    lm = jnp.transpose(lam.reshape(2, n_cb, c), (1, 0, 2)).reshape(n_cb, 1, 2 * c)
    return wg, gb, lm


def kernel(x, c, ctx, c_ctx, ada_w, ada_b, norm_g, w_in, conv_w, conv_b, lru_gate_w, lru_gate_b, lru_lambda,
           diff_lambda, subln_g, w_proj_rnn, w_proj_attn, w_out, ffn_w_gu, ffn_w_down, router_w, router_b,
           moe_w_gu, moe_w_down):
    xt = x.reshape(BATCH * SEQ, D_MODEL)
    ct = ctx.reshape(BATCH * CTX_LEN, D_MODEL)
    cos, sin_signed = _rope_tables()

    cvec = jnp.concatenate([c, c_ctx[None, :], jnp.zeros((2 * SUBLANES - BATCH - 1, D_MODEL), F32)], axis=0)
    mod = ada_modulation(cvec, ada_w, ada_b)

    for l in range(DEPTH):
        last = l == DEPTH - 1
        lam_init = 0.8 - 0.6 * math.exp(-0.3 * l)
        mx = mod[l, :BATCH].reshape(BATCH, 1, 6, D_MODEL)
        mc = mod[l, BATCH:BATCH + 1].reshape(1, 1, 6, D_MODEL)
        sh1x, sc1x, g1x, sh2x, sc2x, g2x = (mx[:, :, i] for i in range(6))
        sh1c, sc1c, g1c, sh2c, sc2c, g2c = (mc[:, :, i] for i in range(6))

        wg, gb, lm = _gate_layout(lru_gate_w[l], lru_gate_b[l], lru_lambda[l])

        proj_x = norm_mod_matmul(xt, norm_g[l, 0], sh1x, sc1x, w_in, l, IN_W, tm=2048, tn=768)
        if last:
            proj_c = norm_mod_matmul(ct, norm_g[l, 0], sh1c, sc1c, w_in, l, CTX_STATE_W, tm=2048, tn=256)
        else:
            proj_c = norm_mod_matmul(ct, norm_g[l, 0], sh1c, sc1c, w_in, l, IN_W, tm=2048, tn=768)

        y_rnn_x, y_rnn_c = rglru(proj_x, proj_c, conv_w[l], conv_b[l], wg, gb, lm, xr_col=COL_XR, gr_col=COL_GR,
                                 c_xr_col=COL_XR, c_gr_col=COL_GR, ctx_out=not last)
        y_attn_x = diff_attention(proj_x, COL_Q, SEQ, proj_x, proj_c, (COL_K, COL_V, COL_K, COL_V), cos,
                                  sin_signed, diff_lambda[l], subln_g[l], lam_init=lam_init, tq=1024)
        wr = w_proj_rnn[l].astype(BF16)
        wa = w_proj_attn[l].astype(BF16)
        wo = w_out[l].astype(BF16)
        xt = merge_out(y_rnn_x, y_attn_x, proj_x, xt, g1x, norm_g[l, 1], wr, wa, wo, tm=512)
        if not last:
            y_attn_c = diff_attention(proj_c, COL_Q, CTX_LEN, None, proj_c, (0, 0, COL_K, COL_V), None, None,
                                      diff_lambda[l], subln_g[l], lam_init=lam_init, tq=CTX_LEN)
            ct = merge_out(y_rnn_c, y_attn_c, proj_c, ct, g1c, norm_g[l, 1], wr, wa, wo, tm=512)

        if l % 2 == 0:
            w_gu = ffn_w_gu[l // 2].astype(BF16)
            w_dn = ffn_w_down[l // 2].astype(BF16)
            xt = ffn_residual(xt, norm_g[l, 2], sh2x, sc2x, g2x, norm_g[l, 3], w_gu, w_dn, tm=512, tf=1408)
            if not last:
                ct = ffn_residual(ct, norm_g[l, 2], sh2c, sc2c, g2c, norm_g[l, 3], w_gu, w_dn, tm=512, tf=1408)
        else:
            w_gu, w_dn = moe_w_gu[l // 2], moe_w_down[l // 2]
            rw, rb = router_w[l // 2], router_b[l // 2]
            xt = moe_residual(xt, norm_g[l, 2], sh2x, sc2x, g2x, norm_g[l, 3], rw, rb, w_gu, w_dn)
            if not last:
                ct = moe_residual(ct, norm_g[l, 2], sh2c, sc2c, g2c, norm_g[l, 3], rw, rb, w_gu, w_dn)
    return xt.reshape(BATCH, SEQ, D_MODEL)
```

```python
import functools
import math

import jax
import jax.numpy as jnp
from jax import lax
from jax.experimental import pallas as pl
from jax.experimental.pallas import tpu as pltpu

F32 = jnp.float32
BF16 = jnp.bfloat16

D_MODEL = 1024
BATCH = 8
SEQ = 2048
DEPTH = 2
CTX_LEN = 256
GRID_W = 64
EPS = 1e-6
D_RNN = 1280
RNN_BLOCKS = 20
RNN_BLOCK_W = D_RNN // RNN_BLOCKS
CONV_W = 4
LRU_C = 8.0
N_HEADS = 8
HEAD_DIM = 64
V_DIM = 2 * HEAD_DIM
QK_W = N_HEADS * 2 * HEAD_DIM
ATTN_W = N_HEADS * V_DIM
ROPE_THETA = 10000.0
ROPE_FREQS = HEAD_DIM // 4
D_FF = 2816
N_EXPERTS = 8
TOP_K = 2

LANES = 128
SUBLANES = 8
VMEM_LIMIT_BYTES = 52 * 1024 * 1024

COL_XR = 0
COL_K = COL_XR + D_RNN
COL_V = COL_K + QK_W
COL_GR = COL_V + ATTN_W
COL_Q = COL_GR + D_RNN
COL_G_RNN = COL_Q + QK_W
COL_G_ATTN = COL_G_RNN + D_MODEL
IN_W = COL_G_ATTN + D_MODEL
CTX_STATE_W = COL_GR
GATE_BLOCK_W = 512

RNN_CB = 256
SCAN_CHUNK = 256
ATTN_SUB = 128
NORM_ROWS = 512


def _params(*sem):
    return pltpu.CompilerParams(dimension_semantics=sem, vmem_limit_bytes=VMEM_LIMIT_BYTES)


def _rms(x):
    return x * lax.rsqrt(jnp.mean(x * x, axis=-1, keepdims=True) + EPS)


def _ada_kernel(c_ref, w_ref, b_ref, o_ref):
    c = c_ref[...]
    s = c * jax.nn.sigmoid(c)
    o_ref[0] = jnp.dot(s, w_ref[0], preferred_element_type=F32, precision=lax.Precision.HIGHEST) + b_ref[0]


def ada_modulation(cvec, ada_w, ada_b):
    rows = cvec.shape[0]
    tn = 1536
    n = 6 * D_MODEL
    return pl.pallas_call(
        _ada_kernel,
        grid=(DEPTH, n // tn),
        in_specs=[
            pl.BlockSpec((rows, D_MODEL), lambda l, j: (0, 0)),
            pl.BlockSpec((1, D_MODEL, tn), lambda l, j: (l, 0, j)),
            pl.BlockSpec((1, 1, tn), lambda l, j: (l, 0, j)),
        ],
        out_specs=pl.BlockSpec((1, rows, tn), lambda l, j: (l, 0, j)),
        out_shape=jax.ShapeDtypeStruct((DEPTH, rows, n), F32),
        compiler_params=_params("parallel", "parallel"),
        name="ada_modulation",
    )(cvec, ada_w, ada_b.reshape(DEPTH, 1, n))


def _nmm_kernel(x_ref, g_ref, sh_ref, sc_ref, w_ref, o_ref, h_ref):
    @pl.when(pl.program_id(1) == 0)
    def _():
        for r0 in range(0, x_ref.shape[0], NORM_ROWS):
            h = _rms(x_ref[r0:r0 + NORM_ROWS, :]) * g_ref[...]
            h = h * (1.0 + sc_ref[0]) + sh_ref[0]
            h_ref[r0:r0 + NORM_ROWS, :] = h.astype(BF16)

    w = w_ref[0].astype(BF16)
    o_ref[...] = jnp.dot(h_ref[...], w, preferred_element_type=F32).astype(o_ref.dtype)


def norm_mod_matmul(x, gain, shift, scale, w, layer, n, *, tm, tn):
    t, d = x.shape
    nb = shift.shape[0]
    assert t % (tm * nb) == 0 and n % tn == 0 and tm % NORM_ROWS == 0, (t, tm, nb, n, tn)
    bpb = (t // tm) // nb
    return pl.pallas_call(
        _nmm_kernel,
        grid=(t // tm, n // tn),
        in_specs=[
            pl.BlockSpec((tm, d), lambda i, j: (i, 0)),
            pl.BlockSpec((1, d), lambda i, j: (0, 0)),
            pl.BlockSpec((1, 1, d), lambda i, j: (i // bpb, 0, 0)),
            pl.BlockSpec((1, 1, d), lambda i, j: (i // bpb, 0, 0)),
            pl.BlockSpec((1, d, tn), lambda i, j: (layer, 0, j)),
        ],
        out_specs=pl.BlockSpec((tm, tn), lambda i, j: (i, j)),
        out_shape=jax.ShapeDtypeStruct((t, n), BF16),
        scratch_shapes=[pltpu.VMEM((tm, d), BF16)],
        compiler_params=_params("parallel", "arbitrary"),
        name="norm_mod_matmul",
    )(x, gain.reshape(1, d), shift, scale, w)


def _group_scan(a, b, reverse):
    rows, c = a.shape
    a = a.reshape(rows // SUBLANES, SUBLANES, c)
    b = b.reshape(rows // SUBLANES, SUBLANES, c)
    sub = lax.broadcasted_iota(jnp.int32, a.shape, 1)
    for sh in (1, 2, 4):
        if reverse:
            keep = sub < SUBLANES - sh
            amount = SUBLANES - sh
        else:
            keep = sub >= sh
            amount = sh
        a_sh = jnp.where(keep, pltpu.roll(a, amount, 1), 1.0)
        b_sh = jnp.where(keep, pltpu.roll(b, amount, 1), 0.0)
        b = a * b_sh + b
        a = a * a_sh
    return a.reshape(rows, c), b.reshape(rows, c)


def _dwconv(x, cw, cb):
    n = x.shape[0]
    rows = lax.broadcasted_iota(jnp.int32, x.shape, 0)
    xm2 = jnp.where(rows >= 2, pltpu.roll(x, 2, 0), 0.0)
    xm1 = jnp.where(rows >= 1, pltpu.roll(x, 1, 0), 0.0)
    xp1 = jnp.where(rows < n - 1, pltpu.roll(x, n - 1, 0), 0.0)
    return cb + xm2 * cw[0:1] + xm1 * cw[1:2] + x * cw[2:3] + xp1 * cw[3:4]


def _gelu_tanh(x):
    return 0.5 * x * (1.0 + jnp.tanh(math.sqrt(2.0 / math.pi) * (x + 0.044715 * (x * x * x))))


def _rglru_kernel(*refs, n_c, n_x, ctx_out):
    if ctx_out:
        (xrx_ref, grx_ref, xrc_ref, grc_ref, cw_ref, cb_ref, wg_ref, gb_ref, lam_ref,
         yx_ref, yc_ref, xc_s, a0_s, b0_s, a1_s, b1_s) = refs
    else:
        (xrx_ref, grx_ref, xrc_ref, cw_ref, cb_ref, wg_ref, gb_ref, lam_ref,
         yx_ref, xc_s, a0_s, b0_s, a1_s, b1_s) = refs
    c = RNN_CB
    n_all = n_c + n_x
    cw = cw_ref[...]
    cb = cb_ref[...]

    xc_s[0:n_c, :] = _dwconv(xrc_ref[...].astype(F32), cw, cb)
    xc_s[n_c:n_all, :] = _dwconv(xrx_ref[...].astype(F32), cw, cb)

    lam = lam_ref[0]
    half_neg_sp = (-0.5 * LRU_C) * jax.nn.softplus(-lam)

    def coeff_chunk(ci, carry):
        r0 = pl.multiple_of(ci * SCAN_CHUNK, SCAN_CHUNK)
        xc = xc_s[pl.ds(r0, SCAN_CHUNK), :]
        t = jnp.tanh(jnp.dot(xc.astype(BF16), wg_ref[0], preferred_element_type=F32) + gb_ref[0])
        half_xc = 0.5 * xc
        for d, (a_s, b_s) in enumerate(((a0_s, b0_s), (a1_s, b1_s))):
            t_r = t[:, (2 * d) * c:(2 * d + 1) * c]
            t_i = t[:, (2 * d + 1) * c:(2 * d + 2) * c]
            half_sp = half_neg_sp[:, d * c:(d + 1) * c]
            a = jnp.exp(half_sp + half_sp * t_r)
            one_m_a2 = 1.0 - a * a
            mult = jnp.where(one_m_a2 > 0.0, one_m_a2 * lax.rsqrt(one_m_a2), 0.0)
            bb = mult * (half_xc + half_xc * t_i)
            a_cum, b_cum = _group_scan(a, bb, reverse=(d == 1))
            a_s[pl.ds(r0, SCAN_CHUNK), :] = a_cum
            b_s[pl.ds(r0, SCAN_CHUNK), :] = b_cum
        return carry

    lax.fori_loop(0, n_all // SCAN_CHUNK, coeff_chunk, 0)

    g_c = n_c // SUBLANES
    g_all = n_all // SUBLANES

    def group_step(a_s, b_s, row0, last, h_in):
        a = a_s[pl.ds(row0, SUBLANES), :]
        b = b_s[pl.ds(row0, SUBLANES), :]
        a_s[pl.ds(row0, SUBLANES), :] = a * h_in + b
        a_tot = jnp.broadcast_to(a[last:last + 1, :], a.shape)
        b_tot = jnp.broadcast_to(b[last:last + 1, :], b.shape)
        return a_tot * h_in + b_tot

    hf = hr = jnp.zeros((SUBLANES, c), F32)
    for k in range(g_all):
        hf = group_step(a0_s, b0_s, k * SUBLANES, SUBLANES - 1, hf)
        kr = g_c - 1 - k if k < g_c else g_all + g_c - 1 - k
        hr = group_step(a1_s, b1_s, kr * SUBLANES, 0, hr)

    yx = (a0_s[n_c:n_all, :] + a1_s[n_c:n_all, :]) * _gelu_tanh(grx_ref[...].astype(F32))
    yx_ref[...] = yx.astype(yx_ref.dtype)
    if ctx_out:
        yc = (a0_s[0:n_c, :] + a1_s[0:n_c, :]) * _gelu_tanh(grc_ref[...].astype(F32))
        yc_ref[...] = yc.astype(yc_ref.dtype)


def rglru(proj_x, proj_c, conv_w, conv_b, wg, gb, lam, *, xr_col, gr_col, c_xr_col, c_gr_col, ctx_out):
    c = RNN_CB
    n_cb = D_RNN // c
    xr_b, gr_b, cxr_b = xr_col // c, gr_col // c, c_xr_col // c
    in_specs = [
        pl.BlockSpec((SEQ, c), lambda b, j: (b, xr_b + j)),
        pl.BlockSpec((SEQ, c), lambda b, j: (b, gr_b + j)),
        pl.BlockSpec((CTX_LEN, c), lambda b, j: (b, cxr_b + j)),
    ]
    args = [proj_x, proj_x, proj_c]
    if ctx_out:
        cgr_b = c_gr_col // c
        in_specs.append(pl.BlockSpec((CTX_LEN, c), lambda b, j: (b, cgr_b + j)))
        args.append(proj_c)
    in_specs += [
        pl.BlockSpec((CONV_W, c), lambda b, j: (0, j)),
        pl.BlockSpec((1, c), lambda b, j: (0, j)),
        pl.BlockSpec((1, c, 4 * c), lambda b, j: (j, 0, 0)),
        pl.BlockSpec((1, 1, 4 * c), lambda b, j: (j, 0, 0)),
        pl.BlockSpec((1, 1, 2 * c), lambda b, j: (j, 0, 0)),
    ]
    args += [conv_w, conv_b.reshape(1, D_RNN), wg, gb, lam]
    out_specs = [pl.BlockSpec((SEQ, c), lambda b, j: (b, j))]
    out_shape = [jax.ShapeDtypeStruct((BATCH * SEQ, D_RNN), BF16)]
    if ctx_out:
        out_specs.append(pl.BlockSpec((CTX_LEN, c), lambda b, j: (b, j)))
        out_shape.append(jax.ShapeDtypeStruct((BATCH * CTX_LEN, D_RNN), BF16))
    n_all = SEQ + CTX_LEN
    out = pl.pallas_call(
        functools.partial(_rglru_kernel, n_c=CTX_LEN, n_x=SEQ, ctx_out=ctx_out),
        grid=(BATCH, n_cb),
        in_specs=in_specs,
        out_specs=out_specs,
        out_shape=out_shape,
        scratch_shapes=[pltpu.VMEM((n_all, c), F32) for _ in range(5)],
        compiler_params=_params("parallel", "parallel"),
        name="rglru",
    )(*args)
    return out if ctx_out else (out[0], None)


def _rope(t, cos, sin_signed):
    lane = lax.broadcasted_iota(jnp.int32, t.shape, 1)
    first_half = (lane % (2 * ROPE_FREQS)) < ROPE_FREQS
    partner = jnp.where(first_half, pltpu.roll(t, LANES - ROPE_FREQS, 1), pltpu.roll(t, ROPE_FREQS, 1))
    return t * cos + partner * sin_signed


def _attn_kernel(*refs, tq, n_c, n_x, lam_init):
    if n_x:
        (q_ref, kx_ref, vx_ref, kc_ref, vc_ref, cos_ref, sin_ref, dl_ref, sg_ref,
         o_ref, k_s, v_s, lam_s) = refs
    else:
        q_ref, kc_ref, vc_ref, dl_ref, sg_ref, o_ref, k_s, v_s, lam_s = refs
    qi = pl.program_id(2)

    @pl.when(qi == 0)
    def _():
        k_s[0:n_c, :] = kc_ref[...]
        v_s[0:n_c, 0:LANES] = vc_ref[...]
        if n_x:
            kx = _rope(kx_ref[...].astype(F32), cos_ref[...], sin_ref[...])
            k_s[n_c:n_c + n_x, :] = kx.astype(BF16)
            v_s[n_c:n_c + n_x, 0:LANES] = vx_ref[...]
        v_s[:, LANES:2 * LANES] = jnp.ones((n_c + n_x, LANES), BF16)
        dl = dl_ref[...]
        e1 = jnp.exp(jnp.sum(dl[0:1] * dl[1:2], axis=-1, keepdims=True))
        e2 = jnp.exp(jnp.sum(dl[2:3] * dl[3:4], axis=-1, keepdims=True))
        lam_s[...] = jnp.broadcast_to(e1 - e2 + lam_init, lam_s.shape)

    q = q_ref[...].astype(F32)
    if n_x:
        r0 = pl.multiple_of(qi * tq, tq)
        q = _rope(q, cos_ref[pl.ds(r0, tq), :], sin_ref[pl.ds(r0, tq), :])
    q = q * (HEAD_DIM ** -0.5 * math.log2(math.e))
    lane = lax.broadcasted_iota(jnp.int32, (ATTN_SUB, LANES), 1)
    lam = lam_s[0:1, 0:1]
    scores = []
    for sb in range(tq // ATTN_SUB):
        qs = q[sb * ATTN_SUB:(sb + 1) * ATTN_SUB]
        qq = jnp.concatenate([jnp.where(lane < HEAD_DIM, qs, 0.0), jnp.where(lane >= HEAD_DIM, qs, 0.0)], axis=0)
        scores.append(lax.dot_general(qq.astype(BF16), k_s[...], (((1,), (1,)), ((), ())),
                                      preferred_element_type=F32))
    for sb, s in enumerate(scores):
        e = jnp.exp2(s - jnp.max(s, axis=-1, keepdims=True))
        ov = jnp.dot(e.astype(BF16), v_s[...], preferred_element_type=F32)
        ov = ov[:, 0:LANES] / ov[:, LANES:2 * LANES]
        o = ov[0:ATTN_SUB] - lam * ov[ATTN_SUB:2 * ATTN_SUB]
        o_ref[sb * ATTN_SUB:(sb + 1) * ATTN_SUB, :] = (_rms(o) * sg_ref[...] * (1.0 - lam_init)).astype(o_ref.dtype)


def diff_attention(q_arr, q_col, n_q, kvx, kvc, cols, cos, sin_signed, diff_lambda, subln_g, *, lam_init, tq):
    kx_col, vx_col, kc_col, vc_col = cols
    qb, kxb, vxb, kcb, vcb = (v // LANES for v in (q_col, kx_col, vx_col, kc_col, vc_col))
    n_x = SEQ if kvx is not None else 0
    n_qb = n_q // tq
    in_specs = [pl.BlockSpec((tq, LANES), lambda b, h, i: (b * n_qb + i, qb + h))]
    args = [q_arr]
    if n_x:
        in_specs += [
            pl.BlockSpec((SEQ, LANES), lambda b, h, i: (b, kxb + h)),
            pl.BlockSpec((SEQ, LANES), lambda b, h, i: (b, vxb + h)),
        ]
        args += [kvx, kvx]
    in_specs += [
        pl.BlockSpec((CTX_LEN, LANES), lambda b, h, i: (b, kcb + h)),
        pl.BlockSpec((CTX_LEN, LANES), lambda b, h, i: (b, vcb + h)),
    ]
    args += [kvc, kvc]
    if n_x:
        in_specs += [
            pl.BlockSpec((SEQ, LANES), lambda b, h, i: (0, 0)),
            pl.BlockSpec((SEQ, LANES), lambda b, h, i: (0, 0)),
        ]
        args += [cos, sin_signed]
    in_specs += [
        pl.BlockSpec((4, HEAD_DIM), lambda b, h, i: (0, 0)),
        pl.BlockSpec((1, V_DIM), lambda b, h, i: (0, 0)),
    ]
    args += [diff_lambda, subln_g.reshape(1, V_DIM)]
    n_kv = CTX_LEN + n_x
    return pl.pallas_call(
        functools.partial(_attn_kernel, tq=tq, n_c=CTX_LEN, n_x=n_x, lam_init=lam_init),
        grid=(BATCH, N_HEADS, n_qb),
        in_specs=in_specs,
        out_specs=pl.BlockSpec((tq, LANES), lambda b, h, i: (b * n_qb + i, h)),
        out_shape=jax.ShapeDtypeStruct((BATCH * n_q, ATTN_W), BF16),
        scratch_shapes=[
            pltpu.VMEM((n_kv, LANES), BF16),
            pltpu.VMEM((n_kv, 2 * LANES), BF16),
            pltpu.VMEM((SUBLANES, LANES), F32),
        ],
        compiler_params=_params("parallel", "parallel", "arbitrary"),
        name="diff_attention",
    )(*args)


def _merge_kernel(yr_ref, ya_ref, gr0_ref, gr1_ref, ga0_ref, ga1_ref, x_ref, g1_ref, ng_ref, wr_ref, wa_ref, wo_ref,
                  o_ref):
    y_rnn = jnp.dot(yr_ref[...], wr_ref[...], preferred_element_type=F32)
    y_attn = jnp.dot(ya_ref[...], wa_ref[...], preferred_element_type=F32)
    g_rnn = jax.nn.sigmoid(jnp.concatenate([gr0_ref[...], gr1_ref[...]], axis=1).astype(F32))
    g_attn = jax.nn.sigmoid(jnp.concatenate([ga0_ref[...], ga1_ref[...]], axis=1).astype(F32))
    m = (g_rnn * y_rnn + g_attn * y_attn).astype(BF16)
    mx = jnp.dot(m, wo_ref[...], preferred_element_type=F32)
    o_ref[...] = x_ref[...] + g1_ref[0] * (_rms(mx) * ng_ref[...])


def merge_out(y_rnn, y_attn, proj, x, g1, ng, w_proj_rnn, w_proj_attn, w_out, *, tm):
    t, d = x.shape
    nb = g1.shape[0]
    assert t % (tm * nb) == 0, (t, tm, nb)
    bpb = (t // tm) // nb
    gw = GATE_BLOCK_W
    grb, gab = COL_G_RNN // gw, COL_G_ATTN // gw
    return pl.pallas_call(
        _merge_kernel,
        grid=(t // tm,),
        in_specs=[
            pl.BlockSpec((tm, D_RNN), lambda i: (i, 0)),
            pl.BlockSpec((tm, ATTN_W), lambda i: (i, 0)),
            pl.BlockSpec((tm, gw), lambda i: (i, grb)),
            pl.BlockSpec((tm, gw), lambda i: (i, grb + 1)),
            pl.BlockSpec((tm, gw), lambda i: (i, gab)),
            pl.BlockSpec((tm, gw), lambda i: (i, gab + 1)),
            pl.BlockSpec((tm, d), lambda i: (i, 0)),
            pl.BlockSpec((1, 1, d), lambda i: (i // bpb, 0, 0)),
            pl.BlockSpec((1, d), lambda i: (0, 0)),
            pl.BlockSpec((D_RNN, d), lambda i: (0, 0)),
            pl.BlockSpec((ATTN_W, d), lambda i: (0, 0)),
            pl.BlockSpec((d, d), lambda i: (0, 0)),
        ],
        out_specs=pl.BlockSpec((tm, d), lambda i: (i, 0)),
        out_shape=jax.ShapeDtypeStruct((t, d), F32),
        compiler_params=_params("parallel"),
        name="merge_out",
    )(y_rnn, y_attn, proj, proj, proj, proj, x, g1, ng.reshape(1, d), w_proj_rnn, w_proj_attn, w_out)


def _swiglu_partial(h, wg, wu, wd):
    gate = jnp.dot(h, wg, preferred_element_type=F32)
    up = jnp.dot(h, wu, preferred_element_type=F32)
    act = (gate * jax.nn.sigmoid(gate) * up).astype(BF16)
    return jnp.dot(act, wd, preferred_element_type=F32)


def _ffn_kernel(x_ref, g_ref, sh_ref, sc_ref, g2_ref, ng_ref, wg_ref, wu_ref, wd_ref, o_ref, h_ref, acc_ref):
    f = pl.program_id(1)

    @pl.when(f == 0)
    def _():
        h = _rms(x_ref[...]) * g_ref[...]
        h = h * (1.0 + sc_ref[0]) + sh_ref[0]
        h_ref[...] = h.astype(BF16)
        acc_ref[...] = jnp.zeros_like(acc_ref)

    acc_ref[...] += _swiglu_partial(h_ref[...], wg_ref[...], wu_ref[...], wd_ref[...])

    @pl.when(f == pl.num_programs(1) - 1)
    def _():
        o_ref[...] = x_ref[...] + g2_ref[0] * (_rms(acc_ref[...]) * ng_ref[...])


def ffn_residual(x, gain, shift, scale, g2, ng, w_gu, w_down, *, tm, tf):
    t, d = x.shape
    ff = w_gu.shape[1] // 2
    nf = ff // tf
    nb = shift.shape[0]
    assert t % (tm * nb) == 0 and ff % tf == 0, (t, tm, nb, ff, tf)
    bpb = (t // tm) // nb
    mod_spec = pl.BlockSpec((1, 1, d), lambda i, f: (i // bpb, 0, 0))
    vec_spec = pl.BlockSpec((1, d), lambda i, f: (0, 0))
    return pl.pallas_call(
        _ffn_kernel,
        grid=(t // tm, nf),
        in_specs=[
            pl.BlockSpec((tm, d), lambda i, f: (i, 0)), vec_spec, mod_spec, mod_spec, mod_spec, vec_spec,
            pl.BlockSpec((d, tf), lambda i, f: (0, f)),
            pl.BlockSpec((d, tf), lambda i, f: (0, nf + f)),
            pl.BlockSpec((tf, d), lambda i, f: (f, 0)),
        ],
        out_specs=pl.BlockSpec((tm, d), lambda i, f: (i, 0)),
        out_shape=jax.ShapeDtypeStruct((t, d), F32),
        scratch_shapes=[pltpu.VMEM((tm, d), BF16), pltpu.VMEM((tm, d), F32)],
        compiler_params=_params("parallel", "arbitrary"),
        name="ffn_residual",
    )(x, gain.reshape(1, d), shift, scale, g2, ng.reshape(1, d), w_gu, w_gu, w_down)


MOE_TM = 512
MOE_TF = 1408
ROW_TILE = 512
DMA_UNROLL = 8
META_E0, META_E1, META_W0, META_W1, META_R0, META_R1 = range(6)


def _lane_pick(rec, lane, k):
    return jnp.sum(jnp.where(lane == k, rec, 0.0), axis=-1, keepdims=True)


def _router_kernel(x_ref, g_ref, sh_ref, sc_ref, w_ref, b_ref, h_ref, meta_ref, meta_t_ref, cnt_ref, carry_s):
    @pl.when(pl.program_id(0) == 0)
    def _():
        carry_s[...] = jnp.zeros_like(carry_s)

    h = _rms(x_ref[...]) * g_ref[...]
    h = h * (1.0 + sc_ref[0]) + sh_ref[0]
    h_ref[...] = h
    logits = jnp.dot(h, w_ref[...], preferred_element_type=F32, precision=lax.Precision.HIGHEST) + b_ref[...]
    lane = lax.broadcasted_iota(jnp.int32, logits.shape, 1).astype(F32)
    neg = jnp.float32(-jnp.inf)
    logits = jnp.where(lane < N_EXPERTS, logits, neg)
    m1 = jnp.max(logits, axis=-1, keepdims=True)
    i1 = jnp.min(jnp.where(logits == m1, lane, float(LANES)), axis=-1, keepdims=True)
    rest = jnp.where(lane == i1, neg, logits)
    m2 = jnp.max(rest, axis=-1, keepdims=True)
    i2 = jnp.min(jnp.where(rest == m2, lane, float(LANES)), axis=-1, keepdims=True)
    e2 = jnp.exp(m2 - m1)
    denom = 1.0 + e2

    tm = logits.shape[0]
    sel1 = lane == i1
    sel2 = lane == i2
    member = jnp.where(jnp.logical_or(sel1, sel2), 1.0, 0.0)
    row = lax.broadcasted_iota(jnp.int32, (tm, tm), 0)
    col = lax.broadcasted_iota(jnp.int32, (tm, tm), 1)
    lower = jnp.where(col < row, 1.0, 0.0).astype(BF16)
    before = jnp.dot(lower, member.astype(BF16), preferred_element_type=F32) + carry_s[0:1, :]
    r1 = jnp.sum(jnp.where(sel1, before, 0.0), axis=-1, keepdims=True)
    r2 = jnp.sum(jnp.where(sel2, before, 0.0), axis=-1, keepdims=True)
    carry_s[0:1, :] = carry_s[0:1, :] + jnp.sum(member, axis=0, keepdims=True)
    cnt_ref[...] = jnp.broadcast_to(carry_s[0:1, :], cnt_ref.shape)

    meta = jnp.zeros_like(logits)
    for k, v in ((META_E0, i1), (META_E1, i2), (META_W0, 1.0 / denom), (META_W1, e2 / denom),
                 (META_R0, r1), (META_R1, r2)):
        meta = jnp.where(lane == k, v, meta)
    meta_ref[...] = meta
    meta_t_ref[...] = meta.T[0:SUBLANES, :]


def moe_router(x, gain, shift, scale, router_w, router_b, *, tm):
    t, d = x.shape
    nb = shift.shape[0]
    bpb = (t // tm) // nb
    w = jnp.zeros((d, LANES), F32).at[:, :N_EXPERTS].set(router_w)
    b = jnp.zeros((1, LANES), F32).at[0, :N_EXPERTS].set(router_b)
    return pl.pallas_call(
        _router_kernel,
        grid=(t // tm,),
        in_specs=[
            pl.BlockSpec((tm, d), lambda i: (i, 0)),
            pl.BlockSpec((1, d), lambda i: (0, 0)),
            pl.BlockSpec((1, 1, d), lambda i: (i // bpb, 0, 0)),
            pl.BlockSpec((1, 1, d), lambda i: (i // bpb, 0, 0)),
            pl.BlockSpec((d, LANES), lambda i: (0, 0)),
            pl.BlockSpec((1, LANES), lambda i: (0, 0)),
        ],
        out_specs=[
            pl.BlockSpec((tm, d), lambda i: (i, 0)),
            pl.BlockSpec((tm, LANES), lambda i: (i, 0)),
            pl.BlockSpec((SUBLANES, tm), lambda i: (0, i)),
            pl.BlockSpec((SUBLANES, LANES), lambda i: (0, 0)),
        ],
        out_shape=[
            jax.ShapeDtypeStruct((t, d), F32),
            jax.ShapeDtypeStruct((t, LANES), F32),
            jax.ShapeDtypeStruct((SUBLANES, t), F32),
            jax.ShapeDtypeStruct((SUBLANES, LANES), F32),
        ],
        scratch_shapes=[pltpu.VMEM((SUBLANES, LANES), F32)],
        compiler_params=_params("arbitrary"),
        name="moe_router",
    )(x, gain.reshape(1, d), shift, scale, w, b)


def _row_copy(src_ref, src_row, dst_ref, dst_row, sem):
    return pltpu.make_async_copy(src_ref.at[pl.ds(src_row, 1), :], dst_ref.at[pl.ds(dst_row, 1), :], sem)


FILL_START, FILL_COUNT = 0, 1


def _dispatch_kernel(fill_ref, pos_ref, h_ref, wgu_ref, wdn_ref, hs_ref, wgu_out_ref, wdn_out_ref, buf_s, zero_s,
                     sems):
    wgu_out_ref[...] = wgu_ref[...].astype(BF16)
    wdn_out_ref[...] = wdn_ref[...].astype(BF16)

    i = pl.program_id(0)
    n_steps = pl.num_programs(0)
    n = h_ref.shape[0]
    slot = i % 2
    zero_sem = sems.at[2]

    @pl.when(i == 0)
    def _():
        zero_s[...] = jnp.zeros_like(zero_s)

        def pad_copy(e, k):
            return _row_copy(zero_s, 0, hs_ref, fill_ref[FILL_START, e] + k, zero_sem)

        def tail_copy(k):
            row = pl.multiple_of(fill_ref[FILL_START, N_EXPERTS] + k * SUBLANES, SUBLANES)
            return pltpu.make_async_copy(zero_s, hs_ref.at[pl.ds(row, SUBLANES), :], zero_sem)

        def for_each(copy, count, op):
            def body(k, carry):
                getattr(copy(k), op)()
                return carry

            lax.fori_loop(0, count, body, 0)

        for op in ("start", "wait"):
            for e in range(N_EXPERTS):
                for_each(functools.partial(pad_copy, e), fill_ref[FILL_COUNT, e], op)
            for_each(tail_copy, fill_ref[FILL_COUNT, N_EXPERTS], op)

    buf_s[slot] = h_ref[...]
    src = buf_s.at[slot]
    sem = sems.at[slot]

    def issue(g, carry):
        for j in range(DMA_UNROLL):
            r = g * DMA_UNROLL + j
            _row_copy(src, r, hs_ref, pos_ref[0, 0, 2 * r], sem).start(priority=0)
            _row_copy(src, r, hs_ref, pos_ref[0, 0, 2 * r + 1], sem).start(priority=1)
        return carry

    lax.fori_loop(0, n // DMA_UNROLL, issue, 0)

    def drain(s):
        for _ in range(TOP_K):
            pltpu.make_async_copy(buf_s.at[s], hs_ref.at[pl.ds(0, n), :], sems.at[s]).wait()

    @pl.when(i > 0)
    def _():
        drain(1 - slot)

    @pl.when(i == n_steps - 1)
    def _():
        drain(slot)


def moe_dispatch(h, pos, fill, n_rows, w_gu, w_down):
    t, d = h.shape
    n_steps = t // ROW_TILE
    wgu2 = w_gu.reshape(-1, w_gu.shape[-1])
    wdn2 = w_down.reshape(-1, w_down.shape[-1])
    gu_rows, dn_rows = wgu2.shape[0] // n_steps, wdn2.shape[0] // n_steps
    bf16_rows = 2 * SUBLANES
    assert wgu2.shape[0] % n_steps == 0 and wdn2.shape[0] % n_steps == 0, (wgu2.shape, wdn2.shape, n_steps)
    assert gu_rows % bf16_rows == 0 and dn_rows % bf16_rows == 0, (gu_rows, dn_rows)
    hs, wgu_bf, wdn_bf = pl.pallas_call(
        _dispatch_kernel,
        grid_spec=pltpu.PrefetchScalarGridSpec(
            num_scalar_prefetch=1,
            grid=(n_steps,),
            in_specs=[
                pl.BlockSpec((1, 1, 2 * ROW_TILE), lambda i, fill: (i, 0, 0), memory_space=pltpu.SMEM),
                pl.BlockSpec((ROW_TILE, d), lambda i, fill: (i, 0)),
                pl.BlockSpec((gu_rows, wgu2.shape[1]), lambda i, fill: (i, 0)),
                pl.BlockSpec((dn_rows, wdn2.shape[1]), lambda i, fill: (i, 0)),
            ],
            out_specs=[
                pl.BlockSpec(memory_space=pl.ANY),
                pl.BlockSpec((gu_rows, wgu2.shape[1]), lambda i, fill: (i, 0)),
                pl.BlockSpec((dn_rows, wdn2.shape[1]), lambda i, fill: (i, 0)),
            ],
            scratch_shapes=[pltpu.VMEM((2, ROW_TILE, d), F32), pltpu.VMEM((SUBLANES, d), F32),
                            pltpu.SemaphoreType.DMA((3,))],
        ),
        out_shape=[
            jax.ShapeDtypeStruct((n_rows, d), F32),
            jax.ShapeDtypeStruct(wgu2.shape, BF16),
            jax.ShapeDtypeStruct(wdn2.shape, BF16),
        ],
        compiler_params=_params("arbitrary"),
        name="moe_dispatch",
    )(fill, pos, h, wgu2, wdn2)
    return hs, wgu_bf.reshape(w_gu.shape), wdn_bf.reshape(w_down.shape)


def _grouped_ffn_kernel(te_ref, na_ref, hs_ref, wg_ref, wu_ref, wd_ref, ys_ref, h_s, acc_s):
    del te_ref
    f = pl.program_id(1)
    active = pl.program_id(0) < na_ref[0]

    @pl.when(jnp.logical_not(active))
    def _():
        ys_ref[...] = jnp.zeros_like(ys_ref)

    @pl.when(active)
    def _():
        @pl.when(f == 0)
        def _():
            h_s[...] = hs_ref[...].astype(BF16)
            acc_s[...] = jnp.zeros_like(acc_s)

        acc_s[...] += _swiglu_partial(h_s[...], wg_ref[0], wu_ref[0], wd_ref[0])

        @pl.when(f == pl.num_programs(1) - 1)
        def _():
            ys_ref[...] = acc_s[...]


def moe_grouped_ffn(hs, tile_expert, n_active, w_gu, w_down):
    n_rows, d = hs.shape
    ff = w_gu.shape[2] // 2
    nf = ff // MOE_TF
    n_tiles = n_rows // MOE_TM

    def row_map(t, f, te, na):
        return (jnp.maximum(jnp.minimum(t, na[0] - 1), 0), 0)

    def f_eff(t, f, na):
        return jnp.where(t < na[0], f, nf - 1)

    return pl.pallas_call(
        _grouped_ffn_kernel,
        grid_spec=pltpu.PrefetchScalarGridSpec(
            num_scalar_prefetch=2,
            grid=(n_tiles, nf),
            in_specs=[
                pl.BlockSpec((MOE_TM, d), row_map),
                pl.BlockSpec((1, d, MOE_TF), lambda t, f, te, na: (te[t], 0, f_eff(t, f, na))),
                pl.BlockSpec((1, d, MOE_TF), lambda t, f, te, na: (te[t], 0, nf + f_eff(t, f, na))),
                pl.BlockSpec((1, MOE_TF, d), lambda t, f, te, na: (te[t], f_eff(t, f, na), 0)),
            ],
            out_specs=pl.BlockSpec((MOE_TM, d), lambda t, f, te, na: (t, 0)),
            scratch_shapes=[pltpu.VMEM((MOE_TM, d), BF16), pltpu.VMEM((MOE_TM, d), F32)],
        ),
        out_shape=jax.ShapeDtypeStruct((n_rows, d), F32),
        compiler_params=_params("arbitrary", "arbitrary"),
        name="moe_grouped_ffn",
    )(tile_expert, n_active, hs, w_gu, w_gu, w_down)


def _combine_kernel(pos_ref, pos_next_ref, ys_ref, x_ref, meta_ref, g2_ref, ng_ref, o_ref, a_s, b_s, sems):
    i = pl.program_id(0)
    n_steps = pl.num_programs(0)
    n = x_ref.shape[0]
    slot = i % 2

    def gather(p_ref, s):
        def issue(g, carry):
            for j in range(DMA_UNROLL):
                r = g * DMA_UNROLL + j
                _row_copy(ys_ref, p_ref[0, 0, 2 * r], a_s.at[s], r, sems.at[s]).start(priority=0)
                _row_copy(ys_ref, p_ref[0, 0, 2 * r + 1], b_s.at[s], r, sems.at[s]).start(priority=1)
            return carry

        lax.fori_loop(0, n // DMA_UNROLL, issue, 0)

    @pl.when(i == 0)
    def _():
        gather(pos_ref, 0)

    @pl.when(i + 1 < n_steps)
    def _():
        gather(pos_next_ref, 1 - slot)

    pltpu.make_async_copy(ys_ref.at[pl.ds(0, n), :], a_s.at[slot], sems.at[slot]).wait()
    pltpu.make_async_copy(ys_ref.at[pl.ds(0, n), :], b_s.at[slot], sems.at[slot]).wait()

    meta = meta_ref[...]
    lane = lax.broadcasted_iota(jnp.int32, meta.shape, 1)
    mixed = _lane_pick(meta, lane, META_W0) * a_s[slot] + _lane_pick(meta, lane, META_W1) * b_s[slot]
    o_ref[...] = x_ref[...] + g2_ref[0] * (_rms(mixed) * ng_ref[...])


def moe_combine(ys, pos, meta, x, g2, ng):
    t, d = x.shape
    nb = g2.shape[0]
    n_steps = t // ROW_TILE
    bpb = n_steps // nb
    return pl.pallas_call(
        _combine_kernel,
        grid=(n_steps,),
        in_specs=[
            pl.BlockSpec((1, 1, 2 * ROW_TILE), lambda i: (i, 0, 0), memory_space=pltpu.SMEM),
            pl.BlockSpec((1, 1, 2 * ROW_TILE), lambda i: (jnp.minimum(i + 1, n_steps - 1), 0, 0),
                         memory_space=pltpu.SMEM),
            pl.BlockSpec(memory_space=pl.ANY),
            pl.BlockSpec((ROW_TILE, d), lambda i: (i, 0)),
            pl.BlockSpec((ROW_TILE, LANES), lambda i: (i, 0)),
            pl.BlockSpec((1, 1, d), lambda i: (i // bpb, 0, 0)),
            pl.BlockSpec((1, d), lambda i: (0, 0)),
        ],
        out_specs=pl.BlockSpec((ROW_TILE, d), lambda i: (i, 0)),
        out_shape=jax.ShapeDtypeStruct((t, d), F32),
        scratch_shapes=[pltpu.VMEM((2, ROW_TILE, d), F32), pltpu.VMEM((2, ROW_TILE, d), F32),
                        pltpu.SemaphoreType.DMA((2,))],
        compiler_params=_params("arbitrary"),
        name="moe_combine",
    )(pos, pos, ys, x, meta, g2, ng.reshape(1, d))


def moe_residual(x, gain, shift, scale, g2, ng, router_w, router_b, w_gu, w_down):
    t, d = x.shape
    h, meta, meta_t, counts = moe_router(x, gain, shift, scale, router_w, router_b, tm=512)
    n_rows = TOP_K * t + N_EXPERTS * MOE_TM
    n_tiles = n_rows // MOE_TM
    cnt = counts[0, :N_EXPERTS].astype(jnp.int32)
    padded = (cnt + MOE_TM - 1) // MOE_TM * MOE_TM
    seg_end = jnp.cumsum(padded)
    seg_start = seg_end - padded
    experts = jnp.arange(N_EXPERTS, dtype=jnp.int32)

    def position(e_field, r_field):
        e = meta_t[e_field].astype(jnp.int32)
        start = jnp.sum(jnp.where(e[None, :] == experts[:, None], seg_start[:, None], 0), axis=0)
        return start + meta_t[r_field].astype(jnp.int32)

    pos = jnp.stack([position(META_E0, META_R0), position(META_E1, META_R1)], axis=-1)
    pos = pos.reshape(t // ROW_TILE, 1, 2 * ROW_TILE)
    n_active = seg_end[-1:] // MOE_TM
    tile_ids = jnp.arange(n_tiles, dtype=jnp.int32)
    tile_expert = jnp.sum((jnp.minimum(tile_ids, n_active - 1)[:, None] * MOE_TM >= seg_end[None, :]), axis=-1)
    tile_expert = jnp.minimum(tile_expert, N_EXPERTS - 1).astype(jnp.int32)

    fill = jnp.stack([
        jnp.concatenate([seg_start + cnt, seg_end[-1:], jnp.zeros((N_EXPERTS - 1,), jnp.int32)]),
        jnp.concatenate([padded - cnt, (n_rows - seg_end[-1:]) // SUBLANES, jnp.zeros((N_EXPERTS - 1,), jnp.int32)]),
    ]).astype(jnp.int32)
    hs, w_gu_bf, w_down_bf = moe_dispatch(h, pos, fill, n_rows, w_gu, w_down)
    ys = moe_grouped_ffn(hs, tile_expert, n_active.astype(jnp.int32), w_gu_bf, w_down_bf)
    return moe_combine(ys, pos, meta, x, g2, ng)


def _rope_tables():
    pos = jnp.arange(SEQ)
    inv_freq = jnp.power(ROPE_THETA, -jnp.arange(ROPE_FREQS, dtype=F32) / ROPE_FREQS)
    ang_r = (pos // GRID_W).astype(F32)[:, None] * inv_freq
    ang_c = (pos % GRID_W).astype(F32)[:, None] * inv_freq
    cos = jnp.concatenate([jnp.cos(ang_r)] * 2 + [jnp.cos(ang_c)] * 2, axis=-1)
    sin = jnp.concatenate([-jnp.sin(ang_r), jnp.sin(ang_r), -jnp.sin(ang_c), jnp.sin(ang_c)], axis=-1)
    return jnp.tile(cos, (1, 2)), jnp.tile(sin, (1, 2))


def _gate_layout(gate_w, gate_b, lam):
    c = RNN_CB
    n_cb = D_RNN // c
    per = c // RNN_BLOCK_W
    gw = gate_w.reshape(2, 2, n_cb, per, RNN_BLOCK_W, RNN_BLOCK_W)
    gw = jnp.transpose(gw, (2, 3, 4, 0, 1, 5))
    half_eye = 0.5 * jnp.eye(per, dtype=gate_w.dtype)
    bd = gw[:, :, :, :, :, None, :] * half_eye[None, :, None, None, None, :, None]
    wg = bd.reshape(n_cb, c, 4 * c).astype(BF16)
    gb = 0.5 * jnp.transpose(gate_b.reshape(2, 2, n_cb, c), (2, 0, 1, 3)).reshape(n_cb, 1, 4 * c)
    lm = jnp.transpose(lam.reshape(2, n_cb, c), (1, 0, 2)).reshape(n_cb, 1, 2 * c)
    return wg, gb, lm


def kernel(x, c, ctx, c_ctx, ada_w, ada_b, norm_g, w_in, conv_w, conv_b, lru_gate_w, lru_gate_b, lru_lambda,
           diff_lambda, subln_g, w_proj_rnn, w_proj_attn, w_out, ffn_w_gu, ffn_w_down, router_w, router_b,
           moe_w_gu, moe_w_down):
    xt = x.reshape(BATCH * SEQ, D_MODEL)
    ct = ctx.reshape(BATCH * CTX_LEN, D_MODEL)
    cos, sin_signed = _rope_tables()

    cvec = jnp.concatenate([c, c_ctx[None, :], jnp.zeros((2 * SUBLANES - BATCH - 1, D_MODEL), F32)], axis=0)
    mod = ada_modulation(cvec, ada_w, ada_b)

    for l in range(DEPTH):
        last = l == DEPTH - 1
        lam_init = 0.8 - 0.6 * math.exp(-0.3 * l)
        mx = mod[l, :BATCH].reshape(BATCH, 1, 6, D_MODEL)
        mc = mod[l, BATCH:BATCH + 1].reshape(1, 1, 6, D_MODEL)
        sh1x, sc1x, g1x, sh2x, sc2x, g2x = (mx[:, :, i] for i in range(6))
        sh1c, sc1c, g1c, sh2c, sc2c, g2c = (mc[:, :, i] for i in range(6))

        wg, gb, lm = _gate_layout(lru_gate_w[l], lru_gate_b[l], lru_lambda[l])

        proj_x = norm_mod_matmul(xt, norm_g[l, 0], sh1x, sc1x, w_in, l, IN_W, tm=2048, tn=768)
        if last:
            proj_c = norm_mod_matmul(ct, norm_g[l, 0], sh1c, sc1c, w_in, l, CTX_STATE_W, tm=2048, tn=256)
        else:
            proj_c = norm_mod_matmul(ct, norm_g[l, 0], sh1c, sc1c, w_in, l, IN_W, tm=2048, tn=768)

        y_rnn_x, y_rnn_c = rglru(proj_x, proj_c, conv_w[l], conv_b[l], wg, gb, lm, xr_col=COL_XR, gr_col=COL_GR,
                                 c_xr_col=COL_XR, c_gr_col=COL_GR, ctx_out=not last)
        y_attn_x = diff_attention(proj_x, COL_Q, SEQ, proj_x, proj_c, (COL_K, COL_V, COL_K, COL_V), cos,
                                  sin_signed, diff_lambda[l], subln_g[l], lam_init=lam_init, tq=1024)
        wr = w_proj_rnn[l].astype(BF16)
        wa = w_proj_attn[l].astype(BF16)
        wo = w_out[l].astype(BF16)
        xt = merge_out(y_rnn_x, y_attn_x, proj_x, xt, g1x, norm_g[l, 1], wr, wa, wo, tm=512)
        if not last:
            y_attn_c = diff_attention(proj_c, COL_Q, CTX_LEN, None, proj_c, (0, 0, COL_K, COL_V), None, None,
                                      diff_lambda[l], subln_g[l], lam_init=lam_init, tq=CTX_LEN)
            ct = merge_out(y_rnn_c, y_attn_c, proj_c, ct, g1c, norm_g[l, 1], wr, wa, wo, tm=512)

        if l % 2 == 0:
            w_gu = ffn_w_gu[l // 2].astype(BF16)
            w_dn = ffn_w_down[l // 2].astype(BF16)
            xt = ffn_residual(xt, norm_g[l, 2], sh2x, sc2x, g2x, norm_g[l, 3], w_gu, w_dn, tm=512, tf=1408)
            if not last:
                ct = ffn_residual(ct, norm_g[l, 2], sh2c, sc2c, g2c, norm_g[l, 3], w_gu, w_dn, tm=512, tf=1408)
        else:
            w_gu, w_dn = moe_w_gu[l // 2], moe_w_down[l // 2]
            rw, rb = router_w[l // 2], router_b[l // 2]
            xt = moe_residual(xt, norm_g[l, 2], sh2x, sc2x, g2x, norm_g[l, 3], rw, rb, w_gu, w_dn)
            if not last:
                ct = moe_residual(ct, norm_g[l, 2], sh2c, sc2c, g2c, norm_g[l, 3], rw, rb, w_gu, w_dn)
    return xt.reshape(BATCH, SEQ, D_MODEL)
```

```python
import functools
import math

import jax
import jax.numpy as jnp
from jax import lax
from jax.experimental import pallas as pl
from jax.experimental.pallas import tpu as pltpu

F32 = jnp.float32
BF16 = jnp.bfloat16

D_MODEL = 1024
BATCH = 8
SEQ = 2048
DEPTH = 2
CTX_LEN = 256
GRID_W = 64
EPS = 1e-6
D_RNN = 1280
RNN_BLOCKS = 20
RNN_BLOCK_W = D_RNN // RNN_BLOCKS
CONV_W = 4
LRU_C = 8.0
N_HEADS = 8
HEAD_DIM = 64
V_DIM = 2 * HEAD_DIM
QK_W = N_HEADS * 2 * HEAD_DIM
ATTN_W = N_HEADS * V_DIM
ROPE_THETA = 10000.0
ROPE_FREQS = HEAD_DIM // 4
D_FF = 2816
N_EXPERTS = 8
TOP_K = 2

LANES = 128
SUBLANES = 8
VMEM_LIMIT_BYTES = 52 * 1024 * 1024

COL_XR = 0
COL_K = COL_XR + D_RNN
COL_V = COL_K + QK_W
COL_GR = COL_V + ATTN_W
COL_Q = COL_GR + D_RNN
COL_G_RNN = COL_Q + QK_W
COL_G_ATTN = COL_G_RNN + D_MODEL
IN_W = COL_G_ATTN + D_MODEL
CTX_STATE_W = COL_GR
GATE_BLOCK_W = 512

RNN_CB = 256
SCAN_CHUNK = 256
ATTN_SUB = 128
NORM_ROWS = 512


def _params(*sem):
    return pltpu.CompilerParams(dimension_semantics=sem, vmem_limit_bytes=VMEM_LIMIT_BYTES)


def _rms(x):
    return x * lax.rsqrt(jnp.mean(x * x, axis=-1, keepdims=True) + EPS)


def _ada_kernel(c_ref, w_ref, b_ref, o_ref):
    c = c_ref[...]
    s = c * jax.nn.sigmoid(c)
    o_ref[0] = jnp.dot(s, w_ref[0], preferred_element_type=F32, precision=lax.Precision.HIGHEST) + b_ref[0]


def ada_modulation(cvec, ada_w, ada_b):
    rows = cvec.shape[0]
    tn = 1536
    n = 6 * D_MODEL
    return pl.pallas_call(
        _ada_kernel,
        grid=(DEPTH, n // tn),
        in_specs=[
            pl.BlockSpec((rows, D_MODEL), lambda l, j: (0, 0)),
            pl.BlockSpec((1, D_MODEL, tn), lambda l, j: (l, 0, j)),
            pl.BlockSpec((1, 1, tn), lambda l, j: (l, 0, j)),
        ],
        out_specs=pl.BlockSpec((1, rows, tn), lambda l, j: (l, 0, j)),
        out_shape=jax.ShapeDtypeStruct((DEPTH, rows, n), F32),
        compiler_params=_params("parallel", "parallel"),
        name="ada_modulation",
    )(cvec, ada_w, ada_b.reshape(DEPTH, 1, n))


def _nmm_kernel(x_ref, g_ref, sh_ref, sc_ref, w_ref, o_ref, h_ref):
    @pl.when(pl.program_id(1) == 0)
    def _():
        for r0 in range(0, x_ref.shape[0], NORM_ROWS):
            h = _rms(x_ref[r0:r0 + NORM_ROWS, :]) * g_ref[...]
            h = h * (1.0 + sc_ref[0]) + sh_ref[0]
            h_ref[r0:r0 + NORM_ROWS, :] = h.astype(BF16)

    w = w_ref[0].astype(BF16)
    o_ref[...] = jnp.dot(h_ref[...], w, preferred_element_type=F32).astype(o_ref.dtype)


def norm_mod_matmul(x, gain, shift, scale, w, layer, n, *, tm, tn):
    t, d = x.shape
    nb = shift.shape[0]
    assert t % (tm * nb) == 0 and n % tn == 0 and tm % NORM_ROWS == 0, (t, tm, nb, n, tn)
    bpb = (t // tm) // nb
    return pl.pallas_call(
        _nmm_kernel,
        grid=(t // tm, n // tn),
        in_specs=[
            pl.BlockSpec((tm, d), lambda i, j: (i, 0)),
            pl.BlockSpec((1, d), lambda i, j: (0, 0)),
            pl.BlockSpec((1, 1, d), lambda i, j: (i // bpb, 0, 0)),
            pl.BlockSpec((1, 1, d), lambda i, j: (i // bpb, 0, 0)),
            pl.BlockSpec((1, d, tn), lambda i, j: (layer, 0, j)),
        ],
        out_specs=pl.BlockSpec((tm, tn), lambda i, j: (i, j)),
        out_shape=jax.ShapeDtypeStruct((t, n), BF16),
        scratch_shapes=[pltpu.VMEM((tm, d), BF16)],
        compiler_params=_params("parallel", "arbitrary"),
        name="norm_mod_matmul",
    )(x, gain.reshape(1, d), shift, scale, w)


def _group_scan(a, b, reverse):
    rows, c = a.shape
    a = a.reshape(rows // SUBLANES, SUBLANES, c)
    b = b.reshape(rows // SUBLANES, SUBLANES, c)
    sub = lax.broadcasted_iota(jnp.int32, a.shape, 1)
    for sh in (1, 2, 4):
        if reverse:
            keep = sub < SUBLANES - sh
            amount = SUBLANES - sh
        else:
            keep = sub >= sh
            amount = sh
        a_sh = jnp.where(keep, pltpu.roll(a, amount, 1), 1.0)
        b_sh = jnp.where(keep, pltpu.roll(b, amount, 1), 0.0)
        b = a * b_sh + b
        a = a * a_sh
    return a.reshape(rows, c), b.reshape(rows, c)


def _dwconv(x, cw, cb):
    n = x.shape[0]
    rows = lax.broadcasted_iota(jnp.int32, x.shape, 0)
    xm2 = jnp.where(rows >= 2, pltpu.roll(x, 2, 0), 0.0)
    xm1 = jnp.where(rows >= 1, pltpu.roll(x, 1, 0), 0.0)
    xp1 = jnp.where(rows < n - 1, pltpu.roll(x, n - 1, 0), 0.0)
    return cb + xm2 * cw[0:1] + xm1 * cw[1:2] + x * cw[2:3] + xp1 * cw[3:4]


def _gelu_tanh(x):
    return 0.5 * x * (1.0 + jnp.tanh(math.sqrt(2.0 / math.pi) * (x + 0.044715 * (x * x * x))))


def _rglru_kernel(*refs, n_c, n_x, ctx_out):
    if ctx_out:
        (xrx_ref, grx_ref, xrc_ref, grc_ref, cw_ref, cb_ref, wg_ref, gb_ref, lam_ref,
         yx_ref, yc_ref, xc_s, a0_s, b0_s, a1_s, b1_s) = refs
    else:
        (xrx_ref, grx_ref, xrc_ref, cw_ref, cb_ref, wg_ref, gb_ref, lam_ref,
         yx_ref, xc_s, a0_s, b0_s, a1_s, b1_s) = refs
    c = RNN_CB
    n_all = n_c + n_x
    cw = cw_ref[...]
    cb = cb_ref[...]

    xc_s[0:n_c, :] = _dwconv(xrc_ref[...].astype(F32), cw, cb)
    xc_s[n_c:n_all, :] = _dwconv(xrx_ref[...].astype(F32), cw, cb)

    lam = lam_ref[0]
    half_neg_sp = (-0.5 * LRU_C) * jax.nn.softplus(-lam)

    def coeff_chunk(ci, carry):
        r0 = pl.multiple_of(ci * SCAN_CHUNK, SCAN_CHUNK)
        xc = xc_s[pl.ds(r0, SCAN_CHUNK), :]
        t = jnp.tanh(jnp.dot(xc.astype(BF16), wg_ref[0], preferred_element_type=F32) + gb_ref[0])
        half_xc = 0.5 * xc
        for d, (a_s, b_s) in enumerate(((a0_s, b0_s), (a1_s, b1_s))):
            t_r = t[:, (2 * d) * c:(2 * d + 1) * c]
            t_i = t[:, (2 * d + 1) * c:(2 * d + 2) * c]
            half_sp = half_neg_sp[:, d * c:(d + 1) * c]
            a = jnp.exp(half_sp + half_sp * t_r)
            one_m_a2 = 1.0 - a * a
            mult = jnp.where(one_m_a2 > 0.0, one_m_a2 * lax.rsqrt(one_m_a2), 0.0)
            bb = mult * (half_xc + half_xc * t_i)
            a_cum, b_cum = _group_scan(a, bb, reverse=(d == 1))
            a_s[pl.ds(r0, SCAN_CHUNK), :] = a_cum
            b_s[pl.ds(r0, SCAN_CHUNK), :] = b_cum
        return carry

    lax.fori_loop(0, n_all // SCAN_CHUNK, coeff_chunk, 0)

    g_c = n_c // SUBLANES
    g_all = n_all // SUBLANES

    def group_step(a_s, b_s, row0, last, h_in):
        a = a_s[pl.ds(row0, SUBLANES), :]
        b = b_s[pl.ds(row0, SUBLANES), :]
        a_s[pl.ds(row0, SUBLANES), :] = a * h_in + b
        a_tot = jnp.broadcast_to(a[last:last + 1, :], a.shape)
        b_tot = jnp.broadcast_to(b[last:last + 1, :], b.shape)
        return a_tot * h_in + b_tot

    hf = hr = jnp.zeros((SUBLANES, c), F32)
    for k in range(g_all):
        hf = group_step(a0_s, b0_s, k * SUBLANES, SUBLANES - 1, hf)
        kr = g_c - 1 - k if k < g_c else g_all + g_c - 1 - k
        hr = group_step(a1_s, b1_s, kr * SUBLANES, 0, hr)

    yx = (a0_s[n_c:n_all, :] + a1_s[n_c:n_all, :]) * _gelu_tanh(grx_ref[...].astype(F32))
    yx_ref[...] = yx.astype(yx_ref.dtype)
    if ctx_out:
        yc = (a0_s[0:n_c, :] + a1_s[0:n_c, :]) * _gelu_tanh(grc_ref[...].astype(F32))
        yc_ref[...] = yc.astype(yc_ref.dtype)


def rglru(proj_x, proj_c, conv_w, conv_b, wg, gb, lam, *, xr_col, gr_col, c_xr_col, c_gr_col, ctx_out):
    c = RNN_CB
    n_cb = D_RNN // c
    xr_b, gr_b, cxr_b = xr_col // c, gr_col // c, c_xr_col // c
    in_specs = [
        pl.BlockSpec((SEQ, c), lambda b, j: (b, xr_b + j)),
        pl.BlockSpec((SEQ, c), lambda b, j: (b, gr_b + j)),
        pl.BlockSpec((CTX_LEN, c), lambda b, j: (b, cxr_b + j)),
    ]
    args = [proj_x, proj_x, proj_c]
    if ctx_out:
        cgr_b = c_gr_col // c
        in_specs.append(pl.BlockSpec((CTX_LEN, c), lambda b, j: (b, cgr_b + j)))
        args.append(proj_c)
    in_specs += [
        pl.BlockSpec((CONV_W, c), lambda b, j: (0, j)),
        pl.BlockSpec((1, c), lambda b, j: (0, j)),
        pl.BlockSpec((1, c, 4 * c), lambda b, j: (j, 0, 0)),
        pl.BlockSpec((1, 1, 4 * c), lambda b, j: (j, 0, 0)),
        pl.BlockSpec((1, 1, 2 * c), lambda b, j: (j, 0, 0)),
    ]
    args += [conv_w, conv_b.reshape(1, D_RNN), wg, gb, lam]
    out_specs = [pl.BlockSpec((SEQ, c), lambda b, j: (b, j))]
    out_shape = [jax.ShapeDtypeStruct((BATCH * SEQ, D_RNN), BF16)]
    if ctx_out:
        out_specs.append(pl.BlockSpec((CTX_LEN, c), lambda b, j: (b, j)))
        out_shape.append(jax.ShapeDtypeStruct((BATCH * CTX_LEN, D_RNN), BF16))
    n_all = SEQ + CTX_LEN
    out = pl.pallas_call(
        functools.partial(_rglru_kernel, n_c=CTX_LEN, n_x=SEQ, ctx_out=ctx_out),
        grid=(BATCH, n_cb),
        in_specs=in_specs,
        out_specs=out_specs,
        out_shape=out_shape,
        scratch_shapes=[pltpu.VMEM((n_all, c), F32) for _ in range(5)],
        compiler_params=_params("parallel", "parallel"),
        name="rglru",
    )(*args)
    return out if ctx_out else (out[0], None)


def _rope(t, cos, sin_signed):
    lane = lax.broadcasted_iota(jnp.int32, t.shape, 1)
    first_half = (lane % (2 * ROPE_FREQS)) < ROPE_FREQS
    partner = jnp.where(first_half, pltpu.roll(t, LANES - ROPE_FREQS, 1), pltpu.roll(t, ROPE_FREQS, 1))
    return t * cos + partner * sin_signed


def _attn_kernel(*refs, tq, n_c, n_x, lam_init):
    if n_x:
        (q_ref, kx_ref, vx_ref, kc_ref, vc_ref, cos_ref, sin_ref, dl_ref, sg_ref,
         o_ref, k_s, v_s, lam_s) = refs
    else:
        q_ref, kc_ref, vc_ref, dl_ref, sg_ref, o_ref, k_s, v_s, lam_s = refs
    qi = pl.program_id(2)

    @pl.when(qi == 0)
    def _():
        k_s[0:n_c, :] = kc_ref[...]
        v_s[0:n_c, 0:LANES] = vc_ref[...]
        if n_x:
            kx = _rope(kx_ref[...].astype(F32), cos_ref[...], sin_ref[...])
            k_s[n_c:n_c + n_x, :] = kx.astype(BF16)
            v_s[n_c:n_c + n_x, 0:LANES] = vx_ref[...]
        v_s[:, LANES:2 * LANES] = jnp.ones((n_c + n_x, LANES), BF16)
        dl = dl_ref[...]
        e1 = jnp.exp(jnp.sum(dl[0:1] * dl[1:2], axis=-1, keepdims=True))
        e2 = jnp.exp(jnp.sum(dl[2:3] * dl[3:4], axis=-1, keepdims=True))
        lam_s[...] = jnp.broadcast_to(e1 - e2 + lam_init, lam_s.shape)

    q = q_ref[...].astype(F32)
    if n_x:
        r0 = pl.multiple_of(qi * tq, tq)
        q = _rope(q, cos_ref[pl.ds(r0, tq), :], sin_ref[pl.ds(r0, tq), :])
    q = q * (HEAD_DIM ** -0.5 * math.log2(math.e))
    lane = lax.broadcasted_iota(jnp.int32, (ATTN_SUB, LANES), 1)
    lam = lam_s[0:1, 0:1]
    scores = []
    for sb in range(tq // ATTN_SUB):
        qs = q[sb * ATTN_SUB:(sb + 1) * ATTN_SUB]
        qq = jnp.concatenate([jnp.where(lane < HEAD_DIM, qs, 0.0), jnp.where(lane >= HEAD_DIM, qs, 0.0)], axis=0)
        scores.append(lax.dot_general(qq.astype(BF16), k_s[...], (((1,), (1,)), ((), ())),
                                      preferred_element_type=F32))
    for sb, s in enumerate(scores):
        e = jnp.exp2(s - jnp.max(s, axis=-1, keepdims=True))
        ov = jnp.dot(e.astype(BF16), v_s[...], preferred_element_type=F32)
        ov = ov[:, 0:LANES] / ov[:, LANES:2 * LANES]
        o = ov[0:ATTN_SUB] - lam * ov[ATTN_SUB:2 * ATTN_SUB]
        o_ref[sb * ATTN_SUB:(sb + 1) * ATTN_SUB, :] = (_rms(o) * sg_ref[...] * (1.0 - lam_init)).astype(o_ref.dtype)


def diff_attention(q_arr, q_col, n_q, kvx, kvc, cols, cos, sin_signed, diff_lambda, subln_g, *, lam_init, tq):
    kx_col, vx_col, kc_col, vc_col = cols
    qb, kxb, vxb, kcb, vcb = (v // LANES for v in (q_col, kx_col, vx_col, kc_col, vc_col))
    n_x = SEQ if kvx is not None else 0
    n_qb = n_q // tq
    in_specs = [pl.BlockSpec((tq, LANES), lambda b, h, i: (b * n_qb + i, qb + h))]
    args = [q_arr]
    if n_x:
        in_specs += [
            pl.BlockSpec((SEQ, LANES), lambda b, h, i: (b, kxb + h)),
            pl.BlockSpec((SEQ, LANES), lambda b, h, i: (b, vxb + h)),
        ]
        args += [kvx, kvx]
    in_specs += [
        pl.BlockSpec((CTX_LEN, LANES), lambda b, h, i: (b, kcb + h)),
        pl.BlockSpec((CTX_LEN, LANES), lambda b, h, i: (b, vcb + h)),
    ]
    args += [kvc, kvc]
    if n_x:
        in_specs += [
            pl.BlockSpec((SEQ, LANES), lambda b, h, i: (0, 0)),
            pl.BlockSpec((SEQ, LANES), lambda b, h, i: (0, 0)),
        ]
        args += [cos, sin_signed]
    in_specs += [
        pl.BlockSpec((4, HEAD_DIM), lambda b, h, i: (0, 0)),
        pl.BlockSpec((1, V_DIM), lambda b, h, i: (0, 0)),
    ]
    args += [diff_lambda, subln_g.reshape(1, V_DIM)]
    n_kv = CTX_LEN + n_x
    return pl.pallas_call(
        functools.partial(_attn_kernel, tq=tq, n_c=CTX_LEN, n_x=n_x, lam_init=lam_init),
        grid=(BATCH, N_HEADS, n_qb),
        in_specs=in_specs,
        out_specs=pl.BlockSpec((tq, LANES), lambda b, h, i: (b * n_qb + i, h)),
        out_shape=jax.ShapeDtypeStruct((BATCH * n_q, ATTN_W), BF16),
        scratch_shapes=[
            pltpu.VMEM((n_kv, LANES), BF16),
            pltpu.VMEM((n_kv, 2 * LANES), BF16),
            pltpu.VMEM((SUBLANES, LANES), F32),
        ],
        compiler_params=_params("parallel", "parallel", "arbitrary"),
        name="diff_attention",
    )(*args)


def _merge_kernel(yr_ref, ya_ref, gr0_ref, gr1_ref, ga0_ref, ga1_ref, x_ref, g1_ref, ng_ref, wr_ref, wa_ref, wo_ref,
                  o_ref):
    y_rnn = jnp.dot(yr_ref[...], wr_ref[...], preferred_element_type=F32)
    y_attn = jnp.dot(ya_ref[...], wa_ref[...], preferred_element_type=F32)
    g_rnn = jax.nn.sigmoid(jnp.concatenate([gr0_ref[...], gr1_ref[...]], axis=1).astype(F32))
    g_attn = jax.nn.sigmoid(jnp.concatenate([ga0_ref[...], ga1_ref[...]], axis=1).astype(F32))
    m = (g_rnn * y_rnn + g_attn * y_attn).astype(BF16)
    mx = jnp.dot(m, wo_ref[...], preferred_element_type=F32)
    o_ref[...] = x_ref[...] + g1_ref[0] * (_rms(mx) * ng_ref[...])


def merge_out(y_rnn, y_attn, proj, x, g1, ng, w_proj_rnn, w_proj_attn, w_out, *, tm):
    t, d = x.shape
    nb = g1.shape[0]
    assert t % (tm * nb) == 0, (t, tm, nb)
    bpb = (t // tm) // nb
    gw = GATE_BLOCK_W
    grb, gab = COL_G_RNN // gw, COL_G_ATTN // gw
    return pl.pallas_call(
        _merge_kernel,
        grid=(t // tm,),
        in_specs=[
            pl.BlockSpec((tm, D_RNN), lambda i: (i, 0)),
            pl.BlockSpec((tm, ATTN_W), lambda i: (i, 0)),
            pl.BlockSpec((tm, gw), lambda i: (i, grb)),
            pl.BlockSpec((tm, gw), lambda i: (i, grb + 1)),
            pl.BlockSpec((tm, gw), lambda i: (i, gab)),
            pl.BlockSpec((tm, gw), lambda i: (i, gab + 1)),
            pl.BlockSpec((tm, d), lambda i: (i, 0)),
            pl.BlockSpec((1, 1, d), lambda i: (i // bpb, 0, 0)),
            pl.BlockSpec((1, d), lambda i: (0, 0)),
            pl.BlockSpec((D_RNN, d), lambda i: (0, 0)),
            pl.BlockSpec((ATTN_W, d), lambda i: (0, 0)),
            pl.BlockSpec((d, d), lambda i: (0, 0)),
        ],
        out_specs=pl.BlockSpec((tm, d), lambda i: (i, 0)),
        out_shape=jax.ShapeDtypeStruct((t, d), F32),
        compiler_params=_params("parallel"),
        name="merge_out",
    )(y_rnn, y_attn, proj, proj, proj, proj, x, g1, ng.reshape(1, d), w_proj_rnn, w_proj_attn, w_out)


def _swiglu_partial(h, wg, wu, wd):
    gate = jnp.dot(h, wg, preferred_element_type=F32)
    up = jnp.dot(h, wu, preferred_element_type=F32)
    act = (gate * jax.nn.sigmoid(gate) * up).astype(BF16)
    return jnp.dot(act, wd, preferred_element_type=F32)


def _ffn_kernel(x_ref, g_ref, sh_ref, sc_ref, g2_ref, ng_ref, wg_ref, wu_ref, wd_ref, o_ref, h_ref, acc_ref):
    f = pl.program_id(1)

    @pl.when(f == 0)
    def _():
        h = _rms(x_ref[...]) * g_ref[...]
        h = h * (1.0 + sc_ref[0]) + sh_ref[0]
        h_ref[...] = h.astype(BF16)
        acc_ref[...] = jnp.zeros_like(acc_ref)

    acc_ref[...] += _swiglu_partial(h_ref[...], wg_ref[...], wu_ref[...], wd_ref[...])

    @pl.when(f == pl.num_programs(1) - 1)
    def _():
        o_ref[...] = x_ref[...] + g2_ref[0] * (_rms(acc_ref[...]) * ng_ref[...])


def ffn_residual(x, gain, shift, scale, g2, ng, w_gu, w_down, *, tm, tf):
    t, d = x.shape
    ff = w_gu.shape[1] // 2
    nf = ff // tf
    nb = shift.shape[0]
    assert t % (tm * nb) == 0 and ff % tf == 0, (t, tm, nb, ff, tf)
    bpb = (t // tm) // nb
    mod_spec = pl.BlockSpec((1, 1, d), lambda i, f: (i // bpb, 0, 0))
    vec_spec = pl.BlockSpec((1, d), lambda i, f: (0, 0))
    return pl.pallas_call(
        _ffn_kernel,
        grid=(t // tm, nf),
        in_specs=[
            pl.BlockSpec((tm, d), lambda i, f: (i, 0)), vec_spec, mod_spec, mod_spec, mod_spec, vec_spec,
            pl.BlockSpec((d, tf), lambda i, f: (0, f)),
            pl.BlockSpec((d, tf), lambda i, f: (0, nf + f)),
            pl.BlockSpec((tf, d), lambda i, f: (f, 0)),
        ],
        out_specs=pl.BlockSpec((tm, d), lambda i, f: (i, 0)),
        out_shape=jax.ShapeDtypeStruct((t, d), F32),
        scratch_shapes=[pltpu.VMEM((tm, d), BF16), pltpu.VMEM((tm, d), F32)],
        compiler_params=_params("parallel", "arbitrary"),
        name="ffn_residual",
    )(x, gain.reshape(1, d), shift, scale, g2, ng.reshape(1, d), w_gu, w_gu, w_down)


MOE_TM = 512
MOE_TF = 256
ROW_TILE = 512
DMA_UNROLL = 8
META_E0, META_E1, META_W0, META_W1, META_R0, META_R1 = range(6)


def _lane_pick(rec, lane, k):
    return jnp.sum(jnp.where(lane == k, rec, 0.0), axis=-1, keepdims=True)


def _router_kernel(x_ref, g_ref, sh_ref, sc_ref, w_ref, b_ref, h_ref, meta_ref, meta_t_ref, cnt_ref, carry_s):
    @pl.when(pl.program_id(0) == 0)
    def _():
        carry_s[...] = jnp.zeros_like(carry_s)

    h = _rms(x_ref[...]) * g_ref[...]
    h = h * (1.0 + sc_ref[0]) + sh_ref[0]
    h_ref[...] = h
    logits = jnp.dot(h, w_ref[...], preferred_element_type=F32, precision=lax.Precision.HIGHEST) + b_ref[...]
    lane = lax.broadcasted_iota(jnp.int32, logits.shape, 1).astype(F32)
    neg = jnp.float32(-jnp.inf)
    logits = jnp.where(lane < N_EXPERTS, logits, neg)
    m1 = jnp.max(logits, axis=-1, keepdims=True)
    i1 = jnp.min(jnp.where(logits == m1, lane, float(LANES)), axis=-1, keepdims=True)
    rest = jnp.where(lane == i1, neg, logits)
    m2 = jnp.max(rest, axis=-1, keepdims=True)
    i2 = jnp.min(jnp.where(rest == m2, lane, float(LANES)), axis=-1, keepdims=True)
    e2 = jnp.exp(m2 - m1)
    denom = 1.0 + e2

    tm = logits.shape[0]
    sel1 = lane == i1
    sel2 = lane == i2
    member = jnp.where(jnp.logical_or(sel1, sel2), 1.0, 0.0)
    row = lax.broadcasted_iota(jnp.int32, (tm, tm), 0)
    col = lax.broadcasted_iota(jnp.int32, (tm, tm), 1)
    lower = jnp.where(col < row, 1.0, 0.0).astype(BF16)
    before = jnp.dot(lower, member.astype(BF16), preferred_element_type=F32) + carry_s[0:1, :]
    r1 = jnp.sum(jnp.where(sel1, before, 0.0), axis=-1, keepdims=True)
    r2 = jnp.sum(jnp.where(sel2, before, 0.0), axis=-1, keepdims=True)
    carry_s[0:1, :] = carry_s[0:1, :] + jnp.sum(member, axis=0, keepdims=True)
    cnt_ref[...] = jnp.broadcast_to(carry_s[0:1, :], cnt_ref.shape)

    meta = jnp.zeros_like(logits)
    for k, v in ((META_E0, i1), (META_E1, i2), (META_W0, 1.0 / denom), (META_W1, e2 / denom),
                 (META_R0, r1), (META_R1, r2)):
        meta = jnp.where(lane == k, v, meta)
    meta_ref[...] = meta
    meta_t_ref[...] = meta.T[0:SUBLANES, :]


def moe_router(x, gain, shift, scale, router_w, router_b, *, tm):
    t, d = x.shape
    nb = shift.shape[0]
    bpb = (t // tm) // nb
    w = jnp.zeros((d, LANES), F32).at[:, :N_EXPERTS].set(router_w)
    b = jnp.zeros((1, LANES), F32).at[0, :N_EXPERTS].set(router_b)
    return pl.pallas_call(
        _router_kernel,
        grid=(t // tm,),
        in_specs=[
            pl.BlockSpec((tm, d), lambda i: (i, 0)),
            pl.BlockSpec((1, d), lambda i: (0, 0)),
            pl.BlockSpec((1, 1, d), lambda i: (i // bpb, 0, 0)),
            pl.BlockSpec((1, 1, d), lambda i: (i // bpb, 0, 0)),
            pl.BlockSpec((d, LANES), lambda i: (0, 0)),
            pl.BlockSpec((1, LANES), lambda i: (0, 0)),
        ],
        out_specs=[
            pl.BlockSpec((tm, d), lambda i: (i, 0)),
            pl.BlockSpec((tm, LANES), lambda i: (i, 0)),
            pl.BlockSpec((SUBLANES, tm), lambda i: (0, i)),
            pl.BlockSpec((SUBLANES, LANES), lambda i: (0, 0)),
        ],
        out_shape=[
            jax.ShapeDtypeStruct((t, d), F32),
            jax.ShapeDtypeStruct((t, LANES), F32),
            jax.ShapeDtypeStruct((SUBLANES, t), F32),
            jax.ShapeDtypeStruct((SUBLANES, LANES), F32),
        ],
        scratch_shapes=[pltpu.VMEM((SUBLANES, LANES), F32)],
        compiler_params=_params("arbitrary"),
        name="moe_router",
    )(x, gain.reshape(1, d), shift, scale, w, b)


def _row_copy(src_ref, src_row, dst_ref, dst_row, sem):
    return pltpu.make_async_copy(src_ref.at[pl.ds(src_row, 1), :], dst_ref.at[pl.ds(dst_row, 1), :], sem)


FILL_START, FILL_COUNT = 0, 1


def _dispatch_kernel(fill_ref, pos_ref, h_ref, wgu_ref, wdn_ref, hs_ref, wgu_out_ref, wdn_out_ref, buf_s, zero_s,
                     sems):
    wgu_out_ref[...] = wgu_ref[...].astype(BF16)
    wdn_out_ref[...] = wdn_ref[...].astype(BF16)

    i = pl.program_id(0)
    n_steps = pl.num_programs(0)
    n = h_ref.shape[0]
    slot = i % 2
    zero_sem = sems.at[2]

    @pl.when(i == 0)
    def _():
        zero_s[...] = jnp.zeros_like(zero_s)

        def pad_copy(e, k):
            return _row_copy(zero_s, 0, hs_ref, fill_ref[FILL_START, e] + k, zero_sem)

        def tail_copy(k):
            row = pl.multiple_of(fill_ref[FILL_START, N_EXPERTS] + k * SUBLANES, SUBLANES)
            return pltpu.make_async_copy(zero_s, hs_ref.at[pl.ds(row, SUBLANES), :], zero_sem)

        def for_each(copy, count, op):
            def body(k, carry):
                getattr(copy(k), op)()
                return carry

            lax.fori_loop(0, count, body, 0)

        for op in ("start", "wait"):
            for e in range(N_EXPERTS):
                for_each(functools.partial(pad_copy, e), fill_ref[FILL_COUNT, e], op)
            for_each(tail_copy, fill_ref[FILL_COUNT, N_EXPERTS], op)

    buf_s[slot] = h_ref[...]
    src = buf_s.at[slot]
    sem = sems.at[slot]

    def issue(g, carry):
        for j in range(DMA_UNROLL):
            r = g * DMA_UNROLL + j
            _row_copy(src, r, hs_ref, pos_ref[0, 0, 2 * r], sem).start(priority=0)
            _row_copy(src, r, hs_ref, pos_ref[0, 0, 2 * r + 1], sem).start(priority=1)
        return carry

    lax.fori_loop(0, n // DMA_UNROLL, issue, 0)

    def drain(s):
        for _ in range(TOP_K):
            pltpu.make_async_copy(buf_s.at[s], hs_ref.at[pl.ds(0, n), :], sems.at[s]).wait()

    @pl.when(i > 0)
    def _():
        drain(1 - slot)

    @pl.when(i == n_steps - 1)
    def _():
        drain(slot)


def moe_dispatch(h, pos, fill, n_rows, w_gu, w_down):
    t, d = h.shape
    n_steps = t // ROW_TILE
    wgu2 = w_gu.reshape(-1, w_gu.shape[-1])
    wdn2 = w_down.reshape(-1, w_down.shape[-1])
    gu_rows, dn_rows = wgu2.shape[0] // n_steps, wdn2.shape[0] // n_steps
    bf16_rows = 2 * SUBLANES
    assert wgu2.shape[0] % n_steps == 0 and wdn2.shape[0] % n_steps == 0, (wgu2.shape, wdn2.shape, n_steps)
    assert gu_rows % bf16_rows == 0 and dn_rows % bf16_rows == 0, (gu_rows, dn_rows)
    hs, wgu_bf, wdn_bf = pl.pallas_call(
        _dispatch_kernel,
        grid_spec=pltpu.PrefetchScalarGridSpec(
            num_scalar_prefetch=1,
            grid=(n_steps,),
            in_specs=[
                pl.BlockSpec((1, 1, 2 * ROW_TILE), lambda i, fill: (i, 0, 0), memory_space=pltpu.SMEM),
                pl.BlockSpec((ROW_TILE, d), lambda i, fill: (i, 0)),
                pl.BlockSpec((gu_rows, wgu2.shape[1]), lambda i, fill: (i, 0)),
                pl.BlockSpec((dn_rows, wdn2.shape[1]), lambda i, fill: (i, 0)),
            ],
            out_specs=[
                pl.BlockSpec(memory_space=pl.ANY),
                pl.BlockSpec((gu_rows, wgu2.shape[1]), lambda i, fill: (i, 0)),
                pl.BlockSpec((dn_rows, wdn2.shape[1]), lambda i, fill: (i, 0)),
            ],
            scratch_shapes=[pltpu.VMEM((2, ROW_TILE, d), F32), pltpu.VMEM((SUBLANES, d), F32),
                            pltpu.SemaphoreType.DMA((3,))],
        ),
        out_shape=[
            jax.ShapeDtypeStruct((n_rows, d), F32),
            jax.ShapeDtypeStruct(wgu2.shape, BF16),
            jax.ShapeDtypeStruct(wdn2.shape, BF16),
        ],
        compiler_params=_params("arbitrary"),
        name="moe_dispatch",
    )(fill, pos, h, wgu2, wdn2)
    return hs, wgu_bf.reshape(w_gu.shape), wdn_bf.reshape(w_down.shape)


def _grouped_ffn_kernel(te_ref, na_ref, hs_ref, wgu_ref, wd_ref, ys_ref):
    del te_ref
    active = pl.program_id(0) < na_ref[0]

    @pl.when(jnp.logical_not(active))
    def _():
        ys_ref[...] = jnp.zeros_like(ys_ref)

    @pl.when(active)
    def _():
        h = hs_ref[...].astype(BF16)
        ff = wd_ref.shape[1]
        acc = None
        for f0 in range(0, ff, MOE_TF):
            y = _swiglu_partial(h, wgu_ref[0, :, f0:f0 + MOE_TF], wgu_ref[0, :, ff + f0:ff + f0 + MOE_TF],
                                wd_ref[0, f0:f0 + MOE_TF, :])
            acc = y if acc is None else acc + y
        ys_ref[...] = acc


def moe_grouped_ffn(hs, tile_expert, n_active, w_gu, w_down):
    n_rows, d = hs.shape
    ff = w_gu.shape[2] // 2
    assert ff % MOE_TF == 0 and MOE_TF % LANES == 0 and n_rows % MOE_TM == 0, (ff, MOE_TF, n_rows)

    def row_map(t, te, na):
        return (jnp.maximum(jnp.minimum(t, na[0] - 1), 0), 0)

    return pl.pallas_call(
        _grouped_ffn_kernel,
        grid_spec=pltpu.PrefetchScalarGridSpec(
            num_scalar_prefetch=2,
            grid=(n_rows // MOE_TM,),
            in_specs=[
                pl.BlockSpec((MOE_TM, d), row_map),
                pl.BlockSpec((1, d, 2 * ff), lambda t, te, na: (te[t], 0, 0)),
                pl.BlockSpec((1, ff, d), lambda t, te, na: (te[t], 0, 0)),
            ],
            out_specs=pl.BlockSpec((MOE_TM, d), lambda t, te, na: (t, 0)),
        ),
        out_shape=jax.ShapeDtypeStruct((n_rows, d), F32),
        compiler_params=_params("arbitrary"),
        name="moe_grouped_ffn",
    )(tile_expert, n_active, hs, w_gu, w_down)


def _combine_kernel(pos_ref, pos_next_ref, ys_ref, x_ref, meta_ref, g2_ref, ng_ref, o_ref, a_s, b_s, sems):
    i = pl.program_id(0)
    n_steps = pl.num_programs(0)
    n = x_ref.shape[0]
    slot = i % 2

    def gather(p_ref, s):
        def issue(g, carry):
            for j in range(DMA_UNROLL):
                r = g * DMA_UNROLL + j
                _row_copy(ys_ref, p_ref[0, 0, 2 * r], a_s.at[s], r, sems.at[s]).start(priority=0)
                _row_copy(ys_ref, p_ref[0, 0, 2 * r + 1], b_s.at[s], r, sems.at[s]).start(priority=1)
            return carry

        lax.fori_loop(0, n // DMA_UNROLL, issue, 0)

    @pl.when(i == 0)
    def _():
        gather(pos_ref, 0)

    @pl.when(i + 1 < n_steps)
    def _():
        gather(pos_next_ref, 1 - slot)

    pltpu.make_async_copy(ys_ref.at[pl.ds(0, n), :], a_s.at[slot], sems.at[slot]).wait()
    pltpu.make_async_copy(ys_ref.at[pl.ds(0, n), :], b_s.at[slot], sems.at[slot]).wait()

    meta = meta_ref[...]
    lane = lax.broadcasted_iota(jnp.int32, meta.shape, 1)
    mixed = _lane_pick(meta, lane, META_W0) * a_s[slot] + _lane_pick(meta, lane, META_W1) * b_s[slot]
    o_ref[...] = x_ref[...] + g2_ref[0] * (_rms(mixed) * ng_ref[...])


def moe_combine(ys, pos, meta, x, g2, ng):
    t, d = x.shape
    nb = g2.shape[0]
    n_steps = t // ROW_TILE
    bpb = n_steps // nb
    return pl.pallas_call(
        _combine_kernel,
        grid=(n_steps,),
        in_specs=[
            pl.BlockSpec((1, 1, 2 * ROW_TILE), lambda i: (i, 0, 0), memory_space=pltpu.SMEM),
            pl.BlockSpec((1, 1, 2 * ROW_TILE), lambda i: (jnp.minimum(i + 1, n_steps - 1), 0, 0),
                         memory_space=pltpu.SMEM),
            pl.BlockSpec(memory_space=pl.ANY),
            pl.BlockSpec((ROW_TILE, d), lambda i: (i, 0)),
            pl.BlockSpec((ROW_TILE, LANES), lambda i: (i, 0)),
            pl.BlockSpec((1, 1, d), lambda i: (i // bpb, 0, 0)),
            pl.BlockSpec((1, d), lambda i: (0, 0)),
        ],
        out_specs=pl.BlockSpec((ROW_TILE, d), lambda i: (i, 0)),
        out_shape=jax.ShapeDtypeStruct((t, d), F32),
        scratch_shapes=[pltpu.VMEM((2, ROW_TILE, d), F32), pltpu.VMEM((2, ROW_TILE, d), F32),
                        pltpu.SemaphoreType.DMA((2,))],
        compiler_params=_params("arbitrary"),
        name="moe_combine",
    )(pos, pos, ys, x, meta, g2, ng.reshape(1, d))


def moe_residual(x, gain, shift, scale, g2, ng, router_w, router_b, w_gu, w_down):
    t, d = x.shape
    h, meta, meta_t, counts = moe_router(x, gain, shift, scale, router_w, router_b, tm=512)
    n_rows = TOP_K * t + N_EXPERTS * MOE_TM
    n_tiles = n_rows // MOE_TM
    cnt = counts[0, :N_EXPERTS].astype(jnp.int32)
    padded = (cnt + MOE_TM - 1) // MOE_TM * MOE_TM
    seg_end = jnp.cumsum(padded)
    seg_start = seg_end - padded
    experts = jnp.arange(N_EXPERTS, dtype=jnp.int32)

    def position(e_field, r_field):
        e = meta_t[e_field].astype(jnp.int32)
        start = jnp.sum(jnp.where(e[None, :] == experts[:, None], seg_start[:, None], 0), axis=0)
        return start + meta_t[r_field].astype(jnp.int32)

    pos = jnp.stack([position(META_E0, META_R0), position(META_E1, META_R1)], axis=-1)
    pos = pos.reshape(t // ROW_TILE, 1, 2 * ROW_TILE)
    n_active = seg_end[-1:] // MOE_TM
    tile_ids = jnp.arange(n_tiles, dtype=jnp.int32)
    tile_expert = jnp.sum((jnp.minimum(tile_ids, n_active - 1)[:, None] * MOE_TM >= seg_end[None, :]), axis=-1)
    tile_expert = jnp.minimum(tile_expert, N_EXPERTS - 1).astype(jnp.int32)

    fill = jnp.stack([
        jnp.concatenate([seg_start + cnt, seg_end[-1:], jnp.zeros((N_EXPERTS - 1,), jnp.int32)]),
        jnp.concatenate([padded - cnt, (n_rows - seg_end[-1:]) // SUBLANES, jnp.zeros((N_EXPERTS - 1,), jnp.int32)]),
    ]).astype(jnp.int32)
    hs, w_gu_bf, w_down_bf = moe_dispatch(h, pos, fill, n_rows, w_gu, w_down)
    ys = moe_grouped_ffn(hs, tile_expert, n_active.astype(jnp.int32), w_gu_bf, w_down_bf)
    return moe_combine(ys, pos, meta, x, g2, ng)


def _rope_tables():
    pos = jnp.arange(SEQ)
    inv_freq = jnp.power(ROPE_THETA, -jnp.arange(ROPE_FREQS, dtype=F32) / ROPE_FREQS)
    ang_r = (pos // GRID_W).astype(F32)[:, None] * inv_freq
    ang_c = (pos % GRID_W).astype(F32)[:, None] * inv_freq
    cos = jnp.concatenate([jnp.cos(ang_r)] * 2 + [jnp.cos(ang_c)] * 2, axis=-1)
    sin = jnp.concatenate([-jnp.sin(ang_r), jnp.sin(ang_r), -jnp.sin(ang_c), jnp.sin(ang_c)], axis=-1)
    return jnp.tile(cos, (1, 2)), jnp.tile(sin, (1, 2))


def _gate_layout(gate_w, gate_b, lam):
    c = RNN_CB
    n_cb = D_RNN // c
    per = c // RNN_BLOCK_W
    gw = gate_w.reshape(2, 2, n_cb, per, RNN_BLOCK_W, RNN_BLOCK_W)
    gw = jnp.transpose(gw, (2, 3, 4, 0, 1, 5))
    half_eye = 0.5 * jnp.eye(per, dtype=gate_w.dtype)
    bd = gw[:, :, :, :, :, None, :] * half_eye[None, :, None, None, None, :, None]
    wg = bd.reshape(n_cb, c, 4 * c).astype(BF16)
    gb = 0.5 * jnp.transpose(gate_b.reshape(2, 2, n_cb, c), (2, 0, 1, 3)).reshape(n_cb, 1, 4 * c)
    lm = jnp.transpose(lam.reshape(2, n_cb, c), (1, 0, 2)).reshape(n_cb, 1, 2 * c)
    return wg, gb, lm


def kernel(x, c, ctx, c_ctx, ada_w, ada_b, norm_g, w_in, conv_w, conv_b, lru_gate_w, lru_gate_b, lru_lambda,
           diff_lambda, subln_g, w_proj_rnn, w_proj_attn, w_out, ffn_w_gu, ffn_w_down, router_w, router_b,
           moe_w_gu, moe_w_down):
    xt = x.reshape(BATCH * SEQ, D_MODEL)
    ct = ctx.reshape(BATCH * CTX_LEN, D_MODEL)
    cos, sin_signed = _rope_tables()

    cvec = jnp.concatenate([c, c_ctx[None, :], jnp.zeros((2 * SUBLANES - BATCH - 1, D_MODEL), F32)], axis=0)
    mod = ada_modulation(cvec, ada_w, ada_b)

    for l in range(DEPTH):
        last = l == DEPTH - 1
        lam_init = 0.8 - 0.6 * math.exp(-0.3 * l)
        mx = mod[l, :BATCH].reshape(BATCH, 1, 6, D_MODEL)
        mc = mod[l, BATCH:BATCH + 1].reshape(1, 1, 6, D_MODEL)
        sh1x, sc1x, g1x, sh2x, sc2x, g2x = (mx[:, :, i] for i in range(6))
        sh1c, sc1c, g1c, sh2c, sc2c, g2c = (mc[:, :, i] for i in range(6))

        wg, gb, lm = _gate_layout(lru_gate_w[l], lru_gate_b[l], lru_lambda[l])

        proj_x = norm_mod_matmul(xt, norm_g[l, 0], sh1x, sc1x, w_in, l, IN_W, tm=2048, tn=768)
        if last:
            proj_c = norm_mod_matmul(ct, norm_g[l, 0], sh1c, sc1c, w_in, l, CTX_STATE_W, tm=2048, tn=256)
        else:
            proj_c = norm_mod_matmul(ct, norm_g[l, 0], sh1c, sc1c, w_in, l, IN_W, tm=2048, tn=768)

        y_rnn_x, y_rnn_c = rglru(proj_x, proj_c, conv_w[l], conv_b[l], wg, gb, lm, xr_col=COL_XR, gr_col=COL_GR,
                                 c_xr_col=COL_XR, c_gr_col=COL_GR, ctx_out=not last)
        y_attn_x = diff_attention(proj_x, COL_Q, SEQ, proj_x, proj_c, (COL_K, COL_V, COL_K, COL_V), cos,
                                  sin_signed, diff_lambda[l], subln_g[l], lam_init=lam_init, tq=1024)
        wr = w_proj_rnn[l].astype(BF16)
        wa = w_proj_attn[l].astype(BF16)
        wo = w_out[l].astype(BF16)
        xt = merge_out(y_rnn_x, y_attn_x, proj_x, xt, g1x, norm_g[l, 1], wr, wa, wo, tm=512)
        if not last:
            y_attn_c = diff_attention(proj_c, COL_Q, CTX_LEN, None, proj_c, (0, 0, COL_K, COL_V), None, None,
                                      diff_lambda[l], subln_g[l], lam_init=lam_init, tq=CTX_LEN)
            ct = merge_out(y_rnn_c, y_attn_c, proj_c, ct, g1c, norm_g[l, 1], wr, wa, wo, tm=512)

        if l % 2 == 0:
            w_gu = ffn_w_gu[l // 2].astype(BF16)
            w_dn = ffn_w_down[l // 2].astype(BF16)
            xt = ffn_residual(xt, norm_g[l, 2], sh2x, sc2x, g2x, norm_g[l, 3], w_gu, w_dn, tm=512, tf=1408)
            if not last:
                ct = ffn_residual(ct, norm_g[l, 2], sh2c, sc2c, g2c, norm_g[l, 3], w_gu, w_dn, tm=512, tf=1408)
        else:
            w_gu, w_dn = moe_w_gu[l // 2], moe_w_down[l // 2]
            rw, rb = router_w[l // 2], router_b[l // 2]
            xt = moe_residual(xt, norm_g[l, 2], sh2x, sc2x, g2x, norm_g[l, 3], rw, rb, w_gu, w_dn)
            if not last:
                ct = moe_residual(ct, norm_g[l, 2], sh2c, sc2c, g2c, norm_g[l, 3], rw, rb, w_gu, w_dn)
    return xt.reshape(BATCH, SEQ, D_MODEL)
```

```python
import functools
import math

import jax
import jax.numpy as jnp
from jax import lax
from jax.experimental import pallas as pl
from jax.experimental.pallas import tpu as pltpu

F32 = jnp.float32
BF16 = jnp.bfloat16

D_MODEL = 1024
BATCH = 8
SEQ = 2048
DEPTH = 2
CTX_LEN = 256
GRID_W = 64
EPS = 1e-6
D_RNN = 1280
RNN_BLOCKS = 20
RNN_BLOCK_W = D_RNN // RNN_BLOCKS
CONV_W = 4
LRU_C = 8.0
N_HEADS = 8
HEAD_DIM = 64
V_DIM = 2 * HEAD_DIM
QK_W = N_HEADS * 2 * HEAD_DIM
ATTN_W = N_HEADS * V_DIM
ROPE_THETA = 10000.0
ROPE_FREQS = HEAD_DIM // 4
D_FF = 2816
N_EXPERTS = 8
TOP_K = 2

LANES = 128
SUBLANES = 8
VMEM_LIMIT_BYTES = 52 * 1024 * 1024

COL_XR = 0
COL_K = COL_XR + D_RNN
COL_V = COL_K + QK_W
COL_GR = COL_V + ATTN_W
COL_Q = COL_GR + D_RNN
COL_G_RNN = COL_Q + QK_W
COL_G_ATTN = COL_G_RNN + D_MODEL
IN_W = COL_G_ATTN + D_MODEL
CTX_STATE_W = COL_GR
GATE_BLOCK_W = 512

RNN_CB = 256
SCAN_CHUNK = 256
ATTN_SUB = 128


def _params(*sem):
    return pltpu.CompilerParams(dimension_semantics=sem, vmem_limit_bytes=VMEM_LIMIT_BYTES)


def _rms(x):
    return x * lax.rsqrt(jnp.mean(x * x, axis=-1, keepdims=True) + EPS)


def _ada_kernel(c_ref, w_ref, b_ref, o_ref):
    c = c_ref[...]
    s = c * jax.nn.sigmoid(c)
    o_ref[0] = jnp.dot(s, w_ref[0], preferred_element_type=F32, precision=lax.Precision.HIGHEST) + b_ref[0]


def ada_modulation(cvec, ada_w, ada_b):
    rows = cvec.shape[0]
    tn = 1536
    n = 6 * D_MODEL
    return pl.pallas_call(
        _ada_kernel,
        grid=(DEPTH, n // tn),
        in_specs=[
            pl.BlockSpec((rows, D_MODEL), lambda l, j: (0, 0)),
            pl.BlockSpec((1, D_MODEL, tn), lambda l, j: (l, 0, j)),
            pl.BlockSpec((1, 1, tn), lambda l, j: (l, 0, j)),
        ],
        out_specs=pl.BlockSpec((1, rows, tn), lambda l, j: (l, 0, j)),
        out_shape=jax.ShapeDtypeStruct((DEPTH, rows, n), F32),
        compiler_params=_params("parallel", "parallel"),
        name="ada_modulation",
    )(cvec, ada_w, ada_b.reshape(DEPTH, 1, n))


def _nmm_kernel(x_ref, g_ref, sh_ref, sc_ref, w_ref, o_ref, *, tn):
    h = _rms(x_ref[...]) * g_ref[...]
    h = (h * (1.0 + sc_ref[0]) + sh_ref[0]).astype(BF16)
    for c0 in range(0, o_ref.shape[1], tn):
        o_ref[:, c0:c0 + tn] = jnp.dot(h, w_ref[0, :, c0:c0 + tn], preferred_element_type=F32).astype(o_ref.dtype)


def norm_mod_matmul(x, gain, shift, scale, w, layer, n, *, tm, tn):
    t, d = x.shape
    nb = shift.shape[0]
    assert t % (tm * nb) == 0 and n % tn == 0 and tn % LANES == 0, (t, tm, nb, n, tn)
    bpb = (t // tm) // nb
    return pl.pallas_call(
        functools.partial(_nmm_kernel, tn=tn),
        grid=(t // tm,),
        in_specs=[
            pl.BlockSpec((tm, d), lambda i: (i, 0)),
            pl.BlockSpec((1, d), lambda i: (0, 0)),
            pl.BlockSpec((1, 1, d), lambda i: (i // bpb, 0, 0)),
            pl.BlockSpec((1, 1, d), lambda i: (i // bpb, 0, 0)),
            pl.BlockSpec((1, d, n), lambda i: (layer, 0, 0)),
        ],
        out_specs=pl.BlockSpec((tm, n), lambda i: (i, 0)),
        out_shape=jax.ShapeDtypeStruct((t, n), BF16),
        compiler_params=_params("parallel"),
        name="norm_mod_matmul",
    )(x, gain.reshape(1, d), shift, scale, w)


def _group_scan(a, b, reverse):
    rows, c = a.shape
    a = a.reshape(rows // SUBLANES, SUBLANES, c)
    b = b.reshape(rows // SUBLANES, SUBLANES, c)
    sub = lax.broadcasted_iota(jnp.int32, a.shape, 1)
    for sh in (1, 2, 4):
        if reverse:
            keep = sub < SUBLANES - sh
            amount = SUBLANES - sh
        else:
            keep = sub >= sh
            amount = sh
        a_sh = jnp.where(keep, pltpu.roll(a, amount, 1), 1.0)
        b_sh = jnp.where(keep, pltpu.roll(b, amount, 1), 0.0)
        b = a * b_sh + b
        a = a * a_sh
    return a.reshape(rows, c), b.reshape(rows, c)


def _dwconv(x, cw, cb):
    n = x.shape[0]
    rows = lax.broadcasted_iota(jnp.int32, x.shape, 0)
    xm2 = jnp.where(rows >= 2, pltpu.roll(x, 2, 0), 0.0)
    xm1 = jnp.where(rows >= 1, pltpu.roll(x, 1, 0), 0.0)
    xp1 = jnp.where(rows < n - 1, pltpu.roll(x, n - 1, 0), 0.0)
    return cb + xm2 * cw[0:1] + xm1 * cw[1:2] + x * cw[2:3] + xp1 * cw[3:4]


def _gelu_tanh(x):
    return 0.5 * x * (1.0 + jnp.tanh(math.sqrt(2.0 / math.pi) * (x + 0.044715 * (x * x * x))))


def _rglru_kernel(*refs, n_c, n_x, ctx_out):
    if ctx_out:
        (xrx_ref, grx_ref, xrc_ref, grc_ref, cw_ref, cb_ref, wg_ref, gb_ref, lam_ref,
         yx_ref, yc_ref, xc_s, a0_s, b0_s, a1_s, b1_s) = refs
    else:
        (xrx_ref, grx_ref, xrc_ref, cw_ref, cb_ref, wg_ref, gb_ref, lam_ref,
         yx_ref, xc_s, a0_s, b0_s, a1_s, b1_s) = refs
    c = RNN_CB
    n_all = n_c + n_x
    cw = cw_ref[...]
    cb = cb_ref[...]

    xc_s[0:n_c, :] = _dwconv(xrc_ref[...].astype(F32), cw, cb)
    xc_s[n_c:n_all, :] = _dwconv(xrx_ref[...].astype(F32), cw, cb)

    lam = lam_ref[0]
    half_neg_sp = (-0.5 * LRU_C) * jax.nn.softplus(-lam)

    def coeff_chunk(ci, carry):
        r0 = pl.multiple_of(ci * SCAN_CHUNK, SCAN_CHUNK)
        xc = xc_s[pl.ds(r0, SCAN_CHUNK), :]
        t = jnp.tanh(jnp.dot(xc.astype(BF16), wg_ref[0], preferred_element_type=F32) + gb_ref[0])
        half_xc = 0.5 * xc
        for d, (a_s, b_s) in enumerate(((a0_s, b0_s), (a1_s, b1_s))):
            t_r = t[:, (2 * d) * c:(2 * d + 1) * c]
            t_i = t[:, (2 * d + 1) * c:(2 * d + 2) * c]
            half_sp = half_neg_sp[:, d * c:(d + 1) * c]
            a = jnp.exp(half_sp + half_sp * t_r)
            one_m_a2 = 1.0 - a * a
            mult = jnp.where(one_m_a2 > 0.0, one_m_a2 * lax.rsqrt(one_m_a2), 0.0)
            bb = mult * (half_xc + half_xc * t_i)
            a_cum, b_cum = _group_scan(a, bb, reverse=(d == 1))
            a_s[pl.ds(r0, SCAN_CHUNK), :] = a_cum
            b_s[pl.ds(r0, SCAN_CHUNK), :] = b_cum
        return carry

    lax.fori_loop(0, n_all // SCAN_CHUNK, coeff_chunk, 0)

    g_c = n_c // SUBLANES
    g_all = n_all // SUBLANES

    def group_step(a_s, b_s, row0, last, h_in):
        a = a_s[pl.ds(row0, SUBLANES), :]
        b = b_s[pl.ds(row0, SUBLANES), :]
        a_s[pl.ds(row0, SUBLANES), :] = a * h_in + b
        a_tot = jnp.broadcast_to(a[last:last + 1, :], a.shape)
        b_tot = jnp.broadcast_to(b[last:last + 1, :], b.shape)
        return a_tot * h_in + b_tot

    hf = hr = jnp.zeros((SUBLANES, c), F32)
    for k in range(g_all):
        hf = group_step(a0_s, b0_s, k * SUBLANES, SUBLANES - 1, hf)
        kr = g_c - 1 - k if k < g_c else g_all + g_c - 1 - k
        hr = group_step(a1_s, b1_s, kr * SUBLANES, 0, hr)

    yx = (a0_s[n_c:n_all, :] + a1_s[n_c:n_all, :]) * _gelu_tanh(grx_ref[...].astype(F32))
    yx_ref[...] = yx.astype(yx_ref.dtype)
    if ctx_out:
        yc = (a0_s[0:n_c, :] + a1_s[0:n_c, :]) * _gelu_tanh(grc_ref[...].astype(F32))
        yc_ref[...] = yc.astype(yc_ref.dtype)


def rglru(proj_x, proj_c, conv_w, conv_b, wg, gb, lam, *, xr_col, gr_col, c_xr_col, c_gr_col, ctx_out):
    c = RNN_CB
    n_cb = D_RNN // c
    xr_b, gr_b, cxr_b = xr_col // c, gr_col // c, c_xr_col // c
    in_specs = [
        pl.BlockSpec((SEQ, c), lambda b, j: (b, xr_b + j)),
        pl.BlockSpec((SEQ, c), lambda b, j: (b, gr_b + j)),
        pl.BlockSpec((CTX_LEN, c), lambda b, j: (b, cxr_b + j)),
    ]
    args = [proj_x, proj_x, proj_c]
    if ctx_out:
        cgr_b = c_gr_col // c
        in_specs.append(pl.BlockSpec((CTX_LEN, c), lambda b, j: (b, cgr_b + j)))
        args.append(proj_c)
    in_specs += [
        pl.BlockSpec((CONV_W, c), lambda b, j: (0, j)),
        pl.BlockSpec((1, c), lambda b, j: (0, j)),
        pl.BlockSpec((1, c, 4 * c), lambda b, j: (j, 0, 0)),
        pl.BlockSpec((1, 1, 4 * c), lambda b, j: (j, 0, 0)),
        pl.BlockSpec((1, 1, 2 * c), lambda b, j: (j, 0, 0)),
    ]
    args += [conv_w, conv_b.reshape(1, D_RNN), wg, gb, lam]
    out_specs = [pl.BlockSpec((SEQ, c), lambda b, j: (b, j))]
    out_shape = [jax.ShapeDtypeStruct((BATCH * SEQ, D_RNN), BF16)]
    if ctx_out:
        out_specs.append(pl.BlockSpec((CTX_LEN, c), lambda b, j: (b, j)))
        out_shape.append(jax.ShapeDtypeStruct((BATCH * CTX_LEN, D_RNN), BF16))
    n_all = SEQ + CTX_LEN
    out = pl.pallas_call(
        functools.partial(_rglru_kernel, n_c=CTX_LEN, n_x=SEQ, ctx_out=ctx_out),
        grid=(BATCH, n_cb),
        in_specs=in_specs,
        out_specs=out_specs,
        out_shape=out_shape,
        scratch_shapes=[pltpu.VMEM((n_all, c), F32) for _ in range(5)],
        compiler_params=_params("parallel", "parallel"),
        name="rglru",
    )(*args)
    return out if ctx_out else (out[0], None)


def _rope(t, cos, sin_signed):
    lane = lax.broadcasted_iota(jnp.int32, t.shape, 1)
    first_half = (lane % (2 * ROPE_FREQS)) < ROPE_FREQS
    partner = jnp.where(first_half, pltpu.roll(t, LANES - ROPE_FREQS, 1), pltpu.roll(t, ROPE_FREQS, 1))
    return t * cos + partner * sin_signed


def _attn_kernel(*refs, tq, n_c, n_x, lam_init):
    if n_x:
        (q_ref, kx_ref, vx_ref, kc_ref, vc_ref, cos_ref, sin_ref, dl_ref, sg_ref,
         o_ref, k_s, v_s, lam_s) = refs
    else:
        q_ref, kc_ref, vc_ref, dl_ref, sg_ref, o_ref, k_s, v_s, lam_s = refs
    qi = pl.program_id(2)

    @pl.when(qi == 0)
    def _():
        k_s[0:n_c, :] = kc_ref[...]
        v_s[0:n_c, 0:LANES] = vc_ref[...]
        if n_x:
            kx = _rope(kx_ref[...].astype(F32), cos_ref[...], sin_ref[...])
            k_s[n_c:n_c + n_x, :] = kx.astype(BF16)
            v_s[n_c:n_c + n_x, 0:LANES] = vx_ref[...]
        v_s[:, LANES:2 * LANES] = jnp.ones((n_c + n_x, LANES), BF16)
        dl = dl_ref[...]
        e1 = jnp.exp(jnp.sum(dl[0:1] * dl[1:2], axis=-1, keepdims=True))
        e2 = jnp.exp(jnp.sum(dl[2:3] * dl[3:4], axis=-1, keepdims=True))
        lam_s[...] = jnp.broadcast_to(e1 - e2 + lam_init, lam_s.shape)

    q = q_ref[...].astype(F32)
    if n_x:
        r0 = pl.multiple_of(qi * tq, tq)
        q = _rope(q, cos_ref[pl.ds(r0, tq), :], sin_ref[pl.ds(r0, tq), :])
    q = q * (HEAD_DIM ** -0.5 * math.log2(math.e))
    lane = lax.broadcasted_iota(jnp.int32, (ATTN_SUB, LANES), 1)
    lam = lam_s[0:1, 0:1]
    scores = []
    for sb in range(tq // ATTN_SUB):
        qs = q[sb * ATTN_SUB:(sb + 1) * ATTN_SUB]
        qq = jnp.concatenate([jnp.where(lane < HEAD_DIM, qs, 0.0), jnp.where(lane >= HEAD_DIM, qs, 0.0)], axis=0)
        scores.append(lax.dot_general(qq.astype(BF16), k_s[...], (((1,), (1,)), ((), ())),
                                      preferred_element_type=F32))
    for sb, s in enumerate(scores):
        e = jnp.exp2(s - jnp.max(s, axis=-1, keepdims=True))
        ov = jnp.dot(e.astype(BF16), v_s[...], preferred_element_type=F32)
        ov = ov[:, 0:LANES] / ov[:, LANES:2 * LANES]
        o = ov[0:ATTN_SUB] - lam * ov[ATTN_SUB:2 * ATTN_SUB]
        o_ref[sb * ATTN_SUB:(sb + 1) * ATTN_SUB, :] = (_rms(o) * sg_ref[...] * (1.0 - lam_init)).astype(o_ref.dtype)


def diff_attention(q_arr, q_col, n_q, kvx, kvc, cols, cos, sin_signed, diff_lambda, subln_g, *, lam_init, tq):
    kx_col, vx_col, kc_col, vc_col = cols
    qb, kxb, vxb, kcb, vcb = (v // LANES for v in (q_col, kx_col, vx_col, kc_col, vc_col))
    n_x = SEQ if kvx is not None else 0
    n_qb = n_q // tq
    in_specs = [pl.BlockSpec((tq, LANES), lambda b, h, i: (b * n_qb + i, qb + h))]
    args = [q_arr]
    if n_x:
        in_specs += [
            pl.BlockSpec((SEQ, LANES), lambda b, h, i: (b, kxb + h)),
            pl.BlockSpec((SEQ, LANES), lambda b, h, i: (b, vxb + h)),
        ]
        args += [kvx, kvx]
    in_specs += [
        pl.BlockSpec((CTX_LEN, LANES), lambda b, h, i: (b, kcb + h)),
        pl.BlockSpec((CTX_LEN, LANES), lambda b, h, i: (b, vcb + h)),
    ]
    args += [kvc, kvc]
    if n_x:
        in_specs += [
            pl.BlockSpec((SEQ, LANES), lambda b, h, i: (0, 0)),
            pl.BlockSpec((SEQ, LANES), lambda b, h, i: (0, 0)),
        ]
        args += [cos, sin_signed]
    in_specs += [
        pl.BlockSpec((4, HEAD_DIM), lambda b, h, i: (0, 0)),
        pl.BlockSpec((1, V_DIM), lambda b, h, i: (0, 0)),
    ]
    args += [diff_lambda, subln_g.reshape(1, V_DIM)]
    n_kv = CTX_LEN + n_x
    return pl.pallas_call(
        functools.partial(_attn_kernel, tq=tq, n_c=CTX_LEN, n_x=n_x, lam_init=lam_init),
        grid=(BATCH, N_HEADS, n_qb),
        in_specs=in_specs,
        out_specs=pl.BlockSpec((tq, LANES), lambda b, h, i: (b * n_qb + i, h)),
        out_shape=jax.ShapeDtypeStruct((BATCH * n_q, ATTN_W), BF16),
        scratch_shapes=[
            pltpu.VMEM((n_kv, LANES), BF16),
            pltpu.VMEM((n_kv, 2 * LANES), BF16),
            pltpu.VMEM((SUBLANES, LANES), F32),
        ],
        compiler_params=_params("parallel", "parallel", "arbitrary"),
        name="diff_attention",
    )(*args)


def _merge_kernel(yr_ref, ya_ref, gr0_ref, gr1_ref, ga0_ref, ga1_ref, x_ref, g1_ref, ng_ref, wr_ref, wa_ref, wo_ref,
                  o_ref):
    y_rnn = jnp.dot(yr_ref[...], wr_ref[...], preferred_element_type=F32)
    y_attn = jnp.dot(ya_ref[...], wa_ref[...], preferred_element_type=F32)
    g_rnn = jax.nn.sigmoid(jnp.concatenate([gr0_ref[...], gr1_ref[...]], axis=1).astype(F32))
    g_attn = jax.nn.sigmoid(jnp.concatenate([ga0_ref[...], ga1_ref[...]], axis=1).astype(F32))
    m = (g_rnn * y_rnn + g_attn * y_attn).astype(BF16)
    mx = jnp.dot(m, wo_ref[...], preferred_element_type=F32)
    o_ref[...] = x_ref[...] + g1_ref[0] * (_rms(mx) * ng_ref[...])


def merge_out(y_rnn, y_attn, proj, x, g1, ng, w_proj_rnn, w_proj_attn, w_out, *, tm):
    t, d = x.shape
    nb = g1.shape[0]
    assert t % (tm * nb) == 0, (t, tm, nb)
    bpb = (t // tm) // nb
    gw = GATE_BLOCK_W
    grb, gab = COL_G_RNN // gw, COL_G_ATTN // gw
    return pl.pallas_call(
        _merge_kernel,
        grid=(t // tm,),
        in_specs=[
            pl.BlockSpec((tm, D_RNN), lambda i: (i, 0)),
            pl.BlockSpec((tm, ATTN_W), lambda i: (i, 0)),
            pl.BlockSpec((tm, gw), lambda i: (i, grb)),
            pl.BlockSpec((tm, gw), lambda i: (i, grb + 1)),
            pl.BlockSpec((tm, gw), lambda i: (i, gab)),
            pl.BlockSpec((tm, gw), lambda i: (i, gab + 1)),
            pl.BlockSpec((tm, d), lambda i: (i, 0)),
            pl.BlockSpec((1, 1, d), lambda i: (i // bpb, 0, 0)),
            pl.BlockSpec((1, d), lambda i: (0, 0)),
            pl.BlockSpec((D_RNN, d), lambda i: (0, 0)),
            pl.BlockSpec((ATTN_W, d), lambda i: (0, 0)),
            pl.BlockSpec((d, d), lambda i: (0, 0)),
        ],
        out_specs=pl.BlockSpec((tm, d), lambda i: (i, 0)),
        out_shape=jax.ShapeDtypeStruct((t, d), F32),
        compiler_params=_params("parallel"),
        name="merge_out",
    )(y_rnn, y_attn, proj, proj, proj, proj, x, g1, ng.reshape(1, d), w_proj_rnn, w_proj_attn, w_out)


def _swiglu_partial(h, wg, wu, wd):
    gate = jnp.dot(h, wg, preferred_element_type=F32)
    up = jnp.dot(h, wu, preferred_element_type=F32)
    act = (gate * jax.nn.sigmoid(gate) * up).astype(BF16)
    return jnp.dot(act, wd, preferred_element_type=F32)


def _swiglu_chunked(h, wgu, wd, tf):
    ff = wd.shape[0]
    acc = None
    for f0 in range(0, ff, tf):
        y = _swiglu_partial(h, wgu[:, f0:f0 + tf], wgu[:, ff + f0:ff + f0 + tf], wd[f0:f0 + tf, :])
        acc = y if acc is None else acc + y
    return acc


def _ffn_kernel(x_ref, g_ref, sh_ref, sc_ref, g2_ref, ng_ref, wgu_ref, wd_ref, o_ref, *, tf):
    h = _rms(x_ref[...]) * g_ref[...]
    h = (h * (1.0 + sc_ref[0]) + sh_ref[0]).astype(BF16)
    y = _swiglu_chunked(h, wgu_ref, wd_ref, tf)
    o_ref[...] = x_ref[...] + g2_ref[0] * (_rms(y) * ng_ref[...])


def ffn_residual(x, gain, shift, scale, g2, ng, w_gu, w_down, *, tm, tf):
    t, d = x.shape
    ff = w_gu.shape[1] // 2
    nb = shift.shape[0]
    assert t % (tm * nb) == 0 and ff % tf == 0 and tf % LANES == 0, (t, tm, nb, ff, tf)
    bpb = (t // tm) // nb
    mod_spec = pl.BlockSpec((1, 1, d), lambda i: (i // bpb, 0, 0))
    vec_spec = pl.BlockSpec((1, d), lambda i: (0, 0))
    return pl.pallas_call(
        functools.partial(_ffn_kernel, tf=tf),
        grid=(t // tm,),
        in_specs=[
            pl.BlockSpec((tm, d), lambda i: (i, 0)), vec_spec, mod_spec, mod_spec, mod_spec, vec_spec,
            pl.BlockSpec((d, 2 * ff), lambda i: (0, 0)),
            pl.BlockSpec((ff, d), lambda i: (0, 0)),
        ],
        out_specs=pl.BlockSpec((tm, d), lambda i: (i, 0)),
        out_shape=jax.ShapeDtypeStruct((t, d), F32),
        compiler_params=_params("parallel"),
        name="ffn_residual",
    )(x, gain.reshape(1, d), shift, scale, g2, ng.reshape(1, d), w_gu, w_down)


MOE_TM = 512
MOE_TF = 256
ROW_TILE = 512
DMA_UNROLL = 8
META_E0, META_E1, META_W0, META_W1, META_R0, META_R1 = range(6)


def _lane_pick(rec, lane, k):
    return jnp.sum(jnp.where(lane == k, rec, 0.0), axis=-1, keepdims=True)


def _router_kernel(x_ref, g_ref, sh_ref, sc_ref, w_ref, b_ref, h_ref, meta_ref, meta_t_ref, cnt_ref, carry_s):
    @pl.when(pl.program_id(0) == 0)
    def _():
        carry_s[...] = jnp.zeros_like(carry_s)

    h = _rms(x_ref[...]) * g_ref[...]
    h = h * (1.0 + sc_ref[0]) + sh_ref[0]
    h_ref[...] = h
    logits = jnp.dot(h, w_ref[...], preferred_element_type=F32, precision=lax.Precision.HIGHEST) + b_ref[...]
    lane = lax.broadcasted_iota(jnp.int32, logits.shape, 1).astype(F32)
    neg = jnp.float32(-jnp.inf)
    logits = jnp.where(lane < N_EXPERTS, logits, neg)
    m1 = jnp.max(logits, axis=-1, keepdims=True)
    i1 = jnp.min(jnp.where(logits == m1, lane, float(LANES)), axis=-1, keepdims=True)
    rest = jnp.where(lane == i1, neg, logits)
    m2 = jnp.max(rest, axis=-1, keepdims=True)
    i2 = jnp.min(jnp.where(rest == m2, lane, float(LANES)), axis=-1, keepdims=True)
    e2 = jnp.exp(m2 - m1)
    denom = 1.0 + e2

    tm = logits.shape[0]
    sel1 = lane == i1
    sel2 = lane == i2
    member = jnp.where(jnp.logical_or(sel1, sel2), 1.0, 0.0)
    row = lax.broadcasted_iota(jnp.int32, (tm, tm), 0)
    col = lax.broadcasted_iota(jnp.int32, (tm, tm), 1)
    lower = jnp.where(col < row, 1.0, 0.0).astype(BF16)
    before = jnp.dot(lower, member.astype(BF16), preferred_element_type=F32) + carry_s[0:1, :]
    r1 = jnp.sum(jnp.where(sel1, before, 0.0), axis=-1, keepdims=True)
    r2 = jnp.sum(jnp.where(sel2, before, 0.0), axis=-1, keepdims=True)
    carry_s[0:1, :] = carry_s[0:1, :] + jnp.sum(member, axis=0, keepdims=True)
    cnt_ref[...] = jnp.broadcast_to(carry_s[0:1, :], cnt_ref.shape)

    meta = jnp.zeros_like(logits)
    for k, v in ((META_E0, i1), (META_E1, i2), (META_W0, 1.0 / denom), (META_W1, e2 / denom),
                 (META_R0, r1), (META_R1, r2)):
        meta = jnp.where(lane == k, v, meta)
    meta_ref[...] = meta
    meta_t_ref[...] = meta.T[0:SUBLANES, :]


def moe_router(x, gain, shift, scale, router_w, router_b, *, tm):
    t, d = x.shape
    nb = shift.shape[0]
    bpb = (t // tm) // nb
    w = jnp.zeros((d, LANES), F32).at[:, :N_EXPERTS].set(router_w)
    b = jnp.zeros((1, LANES), F32).at[0, :N_EXPERTS].set(router_b)
    return pl.pallas_call(
        _router_kernel,
        grid=(t // tm,),
        in_specs=[
            pl.BlockSpec((tm, d), lambda i: (i, 0)),
            pl.BlockSpec((1, d), lambda i: (0, 0)),
            pl.BlockSpec((1, 1, d), lambda i: (i // bpb, 0, 0)),
            pl.BlockSpec((1, 1, d), lambda i: (i // bpb, 0, 0)),
            pl.BlockSpec((d, LANES), lambda i: (0, 0)),
            pl.BlockSpec((1, LANES), lambda i: (0, 0)),
        ],
        out_specs=[
            pl.BlockSpec((tm, d), lambda i: (i, 0)),
            pl.BlockSpec((tm, LANES), lambda i: (i, 0)),
            pl.BlockSpec((SUBLANES, tm), lambda i: (0, i)),
            pl.BlockSpec((SUBLANES, LANES), lambda i: (0, 0)),
        ],
        out_shape=[
            jax.ShapeDtypeStruct((t, d), F32),
            jax.ShapeDtypeStruct((t, LANES), F32),
            jax.ShapeDtypeStruct((SUBLANES, t), F32),
            jax.ShapeDtypeStruct((SUBLANES, LANES), F32),
        ],
        scratch_shapes=[pltpu.VMEM((SUBLANES, LANES), F32)],
        compiler_params=_params("arbitrary"),
        name="moe_router",
    )(x, gain.reshape(1, d), shift, scale, w, b)


def _row_copy(src_ref, src_row, dst_ref, dst_row, sem):
    return pltpu.make_async_copy(src_ref.at[pl.ds(src_row, 1), :], dst_ref.at[pl.ds(dst_row, 1), :], sem)


FILL_START, FILL_COUNT = 0, 1


def _dispatch_kernel(fill_ref, pos_ref, h_ref, wgu_ref, wdn_ref, hs_ref, wgu_out_ref, wdn_out_ref, buf_s, zero_s,
                     sems):
    wgu_out_ref[...] = wgu_ref[...].astype(BF16)
    wdn_out_ref[...] = wdn_ref[...].astype(BF16)

    i = pl.program_id(0)
    n_steps = pl.num_programs(0)
    n = h_ref.shape[0]
    slot = i % 2
    zero_sem = sems.at[2]

    @pl.when(i == 0)
    def _():
        zero_s[...] = jnp.zeros_like(zero_s)

        def pad_copy(e, k):
            return _row_copy(zero_s, 0, hs_ref, fill_ref[FILL_START, e] + k, zero_sem)

        def tail_copy(k):
            row = pl.multiple_of(fill_ref[FILL_START, N_EXPERTS] + k * SUBLANES, SUBLANES)
            return pltpu.make_async_copy(zero_s, hs_ref.at[pl.ds(row, SUBLANES), :], zero_sem)

        def for_each(copy, count, op):
            def body(k, carry):
                getattr(copy(k), op)()
                return carry

            lax.fori_loop(0, count, body, 0)

        for op in ("start", "wait"):
            for e in range(N_EXPERTS):
                for_each(functools.partial(pad_copy, e), fill_ref[FILL_COUNT, e], op)
            for_each(tail_copy, fill_ref[FILL_COUNT, N_EXPERTS], op)

    buf_s[slot] = h_ref[...]
    src = buf_s.at[slot]
    sem = sems.at[slot]

    def issue(g, carry):
        for j in range(DMA_UNROLL):
            r = g * DMA_UNROLL + j
            _row_copy(src, r, hs_ref, pos_ref[0, 0, 2 * r], sem).start(priority=0)
            _row_copy(src, r, hs_ref, pos_ref[0, 0, 2 * r + 1], sem).start(priority=1)
        return carry

    lax.fori_loop(0, n // DMA_UNROLL, issue, 0)

    def drain(s):
        for _ in range(TOP_K):
            pltpu.make_async_copy(buf_s.at[s], hs_ref.at[pl.ds(0, n), :], sems.at[s]).wait()

    @pl.when(i > 0)
    def _():
        drain(1 - slot)

    @pl.when(i == n_steps - 1)
    def _():
        drain(slot)


def moe_dispatch(h, pos, fill, n_rows, w_gu, w_down):
    t, d = h.shape
    n_steps = t // ROW_TILE
    wgu2 = w_gu.reshape(-1, w_gu.shape[-1])
    wdn2 = w_down.reshape(-1, w_down.shape[-1])
    gu_rows, dn_rows = wgu2.shape[0] // n_steps, wdn2.shape[0] // n_steps
    bf16_rows = 2 * SUBLANES
    assert wgu2.shape[0] % n_steps == 0 and wdn2.shape[0] % n_steps == 0, (wgu2.shape, wdn2.shape, n_steps)
    assert gu_rows % bf16_rows == 0 and dn_rows % bf16_rows == 0, (gu_rows, dn_rows)
    hs, wgu_bf, wdn_bf = pl.pallas_call(
        _dispatch_kernel,
        grid_spec=pltpu.PrefetchScalarGridSpec(
            num_scalar_prefetch=1,
            grid=(n_steps,),
            in_specs=[
                pl.BlockSpec((1, 1, 2 * ROW_TILE), lambda i, fill: (i, 0, 0), memory_space=pltpu.SMEM),
                pl.BlockSpec((ROW_TILE, d), lambda i, fill: (i, 0)),
                pl.BlockSpec((gu_rows, wgu2.shape[1]), lambda i, fill: (i, 0)),
                pl.BlockSpec((dn_rows, wdn2.shape[1]), lambda i, fill: (i, 0)),
            ],
            out_specs=[
                pl.BlockSpec(memory_space=pl.ANY),
                pl.BlockSpec((gu_rows, wgu2.shape[1]), lambda i, fill: (i, 0)),
                pl.BlockSpec((dn_rows, wdn2.shape[1]), lambda i, fill: (i, 0)),
            ],
            scratch_shapes=[pltpu.VMEM((2, ROW_TILE, d), F32), pltpu.VMEM((SUBLANES, d), F32),
                            pltpu.SemaphoreType.DMA((3,))],
        ),
        out_shape=[
            jax.ShapeDtypeStruct((n_rows, d), F32),
            jax.ShapeDtypeStruct(wgu2.shape, BF16),
            jax.ShapeDtypeStruct(wdn2.shape, BF16),
        ],
        compiler_params=_params("arbitrary"),
        name="moe_dispatch",
    )(fill, pos, h, wgu2, wdn2)
    return hs, wgu_bf.reshape(w_gu.shape), wdn_bf.reshape(w_down.shape)


def _grouped_ffn_kernel(te_ref, na_ref, hs_ref, wgu_ref, wd_ref, ys_ref):
    del te_ref
    active = pl.program_id(0) < na_ref[0]

    @pl.when(jnp.logical_not(active))
    def _():
        ys_ref[...] = jnp.zeros_like(ys_ref)

    @pl.when(active)
    def _():
        ys_ref[...] = _swiglu_chunked(hs_ref[...].astype(BF16), wgu_ref.at[0], wd_ref.at[0], MOE_TF)


def moe_grouped_ffn(hs, tile_expert, n_active, w_gu, w_down):
    n_rows, d = hs.shape
    ff = w_gu.shape[2] // 2
    assert ff % MOE_TF == 0 and MOE_TF % LANES == 0 and n_rows % MOE_TM == 0, (ff, MOE_TF, n_rows)

    def row_map(t, te, na):
        return (jnp.maximum(jnp.minimum(t, na[0] - 1), 0), 0)

    return pl.pallas_call(
        _grouped_ffn_kernel,
        grid_spec=pltpu.PrefetchScalarGridSpec(
            num_scalar_prefetch=2,
            grid=(n_rows // MOE_TM,),
            in_specs=[
                pl.BlockSpec((MOE_TM, d), row_map),
                pl.BlockSpec((1, d, 2 * ff), lambda t, te, na: (te[t], 0, 0)),
                pl.BlockSpec((1, ff, d), lambda t, te, na: (te[t], 0, 0)),
            ],
            out_specs=pl.BlockSpec((MOE_TM, d), lambda t, te, na: (t, 0)),
        ),
        out_shape=jax.ShapeDtypeStruct((n_rows, d), F32),
        compiler_params=_params("arbitrary"),
        name="moe_grouped_ffn",
    )(tile_expert, n_active, hs, w_gu, w_down)


def _combine_kernel(pos_ref, pos_next_ref, ys_ref, x_ref, meta_ref, g2_ref, ng_ref, o_ref, a_s, b_s, sems):
    i = pl.program_id(0)
    n_steps = pl.num_programs(0)
    n = x_ref.shape[0]
    slot = i % 2

    def gather(p_ref, s):
        def issue(g, carry):
            for j in range(DMA_UNROLL):
                r = g * DMA_UNROLL + j
                _row_copy(ys_ref, p_ref[0, 0, 2 * r], a_s.at[s], r, sems.at[s]).start(priority=0)
                _row_copy(ys_ref, p_ref[0, 0, 2 * r + 1], b_s.at[s], r, sems.at[s]).start(priority=1)
            return carry

        lax.fori_loop(0, n // DMA_UNROLL, issue, 0)

    @pl.when(i == 0)
    def _():
        gather(pos_ref, 0)

    @pl.when(i + 1 < n_steps)
    def _():
        gather(pos_next_ref, 1 - slot)

    pltpu.make_async_copy(ys_ref.at[pl.ds(0, n), :], a_s.at[slot], sems.at[slot]).wait()
    pltpu.make_async_copy(ys_ref.at[pl.ds(0, n), :], b_s.at[slot], sems.at[slot]).wait()

    meta = meta_ref[...]
    lane = lax.broadcasted_iota(jnp.int32, meta.shape, 1)
    mixed = _lane_pick(meta, lane, META_W0) * a_s[slot] + _lane_pick(meta, lane, META_W1) * b_s[slot]
    o_ref[...] = x_ref[...] + g2_ref[0] * (_rms(mixed) * ng_ref[...])


def moe_combine(ys, pos, meta, x, g2, ng):
    t, d = x.shape
    nb = g2.shape[0]
    n_steps = t // ROW_TILE
    bpb = n_steps // nb
    return pl.pallas_call(
        _combine_kernel,
        grid=(n_steps,),
        in_specs=[
            pl.BlockSpec((1, 1, 2 * ROW_TILE), lambda i: (i, 0, 0), memory_space=pltpu.SMEM),
            pl.BlockSpec((1, 1, 2 * ROW_TILE), lambda i: (jnp.minimum(i + 1, n_steps - 1), 0, 0),
                         memory_space=pltpu.SMEM),
            pl.BlockSpec(memory_space=pl.ANY),
            pl.BlockSpec((ROW_TILE, d), lambda i: (i, 0)),
            pl.BlockSpec((ROW_TILE, LANES), lambda i: (i, 0)),
            pl.BlockSpec((1, 1, d), lambda i: (i // bpb, 0, 0)),
            pl.BlockSpec((1, d), lambda i: (0, 0)),
        ],
        out_specs=pl.BlockSpec((ROW_TILE, d), lambda i: (i, 0)),
        out_shape=jax.ShapeDtypeStruct((t, d), F32),
        scratch_shapes=[pltpu.VMEM((2, ROW_TILE, d), F32), pltpu.VMEM((2, ROW_TILE, d), F32),
                        pltpu.SemaphoreType.DMA((2,))],
        compiler_params=_params("arbitrary"),
        name="moe_combine",
    )(pos, pos, ys, x, meta, g2, ng.reshape(1, d))


def moe_residual(x, gain, shift, scale, g2, ng, router_w, router_b, w_gu, w_down):
    t, d = x.shape
    h, meta, meta_t, counts = moe_router(x, gain, shift, scale, router_w, router_b, tm=512)
    n_rows = TOP_K * t + N_EXPERTS * MOE_TM
    n_tiles = n_rows // MOE_TM
    cnt = counts[0, :N_EXPERTS].astype(jnp.int32)
    padded = (cnt + MOE_TM - 1) // MOE_TM * MOE_TM
    seg_end = jnp.cumsum(padded)
    seg_start = seg_end - padded
    experts = jnp.arange(N_EXPERTS, dtype=jnp.int32)

    def position(e_field, r_field):
        e = meta_t[e_field].astype(jnp.int32)
        start = jnp.sum(jnp.where(e[None, :] == experts[:, None], seg_start[:, None], 0), axis=0)
        return start + meta_t[r_field].astype(jnp.int32)

    pos = jnp.stack([position(META_E0, META_R0), position(META_E1, META_R1)], axis=-1)
    pos = pos.reshape(t // ROW_TILE, 1, 2 * ROW_TILE)
    n_active = seg_end[-1:] // MOE_TM
    tile_ids = jnp.arange(n_tiles, dtype=jnp.int32)
    tile_expert = jnp.sum((jnp.minimum(tile_ids, n_active - 1)[:, None] * MOE_TM >= seg_end[None, :]), axis=-1)
    tile_expert = jnp.minimum(tile_expert, N_EXPERTS - 1).astype(jnp.int32)

    fill = jnp.stack([
        jnp.concatenate([seg_start + cnt, seg_end[-1:], jnp.zeros((N_EXPERTS - 1,), jnp.int32)]),
        jnp.concatenate([padded - cnt, (n_rows - seg_end[-1:]) // SUBLANES, jnp.zeros((N_EXPERTS - 1,), jnp.int32)]),
    ]).astype(jnp.int32)
    hs, w_gu_bf, w_down_bf = moe_dispatch(h, pos, fill, n_rows, w_gu, w_down)
    ys = moe_grouped_ffn(hs, tile_expert, n_active.astype(jnp.int32), w_gu_bf, w_down_bf)
    return moe_combine(ys, pos, meta, x, g2, ng)


def _rope_tables():
    pos = jnp.arange(SEQ)
    inv_freq = jnp.power(ROPE_THETA, -jnp.arange(ROPE_FREQS, dtype=F32) / ROPE_FREQS)
    ang_r = (pos // GRID_W).astype(F32)[:, None] * inv_freq
    ang_c = (pos % GRID_W).astype(F32)[:, None] * inv_freq
    cos = jnp.concatenate([jnp.cos(ang_r)] * 2 + [jnp.cos(ang_c)] * 2, axis=-1)
    sin = jnp.concatenate([-jnp.sin(ang_r), jnp.sin(ang_r), -jnp.sin(ang_c), jnp.sin(ang_c)], axis=-1)
    return jnp.tile(cos, (1, 2)), jnp.tile(sin, (1, 2))


def _gate_layout(gate_w, gate_b, lam):
    c = RNN_CB
    n_cb = D_RNN // c
    per = c // RNN_BLOCK_W
    gw = gate_w.reshape(2, 2, n_cb, per, RNN_BLOCK_W, RNN_BLOCK_W)
    gw = jnp.transpose(gw, (2, 3, 4, 0, 1, 5))
    half_eye = 0.5 * jnp.eye(per, dtype=gate_w.dtype)
    bd = gw[:, :, :, :, :, None, :] * half_eye[None, :, None, None, None, :, None]
    wg = bd.reshape(n_cb, c, 4 * c).astype(BF16)
    gb = 0.5 * jnp.transpose(gate_b.reshape(2, 2, n_cb, c), (2, 0, 1, 3)).reshape(n_cb, 1, 4 * c)
    lm = jnp.transpose(lam.reshape(2, n_cb, c), (1, 0, 2)).reshape(n_cb, 1, 2 * c)
    return wg, gb, lm


def kernel(x, c, ctx, c_ctx, ada_w, ada_b, norm_g, w_in, conv_w, conv_b, lru_gate_w, lru_gate_b, lru_lambda,
           diff_lambda, subln_g, w_proj_rnn, w_proj_attn, w_out, ffn_w_gu, ffn_w_down, router_w, router_b,
           moe_w_gu, moe_w_down):
    xt = x.reshape(BATCH * SEQ, D_MODEL)
    ct = ctx.reshape(BATCH * CTX_LEN, D_MODEL)
    cos, sin_signed = _rope_tables()
    w_in_bf = w_in.astype(BF16)

    cvec = jnp.concatenate([c, c_ctx[None, :], jnp.zeros((2 * SUBLANES - BATCH - 1, D_MODEL), F32)], axis=0)
    mod = ada_modulation(cvec, ada_w, ada_b)

    for l in range(DEPTH):
        last = l == DEPTH - 1
        lam_init = 0.8 - 0.6 * math.exp(-0.3 * l)
        mx = mod[l, :BATCH].reshape(BATCH, 1, 6, D_MODEL)
        mc = mod[l, BATCH:BATCH + 1].reshape(1, 1, 6, D_MODEL)
        sh1x, sc1x, g1x, sh2x, sc2x, g2x = (mx[:, :, i] for i in range(6))
        sh1c, sc1c, g1c, sh2c, sc2c, g2c = (mc[:, :, i] for i in range(6))

        wg, gb, lm = _gate_layout(lru_gate_w[l], lru_gate_b[l], lru_lambda[l])

        proj_x = norm_mod_matmul(xt, norm_g[l, 0], sh1x, sc1x, w_in_bf, l, IN_W, tm=512, tn=256)
        proj_c = norm_mod_matmul(ct, norm_g[l, 0], sh1c, sc1c, w_in_bf, l, CTX_STATE_W if last else IN_W, tm=512,
                                 tn=256)

        y_rnn_x, y_rnn_c = rglru(proj_x, proj_c, conv_w[l], conv_b[l], wg, gb, lm, xr_col=COL_XR, gr_col=COL_GR,
                                 c_xr_col=COL_XR, c_gr_col=COL_GR, ctx_out=not last)
        y_attn_x = diff_attention(proj_x, COL_Q, SEQ, proj_x, proj_c, (COL_K, COL_V, COL_K, COL_V), cos,
                                  sin_signed, diff_lambda[l], subln_g[l], lam_init=lam_init, tq=1024)
        wr = w_proj_rnn[l].astype(BF16)
        wa = w_proj_attn[l].astype(BF16)
        wo = w_out[l].astype(BF16)
        xt = merge_out(y_rnn_x, y_attn_x, proj_x, xt, g1x, norm_g[l, 1], wr, wa, wo, tm=512)
        if not last:
            y_attn_c = diff_attention(proj_c, COL_Q, CTX_LEN, None, proj_c, (0, 0, COL_K, COL_V), None, None,
                                      diff_lambda[l], subln_g[l], lam_init=lam_init, tq=CTX_LEN)
            ct = merge_out(y_rnn_c, y_attn_c, proj_c, ct, g1c, norm_g[l, 1], wr, wa, wo, tm=512)

        if l % 2 == 0:
            w_gu = ffn_w_gu[l // 2].astype(BF16)
            w_dn = ffn_w_down[l // 2].astype(BF16)
            xt = ffn_residual(xt, norm_g[l, 2], sh2x, sc2x, g2x, norm_g[l, 3], w_gu, w_dn, tm=512, tf=256)
            if not last:
                ct = ffn_residual(ct, norm_g[l, 2], sh2c, sc2c, g2c, norm_g[l, 3], w_gu, w_dn, tm=512, tf=256)
        else:
            w_gu, w_dn = moe_w_gu[l // 2], moe_w_down[l // 2]
            rw, rb = router_w[l // 2], router_b[l // 2]
            xt = moe_residual(xt, norm_g[l, 2], sh2x, sc2x, g2x, norm_g[l, 3], rw, rb, w_gu, w_dn)
            if not last:
                ct = moe_residual(ct, norm_g[l, 2], sh2c, sc2c, g2c, norm_g[l, 3], rw, rb, w_gu, w_dn)
    return xt.reshape(BATCH, SEQ, D_MODEL)
```

```python
import functools
import math

import jax
import jax.numpy as jnp
from jax import lax
from jax.experimental import pallas as pl
from jax.experimental.pallas import tpu as pltpu

F32 = jnp.float32
BF16 = jnp.bfloat16

D_MODEL = 1024
BATCH = 8
SEQ = 2048
DEPTH = 2
CTX_LEN = 256
GRID_W = 64
EPS = 1e-6
D_RNN = 1280
RNN_BLOCKS = 20
RNN_BLOCK_W = D_RNN // RNN_BLOCKS
CONV_W = 4
LRU_C = 8.0
N_HEADS = 8
HEAD_DIM = 64
V_DIM = 2 * HEAD_DIM
QK_W = N_HEADS * 2 * HEAD_DIM
ATTN_W = N_HEADS * V_DIM
ROPE_THETA = 10000.0
ROPE_FREQS = HEAD_DIM // 4
D_FF = 2816
N_EXPERTS = 8
TOP_K = 2

LANES = 128
SUBLANES = 8
VMEM_LIMIT_BYTES = 52 * 1024 * 1024

COL_XR = 0
COL_K = COL_XR + D_RNN
COL_V = COL_K + QK_W
COL_GR = COL_V + ATTN_W
COL_Q = COL_GR + D_RNN
COL_G_RNN = COL_Q + QK_W
COL_G_ATTN = COL_G_RNN + D_MODEL
IN_W = COL_G_ATTN + D_MODEL
CTX_STATE_W = COL_GR
GATE_BLOCK_W = 512

RNN_CB = 256
SCAN_CHUNK = 256
ATTN_SUB = 128
SUB_ROWS = 256


def _params(*sem):
    return pltpu.CompilerParams(dimension_semantics=sem, vmem_limit_bytes=VMEM_LIMIT_BYTES)


def _rms(x):
    return x * lax.rsqrt(jnp.mean(x * x, axis=-1, keepdims=True) + EPS)


def _ada_kernel(c_ref, w_ref, b_ref, o_ref):
    c = c_ref[...]
    s = c * jax.nn.sigmoid(c)
    o_ref[0] = jnp.dot(s, w_ref[0], preferred_element_type=F32, precision=lax.Precision.HIGHEST) + b_ref[0]


def ada_modulation(cvec, ada_w, ada_b):
    rows = cvec.shape[0]
    tn = 1536
    n = 6 * D_MODEL
    return pl.pallas_call(
        _ada_kernel,
        grid=(DEPTH, n // tn),
        in_specs=[
            pl.BlockSpec((rows, D_MODEL), lambda l, j: (0, 0)),
            pl.BlockSpec((1, D_MODEL, tn), lambda l, j: (l, 0, j)),
            pl.BlockSpec((1, 1, tn), lambda l, j: (l, 0, j)),
        ],
        out_specs=pl.BlockSpec((1, rows, tn), lambda l, j: (l, 0, j)),
        out_shape=jax.ShapeDtypeStruct((DEPTH, rows, n), F32),
        compiler_params=_params("parallel", "parallel"),
        name="ada_modulation",
    )(cvec, ada_w, ada_b.reshape(DEPTH, 1, n))


def _nmm_kernel(x_ref, g_ref, sh_ref, sc_ref, w_ref, o_ref, *, tn):
    h = _rms(x_ref[...]) * g_ref[...]
    h = (h * (1.0 + sc_ref[0]) + sh_ref[0]).astype(BF16)
    for c0 in range(0, o_ref.shape[1], tn):
        o_ref[:, c0:c0 + tn] = jnp.dot(h, w_ref[0, :, c0:c0 + tn], preferred_element_type=F32).astype(o_ref.dtype)


def norm_mod_matmul(x, gain, shift, scale, w, layer, n, *, tm, tn):
    t, d = x.shape
    nb = shift.shape[0]
    assert t % (tm * nb) == 0 and n % tn == 0 and tn % LANES == 0, (t, tm, nb, n, tn)
    bpb = (t // tm) // nb
    return pl.pallas_call(
        functools.partial(_nmm_kernel, tn=tn),
        grid=(t // tm,),
        in_specs=[
            pl.BlockSpec((tm, d), lambda i: (i, 0)),
            pl.BlockSpec((1, d), lambda i: (0, 0)),
            pl.BlockSpec((1, 1, d), lambda i: (i // bpb, 0, 0)),
            pl.BlockSpec((1, 1, d), lambda i: (i // bpb, 0, 0)),
            pl.BlockSpec((1, d, n), lambda i: (layer, 0, 0)),
        ],
        out_specs=pl.BlockSpec((tm, n), lambda i: (i, 0)),
        out_shape=jax.ShapeDtypeStruct((t, n), BF16),
        compiler_params=_params("parallel"),
        name="norm_mod_matmul",
    )(x, gain.reshape(1, d), shift, scale, w)


def _group_scan(a, b, reverse):
    rows, c = a.shape
    a = a.reshape(rows // SUBLANES, SUBLANES, c)
    b = b.reshape(rows // SUBLANES, SUBLANES, c)
    sub = lax.broadcasted_iota(jnp.int32, a.shape, 1)
    for sh in (1, 2, 4):
        if reverse:
            keep = sub < SUBLANES - sh
            amount = SUBLANES - sh
        else:
            keep = sub >= sh
            amount = sh
        a_sh = jnp.where(keep, pltpu.roll(a, amount, 1), 1.0)
        b_sh = jnp.where(keep, pltpu.roll(b, amount, 1), 0.0)
        b = a * b_sh + b
        a = a * a_sh
    return a.reshape(rows, c), b.reshape(rows, c)


def _dwconv(x, cw, cb):
    n = x.shape[0]
    rows = lax.broadcasted_iota(jnp.int32, x.shape, 0)
    xm2 = jnp.where(rows >= 2, pltpu.roll(x, 2, 0), 0.0)
    xm1 = jnp.where(rows >= 1, pltpu.roll(x, 1, 0), 0.0)
    xp1 = jnp.where(rows < n - 1, pltpu.roll(x, n - 1, 0), 0.0)
    return cb + xm2 * cw[0:1] + xm1 * cw[1:2] + x * cw[2:3] + xp1 * cw[3:4]


def _gelu_tanh(x):
    return 0.5 * x * (1.0 + jnp.tanh(math.sqrt(2.0 / math.pi) * (x + 0.044715 * (x * x * x))))


def _rglru_kernel(*refs, n_c, n_x, ctx_out):
    if ctx_out:
        (xrx_ref, grx_ref, xrc_ref, grc_ref, cw_ref, cb_ref, wg_ref, gb_ref, lam_ref,
         yx_ref, yc_ref, xc_s, a0_s, b0_s, a1_s, b1_s) = refs
    else:
        (xrx_ref, grx_ref, xrc_ref, cw_ref, cb_ref, wg_ref, gb_ref, lam_ref,
         yx_ref, xc_s, a0_s, b0_s, a1_s, b1_s) = refs
    c = RNN_CB
    n_all = n_c + n_x
    cw = cw_ref[...]
    cb = cb_ref[...]

    xc_s[0:n_c, :] = _dwconv(xrc_ref[...].astype(F32), cw, cb)
    xc_s[n_c:n_all, :] = _dwconv(xrx_ref[...].astype(F32), cw, cb)

    lam = lam_ref[0]
    half_neg_sp = (-0.5 * LRU_C) * jax.nn.softplus(-lam)

    def coeff_chunk(ci, carry):
        r0 = pl.multiple_of(ci * SCAN_CHUNK, SCAN_CHUNK)
        xc = xc_s[pl.ds(r0, SCAN_CHUNK), :]
        t = jnp.tanh(jnp.dot(xc.astype(BF16), wg_ref[0], preferred_element_type=F32) + gb_ref[0])
        half_xc = 0.5 * xc
        for d, (a_s, b_s) in enumerate(((a0_s, b0_s), (a1_s, b1_s))):
            t_r = t[:, (2 * d) * c:(2 * d + 1) * c]
            t_i = t[:, (2 * d + 1) * c:(2 * d + 2) * c]
            half_sp = half_neg_sp[:, d * c:(d + 1) * c]
            a = jnp.exp(half_sp + half_sp * t_r)
            one_m_a2 = 1.0 - a * a
            mult = jnp.where(one_m_a2 > 0.0, one_m_a2 * lax.rsqrt(one_m_a2), 0.0)
            bb = mult * (half_xc + half_xc * t_i)
            a_cum, b_cum = _group_scan(a, bb, reverse=(d == 1))
            a_s[pl.ds(r0, SCAN_CHUNK), :] = a_cum
            b_s[pl.ds(r0, SCAN_CHUNK), :] = b_cum
        return carry

    lax.fori_loop(0, n_all // SCAN_CHUNK, coeff_chunk, 0)

    g_c = n_c // SUBLANES
    g_all = n_all // SUBLANES

    def group_step(a_s, b_s, row0, last, h_in):
        a = a_s[pl.ds(row0, SUBLANES), :]
        b = b_s[pl.ds(row0, SUBLANES), :]
        a_s[pl.ds(row0, SUBLANES), :] = a * h_in + b
        a_tot = jnp.broadcast_to(a[last:last + 1, :], a.shape)
        b_tot = jnp.broadcast_to(b[last:last + 1, :], b.shape)
        return a_tot * h_in + b_tot

    hf = hr = jnp.zeros((SUBLANES, c), F32)
    for k in range(g_all):
        hf = group_step(a0_s, b0_s, k * SUBLANES, SUBLANES - 1, hf)
        kr = g_c - 1 - k if k < g_c else g_all + g_c - 1 - k
        hr = group_step(a1_s, b1_s, kr * SUBLANES, 0, hr)

    yx = (a0_s[n_c:n_all, :] + a1_s[n_c:n_all, :]) * _gelu_tanh(grx_ref[...].astype(F32))
    yx_ref[...] = yx.astype(yx_ref.dtype)
    if ctx_out:
        yc = (a0_s[0:n_c, :] + a1_s[0:n_c, :]) * _gelu_tanh(grc_ref[...].astype(F32))
        yc_ref[...] = yc.astype(yc_ref.dtype)


def rglru(proj_x, proj_c, conv_w, conv_b, wg, gb, lam, *, xr_col, gr_col, c_xr_col, c_gr_col, ctx_out):
    c = RNN_CB
    n_cb = D_RNN // c
    xr_b, gr_b, cxr_b = xr_col // c, gr_col // c, c_xr_col // c
    in_specs = [
        pl.BlockSpec((SEQ, c), lambda b, j: (b, xr_b + j)),
        pl.BlockSpec((SEQ, c), lambda b, j: (b, gr_b + j)),
        pl.BlockSpec((CTX_LEN, c), lambda b, j: (b, cxr_b + j)),
    ]
    args = [proj_x, proj_x, proj_c]
    if ctx_out:
        cgr_b = c_gr_col // c
        in_specs.append(pl.BlockSpec((CTX_LEN, c), lambda b, j: (b, cgr_b + j)))
        args.append(proj_c)
    in_specs += [
        pl.BlockSpec((CONV_W, c), lambda b, j: (0, j)),
        pl.BlockSpec((1, c), lambda b, j: (0, j)),
        pl.BlockSpec((1, c, 4 * c), lambda b, j: (j, 0, 0)),
        pl.BlockSpec((1, 1, 4 * c), lambda b, j: (j, 0, 0)),
        pl.BlockSpec((1, 1, 2 * c), lambda b, j: (j, 0, 0)),
    ]
    args += [conv_w, conv_b.reshape(1, D_RNN), wg, gb, lam]
    out_specs = [pl.BlockSpec((SEQ, c), lambda b, j: (b, j))]
    out_shape = [jax.ShapeDtypeStruct((BATCH * SEQ, D_RNN), BF16)]
    if ctx_out:
        out_specs.append(pl.BlockSpec((CTX_LEN, c), lambda b, j: (b, j)))
        out_shape.append(jax.ShapeDtypeStruct((BATCH * CTX_LEN, D_RNN), BF16))
    n_all = SEQ + CTX_LEN
    out = pl.pallas_call(
        functools.partial(_rglru_kernel, n_c=CTX_LEN, n_x=SEQ, ctx_out=ctx_out),
        grid=(BATCH, n_cb),
        in_specs=in_specs,
        out_specs=out_specs,
        out_shape=out_shape,
        scratch_shapes=[pltpu.VMEM((n_all, c), F32) for _ in range(5)],
        compiler_params=_params("parallel", "parallel"),
        name="rglru",
    )(*args)
    return out if ctx_out else (out[0], None)


def _rope(t, cos, sin_signed):
    lane = lax.broadcasted_iota(jnp.int32, t.shape, 1)
    first_half = (lane % (2 * ROPE_FREQS)) < ROPE_FREQS
    partner = jnp.where(first_half, pltpu.roll(t, LANES - ROPE_FREQS, 1), pltpu.roll(t, ROPE_FREQS, 1))
    return t * cos + partner * sin_signed


def _attn_kernel(*refs, tq, n_c, n_x, lam_init):
    if n_x:
        (q_ref, kx_ref, vx_ref, kc_ref, vc_ref, cos_ref, sin_ref, dl_ref, sg_ref,
         o_ref, k_s, v_s, lam_s) = refs
    else:
        q_ref, kc_ref, vc_ref, dl_ref, sg_ref, o_ref, k_s, v_s, lam_s = refs
    qi = pl.program_id(2)

    @pl.when(qi == 0)
    def _():
        k_s[0:n_c, :] = kc_ref[...]
        v_s[0:n_c, 0:LANES] = vc_ref[...]
        if n_x:
            kx = _rope(kx_ref[...].astype(F32), cos_ref[...], sin_ref[...])
            k_s[n_c:n_c + n_x, :] = kx.astype(BF16)
            v_s[n_c:n_c + n_x, 0:LANES] = vx_ref[...]
        v_s[:, LANES:2 * LANES] = jnp.ones((n_c + n_x, LANES), BF16)
        dl = dl_ref[...]
        e1 = jnp.exp(jnp.sum(dl[0:1] * dl[1:2], axis=-1, keepdims=True))
        e2 = jnp.exp(jnp.sum(dl[2:3] * dl[3:4], axis=-1, keepdims=True))
        lam_s[...] = jnp.broadcast_to(e1 - e2 + lam_init, lam_s.shape)

    q = q_ref[...].astype(F32)
    if n_x:
        r0 = pl.multiple_of(qi * tq, tq)
        q = _rope(q, cos_ref[pl.ds(r0, tq), :], sin_ref[pl.ds(r0, tq), :])
    q = q * (HEAD_DIM ** -0.5 * math.log2(math.e))
    lane = lax.broadcasted_iota(jnp.int32, (ATTN_SUB, LANES), 1)
    lam = lam_s[0:1, 0:1]
    scores = []
    for sb in range(tq // ATTN_SUB):
        qs = q[sb * ATTN_SUB:(sb + 1) * ATTN_SUB]
        qq = jnp.concatenate([jnp.where(lane < HEAD_DIM, qs, 0.0), jnp.where(lane >= HEAD_DIM, qs, 0.0)], axis=0)
        scores.append(lax.dot_general(qq.astype(BF16), k_s[...], (((1,), (1,)), ((), ())),
                                      preferred_element_type=F32))
    for sb, s in enumerate(scores):
        e = jnp.exp2(s - jnp.max(s, axis=-1, keepdims=True))
        ov = jnp.dot(e.astype(BF16), v_s[...], preferred_element_type=F32)
        ov = ov[:, 0:LANES] / ov[:, LANES:2 * LANES]
        o = ov[0:ATTN_SUB] - lam * ov[ATTN_SUB:2 * ATTN_SUB]
        o_ref[sb * ATTN_SUB:(sb + 1) * ATTN_SUB, :] = (_rms(o) * sg_ref[...] * (1.0 - lam_init)).astype(o_ref.dtype)


def diff_attention(q_arr, q_col, n_q, kvx, kvc, cols, cos, sin_signed, diff_lambda, subln_g, *, lam_init, tq):
    kx_col, vx_col, kc_col, vc_col = cols
    qb, kxb, vxb, kcb, vcb = (v // LANES for v in (q_col, kx_col, vx_col, kc_col, vc_col))
    n_x = SEQ if kvx is not None else 0
    n_qb = n_q // tq
    in_specs = [pl.BlockSpec((tq, LANES), lambda b, h, i: (b * n_qb + i, qb + h))]
    args = [q_arr]
    if n_x:
        in_specs += [
            pl.BlockSpec((SEQ, LANES), lambda b, h, i: (b, kxb + h)),
            pl.BlockSpec((SEQ, LANES), lambda b, h, i: (b, vxb + h)),
        ]
        args += [kvx, kvx]
    in_specs += [
        pl.BlockSpec((CTX_LEN, LANES), lambda b, h, i: (b, kcb + h)),
        pl.BlockSpec((CTX_LEN, LANES), lambda b, h, i: (b, vcb + h)),
    ]
    args += [kvc, kvc]
    if n_x:
        in_specs += [
            pl.BlockSpec((SEQ, LANES), lambda b, h, i: (0, 0)),
            pl.BlockSpec((SEQ, LANES), lambda b, h, i: (0, 0)),
        ]
        args += [cos, sin_signed]
    in_specs += [
        pl.BlockSpec((4, HEAD_DIM), lambda b, h, i: (0, 0)),
        pl.BlockSpec((1, V_DIM), lambda b, h, i: (0, 0)),
    ]
    args += [diff_lambda, subln_g.reshape(1, V_DIM)]
    n_kv = CTX_LEN + n_x
    return pl.pallas_call(
        functools.partial(_attn_kernel, tq=tq, n_c=CTX_LEN, n_x=n_x, lam_init=lam_init),
        grid=(BATCH, N_HEADS, n_qb),
        in_specs=in_specs,
        out_specs=pl.BlockSpec((tq, LANES), lambda b, h, i: (b * n_qb + i, h)),
        out_shape=jax.ShapeDtypeStruct((BATCH * n_q, ATTN_W), BF16),
        scratch_shapes=[
            pltpu.VMEM((n_kv, LANES), BF16),
            pltpu.VMEM((n_kv, 2 * LANES), BF16),
            pltpu.VMEM((SUBLANES, LANES), F32),
        ],
        compiler_params=_params("parallel", "parallel", "arbitrary"),
        name="diff_attention",
    )(*args)


def _merge_kernel(yr_ref, ya_ref, gr0_ref, gr1_ref, ga0_ref, ga1_ref, x_ref, g1_ref, ng_ref, wr_ref, wa_ref, wo_ref,
                  o_ref):
    for r0 in range(0, x_ref.shape[0], SUB_ROWS):
        rows = slice(r0, r0 + SUB_ROWS)
        y_rnn = jnp.dot(yr_ref[rows, :], wr_ref[...], preferred_element_type=F32)
        y_attn = jnp.dot(ya_ref[rows, :], wa_ref[...], preferred_element_type=F32)
        g_rnn = jax.nn.sigmoid(jnp.concatenate([gr0_ref[rows, :], gr1_ref[rows, :]], axis=1).astype(F32))
        g_attn = jax.nn.sigmoid(jnp.concatenate([ga0_ref[rows, :], ga1_ref[rows, :]], axis=1).astype(F32))
        m = (g_rnn * y_rnn + g_attn * y_attn).astype(BF16)
        mx = jnp.dot(m, wo_ref[...], preferred_element_type=F32)
        o_ref[rows, :] = x_ref[rows, :] + g1_ref[0] * (_rms(mx) * ng_ref[...])


def merge_out(y_rnn, y_attn, proj, x, g1, ng, w_proj_rnn, w_proj_attn, w_out, *, tm):
    t, d = x.shape
    nb = g1.shape[0]
    assert t % (tm * nb) == 0, (t, tm, nb)
    bpb = (t // tm) // nb
    gw = GATE_BLOCK_W
    grb, gab = COL_G_RNN // gw, COL_G_ATTN // gw
    return pl.pallas_call(
        _merge_kernel,
        grid=(t // tm,),
        in_specs=[
            pl.BlockSpec((tm, D_RNN), lambda i: (i, 0)),
            pl.BlockSpec((tm, ATTN_W), lambda i: (i, 0)),
            pl.BlockSpec((tm, gw), lambda i: (i, grb)),
            pl.BlockSpec((tm, gw), lambda i: (i, grb + 1)),
            pl.BlockSpec((tm, gw), lambda i: (i, gab)),
            pl.BlockSpec((tm, gw), lambda i: (i, gab + 1)),
            pl.BlockSpec((tm, d), lambda i: (i, 0)),
            pl.BlockSpec((1, 1, d), lambda i: (i // bpb, 0, 0)),
            pl.BlockSpec((1, d), lambda i: (0, 0)),
            pl.BlockSpec((D_RNN, d), lambda i: (0, 0)),
            pl.BlockSpec((ATTN_W, d), lambda i: (0, 0)),
            pl.BlockSpec((d, d), lambda i: (0, 0)),
        ],
        out_specs=pl.BlockSpec((tm, d), lambda i: (i, 0)),
        out_shape=jax.ShapeDtypeStruct((t, d), F32),
        compiler_params=_params("parallel"),
        name="merge_out",
    )(y_rnn, y_attn, proj, proj, proj, proj, x, g1, ng.reshape(1, d), w_proj_rnn, w_proj_attn, w_out)


def _swiglu_partial(h, wg, wu, wd):
    gate = jnp.dot(h, wg, preferred_element_type=F32)
    up = jnp.dot(h, wu, preferred_element_type=F32)
    act = (gate * jax.nn.sigmoid(gate) * up).astype(BF16)
    return jnp.dot(act, wd, preferred_element_type=F32)


def _swiglu_chunked(h, wgu, wd, tf):
    ff = wd.shape[0]
    acc = None
    for f0 in range(0, ff, tf):
        y = _swiglu_partial(h, wgu[:, f0:f0 + tf], wgu[:, ff + f0:ff + f0 + tf], wd[f0:f0 + tf, :])
        acc = y if acc is None else acc + y
    return acc


def _ffn_kernel(x_ref, g_ref, sh_ref, sc_ref, g2_ref, ng_ref, wgu_ref, wd_ref, o_ref, *, tf):
    h = _rms(x_ref[...]) * g_ref[...]
    h = (h * (1.0 + sc_ref[0]) + sh_ref[0]).astype(BF16)
    y = _swiglu_chunked(h, wgu_ref, wd_ref, tf)
    o_ref[...] = x_ref[...] + g2_ref[0] * (_rms(y) * ng_ref[...])


def ffn_residual(x, gain, shift, scale, g2, ng, w_gu, w_down, *, tm, tf):
    t, d = x.shape
    ff = w_gu.shape[1] // 2
    nb = shift.shape[0]
    assert t % (tm * nb) == 0 and ff % tf == 0 and tf % LANES == 0, (t, tm, nb, ff, tf)
    bpb = (t // tm) // nb
    mod_spec = pl.BlockSpec((1, 1, d), lambda i: (i // bpb, 0, 0))
    vec_spec = pl.BlockSpec((1, d), lambda i: (0, 0))
    return pl.pallas_call(
        functools.partial(_ffn_kernel, tf=tf),
        grid=(t // tm,),
        in_specs=[
            pl.BlockSpec((tm, d), lambda i: (i, 0)), vec_spec, mod_spec, mod_spec, mod_spec, vec_spec,
            pl.BlockSpec((d, 2 * ff), lambda i: (0, 0)),
            pl.BlockSpec((ff, d), lambda i: (0, 0)),
        ],
        out_specs=pl.BlockSpec((tm, d), lambda i: (i, 0)),
        out_shape=jax.ShapeDtypeStruct((t, d), F32),
        compiler_params=_params("parallel"),
        name="ffn_residual",
    )(x, gain.reshape(1, d), shift, scale, g2, ng.reshape(1, d), w_gu, w_down)


MOE_TM = 512
MOE_TF = 256
ROW_TILE = 512
DMA_UNROLL = 8
META_E0, META_E1, META_W0, META_W1, META_R0, META_R1 = range(6)


def _lane_pick(rec, lane, k):
    return jnp.sum(jnp.where(lane == k, rec, 0.0), axis=-1, keepdims=True)


def _router_kernel(x_ref, g_ref, sh_ref, sc_ref, w_ref, b_ref, h_ref, meta_ref, meta_t_ref, cnt_ref, carry_s):
    @pl.when(pl.program_id(0) == 0)
    def _():
        carry_s[...] = jnp.zeros_like(carry_s)

    h = _rms(x_ref[...]) * g_ref[...]
    h = h * (1.0 + sc_ref[0]) + sh_ref[0]
    h_ref[...] = h
    logits = jnp.dot(h, w_ref[...], preferred_element_type=F32, precision=lax.Precision.HIGHEST) + b_ref[...]
    lane = lax.broadcasted_iota(jnp.int32, logits.shape, 1).astype(F32)
    neg = jnp.float32(-jnp.inf)
    logits = jnp.where(lane < N_EXPERTS, logits, neg)
    m1 = jnp.max(logits, axis=-1, keepdims=True)
    i1 = jnp.min(jnp.where(logits == m1, lane, float(LANES)), axis=-1, keepdims=True)
    rest = jnp.where(lane == i1, neg, logits)
    m2 = jnp.max(rest, axis=-1, keepdims=True)
    i2 = jnp.min(jnp.where(rest == m2, lane, float(LANES)), axis=-1, keepdims=True)
    e2 = jnp.exp(m2 - m1)
    denom = 1.0 + e2

    tm = logits.shape[0]
    sel1 = lane == i1
    sel2 = lane == i2
    member = jnp.where(jnp.logical_or(sel1, sel2), 1.0, 0.0)
    row = lax.broadcasted_iota(jnp.int32, (tm, tm), 0)
    col = lax.broadcasted_iota(jnp.int32, (tm, tm), 1)
    lower = jnp.where(col < row, 1.0, 0.0).astype(BF16)
    before = jnp.dot(lower, member.astype(BF16), preferred_element_type=F32) + carry_s[0:1, :]
    r1 = jnp.sum(jnp.where(sel1, before, 0.0), axis=-1, keepdims=True)
    r2 = jnp.sum(jnp.where(sel2, before, 0.0), axis=-1, keepdims=True)
    carry_s[0:1, :] = carry_s[0:1, :] + jnp.sum(member, axis=0, keepdims=True)
    cnt_ref[...] = jnp.broadcast_to(carry_s[0:1, :], cnt_ref.shape)

    meta = jnp.zeros_like(logits)
    for k, v in ((META_E0, i1), (META_E1, i2), (META_W0, 1.0 / denom), (META_W1, e2 / denom),
                 (META_R0, r1), (META_R1, r2)):
        meta = jnp.where(lane == k, v, meta)
    meta_ref[...] = meta
    meta_t_ref[...] = meta.T[0:SUBLANES, :]


def moe_router(x, gain, shift, scale, router_w, router_b, *, tm):
    t, d = x.shape
    nb = shift.shape[0]
    bpb = (t // tm) // nb
    w = jnp.zeros((d, LANES), F32).at[:, :N_EXPERTS].set(router_w)
    b = jnp.zeros((1, LANES), F32).at[0, :N_EXPERTS].set(router_b)
    return pl.pallas_call(
        _router_kernel,
        grid=(t // tm,),
        in_specs=[
            pl.BlockSpec((tm, d), lambda i: (i, 0)),
            pl.BlockSpec((1, d), lambda i: (0, 0)),
            pl.BlockSpec((1, 1, d), lambda i: (i // bpb, 0, 0)),
            pl.BlockSpec((1, 1, d), lambda i: (i // bpb, 0, 0)),
            pl.BlockSpec((d, LANES), lambda i: (0, 0)),
            pl.BlockSpec((1, LANES), lambda i: (0, 0)),
        ],
        out_specs=[
            pl.BlockSpec((tm, d), lambda i: (i, 0)),
            pl.BlockSpec((tm, LANES), lambda i: (i, 0)),
            pl.BlockSpec((SUBLANES, tm), lambda i: (0, i)),
            pl.BlockSpec((SUBLANES, LANES), lambda i: (0, 0)),
        ],
        out_shape=[
            jax.ShapeDtypeStruct((t, d), F32),
            jax.ShapeDtypeStruct((t, LANES), F32),
            jax.ShapeDtypeStruct((SUBLANES, t), F32),
            jax.ShapeDtypeStruct((SUBLANES, LANES), F32),
        ],
        scratch_shapes=[pltpu.VMEM((SUBLANES, LANES), F32)],
        compiler_params=_params("arbitrary"),
        name="moe_router",
    )(x, gain.reshape(1, d), shift, scale, w, b)


def _row_copy(src_ref, src_row, dst_ref, dst_row, sem):
    return pltpu.make_async_copy(src_ref.at[pl.ds(src_row, 1), :], dst_ref.at[pl.ds(dst_row, 1), :], sem)


FILL_START, FILL_COUNT = 0, 1


def _dispatch_kernel(fill_ref, pos_ref, h_ref, wgu_ref, wdn_ref, hs_ref, wgu_out_ref, wdn_out_ref, buf_s, zero_s,
                     sems):
    wgu_out_ref[...] = wgu_ref[...].astype(BF16)
    wdn_out_ref[...] = wdn_ref[...].astype(BF16)

    i = pl.program_id(0)
    n_steps = pl.num_programs(0)
    n = h_ref.shape[0]
    slot = i % 2
    zero_sem = sems.at[2]

    @pl.when(i == 0)
    def _():
        zero_s[...] = jnp.zeros_like(zero_s)

        def pad_copy(e, k):
            return _row_copy(zero_s, 0, hs_ref, fill_ref[FILL_START, e] + k, zero_sem)

        def tail_copy(k):
            row = pl.multiple_of(fill_ref[FILL_START, N_EXPERTS] + k * SUBLANES, SUBLANES)
            return pltpu.make_async_copy(zero_s, hs_ref.at[pl.ds(row, SUBLANES), :], zero_sem)

        def for_each(copy, count, op):
            def body(k, carry):
                getattr(copy(k), op)()
                return carry

            lax.fori_loop(0, count, body, 0)

        for op in ("start", "wait"):
            for e in range(N_EXPERTS):
                for_each(functools.partial(pad_copy, e), fill_ref[FILL_COUNT, e], op)
            for_each(tail_copy, fill_ref[FILL_COUNT, N_EXPERTS], op)

    buf_s[slot] = h_ref[...]
    src = buf_s.at[slot]
    sem = sems.at[slot]

    def issue(g, carry):
        for j in range(DMA_UNROLL):
            r = g * DMA_UNROLL + j
            _row_copy(src, r, hs_ref, pos_ref[0, 0, 2 * r], sem).start(priority=0)
            _row_copy(src, r, hs_ref, pos_ref[0, 0, 2 * r + 1], sem).start(priority=1)
        return carry

    lax.fori_loop(0, n // DMA_UNROLL, issue, 0)

    def drain(s):
        for _ in range(TOP_K):
            pltpu.make_async_copy(buf_s.at[s], hs_ref.at[pl.ds(0, n), :], sems.at[s]).wait()

    @pl.when(i > 0)
    def _():
        drain(1 - slot)

    @pl.when(i == n_steps - 1)
    def _():
        drain(slot)


def moe_dispatch(h, pos, fill, n_rows, w_gu, w_down):
    t, d = h.shape
    n_steps = t // ROW_TILE
    wgu2 = w_gu.reshape(-1, w_gu.shape[-1])
    wdn2 = w_down.reshape(-1, w_down.shape[-1])
    gu_rows, dn_rows = wgu2.shape[0] // n_steps, wdn2.shape[0] // n_steps
    bf16_rows = 2 * SUBLANES
    assert wgu2.shape[0] % n_steps == 0 and wdn2.shape[0] % n_steps == 0, (wgu2.shape, wdn2.shape, n_steps)
    assert gu_rows % bf16_rows == 0 and dn_rows % bf16_rows == 0, (gu_rows, dn_rows)
    hs, wgu_bf, wdn_bf = pl.pallas_call(
        _dispatch_kernel,
        grid_spec=pltpu.PrefetchScalarGridSpec(
            num_scalar_prefetch=1,
            grid=(n_steps,),
            in_specs=[
                pl.BlockSpec((1, 1, 2 * ROW_TILE), lambda i, fill: (i, 0, 0), memory_space=pltpu.SMEM),
                pl.BlockSpec((ROW_TILE, d), lambda i, fill: (i, 0)),
                pl.BlockSpec((gu_rows, wgu2.shape[1]), lambda i, fill: (i, 0)),
                pl.BlockSpec((dn_rows, wdn2.shape[1]), lambda i, fill: (i, 0)),
            ],
            out_specs=[
                pl.BlockSpec(memory_space=pl.ANY),
                pl.BlockSpec((gu_rows, wgu2.shape[1]), lambda i, fill: (i, 0)),
                pl.BlockSpec((dn_rows, wdn2.shape[1]), lambda i, fill: (i, 0)),
            ],
            scratch_shapes=[pltpu.VMEM((2, ROW_TILE, d), F32), pltpu.VMEM((SUBLANES, d), F32),
                            pltpu.SemaphoreType.DMA((3,))],
        ),
        out_shape=[
            jax.ShapeDtypeStruct((n_rows, d), F32),
            jax.ShapeDtypeStruct(wgu2.shape, BF16),
            jax.ShapeDtypeStruct(wdn2.shape, BF16),
        ],
        compiler_params=_params("arbitrary"),
        name="moe_dispatch",
    )(fill, pos, h, wgu2, wdn2)
    return hs, wgu_bf.reshape(w_gu.shape), wdn_bf.reshape(w_down.shape)


def _grouped_ffn_kernel(te_ref, na_ref, hs_ref, wgu_ref, wd_ref, ys_ref):
    del te_ref
    active = pl.program_id(0) < na_ref[0]

    @pl.when(jnp.logical_not(active))
    def _():
        ys_ref[...] = jnp.zeros_like(ys_ref)

    @pl.when(active)
    def _():
        ys_ref[...] = _swiglu_chunked(hs_ref[...].astype(BF16), wgu_ref.at[0], wd_ref.at[0], MOE_TF)


def moe_grouped_ffn(hs, tile_expert, n_active, w_gu, w_down):
    n_rows, d = hs.shape
    ff = w_gu.shape[2] // 2
    assert ff % MOE_TF == 0 and MOE_TF % LANES == 0 and n_rows % MOE_TM == 0, (ff, MOE_TF, n_rows)

    def row_map(t, te, na):
        return (jnp.maximum(jnp.minimum(t, na[0] - 1), 0), 0)

    return pl.pallas_call(
        _grouped_ffn_kernel,
        grid_spec=pltpu.PrefetchScalarGridSpec(
            num_scalar_prefetch=2,
            grid=(n_rows // MOE_TM,),
            in_specs=[
                pl.BlockSpec((MOE_TM, d), row_map),
                pl.BlockSpec((1, d, 2 * ff), lambda t, te, na: (te[t], 0, 0)),
                pl.BlockSpec((1, ff, d), lambda t, te, na: (te[t], 0, 0)),
            ],
            out_specs=pl.BlockSpec((MOE_TM, d), lambda t, te, na: (t, 0)),
        ),
        out_shape=jax.ShapeDtypeStruct((n_rows, d), F32),
        compiler_params=_params("arbitrary"),
        name="moe_grouped_ffn",
    )(tile_expert, n_active, hs, w_gu, w_down)


def _combine_kernel(pos_ref, pos_next_ref, ys_ref, x_ref, meta_ref, g2_ref, ng_ref, o_ref, a_s, b_s, sems):
    i = pl.program_id(0)
    n_steps = pl.num_programs(0)
    n = x_ref.shape[0]
    slot = i % 2

    def gather(p_ref, s):
        def issue(g, carry):
            for j in range(DMA_UNROLL):
                r = g * DMA_UNROLL + j
                _row_copy(ys_ref, p_ref[0, 0, 2 * r], a_s.at[s], r, sems.at[s]).start(priority=0)
                _row_copy(ys_ref, p_ref[0, 0, 2 * r + 1], b_s.at[s], r, sems.at[s]).start(priority=1)
            return carry

        lax.fori_loop(0, n // DMA_UNROLL, issue, 0)

    @pl.when(i == 0)
    def _():
        gather(pos_ref, 0)

    @pl.when(i + 1 < n_steps)
    def _():
        gather(pos_next_ref, 1 - slot)

    pltpu.make_async_copy(ys_ref.at[pl.ds(0, n), :], a_s.at[slot], sems.at[slot]).wait()
    pltpu.make_async_copy(ys_ref.at[pl.ds(0, n), :], b_s.at[slot], sems.at[slot]).wait()

    meta = meta_ref[...]
    lane = lax.broadcasted_iota(jnp.int32, meta.shape, 1)
    mixed = _lane_pick(meta, lane, META_W0) * a_s[slot] + _lane_pick(meta, lane, META_W1) * b_s[slot]
    o_ref[...] = x_ref[...] + g2_ref[0] * (_rms(mixed) * ng_ref[...])


def moe_combine(ys, pos, meta, x, g2, ng):
    t, d = x.shape
    nb = g2.shape[0]
    n_steps = t // ROW_TILE
    bpb = n_steps // nb
    return pl.pallas_call(
        _combine_kernel,
        grid=(n_steps,),
        in_specs=[
            pl.BlockSpec((1, 1, 2 * ROW_TILE), lambda i: (i, 0, 0), memory_space=pltpu.SMEM),
            pl.BlockSpec((1, 1, 2 * ROW_TILE), lambda i: (jnp.minimum(i + 1, n_steps - 1), 0, 0),
                         memory_space=pltpu.SMEM),
            pl.BlockSpec(memory_space=pl.ANY),
            pl.BlockSpec((ROW_TILE, d), lambda i: (i, 0)),
            pl.BlockSpec((ROW_TILE, LANES), lambda i: (i, 0)),
            pl.BlockSpec((1, 1, d), lambda i: (i // bpb, 0, 0)),
            pl.BlockSpec((1, d), lambda i: (0, 0)),
        ],
        out_specs=pl.BlockSpec((ROW_TILE, d), lambda i: (i, 0)),
        out_shape=jax.ShapeDtypeStruct((t, d), F32),
        scratch_shapes=[pltpu.VMEM((2, ROW_TILE, d), F32), pltpu.VMEM((2, ROW_TILE, d), F32),
                        pltpu.SemaphoreType.DMA((2,))],
        compiler_params=_params("arbitrary"),
        name="moe_combine",
    )(pos, pos, ys, x, meta, g2, ng.reshape(1, d))


def moe_residual(x, gain, shift, scale, g2, ng, router_w, router_b, w_gu, w_down):
    t, d = x.shape
    h, meta, meta_t, counts = moe_router(x, gain, shift, scale, router_w, router_b, tm=512)
    n_rows = TOP_K * t + N_EXPERTS * MOE_TM
    n_tiles = n_rows // MOE_TM
    cnt = counts[0, :N_EXPERTS].astype(jnp.int32)
    padded = (cnt + MOE_TM - 1) // MOE_TM * MOE_TM
    seg_end = jnp.cumsum(padded)
    seg_start = seg_end - padded
    experts = jnp.arange(N_EXPERTS, dtype=jnp.int32)

    def position(e_field, r_field):
        e = meta_t[e_field].astype(jnp.int32)
        start = jnp.sum(jnp.where(e[None, :] == experts[:, None], seg_start[:, None], 0), axis=0)
        return start + meta_t[r_field].astype(jnp.int32)

    pos = jnp.stack([position(META_E0, META_R0), position(META_E1, META_R1)], axis=-1)
    pos = pos.reshape(t // ROW_TILE, 1, 2 * ROW_TILE)
    n_active = seg_end[-1:] // MOE_TM
    tile_ids = jnp.arange(n_tiles, dtype=jnp.int32)
    tile_expert = jnp.sum((jnp.minimum(tile_ids, n_active - 1)[:, None] * MOE_TM >= seg_end[None, :]), axis=-1)
    tile_expert = jnp.minimum(tile_expert, N_EXPERTS - 1).astype(jnp.int32)

    fill = jnp.stack([
        jnp.concatenate([seg_start + cnt, seg_end[-1:], jnp.zeros((N_EXPERTS - 1,), jnp.int32)]),
        jnp.concatenate([padded - cnt, (n_rows - seg_end[-1:]) // SUBLANES, jnp.zeros((N_EXPERTS - 1,), jnp.int32)]),
    ]).astype(jnp.int32)
    hs, w_gu_bf, w_down_bf = moe_dispatch(h, pos, fill, n_rows, w_gu, w_down)
    ys = moe_grouped_ffn(hs, tile_expert, n_active.astype(jnp.int32), w_gu_bf, w_down_bf)
    return moe_combine(ys, pos, meta, x, g2, ng)


def _rope_tables():
    pos = jnp.arange(SEQ)
    inv_freq = jnp.power(ROPE_THETA, -jnp.arange(ROPE_FREQS, dtype=F32) / ROPE_FREQS)
    ang_r = (pos // GRID_W).astype(F32)[:, None] * inv_freq
    ang_c = (pos % GRID_W).astype(F32)[:, None] * inv_freq
    cos = jnp.concatenate([jnp.cos(ang_r)] * 2 + [jnp.cos(ang_c)] * 2, axis=-1)
    sin = jnp.concatenate([-jnp.sin(ang_r), jnp.sin(ang_r), -jnp.sin(ang_c), jnp.sin(ang_c)], axis=-1)
    return jnp.tile(cos, (1, 2)), jnp.tile(sin, (1, 2))


def _gate_layout(gate_w, gate_b, lam):
    c = RNN_CB
    n_cb = D_RNN // c
    per = c // RNN_BLOCK_W
    gw = gate_w.reshape(2, 2, n_cb, per, RNN_BLOCK_W, RNN_BLOCK_W)
    gw = jnp.transpose(gw, (2, 3, 4, 0, 1, 5))
    half_eye = 0.5 * jnp.eye(per, dtype=gate_w.dtype)
    bd = gw[:, :, :, :, :, None, :] * half_eye[None, :, None, None, None, :, None]
    wg = bd.reshape(n_cb, c, 4 * c).astype(BF16)
    gb = 0.5 * jnp.transpose(gate_b.reshape(2, 2, n_cb, c), (2, 0, 1, 3)).reshape(n_cb, 1, 4 * c)
    lm = jnp.transpose(lam.reshape(2, n_cb, c), (1, 0, 2)).reshape(n_cb, 1, 2 * c)
    return wg, gb, lm


def kernel(x, c, ctx, c_ctx, ada_w, ada_b, norm_g, w_in, conv_w, conv_b, lru_gate_w, lru_gate_b, lru_lambda,
           diff_lambda, subln_g, w_proj_rnn, w_proj_attn, w_out, ffn_w_gu, ffn_w_down, router_w, router_b,
           moe_w_gu, moe_w_down):
    xt = x.reshape(BATCH * SEQ, D_MODEL)
    ct = ctx.reshape(BATCH * CTX_LEN, D_MODEL)
    cos, sin_signed = _rope_tables()
    w_in_bf = w_in.astype(BF16)

    cvec = jnp.concatenate([c, c_ctx[None, :], jnp.zeros((2 * SUBLANES - BATCH - 1, D_MODEL), F32)], axis=0)
    mod = ada_modulation(cvec, ada_w, ada_b)

    for l in range(DEPTH):
        last = l == DEPTH - 1
        lam_init = 0.8 - 0.6 * math.exp(-0.3 * l)
        mx = mod[l, :BATCH].reshape(BATCH, 1, 6, D_MODEL)
        mc = mod[l, BATCH:BATCH + 1].reshape(1, 1, 6, D_MODEL)
        sh1x, sc1x, g1x, sh2x, sc2x, g2x = (mx[:, :, i] for i in range(6))
        sh1c, sc1c, g1c, sh2c, sc2c, g2c = (mc[:, :, i] for i in range(6))

        wg, gb, lm = _gate_layout(lru_gate_w[l], lru_gate_b[l], lru_lambda[l])

        proj_x = norm_mod_matmul(xt, norm_g[l, 0], sh1x, sc1x, w_in_bf, l, IN_W, tm=512, tn=256)
        proj_c = norm_mod_matmul(ct, norm_g[l, 0], sh1c, sc1c, w_in_bf, l, CTX_STATE_W if last else IN_W, tm=512,
                                 tn=256)

        y_rnn_x, y_rnn_c = rglru(proj_x, proj_c, conv_w[l], conv_b[l], wg, gb, lm, xr_col=COL_XR, gr_col=COL_GR,
                                 c_xr_col=COL_XR, c_gr_col=COL_GR, ctx_out=not last)
        y_attn_x = diff_attention(proj_x, COL_Q, SEQ, proj_x, proj_c, (COL_K, COL_V, COL_K, COL_V), cos,
                                  sin_signed, diff_lambda[l], subln_g[l], lam_init=lam_init, tq=1024)
        wr = w_proj_rnn[l].astype(BF16)
        wa = w_proj_attn[l].astype(BF16)
        wo = w_out[l].astype(BF16)
        xt = merge_out(y_rnn_x, y_attn_x, proj_x, xt, g1x, norm_g[l, 1], wr, wa, wo, tm=512)
        if not last:
            y_attn_c = diff_attention(proj_c, COL_Q, CTX_LEN, None, proj_c, (0, 0, COL_K, COL_V), None, None,
                                      diff_lambda[l], subln_g[l], lam_init=lam_init, tq=CTX_LEN)
            ct = merge_out(y_rnn_c, y_attn_c, proj_c, ct, g1c, norm_g[l, 1], wr, wa, wo, tm=512)

        if l % 2 == 0:
            w_gu = ffn_w_gu[l // 2].astype(BF16)
            w_dn = ffn_w_down[l // 2].astype(BF16)
            xt = ffn_residual(xt, norm_g[l, 2], sh2x, sc2x, g2x, norm_g[l, 3], w_gu, w_dn, tm=512, tf=256)
            if not last:
                ct = ffn_residual(ct, norm_g[l, 2], sh2c, sc2c, g2c, norm_g[l, 3], w_gu, w_dn, tm=512, tf=256)
        else:
            w_gu, w_dn = moe_w_gu[l // 2], moe_w_down[l // 2]
            rw, rb = router_w[l // 2], router_b[l // 2]
            xt = moe_residual(xt, norm_g[l, 2], sh2x, sc2x, g2x, norm_g[l, 3], rw, rb, w_gu, w_dn)
            if not last:
                ct = moe_residual(ct, norm_g[l, 2], sh2c, sc2c, g2c, norm_g[l, 3], rw, rb, w_gu, w_dn)
    return xt.reshape(BATCH, SEQ, D_MODEL)
```
